```python
import jax
import jax.numpy as jnp
from jax import lax
import numpy as np

D_MODEL = 1024
BATCH = 16
SEQ = 256
DEPTH = 4
DEC_BATCH = 8
DEC_SEQ = 4096
PAST_LEN = 512

GRID_W = 64
N_DIR = 2
W_RWKV = D_MODEL // 4
W_GLA = D_MODEL // 4
W_LRU = D_MODEL // 4
W_CONV = D_MODEL - W_RWKV - W_GLA - W_LRU

RW_HEADS = 4
RW_HEAD = W_RWKV // RW_HEADS
RW_DECAY_LORA = 32
RW_AAA_LORA = 32
RW_GATE_LORA = 64
RW_LN_EPS = 64e-5
RW_COLS = 3 * W_RWKV + N_DIR * RW_DECAY_LORA + N_DIR * RW_AAA_LORA + RW_GATE_LORA

GLA_HEADS = 4
GLA_DV = W_GLA // GLA_HEADS
GLA_DK = GLA_DV // 2
GLA_GATE_LORA = 16
GLA_GATE_NORM = 16.0
GLA_CHUNK = 64
GLA_COLS = 2 * GLA_HEADS * GLA_DK + 2 * W_GLA + N_DIR * GLA_GATE_LORA

LRU_BLOCKS = 4
LRU_BLOCK = W_LRU // LRU_BLOCKS
LRU_CONV = 4
LRU_C = 8.0
LRU_COLS = 2 * W_LRU

CV_KERNEL = 31
CV_COLS = 2 * W_CONV

P_IN = RW_COLS + GLA_COLS + LRU_COLS + CV_COLS

N_EXPERTS = 16
N_GROUPS = 4
GROUP_SIZE = N_EXPERTS // N_GROUPS
TOP_K = 2
D_EXPERT = 256
NORM_EPS = 1e-6

kernel_name = "hybrid_flow_trunk_step"


def rmsnorm(x, g):
    xf = x.astype(jnp.float32)
    y = xf * lax.rsqrt(jnp.mean(xf * xf, axis=-1, keepdims=True) + NORM_EPS)
    return (y * g.astype(jnp.float32)).astype(x.dtype)


def layernorm(x, g, b, eps):
    xf = x.astype(jnp.float32)
    mu = jnp.mean(xf, axis=-1, keepdims=True)
    var = jnp.mean(jnp.square(xf - mu), axis=-1, keepdims=True)
    y = (xf - mu) * lax.rsqrt(var + eps)
    return (y * g.astype(jnp.float32) + b.astype(jnp.float32)).astype(x.dtype)


def token_shift_ctx(z):
    b, t, ch = z.shape
    z2 = z.reshape(b, t, ch // 2, 2)
    prev = jnp.pad(z2[:, :-1, :, 0], ((0, 0), (1, 0), (0, 0)))
    nxt = jnp.pad(z2[:, 1:, :, 1], ((0, 0), (0, 1), (0, 0)))
    return jnp.stack([prev, nxt], axis=-1).reshape(b, t, ch)


def token_shift_grid(z):
    b, t, ch = z.shape
    rows = t // GRID_W
    g = z.reshape(b, rows, GRID_W, ch // 4, 4)
    left = jnp.pad(g[:, :, :-1, :, 0], ((0, 0), (0, 0), (1, 0), (0, 0)))
    right = jnp.pad(g[:, :, 1:, :, 1], ((0, 0), (0, 0), (0, 1), (0, 0)))
    up = jnp.pad(g[:, :-1, :, :, 2], ((0, 0), (1, 0), (0, 0), (0, 0)))
    down = jnp.pad(g[:, 1:, :, :, 3], ((0, 0), (0, 1), (0, 0), (0, 0)))
    return jnp.stack([left, right, up, down], axis=-1).reshape(b, t, ch)


def dwconv1d(x, w, pad_l, pad_r):
    return lax.conv_general_dilated(x, w[:, None, :].astype(x.dtype), (1,), [(pad_l, pad_r)],
                                    dimension_numbers=("NWC", "WIO", "NWC"), feature_group_count=x.shape[-1])


def dwconv_grid(x, w):
    b, t, ch = x.shape
    rows = t // GRID_W
    half = ch // 2
    pad = (w.shape[0] - 1) // 2
    g = x.reshape(b, rows, GRID_W, ch)
    dn = ("NHWC", "HWIO", "NHWC")
    wk = w.astype(x.dtype)
    yh = lax.conv_general_dilated(g[..., :half], wk[None, :, None, :half], (1, 1), [(0, 0), (pad, pad)],
                                  dimension_numbers=dn, feature_group_count=half)
    yv = lax.conv_general_dilated(g[..., half:], wk[:, None, None, half:], (1, 1), [(pad, pad), (0, 0)],
                                  dimension_numbers=dn, feature_group_count=ch - half)
    return jnp.concatenate([yh, yv], axis=-1).reshape(b, t, ch)


def rwkv7_scan(r, w, k, v, a, bb, s0):
    def step(s, inp):
        r_t, w_t, k_t, v_t, a_t, b_t = inp
        sa = jnp.einsum("bhvk,bhk->bhv", s, a_t)
        s = s * w_t[:, :, None, :] + sa[..., None] * b_t[:, :, None, :] + v_t[..., None] * k_t[:, :, None, :]
        return s, jnp.einsum("bhvk,bhk->bhv", s, r_t)
    xs = tuple(jnp.moveaxis(z, 1, 0) for z in (r, w, k, v, a, bb))
    s_fin, ys = lax.scan(step, s0.astype(jnp.float32), xs)
    return jnp.moveaxis(ys, 0, 1), s_fin


def rwkv7_mixer(z, latent, s0, p):
    b, t, _ = z.shape
    shifted = token_shift_grid(z) if latent else token_shift_ctx(z)
    z = z + (shifted - z) * p["rw_mu"]
    o1 = 3 * W_RWKV
    o2 = o1 + N_DIR * RW_DECAY_LORA
    o3 = o2 + N_DIR * RW_AAA_LORA
    r, k, v, wl, al, gl = jnp.split(z, [W_RWKV, 2 * W_RWKV, o1, o2, o3], axis=-1)
    heads = lambda u: u.reshape(b, t, RW_HEADS, RW_HEAD).astype(jnp.float32)
    rf, kf, vf = heads(r), heads(k), heads(v)
    kk = kf * p["rw_kk"].reshape(RW_HEADS, RW_HEAD).astype(jnp.float32)
    kk = kk * lax.rsqrt(jnp.sum(kk * kk, axis=-1, keepdims=True) + 1e-12)
    wl = wl.reshape(b, t, N_DIR, RW_DECAY_LORA)
    al = al.reshape(b, t, N_DIR, RW_AAA_LORA)
    ka = p["rw_ka"].reshape(RW_HEADS, RW_HEAD).astype(jnp.float32)
    ys, finals = [], []
    for d in range(N_DIR):
        wraw = (p["rw_w0"][d] + jnp.tanh(wl[:, :, d]) @ p["rw_w2"][d]).astype(jnp.float32)
        decay = heads(jnp.exp(-jnp.exp(-jax.nn.softplus(-wraw) - 0.5)))
        icl = heads(jax.nn.sigmoid((p["rw_a0"][d] + al[:, :, d] @ p["rw_a2"][d]).astype(jnp.float32)))
        kd = kf * (1.0 + (icl - 1.0) * ka)
        inputs = (rf, decay, kd, vf, -kk, kk * icl)
        if d == 1:
            inputs = tuple(jnp.flip(u, 1) for u in inputs)
        y, s_fin = rwkv7_scan(*inputs, s0[:, d])
        if d == 1:
            y = jnp.flip(y, 1)
        ys.append(y)
        finals.append(s_fin)
    y = ys[0] + ys[1]
    mu = jnp.mean(y, axis=-1, keepdims=True)
    var = jnp.mean(jnp.square(y - mu), axis=-1, keepdims=True)
    y = ((y - mu) * lax.rsqrt(var + RW_LN_EPS)).reshape(b, t, W_RWKV)
    y = y * p["rw_ln_g"].astype(jnp.float32) + p["rw_ln_b"].astype(jnp.float32)
    bonus = jnp.sum(rf * kf * p["rw_rk"].astype(jnp.float32), axis=-1, keepdims=True) * vf
    y = y + bonus.reshape(b, t, W_RWKV)
    gate = jax.nn.sigmoid(gl) @ p["rw_g2"]
    return y.astype(z.dtype) * gate, jnp.stack(finals, axis=1)


def gla_chunked(q, k, v, log_a, s0):
    b, t, h, dk = q.shape
    dv = v.shape[-1]
    n = t // GLA_CHUNK
    q, k, log_a = (u.reshape(b, n, GLA_CHUNK, h, dk) for u in (q, k, log_a))
    v = v.reshape(b, n, GLA_CHUNK, h, dv)
    cum = jnp.cumsum(log_a, axis=2)
    last = cum[:, :, -1]
    qe = q * jnp.exp(cum)
    ke = k * jnp.exp(-cum)
    kl = k * jnp.exp(last[:, :, None] - cum)
    mask = jnp.tril(jnp.ones((GLA_CHUNK, GLA_CHUNK), dtype=bool))
    att = jnp.where(mask, jnp.einsum("bnthk,bnshk->bnhts", qe, ke), 0.0)
    o = jnp.einsum("bnhts,bnshv->bnthv", att, v)
    ds = jnp.einsum("bnshk,bnshv->bnhkv", kl, v)

    def step(s, inp):
        dec, dsi = inp
        return s * dec[..., None] + dsi, s

    s_fin, s_prev = lax.scan(step, s0.astype(jnp.float32),
                             (jnp.moveaxis(jnp.exp(last), 1, 0), jnp.moveaxis(ds, 1, 0)))
    o = o + jnp.einsum("bnthk,bnhkv->bnthv", qe, jnp.moveaxis(s_prev, 0, 1))
    return o.reshape(b, t, h, dv), s_fin


def gla_mixer(z, s0, p):
    b, t, _ = z.shape
    hk = GLA_HEADS * GLA_DK
    q, k, v, gl, og = jnp.split(z, [hk, 2 * hk, 2 * hk + W_GLA, 2 * hk + W_GLA + N_DIR * GLA_GATE_LORA], axis=-1)
    q = q.reshape(b, t, GLA_HEADS, GLA_DK).astype(jnp.float32) * (GLA_DK ** -0.5)
    k = k.reshape(b, t, GLA_HEADS, GLA_DK).astype(jnp.float32)
    v = v.reshape(b, t, GLA_HEADS, GLA_DV).astype(jnp.float32)
    gl = gl.reshape(b, t, N_DIR, GLA_GATE_LORA)
    ys, finals = [], []
    for d in range(N_DIR):
        logit = (gl[:, :, d] @ p["gla_gk2"][d] + p["gla_gkb"][d]).astype(jnp.float32)
        log_a = (jax.nn.log_sigmoid(logit) / GLA_GATE_NORM).reshape(b, t, GLA_HEADS, GLA_DK)
        inputs = (q, k, v, log_a)
        if d == 1:
            inputs = tuple(jnp.flip(u, 1) for u in inputs)
        y, s_fin = gla_chunked(*inputs, s0[:, d])
        if d == 1:
            y = jnp.flip(y, 1)
        ys.append(y)
        finals.append(s_fin)
    y = ys[0] + ys[1]
    y = y * lax.rsqrt(jnp.mean(y * y, axis=-1, keepdims=True) + NORM_EPS) * p["gla_ln_g"].astype(jnp.float32)
    return y.reshape(b, t, W_GLA).astype(z.dtype) * jax.nn.silu(og), jnp.stack(finals, axis=1)


def lru_combine(left, right):
    a1, b1 = left
    a2, b2 = right
    return a1 * a2, a2 * b1 + b2


def rglru_mixer(z, h0, p):
    b, t, _ = z.shape
    xb, gb = jnp.split(z, 2, axis=-1)
    hs, finals = [], []
    for d in range(N_DIR):
        xd = jnp.flip(xb, 1) if d == 1 else xb
        xc = dwconv1d(xd, p["lru_conv_w"][d], LRU_CONV - 1, 0) + p["lru_conv_b"][d]
        xcb = xc.reshape(b, t, LRU_BLOCKS, LRU_BLOCK)
        gr = jnp.einsum("btgi,gij->btgj", xcb, p["lru_wa"][d]).reshape(b, t, W_LRU) + p["lru_ba"][d]
        gi = jnp.einsum("btgi,gij->btgj", xcb, p["lru_wx"][d]).reshape(b, t, W_LRU) + p["lru_bx"][d]
        gr = jax.nn.sigmoid(gr.astype(jnp.float32))
        gi = jax.nn.sigmoid(gi.astype(jnp.float32))
        log_a = -LRU_C * gr * jax.nn.softplus(-p["lru_lam"][d].astype(jnp.float32))
        a = jnp.exp(log_a)
        xin = jnp.sqrt(-jnp.expm1(2.0 * log_a)) * gi * xc.astype(jnp.float32)
        xin = xin.at[:, 0].add(a[:, 0] * h0[:, d].astype(jnp.float32))
        _, h = lax.associative_scan(lru_combine, (a, xin), axis=1)
        finals.append(h[:, -1])
        hs.append(jnp.flip(h, 1) if d == 1 else h)
    y = (hs[0] + hs[1]).astype(z.dtype)
    return y * jax.nn.gelu(gb), jnp.stack(finals, axis=1)


def conv_module(z, latent, p):
    u = z[..., :W_CONV] * jax.nn.sigmoid(z[..., W_CONV:])
    pad = (CV_KERNEL - 1) // 2
    u = dwconv_grid(u, p["cv_dw_w"]) if latent else dwconv1d(u, p["cv_dw_w"], pad, pad)
    u = jax.nn.silu(layernorm(u + p["cv_dw_b"], p["cv_ln_g"], p["cv_ln_b"], 1e-5))
    return u @ p["cv_pw_w"] + p["cv_pw_b"]


def moe(h, w_router, b_router, w1, w3, w2):
    b, t, d = h.shape
    hf = h.reshape(b * t, d)
    scores = jax.nn.sigmoid((hf @ w_router).astype(jnp.float32))
    sel = scores + b_router.astype(jnp.float32)
    grp_score = jnp.sum(lax.top_k(sel.reshape(-1, N_GROUPS, GROUP_SIZE), TOP_K)[0], axis=-1)
    best = jnp.argmax(grp_score, axis=-1)
    in_group = (jnp.arange(N_EXPERTS) // GROUP_SIZE)[None, :] == best[:, None]
    _, idx = lax.top_k(jnp.where(in_group, sel, -jnp.inf), TOP_K)
    wsel = jnp.take_along_axis(scores, idx, axis=-1)
    wsel = wsel / jnp.sum(wsel, axis=-1, keepdims=True)
    gates = jnp.sum(jax.nn.one_hot(idx, N_EXPERTS, dtype=jnp.float32) * wsel[..., None], axis=1)
    out = jnp.zeros((b * t, d), jnp.float32)
    for e in range(N_EXPERTS):
        he = jax.nn.silu(hf @ w1[e]) * (hf @ w3[e])
        out = out + gates[:, e:e + 1] * (he @ w2[e]).astype(jnp.float32)
    return out.astype(h.dtype).reshape(b, t, d)


def trunk_layer(x, cvec, latent, s_rw, s_gla, s_lru, p, w_router, b_router):
    mod = (jax.nn.silu(cvec) @ p["w_ada"] + p["b_ada"])[:, None, :]
    sh1, sc1, g1, sh2, sc2, g2 = jnp.split(mod, 6, axis=-1)
    h = rmsnorm(x, p["norm1_g"]) * (1.0 + sc1) + sh1
    z = h @ p["w_in"]
    za, zb, zc, zd = jnp.split(z, [RW_COLS, RW_COLS + GLA_COLS, RW_COLS + GLA_COLS + LRU_COLS], axis=-1)
    ya, f_rw = rwkv7_mixer(za, latent, s_rw, p)
    yb, f_gla = gla_mixer(zb, s_gla, p)
    yc, f_lru = rglru_mixer(zc, s_lru, p)
    yd = conv_module(zd, latent, p)
    y = jnp.concatenate([ya, yb, yc, yd], axis=-1) @ p["w_out"]
    x = x + g1 * y
    h2 = rmsnorm(x, p["norm2_g"]) * (1.0 + sc2) + sh2
    x = x + g2 * moe(h2, w_router, b_router, p["e_w1"], p["e_w3"], p["e_w2"])
    return x, f_rw, f_gla, f_lru


def setup_inputs(seed: int = 0) -> dict:
    key = jax.random.key(seed)
    ks = iter(jax.random.split(key, 64))
    L = DEPTH
    E = N_EXPERTS

    def nrm(shape, scale):
        return scale * jax.random.normal(next(ks), shape, jnp.float32)

    def gain(shape):
        return 1.0 + nrm(shape, 0.02)

    u_lam = jax.random.uniform(next(ks), (L, N_DIR, W_LRU), jnp.float32, 0.9, 0.999)
    return {
        "x_prompt": nrm((BATCH, SEQ, D_MODEL), 1.0),
        "x_sample": nrm((DEC_BATCH, DEC_SEQ, D_MODEL), 1.0),
        "state_rwkv": nrm((DEC_BATCH, DEPTH, N_DIR, RW_HEADS, RW_HEAD, RW_HEAD), 0.3),
        "state_gla": nrm((DEC_BATCH, DEPTH, N_DIR, GLA_HEADS, GLA_DK, GLA_DV), 0.3),
        "state_lru": nrm((DEC_BATCH, DEPTH, N_DIR, W_LRU), 0.5),
        "c": nrm((DEC_BATCH, D_MODEL), 1.0),
        "c_ctx": nrm((D_MODEL,), 1.0),
        "norm1_g": gain((L, D_MODEL)),
        "norm2_g": gain((L, D_MODEL)),
        "norm_f_g": gain((D_MODEL,)),
        "w_ada": nrm((L, D_MODEL, 6 * D_MODEL), 0.5 * D_MODEL ** -0.5),
        "b_ada": nrm((L, 6 * D_MODEL), 0.01),
        "w_in": nrm((L, D_MODEL, P_IN), D_MODEL ** -0.5),
        "w_out": nrm((L, D_MODEL, D_MODEL), D_MODEL ** -0.5),
        "rw_mu": jax.random.uniform(next(ks), (L, RW_COLS), jnp.float32),
        "rw_w0": jax.random.uniform(next(ks), (L, N_DIR, W_RWKV), jnp.float32, -6.0, 0.0),
        "rw_w2": nrm((L, N_DIR, RW_DECAY_LORA, W_RWKV), 0.5 * RW_DECAY_LORA ** -0.5),
        "rw_a0": nrm((L, N_DIR, W_RWKV), 0.1),
        "rw_a2": nrm((L, N_DIR, RW_AAA_LORA, W_RWKV), 0.5 * RW_AAA_LORA ** -0.5),
        "rw_g2": nrm((L, RW_GATE_LORA, W_RWKV), RW_GATE_LORA ** -0.5),
        "rw_kk": 1.0 + nrm((L, W_RWKV), 0.1),
        "rw_ka": 1.0 + nrm((L, W_RWKV), 0.1),
        "rw_rk": nrm((L, RW_HEADS, RW_HEAD), 0.1),
        "rw_ln_g": gain((L, W_RWKV)),
        "rw_ln_b": nrm((L, W_RWKV), 0.01),
        "gla_gk2": nrm((L, N_DIR, GLA_GATE_LORA, GLA_HEADS * GLA_DK), GLA_GATE_LORA ** -0.5),
        "gla_gkb": nrm((L, N_DIR, GLA_HEADS * GLA_DK), 0.5),
        "gla_ln_g": gain((L, GLA_DV)),
        "lru_conv_w": nrm((L, N_DIR, LRU_CONV, W_LRU), 0.5),
        "lru_conv_b": nrm((L, N_DIR, W_LRU), 0.01),
        "lru_wa": nrm((L, N_DIR, LRU_BLOCKS, LRU_BLOCK, LRU_BLOCK), LRU_BLOCK ** -0.5),
        "lru_ba": nrm((L, N_DIR, W_LRU), 0.01),
        "lru_wx": nrm((L, N_DIR, LRU_BLOCKS, LRU_BLOCK, LRU_BLOCK), LRU_BLOCK ** -0.5),
        "lru_bx": nrm((L, N_DIR, W_LRU), 0.01),
        "lru_lam": jnp.log(u_lam) - jnp.log1p(-u_lam),
        "cv_dw_w": nrm((L, CV_KERNEL, W_CONV), CV_KERNEL ** -0.5),
        "cv_dw_b": nrm((L, W_CONV), 0.01),
        "cv_ln_g": gain((L, W_CONV)),
        "cv_ln_b": nrm((L, W_CONV), 0.01),
        "cv_pw_w": nrm((L, W_CONV, W_CONV), W_CONV ** -0.5),
        "cv_pw_b": nrm((L, W_CONV), 0.01),
        "w_router": nrm((D_MODEL, E), D_MODEL ** -0.5),
        "b_router": nrm((E,), 0.01),
        "e_w1": nrm((L, E, D_MODEL, D_EXPERT), D_MODEL ** -0.5),
        "e_w3": nrm((L, E, D_MODEL, D_EXPERT), D_MODEL ** -0.5),
        "e_w2": nrm((L, E, D_EXPERT, D_MODEL), D_EXPERT ** -0.5),
    }


def reference(x_prompt, x_sample, state_rwkv, state_gla, state_lru, c, c_ctx,
              norm1_g, norm2_g, norm_f_g, w_ada, b_ada, w_in, w_out,
              rw_mu, rw_w0, rw_w2, rw_a0, rw_a2, rw_g2, rw_kk, rw_ka, rw_rk, rw_ln_g, rw_ln_b,
              gla_gk2, gla_gkb, gla_ln_g,
              lru_conv_w, lru_conv_b, lru_wa, lru_ba, lru_wx, lru_bx, lru_lam,
              cv_dw_w, cv_dw_b, cv_ln_g, cv_ln_b, cv_pw_w, cv_pw_b,
              w_router, b_router, e_w1, e_w3, e_w2):
    stacked = {
        "norm1_g": norm1_g, "norm2_g": norm2_g, "w_ada": w_ada, "b_ada": b_ada,
        "w_in": w_in, "w_out": w_out,
        "rw_mu": rw_mu, "rw_w0": rw_w0, "rw_w2": rw_w2, "rw_a0": rw_a0, "rw_a2": rw_a2,
        "rw_g2": rw_g2, "rw_kk": rw_kk, "rw_ka": rw_ka, "rw_rk": rw_rk,
        "rw_ln_g": rw_ln_g, "rw_ln_b": rw_ln_b,
        "gla_gk2": gla_gk2, "gla_gkb": gla_gkb, "gla_ln_g": gla_ln_g,
        "lru_conv_w": lru_conv_w, "lru_conv_b": lru_conv_b, "lru_wa": lru_wa, "lru_ba": lru_ba,
        "lru_wx": lru_wx, "lru_bx": lru_bx, "lru_lam": lru_lam,
        "cv_dw_w": cv_dw_w, "cv_dw_b": cv_dw_b, "cv_ln_g": cv_ln_g, "cv_ln_b": cv_ln_b,
        "cv_pw_w": cv_pw_w, "cv_pw_b": cv_pw_b,
        "e_w1": e_w1, "e_w3": e_w3, "e_w2": e_w2,
    }
    bp = x_prompt.shape[0]
    zero_rw = jnp.zeros((bp, N_DIR, RW_HEADS, RW_HEAD, RW_HEAD), jnp.float32)
    zero_gla = jnp.zeros((bp, N_DIR, GLA_HEADS, GLA_DK, GLA_DV), jnp.float32)
    zero_lru = jnp.zeros((bp, N_DIR, W_LRU), jnp.float32)
    c_prompt = c_ctx[None, :]
    xp = x_prompt
    xs = x_sample
    fin_rw, fin_gla, fin_lru = [], [], []
    for l in range(DEPTH):
        p = {name: arr[l] for name, arr in stacked.items()}
        xp, s_rw, s_gla, s_lru = trunk_layer(xp, c_prompt, False, zero_rw, zero_gla, zero_lru, p, w_router, b_router)
        fin_rw.append(s_rw)
        fin_gla.append(s_gla)
        fin_lru.append(s_lru)
        xs, _, _, _ = trunk_layer(xs, c, True, state_rwkv[:, l], state_gla[:, l], state_lru[:, l], p,
                                  w_router, b_router)
    y_prompt = rmsnorm(xp, norm_f_g)
    y_sample = rmsnorm(xs, norm_f_g)
    new_state_rwkv = jnp.stack(fin_rw, axis=1)
    new_state_gla = jnp.stack(fin_gla, axis=1)
    new_state_lru = jnp.stack(fin_lru, axis=1)
    return (y_prompt, y_sample, new_state_rwkv, new_state_gla, new_state_lru)
```

```python
import functools

import jax
import jax.numpy as jnp
from jax import lax
from jax.experimental import pallas as pl
from jax.experimental.pallas import tpu as pltpu

F32 = jnp.float32
BF16 = jnp.bfloat16

D_MODEL = 1024
DEPTH = 4
GRID_W = 64
N_DIR = 2
W_MIX = 256
N_HEADS = 4
HEAD = W_MIX // N_HEADS
GLA_DK = 32
GLA_HK = N_HEADS * GLA_DK
RW_LN_EPS = 64e-5
NORM_EPS = 1e-6
GLA_GATE_NORM = 16.0
LRU_C = 8.0
LRU_CONV = 4
CV_KERNEL = 31
CV_PAD = (CV_KERNEL - 1) // 2
N_EXPERTS = 16
GROUP_SIZE = 4
N_GROUPS = 4
D_EXPERT = 256
CHUNK = 64

ZA_W = 1024
ZB_W = 896
ZC_W = 512
ZD_W = 512
P_PAD = ZA_W + ZB_W + ZC_W + ZD_W

VMEM_LIMIT = 48 * 1024 * 1024


def _cparams(sem, vmem=VMEM_LIMIT):
    return pltpu.CompilerParams(dimension_semantics=sem, vmem_limit_bytes=vmem)


def _dot(a, b):
    return jnp.dot(a.astype(BF16), b.astype(BF16), preferred_element_type=F32)


def _dot_nt(a, b):
    return lax.dot_general(a.astype(BF16), b.astype(BF16), (((1,), (1,)), ((), ())),
                           preferred_element_type=F32)


def _dot_tn(a, b):
    return lax.dot_general(a.astype(BF16), b.astype(BF16), (((0,), (0,)), ((), ())),
                           preferred_element_type=F32)


def _split(a):
    hi = a.astype(BF16)
    lo = (a - hi.astype(F32)).astype(BF16)
    return hi, lo


def _dot_x2(a, b_exact):
    hi, lo = _split(a)
    bb = b_exact.astype(BF16)
    return (jnp.dot(hi, bb, preferred_element_type=F32) + jnp.dot(lo, bb, preferred_element_type=F32))


def _dot_left_x2(a_exact, b):
    hi, lo = _split(b)
    aa = a_exact.astype(BF16)
    return (jnp.dot(aa, hi, preferred_element_type=F32) + jnp.dot(aa, lo, preferred_element_type=F32))


def _dot3(a, b):
    ah, al = _split(a)
    bh, bl = _split(b)
    return (jnp.dot(ah, bh, preferred_element_type=F32) + jnp.dot(ah, bl, preferred_element_type=F32)
            + jnp.dot(al, bh, preferred_element_type=F32))


def _dot3_nt(a, b):
    ah, al = _split(a)
    bh, bl = _split(b)
    dn = (((1,), (1,)), ((), ()))
    return (lax.dot_general(ah, bh, dn, preferred_element_type=F32)
            + lax.dot_general(ah, bl, dn, preferred_element_type=F32)
            + lax.dot_general(al, bh, dn, preferred_element_type=F32))


def _iota(shape, axis):
    return lax.broadcasted_iota(jnp.int32, shape, axis)


def _block_mask(rows, cols, rblk, cblk):
    r = _iota((rows, cols), 0) // rblk
    c = _iota((rows, cols), 1) // cblk
    return (r == c).astype(F32)


def _expand(x, bm):
    return jnp.concatenate([x, x, x, x], axis=0) * bm


def _tri(d, strict):
    t = _iota((CHUNK, CHUNK), 0)
    s = _iota((CHUNK, CHUNK), 1)
    if d == 0:
        m = (s < t) if strict else (s <= t)
    else:
        m = (s > t) if strict else (s >= t)
    return m.astype(F32)


def _tri_wide(d, strict):
    t = _iota((CHUNK, N_HEADS * CHUNK), 0)
    s = _iota((CHUNK, N_HEADS * CHUNK), 1) % CHUNK
    if d == 0:
        m = (s < t) if strict else (s <= t)
    else:
        m = (s > t) if strict else (s >= t)
    return m.astype(F32)


ADA_TN = 1536


def _ada_kernel(c_ref, w_ref, b_ref, o_ref):
    c = c_ref[...]
    s = c * jax.nn.sigmoid(c)
    o_ref[0] = _dot3(s, w_ref[0]) + b_ref[0]


def _ada_mod(cvec, w_ada, b_ada):
    rows = cvec.shape[0]
    n_out = w_ada.shape[-1]
    return pl.pallas_call(
        _ada_kernel,
        grid=(DEPTH, n_out // ADA_TN),
        in_specs=[
            pl.BlockSpec((rows, D_MODEL), lambda l, j: (0, 0)),
            pl.BlockSpec((1, D_MODEL, ADA_TN), lambda l, j: (l, 0, j)),
            pl.BlockSpec((1, 1, ADA_TN), lambda l, j: (l, 0, j)),
        ],
        out_specs=pl.BlockSpec((1, rows, ADA_TN), lambda l, j: (l, 0, j)),
        out_shape=jax.ShapeDtypeStruct((DEPTH, rows, n_out), F32),
        compiler_params=_cparams(("arbitrary", "arbitrary")),
        name="ada_mod",
    )(cvec, w_ada, b_ada.reshape(DEPTH, 1, n_out))


IN_TM = 256


def _in_kernel(x_ref, sh_ref, sc_ref, g_ref, w_ref, za_ref, zb_ref, zc_ref, zd_ref):
    x = x_ref[0]
    h = x * lax.rsqrt(jnp.mean(x * x, axis=-1, keepdims=True) + NORM_EPS) * g_ref[...]
    h = (h * (1.0 + sc_ref[0]) + sh_ref[0]).astype(BF16)
    o0, o1, o2 = ZA_W, ZA_W + ZB_W, ZA_W + ZB_W + ZC_W
    za_ref[0] = jnp.dot(h, w_ref[:, 0:o0], preferred_element_type=F32)
    zb_ref[0] = jnp.dot(h, w_ref[:, o0:o1], preferred_element_type=F32)
    zc_ref[0] = jnp.dot(h, w_ref[:, o1:o2], preferred_element_type=F32)
    zd_ref[0] = jnp.dot(h, w_ref[:, o2:P_PAD], preferred_element_type=F32)


def _in_proj(x, m, g, w):
    b, t, _ = x.shape
    tm = min(IN_TM, t)
    tok = lambda w_: pl.BlockSpec((1, tm, w_), lambda i, j: (i, j, 0))
    mod = lambda k: pl.BlockSpec((1, 1, D_MODEL), lambda i, j, k=k: (i, 0, k))
    return pl.pallas_call(
        _in_kernel,
        grid=(b, t // tm),
        in_specs=[tok(D_MODEL), mod(0), mod(1),
                  pl.BlockSpec((1, D_MODEL), lambda i, j: (0, 0)),
                  pl.BlockSpec((D_MODEL, P_PAD), lambda i, j: (0, 0))],
        out_specs=[tok(ZA_W), tok(ZB_W), tok(ZC_W), tok(ZD_W)],
        out_shape=[jax.ShapeDtypeStruct((b, t, w_), F32) for w_ in (ZA_W, ZB_W, ZC_W, ZD_W)],
        compiler_params=_cparams(("arbitrary", "arbitrary")),
        name="in_proj",
    )(x, m, m, g, w)


SHIFT_TB = 512


def _shift_grid_kernel(z_ref, up_ref, dn_ref, mu_ref, o_ref, *, nblk):
    i = pl.program_id(1)
    z = z_ref[0]
    tb, c = z.shape
    col = _iota((tb, c), 0) % GRID_W
    lane = _iota((tb, c), 1) % 4
    left = jnp.where(col == 0, 0.0, pltpu.roll(z, 1, 0))
    right = jnp.where(col == GRID_W - 1, 0.0, pltpu.roll(z, tb - 1, 0))
    up_halo = jnp.where(i > 0, up_ref[0, 0], 0.0)
    dn_halo = jnp.where(i < nblk - 1, dn_ref[0, 0], 0.0)
    up = jnp.concatenate([up_halo, z[:tb - GRID_W]], axis=0)
    down = jnp.concatenate([z[GRID_W:], dn_halo], axis=0)
    sh = jnp.where(lane == 0, left, jnp.where(lane == 1, right, jnp.where(lane == 2, up, down)))
    o_ref[0] = z + (sh - z) * mu_ref[...]


def _shift_ctx_kernel(z_ref, mu_ref, o_ref):
    z = z_ref[0]
    t, c = z.shape
    row = _iota((t, c), 0)
    lane = _iota((t, c), 1) % 2
    prev = jnp.where(row == 0, 0.0, pltpu.roll(z, 1, 0))
    nxt = jnp.where(row == t - 1, 0.0, pltpu.roll(z, t - 1, 0))
    sh = jnp.where(lane == 0, prev, nxt)
    o_ref[0] = z + (sh - z) * mu_ref[...]


def _rw_shift(za, mu, latent):
    b, t, c = za.shape
    mu_spec = pl.BlockSpec((1, c), lambda *_: (0, 0))
    if not latent:
        return pl.pallas_call(
            _shift_ctx_kernel,
            grid=(b,),
            in_specs=[pl.BlockSpec((1, t, c), lambda i: (i, 0, 0)), mu_spec],
            out_specs=pl.BlockSpec((1, t, c), lambda i: (i, 0, 0)),
            out_shape=jax.ShapeDtypeStruct((b, t, c), F32),
            compiler_params=_cparams(("arbitrary",)),
            name="rw_shift_ctx",
        )(za, mu)
    tb = min(SHIFT_TB, t)
    nblk = t // tb
    rpb = tb // GRID_W
    nrow = t // GRID_W
    z4 = za.reshape(b, nrow, GRID_W, c)
    return pl.pallas_call(
        functools.partial(_shift_grid_kernel, nblk=nblk),
        grid=(b, nblk),
        in_specs=[
            pl.BlockSpec((1, tb, c), lambda i, j: (i, j, 0)),
            pl.BlockSpec((1, 1, GRID_W, c), lambda i, j: (i, jnp.maximum(j * rpb - 1, 0), 0, 0)),
            pl.BlockSpec((1, 1, GRID_W, c), lambda i, j: (i, jnp.minimum(j * rpb + rpb, nrow - 1), 0, 0)),
            mu_spec,
        ],
        out_specs=pl.BlockSpec((1, tb, c), lambda i, j: (i, j, 0)),
        out_shape=jax.ShapeDtypeStruct((b, t, c), F32),
        compiler_params=_cparams(("arbitrary", "arbitrary")),
        name="rw_shift_grid",
    )(za, z4, z4, mu)


SCAN_TB = 512


def _rw_scan_kernel(zf_ref, zb_ref, s0_ref, wl_ref, lb_ref, kkp_ref, kap_ref,
                    yf_ref, yb_ref, st_ref,
                    lw_s, r_s, a_s, b_s, kd_s, v_s, *, nchunk):
    i = pl.program_id(1)

    @pl.when(i == 0)
    def _():
        st_ref[...] = s0_ref[...]

    ones_blk = _block_mask(W_MIX, W_MIX, HEAD, HEAD)
    lane128 = _iota((1, 128), 1)

    for d in range(N_DIR):
        z = (zf_ref, zb_ref)[d][0]
        r = z[:, 0:256]
        k = z[:, 256:512]
        v = z[:, 512:768]
        la = z[:, 768:896]
        la_t = jnp.where(lane128 < 64, jnp.tanh(la), la)
        wraw = _dot(la_t, wl_ref[:, 256 * d:256 * d + 256]) + lb_ref[:, 256 * d:256 * d + 256]
        araw = _dot(la_t, wl_ref[:, 512 + 256 * d:768 + 256 * d]) + lb_ref[:, 512 + 256 * d:768 + 256 * d]
        logw = -jnp.exp(-jax.nn.softplus(-wraw) - 0.5)
        icl = jax.nn.sigmoid(araw)
        kkv = k * kkp_ref[...]
        ss = _dot_x2(kkv * kkv, ones_blk)
        kk = kkv * lax.rsqrt(ss + 1e-12)
        lw_s[d] = logw
        r_s[d] = r
        a_s[d] = -kk
        b_s[d] = kk * icl
        kd_s[d] = k * (1.0 + (icl - 1.0) * kap_ref[...])
        v_s[d] = v

    bm = _block_mask(N_HEADS * CHUNK, W_MIX, CHUNK, HEAD)
    tri = [_tri(d, False) for d in range(N_DIR)]
    tw_strict = [_tri_wide(d, True) for d in range(N_DIR)]
    tw_incl = [_tri_wide(d, False) for d in range(N_DIR)]

    def chunk_body(cc, carry):
        for d in range(N_DIR):
            c = cc if d == 0 else nchunk - 1 - cc
            rows = pl.ds(pl.multiple_of(c * CHUNK, CHUNK), CHUNK)
            lw = lw_s[d, rows, :]
            r = r_s[d, rows, :]
            a = a_s[d, rows, :]
            b = b_s[d, rows, :]
            kd = kd_s[d, rows, :]
            v = v_s[d, rows, :]
            cum = _dot_left_x2(tri[d], lw)
            tot = cum[CHUNK - 1:CHUNK, :] if d == 0 else cum[0:1, :]
            g_in = jnp.exp(cum)
            g_ex = jnp.exp(cum - lw)
            g_inv = jnp.exp(-cum)
            g_rem = jnp.exp(tot - cum)
            x = jnp.concatenate([a * g_ex, r * g_in], axis=0)
            bt = b * g_inv
            kt = kd * g_inv
            s = st_ref[0, d]
            p = _dot_nt(x, s)
            gb = _dot_nt(x, _expand(bt, bm))
            gk = _dot_nt(x, _expand(kt, bm))
            vbd = _expand(v, bm)
            a_ab = gb[0:CHUNK] * tw_strict[d]
            u = p[0:CHUNK] + _dot(gk[0:CHUNK] * tw_strict[d], vbd)
            u = u + _dot(a_ab, _expand(u, bm))
            apow = a_ab
            for _ in range(5):
                apow = _dot(apow, _expand(apow, bm))
                u = u + _dot(apow, _expand(u, bm))
            y = (p[CHUNK:] + _dot(gb[CHUNK:] * tw_incl[d], _expand(u, bm))
                 + _dot(gk[CHUNK:] * tw_incl[d], vbd))
            if d == 0:
                yf_ref[0, rows, :] = y
            else:
                yb_ref[0, rows, :] = y
            uv = jnp.concatenate([u, v], axis=0)
            bk = jnp.concatenate([b * g_rem, kd * g_rem], axis=0)
            st_ref[0, d] = s * jnp.exp(tot) + _dot_tn(uv, bk) * ones_blk
        return carry

    lax.fori_loop(0, nchunk, chunk_body, 0)


def _rw_scan(zs, s0, wl, lb, kkp, kap):
    b, t, c = zs.shape
    tb = min(SCAN_TB, t)
    nblk = t // tb
    vec = lambda n: pl.BlockSpec((1, n), lambda i, j: (0, 0))
    st_spec = pl.BlockSpec((1, N_DIR, W_MIX, W_MIX), lambda i, j: (i, 0, 0, 0))
    return pl.pallas_call(
        functools.partial(_rw_scan_kernel, nchunk=tb // CHUNK),
        grid=(b, nblk),
        in_specs=[
            pl.BlockSpec((1, tb, c), lambda i, j: (i, j, 0)),
            pl.BlockSpec((1, tb, c), lambda i, j: (i, nblk - 1 - j, 0)),
            st_spec,
            pl.BlockSpec((128, 1024), lambda i, j: (0, 0)),
            vec(1024), vec(W_MIX), vec(W_MIX),
        ],
        out_specs=[
            pl.BlockSpec((1, tb, W_MIX), lambda i, j: (i, j, 0)),
            pl.BlockSpec((1, tb, W_MIX), lambda i, j: (i, nblk - 1 - j, 0)),
            st_spec,
        ],
        out_shape=[
            jax.ShapeDtypeStruct((b, t, W_MIX), F32),
            jax.ShapeDtypeStruct((b, t, W_MIX), F32),
            jax.ShapeDtypeStruct((b, N_DIR, W_MIX, W_MIX), F32),
        ],
        scratch_shapes=[pltpu.VMEM((N_DIR, tb, W_MIX), F32) for _ in range(6)],
        compiler_params=_cparams(("arbitrary", "arbitrary")),
        name="rw_scan",
    )(zs, zs, s0, wl, lb, kkp, kap)


EPI_TM = 512


def _rw_epi_kernel(yf_ref, yb_ref, z_ref, lng_ref, lnb_ref, rk_ref, g2_ref, o_ref):
    ones_blk = _block_mask(W_MIX, W_MIX, HEAD, HEAD)
    y = yf_ref[0] + yb_ref[0]
    mu = _dot_x2(y, ones_blk) * (1.0 / HEAD)
    yc = y - mu
    var = _dot_x2(yc * yc, ones_blk) * (1.0 / HEAD)
    yn = yc * lax.rsqrt(var + RW_LN_EPS) * lng_ref[...] + lnb_ref[...]
    z = z_ref[0]
    r = z[:, 0:256]
    k = z[:, 256:512]
    v = z[:, 512:768]
    gl = z[:, 896:1024]
    bonus = _dot_x2(r * k * rk_ref[...], ones_blk) * v
    gate = _dot(jax.nn.sigmoid(gl), g2_ref[...])
    o_ref[0] = (yn + bonus) * gate


def _rw_epi(yf, yb, zs, lng, lnb, rk, g2p):
    b, t, _ = yf.shape
    tm = min(EPI_TM, t)
    tok = lambda w_: pl.BlockSpec((1, tm, w_), lambda i, j: (i, j, 0))
    vec = pl.BlockSpec((1, W_MIX), lambda i, j: (0, 0))
    return pl.pallas_call(
        _rw_epi_kernel,
        grid=(b, t // tm),
        in_specs=[tok(W_MIX), tok(W_MIX), tok(ZA_W), vec, vec, vec,
                  pl.BlockSpec((128, W_MIX), lambda i, j: (0, 0))],
        out_specs=tok(W_MIX),
        out_shape=jax.ShapeDtypeStruct((b, t, W_MIX), F32),
        compiler_params=_cparams(("arbitrary", "arbitrary")),
        name="rw_epi",
    )(yf, yb, zs, lng, lnb, rk, g2p)


def _rwkv_mixer(za, s0_bd, p, latent):
    zs = _rw_shift(za, p["rw_mu"], latent)
    yf, yb, st = _rw_scan(zs, s0_bd, p["rw_wl"], p["rw_lb"], p["rw_kk"], p["rw_ka"])
    ya = _rw_epi(yf, yb, zs, p["rw_ln_g"], p["rw_ln_b"], p["rw_rk"], p["rw_g2"])
    return ya, st


def _gla_scan_kernel(zf_ref, zb_ref, s0_ref, wg_ref, gb_ref, yf_ref, yb_ref, st_ref,
                     la_s, q_s, k_s, v_s, *, nchunk):
    i = pl.program_id(1)

    @pl.when(i == 0)
    def _():
        st_ref[...] = s0_ref[...]

    for d in range(N_DIR):
        z = (zf_ref, zb_ref)[d][0]
        logit = _dot(z[:, 768:896], wg_ref[d]) + gb_ref[d]
        la_s[d] = jax.nn.log_sigmoid(logit) * (1.0 / GLA_GATE_NORM)
        q_s[d] = z[:, 0:128] * (GLA_DK ** -0.5)
        k_s[d] = z[:, 128:256]
        v_s[d] = z[:, 256:512]

    bm_k = _block_mask(N_HEADS * CHUNK, GLA_HK, CHUNK, GLA_DK)
    bm_v = _block_mask(N_HEADS * CHUNK, W_MIX, CHUNK, HEAD)
    bm_s = _block_mask(W_MIX, GLA_HK, HEAD, GLA_DK)
    tri = [_tri(d, False) for d in range(N_DIR)]
    tw_incl = [_tri_wide(d, False) for d in range(N_DIR)]

    def chunk_body(cc, carry):
        for d in range(N_DIR):
            c = cc if d == 0 else nchunk - 1 - cc
            rows = pl.ds(pl.multiple_of(c * CHUNK, CHUNK), CHUNK)
            la = la_s[d, rows, :]
            q = q_s[d, rows, :]
            k = k_s[d, rows, :]
            v = v_s[d, rows, :]
            cum = _dot_left_x2(tri[d], la)
            last = cum[CHUNK - 1:CHUNK, :] if d == 0 else cum[0:1, :]
            qe = q * jnp.exp(cum)
            ke = k * jnp.exp(-cum)
            kl = k * jnp.exp(last - cum)
            s = st_ref[0, d]
            att = _dot_nt(qe, _expand(ke, bm_k)) * tw_incl[d]
            o = _dot(att, _expand(v, bm_v)) + _dot_nt(qe, s)
            if d == 0:
                yf_ref[0, rows, :] = o
            else:
                yb_ref[0, rows, :] = o
            st_ref[0, d] = s * jnp.exp(last) + _dot_tn(v, kl) * bm_s
        return carry

    lax.fori_loop(0, nchunk, chunk_body, 0)


def _gla_scan(zb, s0, wg, gb):
    b, t, c = zb.shape
    tb = min(SCAN_TB, t)
    nblk = t // tb
    st_spec = pl.BlockSpec((1, N_DIR, W_MIX, GLA_HK), lambda i, j: (i, 0, 0, 0))
    return pl.pallas_call(
        functools.partial(_gla_scan_kernel, nchunk=tb // CHUNK),
        grid=(b, nblk),
        in_specs=[
            pl.BlockSpec((1, tb, c), lambda i, j: (i, j, 0)),
            pl.BlockSpec((1, tb, c), lambda i, j: (i, nblk - 1 - j, 0)),
            st_spec,
            pl.BlockSpec((N_DIR, 128, GLA_HK), lambda i, j: (0, 0, 0)),
            pl.BlockSpec((N_DIR, 1, GLA_HK), lambda i, j: (0, 0, 0)),
        ],
        out_specs=[
            pl.BlockSpec((1, tb, W_MIX), lambda i, j: (i, j, 0)),
            pl.BlockSpec((1, tb, W_MIX), lambda i, j: (i, nblk - 1 - j, 0)),
            st_spec,
        ],
        out_shape=[
            jax.ShapeDtypeStruct((b, t, W_MIX), F32),
            jax.ShapeDtypeStruct((b, t, W_MIX), F32),
            jax.ShapeDtypeStruct((b, N_DIR, W_MIX, GLA_HK), F32),
        ],
        scratch_shapes=[pltpu.VMEM((N_DIR, tb, GLA_HK), F32) for _ in range(3)]
        + [pltpu.VMEM((N_DIR, tb, W_MIX), F32)],
        compiler_params=_cparams(("arbitrary", "arbitrary")),
        name="gla_scan",
    )(zb, zb, s0, wg, gb)


def _gla_epi_kernel(yf_ref, yb_ref, z_ref, g_ref, o_ref):
    ones_blk = _block_mask(W_MIX, W_MIX, HEAD, HEAD)
    y = yf_ref[0] + yb_ref[0]
    ms = _dot_x2(y * y, ones_blk) * (1.0 / HEAD)
    y = y * lax.rsqrt(ms + NORM_EPS) * g_ref[...]
    og = z_ref[0, :, 512:768]
    o_ref[0] = y * (og * jax.nn.sigmoid(og))


def _gla_epi(yf, yb, zb, g):
    b, t, _ = yf.shape
    tm = min(EPI_TM, t)
    tok = lambda w_: pl.BlockSpec((1, tm, w_), lambda i, j: (i, j, 0))
    return pl.pallas_call(
        _gla_epi_kernel,
        grid=(b, t // tm),
        in_specs=[tok(W_MIX), tok(W_MIX), tok(ZB_W), pl.BlockSpec((1, W_MIX), lambda i, j: (0, 0))],
        out_specs=tok(W_MIX),
        out_shape=jax.ShapeDtypeStruct((b, t, W_MIX), F32),
        compiler_params=_cparams(("arbitrary", "arbitrary")),
        name="gla_epi",
    )(yf, yb, zb, g)


def _gla_mixer(zb, s0_bd, p):
    yf, yb, st = _gla_scan(zb, s0_bd, p["gla_wg"], p["gla_gb"])
    return _gla_epi(yf, yb, zb, p["gla_ln_g"]), st


LRU_TT = 256
LRU_HALO = 8


def _lru_kernel(z_ref, h0_ref, cw_ref, cb_ref, wax_ref, bax_ref, lam_ref, y_ref, hf_ref, xpad, *, t):
    tt = min(LRU_TT, t)
    ntile = t // tt
    xpad[0:LRU_HALO, :] = jnp.zeros((LRU_HALO, W_MIX), F32)
    xpad[LRU_HALO + t:2 * LRU_HALO + t, :] = jnp.zeros((LRU_HALO, W_MIX), F32)

    def fill(j, carry):
        base = pl.multiple_of(j * tt, tt)
        xpad[pl.ds(base + LRU_HALO, tt), :] = z_ref[0, pl.ds(base, tt), 0:W_MIX]
        return carry

    lax.fori_loop(0, ntile, fill, 0)
    row = _iota((tt, W_MIX), 0)

    def tile_scan(j, h, d):
        base = pl.multiple_of(j * tt, tt)
        win = xpad[pl.ds(base, tt + 2 * LRU_HALO), :]
        xc = jnp.zeros((tt, W_MIX), F32) + cb_ref[d]
        for tap in range(LRU_CONV):
            off = LRU_HALO - (LRU_CONV - 1) + tap if d == 0 else LRU_HALO + (LRU_CONV - 1) - tap
            xc = xc + cw_ref[d, tap:tap + 1, :] * win[off:off + tt, :]
        g = _dot(xc, wax_ref[d]) + bax_ref[d]
        gr = jax.nn.sigmoid(g[:, 0:W_MIX])
        gi = jax.nn.sigmoid(g[:, W_MIX:2 * W_MIX])
        log_a = -LRU_C * gr * jax.nn.softplus(-lam_ref[d])
        a = jnp.exp(log_a)
        bv = jnp.sqrt(1.0 - jnp.exp(2.0 * log_a)) * gi * xc
        s = 1
        while s < tt:
            if d == 0:
                keep = row >= s
                a_sh = jnp.where(keep, pltpu.roll(a, s, 0), 1.0)
                b_sh = jnp.where(keep, pltpu.roll(bv, s, 0), 0.0)
            else:
                keep = row < tt - s
                a_sh = jnp.where(keep, pltpu.roll(a, tt - s, 0), 1.0)
                b_sh = jnp.where(keep, pltpu.roll(bv, tt - s, 0), 0.0)
            bv = a * b_sh + bv
            a = a * a_sh
            s *= 2
        return a * h + bv, base

    def fwd(j, h):
        ht, base = tile_scan(j, h, 0)
        y_ref[0, pl.ds(base, tt), :] = ht
        return ht[tt - 1:tt, :]

    h_end = lax.fori_loop(0, ntile, fwd, h0_ref[0, 0:1, :])
    hf_ref[0, 0:1, :] = h_end

    def bwd(jj, h):
        j = ntile - 1 - jj
        ht, base = tile_scan(j, h, 1)
        gb = z_ref[0, pl.ds(base, tt), W_MIX:2 * W_MIX]
        y_ref[0, pl.ds(base, tt), :] = (y_ref[0, pl.ds(base, tt), :] + ht) * jax.nn.gelu(gb)
        return ht[0:1, :]

    h_end = lax.fori_loop(0, ntile, bwd, h0_ref[0, 1:2, :])
    hf_ref[0, 1:2, :] = h_end


def _lru_mixer(zc, h0, p):
    b, t, c = zc.shape
    full = lambda *s: pl.BlockSpec(s, lambda i: (0,) * len(s))
    return pl.pallas_call(
        functools.partial(_lru_kernel, t=t),
        grid=(b,),
        in_specs=[
            pl.BlockSpec((1, t, c), lambda i: (i, 0, 0)),
            pl.BlockSpec((1, N_DIR, W_MIX), lambda i: (i, 0, 0)),
            full(N_DIR, LRU_CONV, W_MIX), full(N_DIR, 1, W_MIX),
            full(N_DIR, W_MIX, 2 * W_MIX), full(N_DIR, 1, 2 * W_MIX), full(N_DIR, 1, W_MIX),
        ],
        out_specs=[pl.BlockSpec((1, t, W_MIX), lambda i: (i, 0, 0)),
                   pl.BlockSpec((1, N_DIR, W_MIX), lambda i: (i, 0, 0))],
        out_shape=[jax.ShapeDtypeStruct((b, t, W_MIX), F32), jax.ShapeDtypeStruct((b, N_DIR, W_MIX), F32)],
        scratch_shapes=[pltpu.VMEM((t + 2 * LRU_HALO, W_MIX), F32)],
        compiler_params=_cparams(("arbitrary",)),
        name="lru",
    )(zc, h0, p["lru_cw"], p["lru_cb"], p["lru_wax"], p["lru_bax"], p["lru_lam"])


CV_TT = 256
CV_WIN = 16


def _conv_kernel(z_ref, dw_ref, dwb_ref, lng_ref, lnb_ref, pw_ref, pwb_ref, y_ref, upad, *, t, latent):
    tt = min(CV_TT, t)
    ntile = t // tt
    pad = CV_PAD * GRID_W if latent else CV_WIN
    upad[0:pad, :] = jnp.zeros((pad, W_MIX), F32)
    upad[pad + t:2 * pad + t, :] = jnp.zeros((pad, W_MIX), F32)

    def fill(j, carry):
        base = pl.multiple_of(j * tt, tt)
        z = z_ref[0, pl.ds(base, tt), :]
        upad[pl.ds(base + pad, tt), :] = z[:, 0:W_MIX] * jax.nn.sigmoid(z[:, W_MIX:2 * W_MIX])
        return carry

    lax.fori_loop(0, ntile, fill, 0)

    def tile(j, carry):
        base = pl.multiple_of(j * tt, tt)
        if latent:
            half = W_MIX // 2
            col = _iota((tt, half), 0) % GRID_W
            win = upad[pl.ds(pl.multiple_of(base + pad - CV_WIN, CV_WIN), tt + 2 * CV_WIN), 0:half]
            accw = jnp.zeros((tt, half), F32)
            acch = jnp.zeros((tt, half), F32)
            for tap in range(CV_KERNEL):
                dlt = tap - CV_PAD
                x = win[CV_WIN + dlt:CV_WIN + dlt + tt, :]
                if dlt < 0:
                    x = jnp.where(col >= -dlt, x, 0.0)
                elif dlt > 0:
                    x = jnp.where(col < GRID_W - dlt, x, 0.0)
                accw = accw + dw_ref[tap:tap + 1, 0:half] * x
                rows = pl.ds(pl.multiple_of(base + pad + dlt * GRID_W, GRID_W), tt)
                acch = acch + dw_ref[tap:tap + 1, half:W_MIX] * upad[rows, half:W_MIX]
            u = jnp.concatenate([accw, acch], axis=1)
        else:
            win = upad[pl.ds(base + pad - CV_WIN, tt + 2 * CV_WIN), :]
            u = jnp.zeros((tt, W_MIX), F32)
            for tap in range(CV_KERNEL):
                dlt = tap - CV_PAD
                u = u + dw_ref[tap:tap + 1, :] * win[CV_WIN + dlt:CV_WIN + dlt + tt, :]
        u = u + dwb_ref[...]
        mu = jnp.mean(u, axis=-1, keepdims=True)
        uc = u - mu
        var = jnp.mean(uc * uc, axis=-1, keepdims=True)
        un = uc * lax.rsqrt(var + 1e-5) * lng_ref[...] + lnb_ref[...]
        un = un * jax.nn.sigmoid(un)
        y_ref[0, pl.ds(base, tt), :] = _dot(un, pw_ref[...]) + pwb_ref[...]
        return carry

    lax.fori_loop(0, ntile, tile, 0)


def _conv_mixer(zd, p, latent):
    b, t, c = zd.shape
    pad = CV_PAD * GRID_W if latent else CV_WIN
    full = lambda *s: pl.BlockSpec(s, lambda i: (0,) * len(s))
    vec = full(1, W_MIX)
    return pl.pallas_call(
        functools.partial(_conv_kernel, t=t, latent=latent),
        grid=(b,),
        in_specs=[pl.BlockSpec((1, t, c), lambda i: (i, 0, 0)),
                  full(CV_KERNEL, W_MIX), vec, vec, vec, full(W_MIX, W_MIX), vec],
        out_specs=pl.BlockSpec((1, t, W_MIX), lambda i: (i, 0, 0)),
        out_shape=jax.ShapeDtypeStruct((b, t, W_MIX), F32),
        scratch_shapes=[pltpu.VMEM((t + 2 * pad, W_MIX), F32)],
        compiler_params=_cparams(("arbitrary",)),
        name="conv_grid" if latent else "conv_ctx",
    )(zd, p["cv_dw_w"], p["cv_dw_b"], p["cv_ln_g"], p["cv_ln_b"], p["cv_pw_w"], p["cv_pw_b"])


MOE_TM = 512
E_PAD = 128


def _route(sel, scores):
    grp = []
    for g in range(N_GROUPS):
        s = sel[GROUP_SIZE * g:GROUP_SIZE * (g + 1)]
        best_pair = None
        for i in range(GROUP_SIZE):
            for j in range(i + 1, GROUP_SIZE):
                pair = s[i] + s[j]
                best_pair = pair if best_pair is None else jnp.maximum(best_pair, pair)
        grp.append(best_pair)
    best = jnp.zeros_like(grp[0], dtype=jnp.int32)
    top = grp[0]
    for g in range(1, N_GROUPS):
        better = grp[g] > top
        best = jnp.where(better, g, best)
        top = jnp.where(better, grp[g], top)
    neg = jnp.full_like(sel[0], -jnp.inf)
    msel = [jnp.where(best == e // GROUP_SIZE, sel[e], neg) for e in range(N_EXPERTS)]
    picks = []
    for _ in range(2):
        idx = jnp.zeros_like(best)
        top = msel[0]
        for e in range(1, N_EXPERTS):
            better = msel[e] > top
            idx = jnp.where(better, e, idx)
            top = jnp.where(better, msel[e], top)
        picks.append(idx)
        msel = [jnp.where(idx == e, neg, msel[e]) for e in range(N_EXPERTS)]
    chosen = [jnp.where((picks[0] == e) | (picks[1] == e), scores[e], 0.0) for e in range(N_EXPERTS)]
    total = chosen[0]
    for e in range(1, N_EXPERTS):
        total = total + chosen[e]
    return [ch / total for ch in chosen]


def _moe_kernel(x_ref, ya_ref, yb_ref, yc_ref, yd_ref, wo_ref, g1_ref, sh2_ref, sc2_ref, g2_ref, n2_ref,
                wr_ref, br_ref, w13_ref, w2_ref, nf_ref, o_ref, x1_s, h2_s, gt_s, gate_s, acc_s, *, final_norm):
    e = pl.program_id(2)

    @pl.when(e == 0)
    def _():
        y = (_dot(ya_ref[0], wo_ref[0:256, :]) + _dot(yb_ref[0], wo_ref[256:512, :])
             + _dot(yc_ref[0], wo_ref[512:768, :]) + _dot(yd_ref[0], wo_ref[768:1024, :]))
        x1 = x_ref[0] + g1_ref[0] * y
        x1_s[...] = x1
        h2 = x1 * lax.rsqrt(jnp.mean(x1 * x1, axis=-1, keepdims=True) + NORM_EPS) * n2_ref[...]
        h2 = h2 * (1.0 + sc2_ref[0]) + sh2_ref[0]
        h2_s[...] = h2.astype(BF16)
        logits = _dot3_nt(wr_ref[...], h2)
        scores = jax.nn.sigmoid(logits)
        selm = scores + br_ref[...]
        gates = _route([selm[i:i + 1, :] for i in range(N_EXPERTS)],
                       [scores[i:i + 1, :] for i in range(N_EXPERTS)])
        gt_s[...] = jnp.zeros(gt_s.shape, F32)
        for i in range(N_EXPERTS):
            gt_s[i:i + 1, :] = gates[i]
        gate_s[...] = gt_s[...].T
        acc_s[...] = jnp.zeros(acc_s.shape, F32)

    pick = (_iota((E_PAD, 128), 0) == e).astype(F32)
    g = _dot_x2(gate_s[...], pick)
    hh = jnp.dot(h2_s[...], w13_ref[0], preferred_element_type=F32)
    he = hh[:, 0:D_EXPERT]
    he = he * jax.nn.sigmoid(he) * hh[:, D_EXPERT:2 * D_EXPERT]
    he = he * jnp.concatenate([g, g], axis=1)
    acc_s[...] += jnp.dot(he.astype(BF16), w2_ref[0], preferred_element_type=F32)

    @pl.when(e == N_EXPERTS - 1)
    def _():
        x2 = x1_s[...] + g2_ref[0] * acc_s[...]
        if final_norm:
            x2 = x2 * lax.rsqrt(jnp.mean(x2 * x2, axis=-1, keepdims=True) + NORM_EPS) * nf_ref[...]
        o_ref[0] = x2


def _out_moe(x, ys, m, p, norm_f, final_norm):
    b, t, _ = x.shape
    tm = min(MOE_TM, t)
    tok = lambda w_: pl.BlockSpec((1, tm, w_), lambda i, j, e: (i, j, 0))
    mod = lambda k: pl.BlockSpec((1, 1, D_MODEL), lambda i, j, e, k=k: (i, 0, k))
    full = lambda *s: pl.BlockSpec(s, lambda i, j, e: (0,) * len(s))
    return pl.pallas_call(
        functools.partial(_moe_kernel, final_norm=final_norm),
        grid=(b, t // tm, N_EXPERTS),
        in_specs=[tok(D_MODEL), tok(W_MIX), tok(W_MIX), tok(W_MIX), tok(W_MIX),
                  full(D_MODEL, D_MODEL), mod(2), mod(3), mod(4), mod(5), full(1, D_MODEL),
                  full(N_EXPERTS, D_MODEL), full(N_EXPERTS, 1),
                  pl.BlockSpec((1, D_MODEL, 2 * D_EXPERT), lambda i, j, e: (e, 0, 0)),
                  pl.BlockSpec((1, D_EXPERT, D_MODEL), lambda i, j, e: (e, 0, 0)),
                  full(1, D_MODEL)],
        out_specs=tok(D_MODEL),
        out_shape=jax.ShapeDtypeStruct((b, t, D_MODEL), F32),
        scratch_shapes=[pltpu.VMEM((tm, D_MODEL), F32), pltpu.VMEM((tm, D_MODEL), BF16),
                        pltpu.VMEM((E_PAD, tm), F32), pltpu.VMEM((tm, E_PAD), F32),
                        pltpu.VMEM((tm, D_MODEL), F32)],
        compiler_params=_cparams(("arbitrary", "arbitrary", "arbitrary")),
        name="out_moe",
    )(x, *ys, p["w_out"], m, m, m, m, p["norm2_g"], p["w_router_t"], p["b_router"], p["e_w13"], p["e_w2"], norm_f)


def _gla_pack(z):
    hk = GLA_HK
    lead = z.shape[:-1]
    gl = z[..., 2 * hk + W_MIX:2 * hk + W_MIX + 32]
    return jnp.concatenate([z[..., 0:2 * hk + W_MIX], z[..., 2 * hk + W_MIX + 32:], gl,
                            jnp.zeros(lead + (96,), z.dtype)], -1)


def _gla_state_in(s):
    b = s.shape[0]
    eye = jnp.eye(N_HEADS, dtype=s.dtype)
    return jnp.einsum("bdhkv,hg->bdhvgk", s, eye).reshape(b, N_DIR, W_MIX, GLA_HK)


def _gla_state_out(st):
    b = st.shape[0]
    eye = jnp.eye(N_HEADS, dtype=st.dtype)
    return jnp.einsum("bdhvgk,hg->bdhkv", st.reshape(b, N_DIR, N_HEADS, HEAD, N_HEADS, GLA_DK), eye)


def _rw_state_in(s):
    b = s.shape[0]
    eye = jnp.eye(N_HEADS, dtype=s.dtype)
    return jnp.einsum("bdhvk,hg->bdhvgk", s, eye).reshape(b, N_DIR, W_MIX, W_MIX)


def _rw_state_out(st):
    b = st.shape[0]
    eye = jnp.eye(N_HEADS, dtype=st.dtype)
    return jnp.einsum("bdhvgk,hg->bdhvk", st.reshape(b, N_DIR, N_HEADS, HEAD, N_HEADS, HEAD), eye)


def _prep_params(d):
    L = DEPTH
    z = lambda *s: jnp.zeros(s, F32)
    out = {}
    out["rw_mu"] = jnp.concatenate([d["rw_mu"], z(L, ZA_W - d["rw_mu"].shape[-1])], -1).reshape(L, 1, ZA_W)
    w2, a2 = d["rw_w2"], d["rw_a2"]
    zz = z(L, 32, 256)
    rows = [
        jnp.concatenate([w2[:, 0], zz, zz, zz], -1),
        jnp.concatenate([zz, w2[:, 1], zz, zz], -1),
        jnp.concatenate([zz, zz, a2[:, 0], zz], -1),
        jnp.concatenate([zz, zz, zz, a2[:, 1]], -1),
    ]
    out["rw_wl"] = jnp.concatenate(rows, 1).astype(BF16)
    out["rw_lb"] = jnp.concatenate([d["rw_w0"][:, 0], d["rw_w0"][:, 1], d["rw_a0"][:, 0], d["rw_a0"][:, 1]],
                                   -1).reshape(L, 1, 1024)
    for n in ("rw_kk", "rw_ka", "rw_ln_g", "rw_ln_b"):
        out[n] = d[n].reshape(L, 1, W_MIX)
    out["rw_rk"] = d["rw_rk"].reshape(L, 1, W_MIX)
    out["rw_g2"] = jnp.concatenate([d["rw_g2"], z(L, 64, W_MIX)], 1).astype(BF16)

    gk2 = d["gla_gk2"]
    z16, z96 = z(L, 16, GLA_HK), z(L, 96, GLA_HK)
    out["gla_wg"] = jnp.stack([jnp.concatenate([gk2[:, 0], z16, z96], 1),
                               jnp.concatenate([z16, gk2[:, 1], z96], 1)], 1).astype(BF16)
    out["gla_gb"] = d["gla_gkb"].reshape(L, N_DIR, 1, GLA_HK)
    out["gla_ln_g"] = jnp.tile(d["gla_ln_g"], (1, N_HEADS)).reshape(L, 1, W_MIX)

    eye = jnp.eye(4, dtype=F32)
    bd = lambda w: jnp.einsum("ldgij,gh->ldgihj", w, eye).reshape(L, N_DIR, W_MIX, W_MIX)
    out["lru_cw"] = d["lru_conv_w"]
    out["lru_cb"] = d["lru_conv_b"].reshape(L, N_DIR, 1, W_MIX)
    out["lru_wax"] = jnp.concatenate([bd(d["lru_wa"]), bd(d["lru_wx"])], -1).astype(BF16)
    out["lru_bax"] = jnp.concatenate([d["lru_ba"], d["lru_bx"]], -1).reshape(L, N_DIR, 1, 2 * W_MIX)
    out["lru_lam"] = d["lru_lam"].reshape(L, N_DIR, 1, W_MIX)

    out["cv_dw_w"] = d["cv_dw_w"]
    for n in ("cv_dw_b", "cv_ln_g", "cv_ln_b", "cv_pw_b"):
        out[n] = d[n].reshape(L, 1, W_MIX)
    out["cv_pw_w"] = d["cv_pw_w"].astype(BF16)

    if "w_in" in d:
        w_in = d["w_in"]
        o1, o2, o3 = 960, 1760, 2272
        out["w_in"] = jnp.concatenate([w_in[..., 0:o1], z(L, D_MODEL, ZA_W - o1), _gla_pack(w_in[..., o1:o2]),
                                       w_in[..., o2:o3], w_in[..., o3:]], -1).astype(BF16)
        out["w_out"] = d["w_out"].astype(BF16)
        out["norm1_g"] = d["norm1_g"].reshape(L, 1, D_MODEL)
        out["norm2_g"] = d["norm2_g"].reshape(L, 1, D_MODEL)
        out["e_w13"] = jnp.concatenate([d["e_w1"], d["e_w3"]], -1).astype(BF16)
        out["e_w2"] = d["e_w2"].astype(BF16)
    return out


def kernel(x_prompt, x_sample, state_rwkv, state_gla, state_lru, c, c_ctx, norm1_g, norm2_g, norm_f_g, w_ada, b_ada, w_in, w_out, rw_mu, rw_w0, rw_w2, rw_a0, rw_a2, rw_g2, rw_kk, rw_ka, rw_rk, rw_ln_g, rw_ln_b, gla_gk2, gla_gkb, gla_ln_g, lru_conv_w, lru_conv_b, lru_wa, lru_ba, lru_wx, lru_bx, lru_lam, cv_dw_w, cv_dw_b, cv_ln_g, cv_ln_b, cv_pw_w, cv_pw_b, w_router, b_router, e_w1, e_w3, e_w2):
    d = dict(norm1_g=norm1_g, norm2_g=norm2_g, w_in=w_in, w_out=w_out,
             rw_mu=rw_mu, rw_w0=rw_w0, rw_w2=rw_w2, rw_a0=rw_a0, rw_a2=rw_a2, rw_g2=rw_g2, rw_kk=rw_kk,
             rw_ka=rw_ka, rw_rk=rw_rk, rw_ln_g=rw_ln_g, rw_ln_b=rw_ln_b,
             gla_gk2=gla_gk2, gla_gkb=gla_gkb, gla_ln_g=gla_ln_g,
             lru_conv_w=lru_conv_w, lru_conv_b=lru_conv_b, lru_wa=lru_wa, lru_ba=lru_ba, lru_wx=lru_wx,
             lru_bx=lru_bx, lru_lam=lru_lam,
             cv_dw_w=cv_dw_w, cv_dw_b=cv_dw_b, cv_ln_g=cv_ln_g, cv_ln_b=cv_ln_b, cv_pw_w=cv_pw_w, cv_pw_b=cv_pw_b,
             e_w1=e_w1, e_w3=e_w3, e_w2=e_w2)
    P = _prep_params(d)
    shared = {"w_router_t": w_router.T, "b_router": b_router.reshape(N_EXPERTS, 1)}
    norm_f = norm_f_g.reshape(1, D_MODEL)

    bp, tp, _ = x_prompt.shape
    bs, ts, _ = x_sample.shape
    n_mod = 1 + bs
    mod_rows = -(-n_mod // 8) * 8
    cvec = jnp.concatenate([c_ctx[None], c, jnp.zeros((mod_rows - n_mod, D_MODEL), F32)], 0)
    mod = _ada_mod(cvec, w_ada, b_ada)

    tp_flat = min(bp * tp, 1024)
    bp_flat = bp * tp // tp_flat
    xp = x_prompt.reshape(bp_flat, tp_flat, D_MODEL)
    xs = x_sample
    zero_rw = jnp.zeros((bp, N_DIR, W_MIX, W_MIX), F32)
    zero_gla = jnp.zeros((bp, N_DIR, W_MIX, GLA_HK), F32)
    zero_lru = jnp.zeros((bp, N_DIR, W_MIX), F32)

    def layer(x, m, seq_shape, latent, s_rw, s_gla, s_lru, p, last):
        flat_shape = x.shape[:2]
        zs = _in_proj(x, m, p["norm1_g"], p["w_in"])
        za, zb, zc, zd = (z.reshape(seq_shape + (z.shape[-1],)) for z in zs)
        ya, f_rw = _rwkv_mixer(za, s_rw, p, latent)
        yb, f_gla = _gla_mixer(zb, s_gla, p)
        yc, f_lru = _lru_mixer(zc, s_lru, p)
        yd = _conv_mixer(zd, p, latent)
        ys = [y.reshape(flat_shape + (W_MIX,)) for y in (ya, yb, yc, yd)]
        return _out_moe(x, ys, m, p, norm_f, last), f_rw, f_gla, f_lru

    fin_rw, fin_gla, fin_lru = [], [], []
    for l in range(DEPTH):
        p = {n: a[l] for n, a in P.items()}
        p.update(shared)
        last = l == DEPTH - 1
        m_p = jnp.broadcast_to(mod[l, 0:1], (bp_flat, 6 * D_MODEL)).reshape(bp_flat, 1, 6 * D_MODEL)
        xp, f_rw, f_gla, f_lru = layer(xp, m_p, (bp, tp), False, zero_rw, zero_gla, zero_lru, p, last)
        fin_rw.append(_rw_state_out(f_rw))
        fin_gla.append(_gla_state_out(f_gla))
        fin_lru.append(f_lru)
        m_s = mod[l, 1:1 + bs].reshape(bs, 1, 6 * D_MODEL)
        xs, _, _, _ = layer(xs, m_s, (bs, ts), True, _rw_state_in(state_rwkv[:, l]),
                            _gla_state_in(state_gla[:, l]), state_lru[:, l], p, last)
    return (xp.reshape(bp, tp, D_MODEL), xs, jnp.stack(fin_rw, axis=1), jnp.stack(fin_gla, axis=1),
            jnp.stack(fin_lru, axis=1))
```

```python
import functools

import jax
import jax.numpy as jnp
from jax import lax
from jax.experimental import pallas as pl
from jax.experimental.pallas import tpu as pltpu

F32 = jnp.float32
BF16 = jnp.bfloat16

D_MODEL = 1024
DEPTH = 4
GRID_W = 64
N_DIR = 2
W_MIX = 256
N_HEADS = 4
HEAD = W_MIX // N_HEADS
GLA_DK = 32
GLA_HK = N_HEADS * GLA_DK
RW_LN_EPS = 64e-5
NORM_EPS = 1e-6
GLA_GATE_NORM = 16.0
LRU_C = 8.0
LRU_CONV = 4
CV_KERNEL = 31
CV_PAD = (CV_KERNEL - 1) // 2
N_EXPERTS = 16
GROUP_SIZE = 4
N_GROUPS = 4
D_EXPERT = 256
CHUNK = 64

ZA_W = 1024
ZB_W = 896
ZC_W = 512
ZD_W = 512
P_PAD = ZA_W + ZB_W + ZC_W + ZD_W

VMEM_LIMIT = 48 * 1024 * 1024


def _cparams(sem, vmem=VMEM_LIMIT):
    return pltpu.CompilerParams(dimension_semantics=sem, vmem_limit_bytes=vmem)


def _dot(a, b):
    return jnp.dot(a.astype(BF16), b.astype(BF16), preferred_element_type=F32)


def _dot_nt(a, b):
    return lax.dot_general(a.astype(BF16), b.astype(BF16), (((1,), (1,)), ((), ())),
                           preferred_element_type=F32)


def _dot_tn(a, b):
    return lax.dot_general(a.astype(BF16), b.astype(BF16), (((0,), (0,)), ((), ())),
                           preferred_element_type=F32)


def _split(a):
    hi = a.astype(BF16)
    lo = (a - hi.astype(F32)).astype(BF16)
    return hi, lo


def _dot_x2(a, b_exact):
    hi, lo = _split(a)
    bb = b_exact.astype(BF16)
    return (jnp.dot(hi, bb, preferred_element_type=F32) + jnp.dot(lo, bb, preferred_element_type=F32))


def _dot_left_x2(a_exact, b):
    hi, lo = _split(b)
    aa = a_exact.astype(BF16)
    return (jnp.dot(aa, hi, preferred_element_type=F32) + jnp.dot(aa, lo, preferred_element_type=F32))


def _dot3(a, b):
    ah, al = _split(a)
    bh, bl = _split(b)
    return (jnp.dot(ah, bh, preferred_element_type=F32) + jnp.dot(ah, bl, preferred_element_type=F32)
            + jnp.dot(al, bh, preferred_element_type=F32))


def _dot3_nt(a, b):
    ah, al = _split(a)
    bh, bl = _split(b)
    dn = (((1,), (1,)), ((), ()))
    return (lax.dot_general(ah, bh, dn, preferred_element_type=F32)
            + lax.dot_general(ah, bl, dn, preferred_element_type=F32)
            + lax.dot_general(al, bh, dn, preferred_element_type=F32))


def _iota(shape, axis):
    return lax.broadcasted_iota(jnp.int32, shape, axis)


def _block_mask(rows, cols, rblk, cblk):
    r = _iota((rows, cols), 0) // rblk
    c = _iota((rows, cols), 1) // cblk
    return (r == c).astype(F32)


def _expand(x, bm):
    xb = x.astype(BF16)
    return jnp.concatenate([xb, xb, xb, xb], axis=0) * bm


def _tri(d, strict):
    t = _iota((CHUNK, CHUNK), 0)
    s = _iota((CHUNK, CHUNK), 1)
    if d == 0:
        m = (s < t) if strict else (s <= t)
    else:
        m = (s > t) if strict else (s >= t)
    return m.astype(F32)


def _tri_wide(d, strict):
    t = _iota((CHUNK, N_HEADS * CHUNK), 0)
    s = _iota((CHUNK, N_HEADS * CHUNK), 1) % CHUNK
    if d == 0:
        m = (s < t) if strict else (s <= t)
    else:
        m = (s > t) if strict else (s >= t)
    return m.astype(F32)


ADA_TN = 1536


def _ada_kernel(c_ref, w_ref, b_ref, o_ref):
    c = c_ref[...]
    s = c * jax.nn.sigmoid(c)
    o_ref[0] = _dot3(s, w_ref[0]) + b_ref[0]


def _ada_mod(cvec, w_ada, b_ada):
    rows = cvec.shape[0]
    n_out = w_ada.shape[-1]
    return pl.pallas_call(
        _ada_kernel,
        grid=(DEPTH, n_out // ADA_TN),
        in_specs=[
            pl.BlockSpec((rows, D_MODEL), lambda l, j: (0, 0)),
            pl.BlockSpec((1, D_MODEL, ADA_TN), lambda l, j: (l, 0, j)),
            pl.BlockSpec((1, 1, ADA_TN), lambda l, j: (l, 0, j)),
        ],
        out_specs=pl.BlockSpec((1, rows, ADA_TN), lambda l, j: (l, 0, j)),
        out_shape=jax.ShapeDtypeStruct((DEPTH, rows, n_out), F32),
        compiler_params=_cparams(("arbitrary", "arbitrary")),
        name="ada_mod",
    )(cvec, w_ada, b_ada.reshape(DEPTH, 1, n_out))


IN_TM = 256


def _in_kernel(x_ref, sh_ref, sc_ref, g_ref, w_ref, za_ref, zb_ref, zc_ref, zd_ref):
    x = x_ref[0]
    h = x * lax.rsqrt(jnp.mean(x * x, axis=-1, keepdims=True) + NORM_EPS) * g_ref[...]
    h = (h * (1.0 + sc_ref[0]) + sh_ref[0]).astype(BF16)
    o0, o1, o2 = ZA_W, ZA_W + ZB_W, ZA_W + ZB_W + ZC_W
    za_ref[0] = jnp.dot(h, w_ref[:, 0:o0], preferred_element_type=F32)
    zb_ref[0] = jnp.dot(h, w_ref[:, o0:o1], preferred_element_type=F32)
    zc_ref[0] = jnp.dot(h, w_ref[:, o1:o2], preferred_element_type=F32)
    zd_ref[0] = jnp.dot(h, w_ref[:, o2:P_PAD], preferred_element_type=F32)


def _in_proj(x, m, g, w):
    b, t, _ = x.shape
    tm = min(IN_TM, t)
    tok = lambda w_: pl.BlockSpec((1, tm, w_), lambda i, j: (i, j, 0))
    mod = lambda k: pl.BlockSpec((1, 1, D_MODEL), lambda i, j, k=k: (i, 0, k))
    return pl.pallas_call(
        _in_kernel,
        grid=(b, t // tm),
        in_specs=[tok(D_MODEL), mod(0), mod(1),
                  pl.BlockSpec((1, D_MODEL), lambda i, j: (0, 0)),
                  pl.BlockSpec((D_MODEL, P_PAD), lambda i, j: (0, 0))],
        out_specs=[tok(ZA_W), tok(ZB_W), tok(ZC_W), tok(ZD_W)],
        out_shape=[jax.ShapeDtypeStruct((b, t, w_), F32) for w_ in (ZA_W, ZB_W, ZC_W, ZD_W)],
        compiler_params=_cparams(("arbitrary", "arbitrary")),
        name="in_proj",
    )(x, m, m, g, w)


SHIFT_TB = 512


def _shift_grid_kernel(z_ref, up_ref, dn_ref, mu_ref, o_ref, *, nblk):
    i = pl.program_id(1)
    z = z_ref[0]
    tb, c = z.shape
    col = _iota((tb, c), 0) % GRID_W
    lane = _iota((tb, c), 1) % 4
    left = jnp.where(col == 0, 0.0, pltpu.roll(z, 1, 0))
    right = jnp.where(col == GRID_W - 1, 0.0, pltpu.roll(z, tb - 1, 0))
    up_halo = jnp.where(i > 0, up_ref[0, 0], 0.0)
    dn_halo = jnp.where(i < nblk - 1, dn_ref[0, 0], 0.0)
    up = jnp.concatenate([up_halo, z[:tb - GRID_W]], axis=0)
    down = jnp.concatenate([z[GRID_W:], dn_halo], axis=0)
    sh = jnp.where(lane == 0, left, jnp.where(lane == 1, right, jnp.where(lane == 2, up, down)))
    o_ref[0] = z + (sh - z) * mu_ref[...]


def _shift_ctx_kernel(z_ref, mu_ref, o_ref):
    z = z_ref[0]
    t, c = z.shape
    row = _iota((t, c), 0)
    lane = _iota((t, c), 1) % 2
    prev = jnp.where(row == 0, 0.0, pltpu.roll(z, 1, 0))
    nxt = jnp.where(row == t - 1, 0.0, pltpu.roll(z, t - 1, 0))
    sh = jnp.where(lane == 0, prev, nxt)
    o_ref[0] = z + (sh - z) * mu_ref[...]


def _rw_shift(za, mu, latent):
    b, t, c = za.shape
    mu_spec = pl.BlockSpec((1, c), lambda *_: (0, 0))
    if not latent:
        return pl.pallas_call(
            _shift_ctx_kernel,
            grid=(b,),
            in_specs=[pl.BlockSpec((1, t, c), lambda i: (i, 0, 0)), mu_spec],
            out_specs=pl.BlockSpec((1, t, c), lambda i: (i, 0, 0)),
            out_shape=jax.ShapeDtypeStruct((b, t, c), F32),
            compiler_params=_cparams(("arbitrary",)),
            name="rw_shift_ctx",
        )(za, mu)
    tb = min(SHIFT_TB, t)
    nblk = t // tb
    rpb = tb // GRID_W
    nrow = t // GRID_W
    z4 = za.reshape(b, nrow, GRID_W, c)
    return pl.pallas_call(
        functools.partial(_shift_grid_kernel, nblk=nblk),
        grid=(b, nblk),
        in_specs=[
            pl.BlockSpec((1, tb, c), lambda i, j: (i, j, 0)),
            pl.BlockSpec((1, 1, GRID_W, c), lambda i, j: (i, jnp.maximum(j * rpb - 1, 0), 0, 0)),
            pl.BlockSpec((1, 1, GRID_W, c), lambda i, j: (i, jnp.minimum(j * rpb + rpb, nrow - 1), 0, 0)),
            mu_spec,
        ],
        out_specs=pl.BlockSpec((1, tb, c), lambda i, j: (i, j, 0)),
        out_shape=jax.ShapeDtypeStruct((b, t, c), F32),
        compiler_params=_cparams(("arbitrary", "arbitrary")),
        name="rw_shift_grid",
    )(za, z4, z4, mu)


SCAN_TB = 512
PREP_GROUP = 4


def _rw_scan_kernel(zf_ref, zb_ref, s0_ref, wl_ref, lb_ref, kkp_ref, kap_ref,
                    yf_ref, yb_ref, st_ref,
                    lw_s, r_s, a_s, b_s, kd_s, v_s, a2_s, r2_s, bk_s, uv_s, gt_s, *, nchunk):
    i = pl.program_id(1)

    @pl.when(i == 0)
    def _():
        st_ref[...] = s0_ref[...]

    ones_blk = _block_mask(W_MIX, W_MIX, HEAD, HEAD)
    lane128 = _iota((1, 128), 1)

    for d in range(N_DIR):
        z = (zf_ref, zb_ref)[d][0]
        r = z[:, 0:256]
        k = z[:, 256:512]
        v = z[:, 512:768]
        la = z[:, 768:896]
        la_t = jnp.where(lane128 < 64, jnp.tanh(la), la)
        wraw = _dot(la_t, wl_ref[:, 256 * d:256 * d + 256]) + lb_ref[:, 256 * d:256 * d + 256]
        araw = _dot(la_t, wl_ref[:, 512 + 256 * d:768 + 256 * d]) + lb_ref[:, 512 + 256 * d:768 + 256 * d]
        logw = -jnp.exp(-jax.nn.softplus(-wraw) - 0.5)
        icl = jax.nn.sigmoid(araw)
        kkv = k * kkp_ref[...]
        ss = _dot_x2(kkv * kkv, ones_blk)
        kk = kkv * lax.rsqrt(ss + 1e-12)
        lw_s[d] = logw
        r_s[d] = r
        a_s[d] = -kk
        b_s[d] = kk * icl
        kd_s[d] = k * (1.0 + (icl - 1.0) * kap_ref[...])
        v_s[d] = v

    bm = _block_mask(N_HEADS * CHUNK, W_MIX, CHUNK, HEAD).astype(BF16)
    tri = [_tri(d, False) for d in range(N_DIR)]
    tw_strict = [_tri_wide(d, True) for d in range(N_DIR)]
    tw_incl = [_tri_wide(d, False) for d in range(N_DIR)]
    eye_w = (_iota((CHUNK, N_HEADS * CHUNK), 0) == _iota((CHUNK, N_HEADS * CHUNK), 1) % CHUNK).astype(F32)
    zeros_c = jnp.zeros((CHUNK, W_MIX), BF16)

    def rows_of(c, n):
        return pl.ds(pl.multiple_of(c * n, n), n)

    group = min(PREP_GROUP, nchunk)
    y_refs = (yf_ref, yb_ref)

    def prep_body(gi, carry):
        ch = [(d, gi * group + j) for j in range(group) for d in range(N_DIR)]
        each = lambda fn, *ls: [fn(*xs) for xs in zip(*ls)]
        ds_ = [d for d, _ in ch]
        load = lambda ref: [ref[d, rows_of(c, CHUNK), :] for d, c in ch]
        lw, r, a, b, kd, v = load(lw_s), load(r_s), load(a_s), load(b_s), load(kd_s), load(v_s)
        cum = each(lambda d, x: _dot_left_x2(tri[d], x), ds_, lw)
        tot = each(lambda d, x: x[CHUNK - 1:CHUNK, :] if d == 0 else x[0:1, :], ds_, cum)
        g_inv = each(lambda x: jnp.exp(-x), cum)
        g_rem = each(lambda t_, x: jnp.exp(t_ - x), tot, cum)
        at = each(lambda a_, x, l: a_ * jnp.exp(x - l), a, cum, lw)
        rt = each(lambda r_, x: r_ * jnp.exp(x), r, cum)
        x = each(lambda p, q: jnp.concatenate([p, q], axis=0), at, rt)
        gb = each(lambda x_, b_, g: _dot_nt(x_, _expand(b_ * g, bm)), x, b, g_inv)
        gk = each(lambda x_, k_, g: _dot_nt(x_, _expand(k_ * g, bm)), x, kd, g_inv)
        vbd = each(lambda v_: _expand(v_, bm), v)
        a_ab = each(lambda d, g: g[0:CHUNK] * tw_strict[d], ds_, gb)
        av = each(lambda d, g, vb: _dot(g[0:CHUNK] * tw_strict[d], vb), ds_, gk, vbd)
        tw = each(lambda m: eye_w + m, a_ab)
        apow = a_ab
        for _ in range(5):
            apow = each(lambda m: _dot(m, _expand(m, bm)), apow)
            tw = each(lambda t_, m: t_ + _dot(m, _expand(t_, bm)), tw, apow)
        a2 = each(lambda t_, m: _dot(t_, _expand(m, bm)), tw, at)
        u0 = each(lambda t_, m: _dot(t_, _expand(m, bm)), tw, av)
        r_b = each(lambda d, g: g[CHUNK:] * tw_incl[d], ds_, gb)
        r2 = each(lambda rt_, rb, m: rt_ + _dot(rb, _expand(m, bm)), rt, r_b, a2)
        y0 = each(lambda d, rb, u, g, vb: _dot(rb, _expand(u, bm)) + _dot(g[CHUNK:] * tw_incl[d], vb),
                  ds_, r_b, u0, gk, vbd)
        for n, (d, c) in enumerate(ch):
            rows = rows_of(c, CHUNK)
            y_refs[d][0, rows, :] = y0[n]
            r2_s[d, rows, :] = r2[n].astype(BF16)
            a2_s[d, rows_of(c, 2 * CHUNK), :] = jnp.concatenate([a2[n].astype(BF16), zeros_c], axis=0)
            bk_s[d, rows_of(c, 2 * CHUNK), :] = jnp.concatenate([b[n] * g_rem[n], kd[n] * g_rem[n]],
                                                                axis=0).astype(BF16)
            uv_s[d, rows_of(c, W_MIX), :] = jnp.concatenate([u0[n], v[n]], axis=0).T
            gt_s[d, rows_of(c, 8), :] = jnp.broadcast_to(jnp.exp(tot[n]), (8, W_MIX))
        return carry

    lax.fori_loop(0, nchunk // group, prep_body, 0)

    def scan_body(cc, carry):
        cs = [cc, nchunk - 1 - cc]
        s = [st_ref[0, d] for d in range(N_DIR)]
        sb = [x.astype(BF16) for x in s]
        ys = [_dot_nt(r2_s[d, rows_of(cs[d], CHUNK), :], sb[d]) for d in range(N_DIR)]
        uvt = [_dot_nt(sb[d], a2_s[d, rows_of(cs[d], 2 * CHUNK), :]) + uv_s[d, rows_of(cs[d], W_MIX), :]
               for d in range(N_DIR)]
        upd = [_dot(uvt[d], bk_s[d, rows_of(cs[d], 2 * CHUNK), :]) for d in range(N_DIR)]
        for d in range(N_DIR):
            y_refs[d][0, rows_of(cs[d], CHUNK), :] += ys[d]
            st_ref[0, d] = s[d] * gt_s[d, rows_of(cs[d], 8), :][0:1, :] + upd[d] * ones_blk
        return carry

    lax.fori_loop(0, nchunk, scan_body, 0)


def _rw_scan(zs, s0, wl, lb, kkp, kap):
    b, t, c = zs.shape
    tb = min(SCAN_TB, t)
    nblk = t // tb
    nchunk = tb // CHUNK
    vec = lambda n: pl.BlockSpec((1, n), lambda i, j: (0, 0))
    st_spec = pl.BlockSpec((1, N_DIR, W_MIX, W_MIX), lambda i, j: (i, 0, 0, 0))
    return pl.pallas_call(
        functools.partial(_rw_scan_kernel, nchunk=nchunk),
        grid=(b, nblk),
        in_specs=[
            pl.BlockSpec((1, tb, c), lambda i, j: (i, j, 0)),
            pl.BlockSpec((1, tb, c), lambda i, j: (i, nblk - 1 - j, 0)),
            st_spec,
            pl.BlockSpec((128, 1024), lambda i, j: (0, 0)),
            vec(1024), vec(W_MIX), vec(W_MIX),
        ],
        out_specs=[
            pl.BlockSpec((1, tb, W_MIX), lambda i, j: (i, j, 0)),
            pl.BlockSpec((1, tb, W_MIX), lambda i, j: (i, nblk - 1 - j, 0)),
            st_spec,
        ],
        out_shape=[
            jax.ShapeDtypeStruct((b, t, W_MIX), F32),
            jax.ShapeDtypeStruct((b, t, W_MIX), F32),
            jax.ShapeDtypeStruct((b, N_DIR, W_MIX, W_MIX), F32),
        ],
        scratch_shapes=[pltpu.VMEM((N_DIR, tb, W_MIX), F32) for _ in range(6)] + [
            pltpu.VMEM((N_DIR, 2 * tb, W_MIX), BF16),
            pltpu.VMEM((N_DIR, tb, W_MIX), BF16),
            pltpu.VMEM((N_DIR, 2 * tb, W_MIX), BF16),
            pltpu.VMEM((N_DIR, nchunk * W_MIX, 2 * CHUNK), F32),
            pltpu.VMEM((N_DIR, nchunk * 8, W_MIX), F32),
        ],
        compiler_params=_cparams(("arbitrary", "arbitrary")),
        name="rw_scan",
    )(zs, zs, s0, wl, lb, kkp, kap)


EPI_TM = 512


def _rw_epi_kernel(yf_ref, yb_ref, z_ref, lng_ref, lnb_ref, rk_ref, g2_ref, o_ref):
    ones_blk = _block_mask(W_MIX, W_MIX, HEAD, HEAD)
    y = yf_ref[0] + yb_ref[0]
    mu = _dot_x2(y, ones_blk) * (1.0 / HEAD)
    yc = y - mu
    var = _dot_x2(yc * yc, ones_blk) * (1.0 / HEAD)
    yn = yc * lax.rsqrt(var + RW_LN_EPS) * lng_ref[...] + lnb_ref[...]
    z = z_ref[0]
    r = z[:, 0:256]
    k = z[:, 256:512]
    v = z[:, 512:768]
    gl = z[:, 896:1024]
    bonus = _dot_x2(r * k * rk_ref[...], ones_blk) * v
    gate = _dot(jax.nn.sigmoid(gl), g2_ref[...])
    o_ref[0] = (yn + bonus) * gate


def _rw_epi(yf, yb, zs, lng, lnb, rk, g2p):
    b, t, _ = yf.shape
    tm = min(EPI_TM, t)
    tok = lambda w_: pl.BlockSpec((1, tm, w_), lambda i, j: (i, j, 0))
    vec = pl.BlockSpec((1, W_MIX), lambda i, j: (0, 0))
    return pl.pallas_call(
        _rw_epi_kernel,
        grid=(b, t // tm),
        in_specs=[tok(W_MIX), tok(W_MIX), tok(ZA_W), vec, vec, vec,
                  pl.BlockSpec((128, W_MIX), lambda i, j: (0, 0))],
        out_specs=tok(W_MIX),
        out_shape=jax.ShapeDtypeStruct((b, t, W_MIX), F32),
        compiler_params=_cparams(("arbitrary", "arbitrary")),
        name="rw_epi",
    )(yf, yb, zs, lng, lnb, rk, g2p)


def _rwkv_mixer(za, s0_bd, p, latent):
    zs = _rw_shift(za, p["rw_mu"], latent)
    yf, yb, st = _rw_scan(zs, s0_bd, p["rw_wl"], p["rw_lb"], p["rw_kk"], p["rw_ka"])
    ya = _rw_epi(yf, yb, zs, p["rw_ln_g"], p["rw_ln_b"], p["rw_rk"], p["rw_g2"])
    return ya, st


def _gla_scan_kernel(zf_ref, zb_ref, s0_ref, wg_ref, gb_ref, yf_ref, yb_ref, st_ref,
                     la_s, qe_s, ds_s, dec_s, *, nchunk):
    i = pl.program_id(1)

    @pl.when(i == 0)
    def _():
        st_ref[...] = s0_ref[...]

    z_refs = (zf_ref, zb_ref)
    y_refs = (yf_ref, yb_ref)
    for d in range(N_DIR):
        logit = _dot(z_refs[d][0, :, 768:896], wg_ref[d]) + gb_ref[d]
        la_s[d] = jax.nn.log_sigmoid(logit) * (1.0 / GLA_GATE_NORM)

    bm_k = _block_mask(N_HEADS * CHUNK, GLA_HK, CHUNK, GLA_DK).astype(BF16)
    bm_v = _block_mask(N_HEADS * CHUNK, W_MIX, CHUNK, HEAD).astype(BF16)
    bm_s = _block_mask(W_MIX, GLA_HK, HEAD, GLA_DK)
    tri = [_tri(d, False) for d in range(N_DIR)]
    tw_incl = [_tri_wide(d, False) for d in range(N_DIR)]

    def rows_of(c, n):
        return pl.ds(pl.multiple_of(c * n, n), n)

    group = min(PREP_GROUP, nchunk)

    def prep_body(gi, carry):
        ch = [(d, gi * group + j) for j in range(group) for d in range(N_DIR)]
        each = lambda fn, *ls: [fn(*xs) for xs in zip(*ls)]
        ds_ = [d for d, _ in ch]
        la = [la_s[d, rows_of(c, CHUNK), :] for d, c in ch]
        q = [z_refs[d][0, rows_of(c, CHUNK), 0:128] * (GLA_DK ** -0.5) for d, c in ch]
        k = [z_refs[d][0, rows_of(c, CHUNK), 128:256] for d, c in ch]
        v = [z_refs[d][0, rows_of(c, CHUNK), 256:512] for d, c in ch]
        cum = each(lambda d, x: _dot_left_x2(tri[d], x), ds_, la)
        last = each(lambda d, x: x[CHUNK - 1:CHUNK, :] if d == 0 else x[0:1, :], ds_, cum)
        qe = each(lambda q_, x: q_ * jnp.exp(x), q, cum)
        ke = each(lambda k_, x: k_ * jnp.exp(-x), k, cum)
        kl = each(lambda k_, l, x: k_ * jnp.exp(l - x), k, last, cum)
        att = each(lambda d, q_, k_: _dot_nt(q_, _expand(k_, bm_k)) * tw_incl[d], ds_, qe, ke)
        o = each(lambda a_, v_: _dot(a_, _expand(v_, bm_v)), att, v)
        dst = each(lambda v_, k_: _dot_tn(v_, k_) * bm_s, v, kl)
        for n, (d, c) in enumerate(ch):
            y_refs[d][0, rows_of(c, CHUNK), :] = o[n]
            qe_s[d, rows_of(c, CHUNK), :] = qe[n].astype(BF16)
            ds_s[d, rows_of(c, W_MIX), :] = dst[n]
            dec_s[d, rows_of(c, 8), :] = jnp.broadcast_to(jnp.exp(last[n]), (8, GLA_HK))
        return carry

    lax.fori_loop(0, nchunk // group, prep_body, 0)

    def scan_body(cc, carry):
        cs = [cc, nchunk - 1 - cc]
        for d in range(N_DIR):
            s = st_ref[0, d]
            y_refs[d][0, rows_of(cs[d], CHUNK), :] += _dot_nt(qe_s[d, rows_of(cs[d], CHUNK), :], s)
            st_ref[0, d] = s * dec_s[d, rows_of(cs[d], 8), :][0:1, :] + ds_s[d, rows_of(cs[d], W_MIX), :]
        return carry

    lax.fori_loop(0, nchunk, scan_body, 0)


def _gla_scan(zb, s0, wg, gb):
    b, t, c = zb.shape
    tb = min(SCAN_TB, t)
    nblk = t // tb
    st_spec = pl.BlockSpec((1, N_DIR, W_MIX, GLA_HK), lambda i, j: (i, 0, 0, 0))
    nchunk = tb // CHUNK
    return pl.pallas_call(
        functools.partial(_gla_scan_kernel, nchunk=nchunk),
        grid=(b, nblk),
        in_specs=[
            pl.BlockSpec((1, tb, c), lambda i, j: (i, j, 0)),
            pl.BlockSpec((1, tb, c), lambda i, j: (i, nblk - 1 - j, 0)),
            st_spec,
            pl.BlockSpec((N_DIR, 128, GLA_HK), lambda i, j: (0, 0, 0)),
            pl.BlockSpec((N_DIR, 1, GLA_HK), lambda i, j: (0, 0, 0)),
        ],
        out_specs=[
            pl.BlockSpec((1, tb, W_MIX), lambda i, j: (i, j, 0)),
            pl.BlockSpec((1, tb, W_MIX), lambda i, j: (i, nblk - 1 - j, 0)),
            st_spec,
        ],
        out_shape=[
            jax.ShapeDtypeStruct((b, t, W_MIX), F32),
            jax.ShapeDtypeStruct((b, t, W_MIX), F32),
            jax.ShapeDtypeStruct((b, N_DIR, W_MIX, GLA_HK), F32),
        ],
        scratch_shapes=[
            pltpu.VMEM((N_DIR, tb, GLA_HK), F32),
            pltpu.VMEM((N_DIR, tb, GLA_HK), BF16),
            pltpu.VMEM((N_DIR, nchunk * W_MIX, GLA_HK), F32),
            pltpu.VMEM((N_DIR, nchunk * 8, GLA_HK), F32),
        ],
        compiler_params=_cparams(("arbitrary", "arbitrary")),
        name="gla_scan",
    )(zb, zb, s0, wg, gb)


def _gla_epi_kernel(yf_ref, yb_ref, z_ref, g_ref, o_ref):
    ones_blk = _block_mask(W_MIX, W_MIX, HEAD, HEAD)
    y = yf_ref[0] + yb_ref[0]
    ms = _dot_x2(y * y, ones_blk) * (1.0 / HEAD)
    y = y * lax.rsqrt(ms + NORM_EPS) * g_ref[...]
    og = z_ref[0, :, 512:768]
    o_ref[0] = y * (og * jax.nn.sigmoid(og))


def _gla_epi(yf, yb, zb, g):
    b, t, _ = yf.shape
    tm = min(EPI_TM, t)
    tok = lambda w_: pl.BlockSpec((1, tm, w_), lambda i, j: (i, j, 0))
    return pl.pallas_call(
        _gla_epi_kernel,
        grid=(b, t // tm),
        in_specs=[tok(W_MIX), tok(W_MIX), tok(ZB_W), pl.BlockSpec((1, W_MIX), lambda i, j: (0, 0))],
        out_specs=tok(W_MIX),
        out_shape=jax.ShapeDtypeStruct((b, t, W_MIX), F32),
        compiler_params=_cparams(("arbitrary", "arbitrary")),
        name="gla_epi",
    )(yf, yb, zb, g)


def _gla_mixer(zb, s0_bd, p):
    yf, yb, st = _gla_scan(zb, s0_bd, p["gla_wg"], p["gla_gb"])
    return _gla_epi(yf, yb, zb, p["gla_ln_g"]), st


LRU_TT = 256
LRU_HALO = 8


def _lru_kernel(z_ref, h0_ref, cw_ref, cb_ref, wax_ref, bax_ref, lam_ref, y_ref, hf_ref, xpad, *, t):
    tt = min(LRU_TT, t)
    ntile = t // tt
    xpad[0:LRU_HALO, :] = jnp.zeros((LRU_HALO, W_MIX), F32)
    xpad[LRU_HALO + t:2 * LRU_HALO + t, :] = jnp.zeros((LRU_HALO, W_MIX), F32)

    def fill(j, carry):
        base = pl.multiple_of(j * tt, tt)
        xpad[pl.ds(base + LRU_HALO, tt), :] = z_ref[0, pl.ds(base, tt), 0:W_MIX]
        return carry

    lax.fori_loop(0, ntile, fill, 0)
    row = _iota((tt, W_MIX), 0)

    def tile_scan(j, h, d):
        base = pl.multiple_of(j * tt, tt)
        win = xpad[pl.ds(base, tt + 2 * LRU_HALO), :]
        xc = jnp.zeros((tt, W_MIX), F32) + cb_ref[d]
        for tap in range(LRU_CONV):
            off = LRU_HALO - (LRU_CONV - 1) + tap if d == 0 else LRU_HALO + (LRU_CONV - 1) - tap
            xc = xc + cw_ref[d, tap:tap + 1, :] * win[off:off + tt, :]
        g = _dot(xc, wax_ref[d]) + bax_ref[d]
        gr = jax.nn.sigmoid(g[:, 0:W_MIX])
        gi = jax.nn.sigmoid(g[:, W_MIX:2 * W_MIX])
        log_a = -LRU_C * gr * jax.nn.softplus(-lam_ref[d])
        a = jnp.exp(log_a)
        bv = jnp.sqrt(1.0 - jnp.exp(2.0 * log_a)) * gi * xc
        s = 1
        while s < tt:
            if d == 0:
                keep = row >= s
                a_sh = jnp.where(keep, pltpu.roll(a, s, 0), 1.0)
                b_sh = jnp.where(keep, pltpu.roll(bv, s, 0), 0.0)
            else:
                keep = row < tt - s
                a_sh = jnp.where(keep, pltpu.roll(a, tt - s, 0), 1.0)
                b_sh = jnp.where(keep, pltpu.roll(bv, tt - s, 0), 0.0)
            bv = a * b_sh + bv
            a = a * a_sh
            s *= 2
        return a * h + bv, base

    def fwd(j, h):
        ht, base = tile_scan(j, h, 0)
        y_ref[0, pl.ds(base, tt), :] = ht
        return ht[tt - 1:tt, :]

    h_end = lax.fori_loop(0, ntile, fwd, h0_ref[0, 0:1, :])
    hf_ref[0, 0:1, :] = h_end

    def bwd(jj, h):
        j = ntile - 1 - jj
        ht, base = tile_scan(j, h, 1)
        gb = z_ref[0, pl.ds(base, tt), W_MIX:2 * W_MIX]
        y_ref[0, pl.ds(base, tt), :] = (y_ref[0, pl.ds(base, tt), :] + ht) * jax.nn.gelu(gb)
        return ht[0:1, :]

    h_end = lax.fori_loop(0, ntile, bwd, h0_ref[0, 1:2, :])
    hf_ref[0, 1:2, :] = h_end


def _lru_mixer(zc, h0, p):
    b, t, c = zc.shape
    full = lambda *s: pl.BlockSpec(s, lambda i: (0,) * len(s))
    return pl.pallas_call(
        functools.partial(_lru_kernel, t=t),
        grid=(b,),
        in_specs=[
            pl.BlockSpec((1, t, c), lambda i: (i, 0, 0)),
            pl.BlockSpec((1, N_DIR, W_MIX), lambda i: (i, 0, 0)),
            full(N_DIR, LRU_CONV, W_MIX), full(N_DIR, 1, W_MIX),
            full(N_DIR, W_MIX, 2 * W_MIX), full(N_DIR, 1, 2 * W_MIX), full(N_DIR, 1, W_MIX),
        ],
        out_specs=[pl.BlockSpec((1, t, W_MIX), lambda i: (i, 0, 0)),
                   pl.BlockSpec((1, N_DIR, W_MIX), lambda i: (i, 0, 0))],
        out_shape=[jax.ShapeDtypeStruct((b, t, W_MIX), F32), jax.ShapeDtypeStruct((b, N_DIR, W_MIX), F32)],
        scratch_shapes=[pltpu.VMEM((t + 2 * LRU_HALO, W_MIX), F32)],
        compiler_params=_cparams(("arbitrary",)),
        name="lru",
    )(zc, h0, p["lru_cw"], p["lru_cb"], p["lru_wax"], p["lru_bax"], p["lru_lam"])


CV_TT = 256
CV_WIN = 16


def _conv_kernel(z_ref, dw_ref, dwb_ref, lng_ref, lnb_ref, pw_ref, pwb_ref, y_ref, upad, *, t, latent):
    tt = min(CV_TT, t)
    ntile = t // tt
    pad = CV_PAD * GRID_W if latent else CV_WIN
    upad[0:pad, :] = jnp.zeros((pad, W_MIX), F32)
    upad[pad + t:2 * pad + t, :] = jnp.zeros((pad, W_MIX), F32)

    def fill(j, carry):
        base = pl.multiple_of(j * tt, tt)
        z = z_ref[0, pl.ds(base, tt), :]
        upad[pl.ds(base + pad, tt), :] = z[:, 0:W_MIX] * jax.nn.sigmoid(z[:, W_MIX:2 * W_MIX])
        return carry

    lax.fori_loop(0, ntile, fill, 0)

    def tile(j, carry):
        base = pl.multiple_of(j * tt, tt)
        if latent:
            half = W_MIX // 2
            col = _iota((tt, half), 0) % GRID_W
            win = upad[pl.ds(pl.multiple_of(base + pad - CV_WIN, CV_WIN), tt + 2 * CV_WIN), 0:half]
            accw = jnp.zeros((tt, half), F32)
            acch = jnp.zeros((tt, half), F32)
            for tap in range(CV_KERNEL):
                dlt = tap - CV_PAD
                x = win[CV_WIN + dlt:CV_WIN + dlt + tt, :]
                if dlt < 0:
                    x = jnp.where(col >= -dlt, x, 0.0)
                elif dlt > 0:
                    x = jnp.where(col < GRID_W - dlt, x, 0.0)
                accw = accw + dw_ref[tap:tap + 1, 0:half] * x
                rows = pl.ds(pl.multiple_of(base + pad + dlt * GRID_W, GRID_W), tt)
                acch = acch + dw_ref[tap:tap + 1, half:W_MIX] * upad[rows, half:W_MIX]
            u = jnp.concatenate([accw, acch], axis=1)
        else:
            win = upad[pl.ds(base + pad - CV_WIN, tt + 2 * CV_WIN), :]
            u = jnp.zeros((tt, W_MIX), F32)
            for tap in range(CV_KERNEL):
                dlt = tap - CV_PAD
                u = u + dw_ref[tap:tap + 1, :] * win[CV_WIN + dlt:CV_WIN + dlt + tt, :]
        u = u + dwb_ref[...]
        mu = jnp.mean(u, axis=-1, keepdims=True)
        uc = u - mu
        var = jnp.mean(uc * uc, axis=-1, keepdims=True)
        un = uc * lax.rsqrt(var + 1e-5) * lng_ref[...] + lnb_ref[...]
        un = un * jax.nn.sigmoid(un)
        y_ref[0, pl.ds(base, tt), :] = _dot(un, pw_ref[...]) + pwb_ref[...]
        return carry

    lax.fori_loop(0, ntile, tile, 0)


def _conv_mixer(zd, p, latent):
    b, t, c = zd.shape
    pad = CV_PAD * GRID_W if latent else CV_WIN
    full = lambda *s: pl.BlockSpec(s, lambda i: (0,) * len(s))
    vec = full(1, W_MIX)
    return pl.pallas_call(
        functools.partial(_conv_kernel, t=t, latent=latent),
        grid=(b,),
        in_specs=[pl.BlockSpec((1, t, c), lambda i: (i, 0, 0)),
                  full(CV_KERNEL, W_MIX), vec, vec, vec, full(W_MIX, W_MIX), vec],
        out_specs=pl.BlockSpec((1, t, W_MIX), lambda i: (i, 0, 0)),
        out_shape=jax.ShapeDtypeStruct((b, t, W_MIX), F32),
        scratch_shapes=[pltpu.VMEM((t + 2 * pad, W_MIX), F32)],
        compiler_params=_cparams(("arbitrary",)),
        name="conv_grid" if latent else "conv_ctx",
    )(zd, p["cv_dw_w"], p["cv_dw_b"], p["cv_ln_g"], p["cv_ln_b"], p["cv_pw_w"], p["cv_pw_b"])


MOE_TM = 512
E_PAD = 128


def _route(sel, scores):
    grp = []
    for g in range(N_GROUPS):
        s = sel[GROUP_SIZE * g:GROUP_SIZE * (g + 1)]
        best_pair = None
        for i in range(GROUP_SIZE):
            for j in range(i + 1, GROUP_SIZE):
                pair = s[i] + s[j]
                best_pair = pair if best_pair is None else jnp.maximum(best_pair, pair)
        grp.append(best_pair)
    best = jnp.zeros_like(grp[0], dtype=jnp.int32)
    top = grp[0]
    for g in range(1, N_GROUPS):
        better = grp[g] > top
        best = jnp.where(better, g, best)
        top = jnp.where(better, grp[g], top)
    neg = jnp.full_like(sel[0], -jnp.inf)
    msel = [jnp.where(best == e // GROUP_SIZE, sel[e], neg) for e in range(N_EXPERTS)]
    picks = []
    for _ in range(2):
        idx = jnp.zeros_like(best)
        top = msel[0]
        for e in range(1, N_EXPERTS):
            better = msel[e] > top
            idx = jnp.where(better, e, idx)
            top = jnp.where(better, msel[e], top)
        picks.append(idx)
        msel = [jnp.where(idx == e, neg, msel[e]) for e in range(N_EXPERTS)]
    chosen = [jnp.where((picks[0] == e) | (picks[1] == e), scores[e], 0.0) for e in range(N_EXPERTS)]
    total = chosen[0]
    for e in range(1, N_EXPERTS):
        total = total + chosen[e]
    return [ch / total for ch in chosen]


def _moe_kernel(x_ref, ya_ref, yb_ref, yc_ref, yd_ref, wo_ref, g1_ref, sh2_ref, sc2_ref, g2_ref, n2_ref,
                wr_ref, br_ref, w13_ref, w2_ref, nf_ref, o_ref, x1_s, h2_s, gt_s, gate_s, acc_s, *, final_norm):
    e = pl.program_id(2)

    @pl.when(e == 0)
    def _():
        y = (_dot(ya_ref[0], wo_ref[0:256, :]) + _dot(yb_ref[0], wo_ref[256:512, :])
             + _dot(yc_ref[0], wo_ref[512:768, :]) + _dot(yd_ref[0], wo_ref[768:1024, :]))
        x1 = x_ref[0] + g1_ref[0] * y
        x1_s[...] = x1
        h2 = x1 * lax.rsqrt(jnp.mean(x1 * x1, axis=-1, keepdims=True) + NORM_EPS) * n2_ref[...]
        h2 = h2 * (1.0 + sc2_ref[0]) + sh2_ref[0]
        h2_s[...] = h2.astype(BF16)
        logits = _dot3_nt(wr_ref[...], h2)
        scores = jax.nn.sigmoid(logits)
        selm = scores + br_ref[...]
        gates = _route([selm[i:i + 1, :] for i in range(N_EXPERTS)],
                       [scores[i:i + 1, :] for i in range(N_EXPERTS)])
        gt_s[...] = jnp.zeros(gt_s.shape, F32)
        for i in range(N_EXPERTS):
            gt_s[i:i + 1, :] = gates[i]
        gate_s[...] = gt_s[...].T
        acc_s[...] = jnp.zeros(acc_s.shape, F32)

    pick = (_iota((E_PAD, 128), 0) == e).astype(F32)
    g = _dot_x2(gate_s[...], pick)
    hh = jnp.dot(h2_s[...], w13_ref[0], preferred_element_type=F32)
    he = hh[:, 0:D_EXPERT]
    he = he * jax.nn.sigmoid(he) * hh[:, D_EXPERT:2 * D_EXPERT]
    he = he * jnp.concatenate([g, g], axis=1)
    acc_s[...] += jnp.dot(he.astype(BF16), w2_ref[0], preferred_element_type=F32)

    @pl.when(e == N_EXPERTS - 1)
    def _():
        x2 = x1_s[...] + g2_ref[0] * acc_s[...]
        if final_norm:
            x2 = x2 * lax.rsqrt(jnp.mean(x2 * x2, axis=-1, keepdims=True) + NORM_EPS) * nf_ref[...]
        o_ref[0] = x2


def _out_moe(x, ys, m, p, norm_f, final_norm):
    b, t, _ = x.shape
    tm = min(MOE_TM, t)
    tok = lambda w_: pl.BlockSpec((1, tm, w_), lambda i, j, e: (i, j, 0))
    mod = lambda k: pl.BlockSpec((1, 1, D_MODEL), lambda i, j, e, k=k: (i, 0, k))
    full = lambda *s: pl.BlockSpec(s, lambda i, j, e: (0,) * len(s))
    return pl.pallas_call(
        functools.partial(_moe_kernel, final_norm=final_norm),
        grid=(b, t // tm, N_EXPERTS),
        in_specs=[tok(D_MODEL), tok(W_MIX), tok(W_MIX), tok(W_MIX), tok(W_MIX),
                  full(D_MODEL, D_MODEL), mod(2), mod(3), mod(4), mod(5), full(1, D_MODEL),
                  full(N_EXPERTS, D_MODEL), full(N_EXPERTS, 1),
                  pl.BlockSpec((1, D_MODEL, 2 * D_EXPERT), lambda i, j, e: (e, 0, 0)),
                  pl.BlockSpec((1, D_EXPERT, D_MODEL), lambda i, j, e: (e, 0, 0)),
                  full(1, D_MODEL)],
        out_specs=tok(D_MODEL),
        out_shape=jax.ShapeDtypeStruct((b, t, D_MODEL), F32),
        scratch_shapes=[pltpu.VMEM((tm, D_MODEL), F32), pltpu.VMEM((tm, D_MODEL), BF16),
                        pltpu.VMEM((E_PAD, tm), F32), pltpu.VMEM((tm, E_PAD), F32),
                        pltpu.VMEM((tm, D_MODEL), F32)],
        compiler_params=_cparams(("arbitrary", "arbitrary", "arbitrary")),
        name="out_moe",
    )(x, *ys, p["w_out"], m, m, m, m, p["norm2_g"], p["w_router_t"], p["b_router"], p["e_w13"], p["e_w2"], norm_f)


def _gla_pack(z):
    hk = GLA_HK
    lead = z.shape[:-1]
    gl = z[..., 2 * hk + W_MIX:2 * hk + W_MIX + 32]
    return jnp.concatenate([z[..., 0:2 * hk + W_MIX], z[..., 2 * hk + W_MIX + 32:], gl,
                            jnp.zeros(lead + (96,), z.dtype)], -1)


def _gla_state_in(s):
    b = s.shape[0]
    eye = jnp.eye(N_HEADS, dtype=s.dtype)
    return jnp.einsum("bdhkv,hg->bdhvgk", s, eye).reshape(b, N_DIR, W_MIX, GLA_HK)


def _gla_state_out(st):
    b = st.shape[0]
    eye = jnp.eye(N_HEADS, dtype=st.dtype)
    return jnp.einsum("bdhvgk,hg->bdhkv", st.reshape(b, N_DIR, N_HEADS, HEAD, N_HEADS, GLA_DK), eye)


def _rw_state_in(s):
    b = s.shape[0]
    eye = jnp.eye(N_HEADS, dtype=s.dtype)
    return jnp.einsum("bdhvk,hg->bdhvgk", s, eye).reshape(b, N_DIR, W_MIX, W_MIX)


def _rw_state_out(st):
    b = st.shape[0]
    eye = jnp.eye(N_HEADS, dtype=st.dtype)
    return jnp.einsum("bdhvgk,hg->bdhvk", st.reshape(b, N_DIR, N_HEADS, HEAD, N_HEADS, HEAD), eye)


def _prep_params(d):
    L = DEPTH
    z = lambda *s: jnp.zeros(s, F32)
    out = {}
    out["rw_mu"] = jnp.concatenate([d["rw_mu"], z(L, ZA_W - d["rw_mu"].shape[-1])], -1).reshape(L, 1, ZA_W)
    w2, a2 = d["rw_w2"], d["rw_a2"]
    zz = z(L, 32, 256)
    rows = [
        jnp.concatenate([w2[:, 0], zz, zz, zz], -1),
        jnp.concatenate([zz, w2[:, 1], zz, zz], -1),
        jnp.concatenate([zz, zz, a2[:, 0], zz], -1),
        jnp.concatenate([zz, zz, zz, a2[:, 1]], -1),
    ]
    out["rw_wl"] = jnp.concatenate(rows, 1).astype(BF16)
    out["rw_lb"] = jnp.concatenate([d["rw_w0"][:, 0], d["rw_w0"][:, 1], d["rw_a0"][:, 0], d["rw_a0"][:, 1]],
                                   -1).reshape(L, 1, 1024)
    for n in ("rw_kk", "rw_ka", "rw_ln_g", "rw_ln_b"):
        out[n] = d[n].reshape(L, 1, W_MIX)
    out["rw_rk"] = d["rw_rk"].reshape(L, 1, W_MIX)
    out["rw_g2"] = jnp.concatenate([d["rw_g2"], z(L, 64, W_MIX)], 1).astype(BF16)

    gk2 = d["gla_gk2"]
    z16, z96 = z(L, 16, GLA_HK), z(L, 96, GLA_HK)
    out["gla_wg"] = jnp.stack([jnp.concatenate([gk2[:, 0], z16, z96], 1),
                               jnp.concatenate([z16, gk2[:, 1], z96], 1)], 1).astype(BF16)
    out["gla_gb"] = d["gla_gkb"].reshape(L, N_DIR, 1, GLA_HK)
    out["gla_ln_g"] = jnp.tile(d["gla_ln_g"], (1, N_HEADS)).reshape(L, 1, W_MIX)

    eye = jnp.eye(4, dtype=F32)
    bd = lambda w: jnp.einsum("ldgij,gh->ldgihj", w, eye).reshape(L, N_DIR, W_MIX, W_MIX)
    out["lru_cw"] = d["lru_conv_w"]
    out["lru_cb"] = d["lru_conv_b"].reshape(L, N_DIR, 1, W_MIX)
    out["lru_wax"] = jnp.concatenate([bd(d["lru_wa"]), bd(d["lru_wx"])], -1).astype(BF16)
    out["lru_bax"] = jnp.concatenate([d["lru_ba"], d["lru_bx"]], -1).reshape(L, N_DIR, 1, 2 * W_MIX)
    out["lru_lam"] = d["lru_lam"].reshape(L, N_DIR, 1, W_MIX)

    out["cv_dw_w"] = d["cv_dw_w"]
    for n in ("cv_dw_b", "cv_ln_g", "cv_ln_b", "cv_pw_b"):
        out[n] = d[n].reshape(L, 1, W_MIX)
    out["cv_pw_w"] = d["cv_pw_w"].astype(BF16)

    if "w_in" in d:
        w_in = d["w_in"]
        o1, o2, o3 = 960, 1760, 2272
        out["w_in"] = jnp.concatenate([w_in[..., 0:o1], z(L, D_MODEL, ZA_W - o1), _gla_pack(w_in[..., o1:o2]),
                                       w_in[..., o2:o3], w_in[..., o3:]], -1).astype(BF16)
        out["w_out"] = d["w_out"].astype(BF16)
        out["norm1_g"] = d["norm1_g"].reshape(L, 1, D_MODEL)
        out["norm2_g"] = d["norm2_g"].reshape(L, 1, D_MODEL)
        out["e_w13"] = jnp.concatenate([d["e_w1"], d["e_w3"]], -1).astype(BF16)
        out["e_w2"] = d["e_w2"].astype(BF16)
    return out


def kernel(x_prompt, x_sample, state_rwkv, state_gla, state_lru, c, c_ctx, norm1_g, norm2_g, norm_f_g, w_ada, b_ada, w_in, w_out, rw_mu, rw_w0, rw_w2, rw_a0, rw_a2, rw_g2, rw_kk, rw_ka, rw_rk, rw_ln_g, rw_ln_b, gla_gk2, gla_gkb, gla_ln_g, lru_conv_w, lru_conv_b, lru_wa, lru_ba, lru_wx, lru_bx, lru_lam, cv_dw_w, cv_dw_b, cv_ln_g, cv_ln_b, cv_pw_w, cv_pw_b, w_router, b_router, e_w1, e_w3, e_w2):
    d = dict(norm1_g=norm1_g, norm2_g=norm2_g, w_in=w_in, w_out=w_out,
             rw_mu=rw_mu, rw_w0=rw_w0, rw_w2=rw_w2, rw_a0=rw_a0, rw_a2=rw_a2, rw_g2=rw_g2, rw_kk=rw_kk,
             rw_ka=rw_ka, rw_rk=rw_rk, rw_ln_g=rw_ln_g, rw_ln_b=rw_ln_b,
             gla_gk2=gla_gk2, gla_gkb=gla_gkb, gla_ln_g=gla_ln_g,
             lru_conv_w=lru_conv_w, lru_conv_b=lru_conv_b, lru_wa=lru_wa, lru_ba=lru_ba, lru_wx=lru_wx,
             lru_bx=lru_bx, lru_lam=lru_lam,
             cv_dw_w=cv_dw_w, cv_dw_b=cv_dw_b, cv_ln_g=cv_ln_g, cv_ln_b=cv_ln_b, cv_pw_w=cv_pw_w, cv_pw_b=cv_pw_b,
             e_w1=e_w1, e_w3=e_w3, e_w2=e_w2)
    P = _prep_params(d)
    shared = {"w_router_t": w_router.T, "b_router": b_router.reshape(N_EXPERTS, 1)}
    norm_f = norm_f_g.reshape(1, D_MODEL)

    bp, tp, _ = x_prompt.shape
    bs, ts, _ = x_sample.shape
    n_mod = 1 + bs
    mod_rows = -(-n_mod // 8) * 8
    cvec = jnp.concatenate([c_ctx[None], c, jnp.zeros((mod_rows - n_mod, D_MODEL), F32)], 0)
    mod = _ada_mod(cvec, w_ada, b_ada)

    tp_flat = min(bp * tp, 1024)
    bp_flat = bp * tp // tp_flat
    xp = x_prompt.reshape(bp_flat, tp_flat, D_MODEL)
    xs = x_sample
    zero_rw = jnp.zeros((bp, N_DIR, W_MIX, W_MIX), F32)
    zero_gla = jnp.zeros((bp, N_DIR, W_MIX, GLA_HK), F32)
    zero_lru = jnp.zeros((bp, N_DIR, W_MIX), F32)

    def layer(x, m, seq_shape, latent, s_rw, s_gla, s_lru, p, last):
        flat_shape = x.shape[:2]
        zs = _in_proj(x, m, p["norm1_g"], p["w_in"])
        za, zb, zc, zd = (z.reshape(seq_shape + (z.shape[-1],)) for z in zs)
        ya, f_rw = _rwkv_mixer(za, s_rw, p, latent)
        yb, f_gla = _gla_mixer(zb, s_gla, p)
        yc, f_lru = _lru_mixer(zc, s_lru, p)
        yd = _conv_mixer(zd, p, latent)
        ys = [y.reshape(flat_shape + (W_MIX,)) for y in (ya, yb, yc, yd)]
        return _out_moe(x, ys, m, p, norm_f, last), f_rw, f_gla, f_lru

    fin_rw, fin_gla, fin_lru = [], [], []
    for l in range(DEPTH):
        p = {n: a[l] for n, a in P.items()}
        p.update(shared)
        last = l == DEPTH - 1
        m_p = jnp.broadcast_to(mod[l, 0:1], (bp_flat, 6 * D_MODEL)).reshape(bp_flat, 1, 6 * D_MODEL)
        xp, f_rw, f_gla, f_lru = layer(xp, m_p, (bp, tp), False, zero_rw, zero_gla, zero_lru, p, last)
        fin_rw.append(_rw_state_out(f_rw))
        fin_gla.append(_gla_state_out(f_gla))
        fin_lru.append(f_lru)
        m_s = mod[l, 1:1 + bs].reshape(bs, 1, 6 * D_MODEL)
        xs, _, _, _ = layer(xs, m_s, (bs, ts), True, _rw_state_in(state_rwkv[:, l]),
                            _gla_state_in(state_gla[:, l]), state_lru[:, l], p, last)
    return (xp.reshape(bp, tp, D_MODEL), xs, jnp.stack(fin_rw, axis=1), jnp.stack(fin_gla, axis=1),
            jnp.stack(fin_lru, axis=1))
```

```python
import functools

import jax
import jax.numpy as jnp
from jax import lax
from jax.experimental import pallas as pl
from jax.experimental.pallas import tpu as pltpu

F32 = jnp.float32
BF16 = jnp.bfloat16

D_MODEL = 1024
DEPTH = 4
GRID_W = 64
N_DIR = 2
W_MIX = 256
N_HEADS = 4
HEAD = W_MIX // N_HEADS
GLA_DK = 32
GLA_HK = N_HEADS * GLA_DK
RW_LN_EPS = 64e-5
NORM_EPS = 1e-6
GLA_GATE_NORM = 16.0
LRU_C = 8.0
LRU_CONV = 4
CV_KERNEL = 31
CV_PAD = (CV_KERNEL - 1) // 2
N_EXPERTS = 16
GROUP_SIZE = 4
N_GROUPS = 4
D_EXPERT = 256
CHUNK = 64
EXP_M_HALF = 0.6065306597126334

ZA_W = 1024
ZB_W = 896
ZC_W = 512
ZD_W = 512
P_PAD = ZA_W + ZB_W + ZC_W + ZD_W

VMEM_LIMIT = 48 * 1024 * 1024


def _cparams(sem, vmem=VMEM_LIMIT):
    return pltpu.CompilerParams(dimension_semantics=sem, vmem_limit_bytes=vmem)


def _dot(a, b):
    return jnp.dot(a.astype(BF16), b.astype(BF16), preferred_element_type=F32)


def _dot_nt(a, b):
    return lax.dot_general(a.astype(BF16), b.astype(BF16), (((1,), (1,)), ((), ())),
                           preferred_element_type=F32)


def _dot_tn(a, b):
    return lax.dot_general(a.astype(BF16), b.astype(BF16), (((0,), (0,)), ((), ())),
                           preferred_element_type=F32)


def _split(a):
    hi = a.astype(BF16)
    lo = (a - hi.astype(F32)).astype(BF16)
    return hi, lo


def _dot_x2(a, b_exact):
    hi, lo = _split(a)
    bb = b_exact.astype(BF16)
    return (jnp.dot(hi, bb, preferred_element_type=F32) + jnp.dot(lo, bb, preferred_element_type=F32))


def _dot_left_x2(a_exact, b):
    hi, lo = _split(b)
    aa = a_exact.astype(BF16)
    return (jnp.dot(aa, hi, preferred_element_type=F32) + jnp.dot(aa, lo, preferred_element_type=F32))


def _dot3(a, b):
    ah, al = _split(a)
    bh, bl = _split(b)
    return (jnp.dot(ah, bh, preferred_element_type=F32) + jnp.dot(ah, bl, preferred_element_type=F32)
            + jnp.dot(al, bh, preferred_element_type=F32))


def _dot3_nt(a, b):
    ah, al = _split(a)
    bh, bl = _split(b)
    dn = (((1,), (1,)), ((), ()))
    return (lax.dot_general(ah, bh, dn, preferred_element_type=F32)
            + lax.dot_general(ah, bl, dn, preferred_element_type=F32)
            + lax.dot_general(al, bh, dn, preferred_element_type=F32))


def _iota(shape, axis):
    return lax.broadcasted_iota(jnp.int32, shape, axis)


def _block_mask(rows, cols, rblk, cblk):
    r = _iota((rows, cols), 0) // rblk
    c = _iota((rows, cols), 1) // cblk
    return (r == c).astype(F32)


def _expand(x, bm):
    xb = x.astype(BF16)
    return jnp.concatenate([xb, xb, xb, xb], axis=0) * bm


def _tri(d, strict):
    t = _iota((CHUNK, CHUNK), 0)
    s = _iota((CHUNK, CHUNK), 1)
    if d == 0:
        m = (s < t) if strict else (s <= t)
    else:
        m = (s > t) if strict else (s >= t)
    return m.astype(F32)


def _tri_wide(d, strict):
    t = _iota((CHUNK, N_HEADS * CHUNK), 0)
    s = _iota((CHUNK, N_HEADS * CHUNK), 1) % CHUNK
    if d == 0:
        m = (s < t) if strict else (s <= t)
    else:
        m = (s > t) if strict else (s >= t)
    return m.astype(F32)


ADA_TN = 1536


def _ada_kernel(c_ref, w_ref, b_ref, o_ref):
    c = c_ref[...]
    s = c * jax.nn.sigmoid(c)
    o_ref[0] = _dot3(s, w_ref[0]) + b_ref[0]


def _ada_mod(cvec, w_ada, b_ada):
    rows = cvec.shape[0]
    n_out = w_ada.shape[-1]
    return pl.pallas_call(
        _ada_kernel,
        grid=(DEPTH, n_out // ADA_TN),
        in_specs=[
            pl.BlockSpec((rows, D_MODEL), lambda l, j: (0, 0)),
            pl.BlockSpec((1, D_MODEL, ADA_TN), lambda l, j: (l, 0, j)),
            pl.BlockSpec((1, 1, ADA_TN), lambda l, j: (l, 0, j)),
        ],
        out_specs=pl.BlockSpec((1, rows, ADA_TN), lambda l, j: (l, 0, j)),
        out_shape=jax.ShapeDtypeStruct((DEPTH, rows, n_out), F32),
        compiler_params=_cparams(("arbitrary", "arbitrary")),
        name="ada_mod",
    )(cvec, w_ada, b_ada.reshape(DEPTH, 1, n_out))


IN_TM = 512


def _in_kernel(x_ref, sh_ref, sc_ref, g_ref, w_ref, za_ref, zb_ref, zc_ref, zd_ref):
    x = x_ref[0]
    h = x * lax.rsqrt(jnp.mean(x * x, axis=-1, keepdims=True) + NORM_EPS) * g_ref[...]
    h = (h * (1.0 + sc_ref[0]) + sh_ref[0]).astype(BF16)
    o0, o1, o2 = ZA_W, ZA_W + ZB_W, ZA_W + ZB_W + ZC_W
    za_ref[0] = jnp.dot(h, w_ref[:, 0:o0], preferred_element_type=F32)
    zb_ref[0] = jnp.dot(h, w_ref[:, o0:o1], preferred_element_type=F32)
    zc_ref[0] = jnp.dot(h, w_ref[:, o1:o2], preferred_element_type=F32)
    zd_ref[0] = jnp.dot(h, w_ref[:, o2:P_PAD], preferred_element_type=F32)


def _in_proj(x, m, g, w):
    b, t, _ = x.shape
    tm = min(IN_TM, t)
    tok = lambda w_: pl.BlockSpec((1, tm, w_), lambda i, j: (i, j, 0))
    mod = lambda k: pl.BlockSpec((1, 1, D_MODEL), lambda i, j, k=k: (i, 0, k))
    return pl.pallas_call(
        _in_kernel,
        grid=(b, t // tm),
        in_specs=[tok(D_MODEL), mod(0), mod(1),
                  pl.BlockSpec((1, D_MODEL), lambda i, j: (0, 0)),
                  pl.BlockSpec((D_MODEL, P_PAD), lambda i, j: (0, 0))],
        out_specs=[tok(ZA_W), tok(ZB_W), tok(ZC_W), tok(ZD_W)],
        out_shape=[jax.ShapeDtypeStruct((b, t, w_), F32) for w_ in (ZA_W, ZB_W, ZC_W, ZD_W)],
        compiler_params=_cparams(("arbitrary", "arbitrary")),
        name="in_proj",
    )(x, m, m, g, w)


SHIFT_TB = 512


def _shift_grid_kernel(z_ref, up_ref, dn_ref, mu_ref, o_ref, *, nblk):
    i = pl.program_id(1)
    z = z_ref[0]
    tb, c = z.shape
    col = _iota((tb, c), 0) % GRID_W
    lane = _iota((tb, c), 1) % 4
    left = jnp.where(col == 0, 0.0, pltpu.roll(z, 1, 0))
    right = jnp.where(col == GRID_W - 1, 0.0, pltpu.roll(z, tb - 1, 0))
    up_halo = jnp.where(i > 0, up_ref[0, 0], 0.0)
    dn_halo = jnp.where(i < nblk - 1, dn_ref[0, 0], 0.0)
    up = jnp.concatenate([up_halo, z[:tb - GRID_W]], axis=0)
    down = jnp.concatenate([z[GRID_W:], dn_halo], axis=0)
    sh = jnp.where(lane == 0, left, jnp.where(lane == 1, right, jnp.where(lane == 2, up, down)))
    o_ref[0] = z + (sh - z) * mu_ref[...]


def _shift_ctx_kernel(z_ref, mu_ref, o_ref):
    z = z_ref[0]
    t, c = z.shape
    row = _iota((t, c), 0)
    lane = _iota((t, c), 1) % 2
    prev = jnp.where(row == 0, 0.0, pltpu.roll(z, 1, 0))
    nxt = jnp.where(row == t - 1, 0.0, pltpu.roll(z, t - 1, 0))
    sh = jnp.where(lane == 0, prev, nxt)
    o_ref[0] = z + (sh - z) * mu_ref[...]


def _rw_shift(za, mu, latent):
    b, t, c = za.shape
    mu_spec = pl.BlockSpec((1, c), lambda *_: (0, 0))
    if not latent:
        return pl.pallas_call(
            _shift_ctx_kernel,
            grid=(b,),
            in_specs=[pl.BlockSpec((1, t, c), lambda i: (i, 0, 0)), mu_spec],
            out_specs=pl.BlockSpec((1, t, c), lambda i: (i, 0, 0)),
            out_shape=jax.ShapeDtypeStruct((b, t, c), F32),
            compiler_params=_cparams(("arbitrary",)),
            name="rw_shift_ctx",
        )(za, mu)
    tb = min(SHIFT_TB, t)
    nblk = t // tb
    rpb = tb // GRID_W
    nrow = t // GRID_W
    z4 = za.reshape(b, nrow, GRID_W, c)
    return pl.pallas_call(
        functools.partial(_shift_grid_kernel, nblk=nblk),
        grid=(b, nblk),
        in_specs=[
            pl.BlockSpec((1, tb, c), lambda i, j: (i, j, 0)),
            pl.BlockSpec((1, 1, GRID_W, c), lambda i, j: (i, jnp.maximum(j * rpb - 1, 0), 0, 0)),
            pl.BlockSpec((1, 1, GRID_W, c), lambda i, j: (i, jnp.minimum(j * rpb + rpb, nrow - 1), 0, 0)),
            mu_spec,
        ],
        out_specs=pl.BlockSpec((1, tb, c), lambda i, j: (i, j, 0)),
        out_shape=jax.ShapeDtypeStruct((b, t, c), F32),
        compiler_params=_cparams(("arbitrary", "arbitrary")),
        name="rw_shift_grid",
    )(za, z4, z4, mu)


SCAN_TB = 512
PREP_GROUP = 4


def _rw_scan_kernel(zf_ref, zb_ref, s0_ref, wl_ref, lb_ref, kkp_ref, kap_ref,
                    yf_ref, yb_ref, st_ref,
                    lw_s, a_s, b_s, kd_s, r2_s, mc_s, ds_s, gt_s, *, nchunk):
    i = pl.program_id(1)

    @pl.when(i == 0)
    def _():
        st_ref[...] = s0_ref[...]

    ones_blk = _block_mask(W_MIX, W_MIX, HEAD, HEAD)
    lane128 = _iota((1, 128), 1)
    z_refs = (zf_ref, zb_ref)

    for d in range(N_DIR):
        k = z_refs[d][0, :, 256:512]
        la = z_refs[d][0, :, 768:896]
        la_t = jnp.where(lane128 < 64, jnp.tanh(la), la)
        wraw = _dot(la_t, wl_ref[:, 256 * d:256 * d + 256]) + lb_ref[:, 256 * d:256 * d + 256]
        araw = _dot(la_t, wl_ref[:, 512 + 256 * d:768 + 256 * d]) + lb_ref[:, 512 + 256 * d:768 + 256 * d]
        lw_s[d] = jax.nn.sigmoid(wraw) * (-EXP_M_HALF)
        icl = jax.nn.sigmoid(araw)
        kkv = k * kkp_ref[...]
        ss = _dot_x2(kkv * kkv, ones_blk)
        kk = kkv * lax.rsqrt(ss + 1e-12)
        a_s[d] = -kk
        b_s[d] = kk * icl
        kd_s[d] = k * (1.0 + (icl - 1.0) * kap_ref[...])

    bm = _block_mask(N_HEADS * CHUNK, W_MIX, CHUNK, HEAD).astype(BF16)
    tri = [_tri(d, False) for d in range(N_DIR)]
    tw_strict = [_tri_wide(d, True) for d in range(N_DIR)]
    tw_incl = [_tri_wide(d, False) for d in range(N_DIR)]
    eye_w = (_iota((CHUNK, N_HEADS * CHUNK), 0) == _iota((CHUNK, N_HEADS * CHUNK), 1) % CHUNK).astype(F32)

    def rows_of(c, n):
        return pl.ds(pl.multiple_of(c * n, n), n)

    group = min(PREP_GROUP, nchunk)
    y_refs = (yf_ref, yb_ref)

    def prep_body(gi, carry):
        ch = [(d, gi * group + j) for j in range(group) for d in range(N_DIR)]
        each = lambda fn, *ls: [fn(*xs) for xs in zip(*ls)]
        ds_ = [d for d, _ in ch]
        load = lambda ref: [ref[d, rows_of(c, CHUNK), :] for d, c in ch]
        lw, a, b, kd = load(lw_s), load(a_s), load(b_s), load(kd_s)
        r = [z_refs[d][0, rows_of(c, CHUNK), 0:256] for d, c in ch]
        v = [z_refs[d][0, rows_of(c, CHUNK), 512:768] for d, c in ch]
        cum = each(lambda d, x: _dot_left_x2(tri[d], x), ds_, lw)
        tot = each(lambda d, x: x[CHUNK - 1:CHUNK, :] if d == 0 else x[0:1, :], ds_, cum)
        g_inv = each(lambda x: jnp.exp(-x), cum)
        g_rem = each(lambda t_, x: jnp.exp(t_ - x), tot, cum)
        at = each(lambda a_, x, l: a_ * jnp.exp(x - l), a, cum, lw)
        rt = each(lambda r_, x: r_ * jnp.exp(x), r, cum)
        x = each(lambda p, q: jnp.concatenate([p, q], axis=0), at, rt)
        gb = each(lambda x_, b_, g: _dot_nt(x_, _expand(b_ * g, bm)), x, b, g_inv)
        gk = each(lambda x_, k_, g: _dot_nt(x_, _expand(k_ * g, bm)), x, kd, g_inv)
        vbd = each(lambda v_: _expand(v_, bm), v)
        a_ab = each(lambda d, g: g[0:CHUNK] * tw_strict[d], ds_, gb)
        av = each(lambda d, g, vb: _dot(g[0:CHUNK] * tw_strict[d], vb), ds_, gk, vbd)
        tw = each(lambda m: eye_w + m, a_ab)
        apow = a_ab
        for _ in range(5):
            apow = each(lambda m: _dot(m, _expand(m, bm)), apow)
            tw = each(lambda t_, m: t_ + _dot(m, _expand(t_, bm)), tw, apow)
        a2 = each(lambda t_, m: _dot(t_, _expand(m, bm)), tw, at)
        u0 = each(lambda t_, m: _dot(t_, _expand(m, bm)), tw, av)
        r_b = each(lambda d, g: g[CHUNK:] * tw_incl[d], ds_, gb)
        r2 = each(lambda rt_, rb, m: rt_ + _dot(rb, _expand(m, bm)), rt, r_b, a2)
        y0 = each(lambda d, rb, u, g, vb: _dot(rb, _expand(u, bm)) + _dot(g[CHUNK:] * tw_incl[d], vb),
                  ds_, r_b, u0, gk, vbd)
        bl = each(lambda b_, g: b_ * g, b, g_rem)
        kl = each(lambda k_, g: k_ * g, kd, g_rem)
        mc = each(lambda m, bl_: _dot_tn(m, bl_) * ones_blk, a2, bl)
        ds0 = each(lambda u, v_, bl_, kl_: _dot_tn(jnp.concatenate([u, v_], axis=0),
                                                   jnp.concatenate([bl_, kl_], axis=0)) * ones_blk,
                   u0, v, bl, kl)
        for n, (d, c) in enumerate(ch):
            rows = rows_of(c, CHUNK)
            y_refs[d][0, rows, :] = y0[n]
            r2_s[d, rows, :] = r2[n].astype(BF16)
            mc_s[d, rows_of(c, W_MIX), :] = mc[n].astype(BF16)
            ds_s[d, rows_of(c, W_MIX), :] = ds0[n]
            gt_s[d, rows_of(c, 8), :] = jnp.broadcast_to(jnp.exp(tot[n]), (8, W_MIX))
        return carry

    lax.fori_loop(0, nchunk // group, prep_body, 0)

    def scan_body(cc, carry):
        cs = [cc, nchunk - 1 - cc]
        s = [st_ref[0, d] for d in range(N_DIR)]
        sb = [x.astype(BF16) for x in s]
        upd = [_dot(sb[d], mc_s[d, rows_of(cs[d], W_MIX), :]) for d in range(N_DIR)]
        ys = [_dot_nt(r2_s[d, rows_of(cs[d], CHUNK), :], sb[d]) for d in range(N_DIR)]
        for d in range(N_DIR):
            st_ref[0, d] = (s[d] * gt_s[d, rows_of(cs[d], 8), :][0:1, :] + upd[d]
                            + ds_s[d, rows_of(cs[d], W_MIX), :])
            y_refs[d][0, rows_of(cs[d], CHUNK), :] += ys[d]
        return carry

    lax.fori_loop(0, nchunk, scan_body, 0)


def _rw_scan(zs, s0, wl, lb, kkp, kap):
    b, t, c = zs.shape
    tb = min(SCAN_TB, t)
    nblk = t // tb
    nchunk = tb // CHUNK
    vec = lambda n: pl.BlockSpec((1, n), lambda i, j: (0, 0))
    st_spec = pl.BlockSpec((1, N_DIR, W_MIX, W_MIX), lambda i, j: (i, 0, 0, 0))
    return pl.pallas_call(
        functools.partial(_rw_scan_kernel, nchunk=nchunk),
        grid=(b, nblk),
        in_specs=[
            pl.BlockSpec((1, tb, c), lambda i, j: (i, j, 0)),
            pl.BlockSpec((1, tb, c), lambda i, j: (i, nblk - 1 - j, 0)),
            st_spec,
            pl.BlockSpec((128, 1024), lambda i, j: (0, 0)),
            vec(1024), vec(W_MIX), vec(W_MIX),
        ],
        out_specs=[
            pl.BlockSpec((1, tb, W_MIX), lambda i, j: (i, j, 0)),
            pl.BlockSpec((1, tb, W_MIX), lambda i, j: (i, nblk - 1 - j, 0)),
            st_spec,
        ],
        out_shape=[
            jax.ShapeDtypeStruct((b, t, W_MIX), F32),
            jax.ShapeDtypeStruct((b, t, W_MIX), F32),
            jax.ShapeDtypeStruct((b, N_DIR, W_MIX, W_MIX), F32),
        ],
        scratch_shapes=[pltpu.VMEM((N_DIR, tb, W_MIX), F32) for _ in range(4)] + [
            pltpu.VMEM((N_DIR, tb, W_MIX), BF16),
            pltpu.VMEM((N_DIR, nchunk * W_MIX, W_MIX), BF16),
            pltpu.VMEM((N_DIR, nchunk * W_MIX, W_MIX), F32),
            pltpu.VMEM((N_DIR, nchunk * 8, W_MIX), F32),
        ],
        compiler_params=_cparams(("arbitrary", "arbitrary")),
        name="rw_scan",
    )(zs, zs, s0, wl, lb, kkp, kap)


EPI_TM = 512


def _rw_epi_kernel(yf_ref, yb_ref, z_ref, lng_ref, lnb_ref, rk_ref, g2_ref, o_ref):
    ones_blk = _block_mask(W_MIX, W_MIX, HEAD, HEAD)
    y = yf_ref[0] + yb_ref[0]
    mu = _dot_x2(y, ones_blk) * (1.0 / HEAD)
    yc = y - mu
    var = _dot_x2(yc * yc, ones_blk) * (1.0 / HEAD)
    yn = yc * lax.rsqrt(var + RW_LN_EPS) * lng_ref[...] + lnb_ref[...]
    z = z_ref[0]
    r = z[:, 0:256]
    k = z[:, 256:512]
    v = z[:, 512:768]
    gl = z[:, 896:1024]
    bonus = _dot_x2(r * k * rk_ref[...], ones_blk) * v
    gate = _dot(jax.nn.sigmoid(gl), g2_ref[...])
    o_ref[0] = (yn + bonus) * gate


def _rw_epi(yf, yb, zs, lng, lnb, rk, g2p):
    b, t, _ = yf.shape
    tm = min(EPI_TM, t)
    tok = lambda w_: pl.BlockSpec((1, tm, w_), lambda i, j: (i, j, 0))
    vec = pl.BlockSpec((1, W_MIX), lambda i, j: (0, 0))
    return pl.pallas_call(
        _rw_epi_kernel,
        grid=(b, t // tm),
        in_specs=[tok(W_MIX), tok(W_MIX), tok(ZA_W), vec, vec, vec,
                  pl.BlockSpec((128, W_MIX), lambda i, j: (0, 0))],
        out_specs=tok(W_MIX),
        out_shape=jax.ShapeDtypeStruct((b, t, W_MIX), F32),
        compiler_params=_cparams(("arbitrary", "arbitrary")),
        name="rw_epi",
    )(yf, yb, zs, lng, lnb, rk, g2p)


def _rwkv_mixer(za, s0_bd, p, latent):
    zs = _rw_shift(za, p["rw_mu"], latent)
    yf, yb, st = _rw_scan(zs, s0_bd, p["rw_wl"], p["rw_lb"], p["rw_kk"], p["rw_ka"])
    ya = _rw_epi(yf, yb, zs, p["rw_ln_g"], p["rw_ln_b"], p["rw_rk"], p["rw_g2"])
    return ya, st


def _gla_scan_kernel(zf_ref, zb_ref, s0_ref, wg_ref, gb_ref, yf_ref, yb_ref, st_ref,
                     la_s, qe_s, ds_s, dec_s, *, nchunk):
    i = pl.program_id(1)

    @pl.when(i == 0)
    def _():
        st_ref[...] = s0_ref[...]

    z_refs = (zf_ref, zb_ref)
    y_refs = (yf_ref, yb_ref)
    for d in range(N_DIR):
        logit = _dot(z_refs[d][0, :, 768:896], wg_ref[d]) + gb_ref[d]
        la_s[d] = jax.nn.log_sigmoid(logit) * (1.0 / GLA_GATE_NORM)

    bm_k = _block_mask(N_HEADS * CHUNK, GLA_HK, CHUNK, GLA_DK).astype(BF16)
    bm_v = _block_mask(N_HEADS * CHUNK, W_MIX, CHUNK, HEAD).astype(BF16)
    bm_s = _block_mask(W_MIX, GLA_HK, HEAD, GLA_DK)
    tri = [_tri(d, False) for d in range(N_DIR)]
    tw_incl = [_tri_wide(d, False) for d in range(N_DIR)]

    def rows_of(c, n):
        return pl.ds(pl.multiple_of(c * n, n), n)

    group = min(PREP_GROUP, nchunk)

    def prep_body(gi, carry):
        ch = [(d, gi * group + j) for j in range(group) for d in range(N_DIR)]
        each = lambda fn, *ls: [fn(*xs) for xs in zip(*ls)]
        ds_ = [d for d, _ in ch]
        la = [la_s[d, rows_of(c, CHUNK), :] for d, c in ch]
        q = [z_refs[d][0, rows_of(c, CHUNK), 0:128] * (GLA_DK ** -0.5) for d, c in ch]
        k = [z_refs[d][0, rows_of(c, CHUNK), 128:256] for d, c in ch]
        v = [z_refs[d][0, rows_of(c, CHUNK), 256:512] for d, c in ch]
        cum = each(lambda d, x: _dot_left_x2(tri[d], x), ds_, la)
        last = each(lambda d, x: x[CHUNK - 1:CHUNK, :] if d == 0 else x[0:1, :], ds_, cum)
        qe = each(lambda q_, x: q_ * jnp.exp(x), q, cum)
        ke = each(lambda k_, x: k_ * jnp.exp(-x), k, cum)
        kl = each(lambda k_, l, x: k_ * jnp.exp(l - x), k, last, cum)
        att = each(lambda d, q_, k_: _dot_nt(q_, _expand(k_, bm_k)) * tw_incl[d], ds_, qe, ke)
        o = each(lambda a_, v_: _dot(a_, _expand(v_, bm_v)), att, v)
        dst = each(lambda v_, k_: _dot_tn(v_, k_) * bm_s, v, kl)
        for n, (d, c) in enumerate(ch):
            y_refs[d][0, rows_of(c, CHUNK), :] = o[n]
            qe_s[d, rows_of(c, CHUNK), :] = qe[n].astype(BF16)
            ds_s[d, rows_of(c, W_MIX), :] = dst[n]
            dec_s[d, rows_of(c, 8), :] = jnp.broadcast_to(jnp.exp(last[n]), (8, GLA_HK))
        return carry

    lax.fori_loop(0, nchunk // group, prep_body, 0)

    def scan_body(cc, carry):
        cs = [cc, nchunk - 1 - cc]
        for d in range(N_DIR):
            s = st_ref[0, d]
            y_refs[d][0, rows_of(cs[d], CHUNK), :] += _dot_nt(qe_s[d, rows_of(cs[d], CHUNK), :], s)
            st_ref[0, d] = s * dec_s[d, rows_of(cs[d], 8), :][0:1, :] + ds_s[d, rows_of(cs[d], W_MIX), :]
        return carry

    lax.fori_loop(0, nchunk, scan_body, 0)


def _gla_scan(zb, s0, wg, gb):
    b, t, c = zb.shape
    tb = min(SCAN_TB, t)
    nblk = t // tb
    st_spec = pl.BlockSpec((1, N_DIR, W_MIX, GLA_HK), lambda i, j: (i, 0, 0, 0))
    nchunk = tb // CHUNK
    return pl.pallas_call(
        functools.partial(_gla_scan_kernel, nchunk=nchunk),
        grid=(b, nblk),
        in_specs=[
            pl.BlockSpec((1, tb, c), lambda i, j: (i, j, 0)),
            pl.BlockSpec((1, tb, c), lambda i, j: (i, nblk - 1 - j, 0)),
            st_spec,
            pl.BlockSpec((N_DIR, 128, GLA_HK), lambda i, j: (0, 0, 0)),
            pl.BlockSpec((N_DIR, 1, GLA_HK), lambda i, j: (0, 0, 0)),
        ],
        out_specs=[
            pl.BlockSpec((1, tb, W_MIX), lambda i, j: (i, j, 0)),
            pl.BlockSpec((1, tb, W_MIX), lambda i, j: (i, nblk - 1 - j, 0)),
            st_spec,
        ],
        out_shape=[
            jax.ShapeDtypeStruct((b, t, W_MIX), F32),
            jax.ShapeDtypeStruct((b, t, W_MIX), F32),
            jax.ShapeDtypeStruct((b, N_DIR, W_MIX, GLA_HK), F32),
        ],
        scratch_shapes=[
            pltpu.VMEM((N_DIR, tb, GLA_HK), F32),
            pltpu.VMEM((N_DIR, tb, GLA_HK), BF16),
            pltpu.VMEM((N_DIR, nchunk * W_MIX, GLA_HK), F32),
            pltpu.VMEM((N_DIR, nchunk * 8, GLA_HK), F32),
        ],
        compiler_params=_cparams(("arbitrary", "arbitrary")),
        name="gla_scan",
    )(zb, zb, s0, wg, gb)


def _gla_epi_kernel(yf_ref, yb_ref, z_ref, g_ref, o_ref):
    ones_blk = _block_mask(W_MIX, W_MIX, HEAD, HEAD)
    y = yf_ref[0] + yb_ref[0]
    ms = _dot_x2(y * y, ones_blk) * (1.0 / HEAD)
    y = y * lax.rsqrt(ms + NORM_EPS) * g_ref[...]
    og = z_ref[0, :, 512:768]
    o_ref[0] = y * (og * jax.nn.sigmoid(og))


def _gla_epi(yf, yb, zb, g):
    b, t, _ = yf.shape
    tm = min(EPI_TM, t)
    tok = lambda w_: pl.BlockSpec((1, tm, w_), lambda i, j: (i, j, 0))
    return pl.pallas_call(
        _gla_epi_kernel,
        grid=(b, t // tm),
        in_specs=[tok(W_MIX), tok(W_MIX), tok(ZB_W), pl.BlockSpec((1, W_MIX), lambda i, j: (0, 0))],
        out_specs=tok(W_MIX),
        out_shape=jax.ShapeDtypeStruct((b, t, W_MIX), F32),
        compiler_params=_cparams(("arbitrary", "arbitrary")),
        name="gla_epi",
    )(yf, yb, zb, g)


def _gla_mixer(zb, s0_bd, p):
    yf, yb, st = _gla_scan(zb, s0_bd, p["gla_wg"], p["gla_gb"])
    return _gla_epi(yf, yb, zb, p["gla_ln_g"]), st


LRU_TT = 256
LRU_HALO = 8


def _lru_kernel(z_ref, h0_ref, cw_ref, cb_ref, wax_ref, bax_ref, lam_ref, y_ref, hf_ref, xpad, *, t):
    tt = min(LRU_TT, t)
    ntile = t // tt
    xpad[0:LRU_HALO, :] = jnp.zeros((LRU_HALO, W_MIX), F32)
    xpad[LRU_HALO + t:2 * LRU_HALO + t, :] = jnp.zeros((LRU_HALO, W_MIX), F32)

    def fill(j, carry):
        base = pl.multiple_of(j * tt, tt)
        xpad[pl.ds(base + LRU_HALO, tt), :] = z_ref[0, pl.ds(base, tt), 0:W_MIX]
        return carry

    lax.fori_loop(0, ntile, fill, 0)
    row = _iota((tt, W_MIX), 0)

    def tile_scan(j, h, d):
        base = pl.multiple_of(j * tt, tt)
        win = xpad[pl.ds(base, tt + 2 * LRU_HALO), :]
        xc = jnp.zeros((tt, W_MIX), F32) + cb_ref[d]
        for tap in range(LRU_CONV):
            off = LRU_HALO - (LRU_CONV - 1) + tap if d == 0 else LRU_HALO + (LRU_CONV - 1) - tap
            xc = xc + cw_ref[d, tap:tap + 1, :] * win[off:off + tt, :]
        g = _dot(xc, wax_ref[d]) + bax_ref[d]
        gr = jax.nn.sigmoid(g[:, 0:W_MIX])
        gi = jax.nn.sigmoid(g[:, W_MIX:2 * W_MIX])
        log_a = -LRU_C * gr * jax.nn.softplus(-lam_ref[d])
        a = jnp.exp(log_a)
        bv = jnp.sqrt(1.0 - jnp.exp(2.0 * log_a)) * gi * xc
        s = 1
        while s < tt:
            if d == 0:
                keep = row >= s
                a_sh = jnp.where(keep, pltpu.roll(a, s, 0), 1.0)
                b_sh = jnp.where(keep, pltpu.roll(bv, s, 0), 0.0)
            else:
                keep = row < tt - s
                a_sh = jnp.where(keep, pltpu.roll(a, tt - s, 0), 1.0)
                b_sh = jnp.where(keep, pltpu.roll(bv, tt - s, 0), 0.0)
            bv = a * b_sh + bv
            a = a * a_sh
            s *= 2
        return a * h + bv, base

    def fwd(j, h):
        ht, base = tile_scan(j, h, 0)
        y_ref[0, pl.ds(base, tt), :] = ht
        return ht[tt - 1:tt, :]

    h_end = lax.fori_loop(0, ntile, fwd, h0_ref[0, 0:1, :])
    hf_ref[0, 0:1, :] = h_end

    def bwd(jj, h):
        j = ntile - 1 - jj
        ht, base = tile_scan(j, h, 1)
        gb = z_ref[0, pl.ds(base, tt), W_MIX:2 * W_MIX]
        y_ref[0, pl.ds(base, tt), :] = (y_ref[0, pl.ds(base, tt), :] + ht) * jax.nn.gelu(gb)
        return ht[0:1, :]

    h_end = lax.fori_loop(0, ntile, bwd, h0_ref[0, 1:2, :])
    hf_ref[0, 1:2, :] = h_end


def _lru_mixer(zc, h0, p):
    b, t, c = zc.shape
    full = lambda *s: pl.BlockSpec(s, lambda i: (0,) * len(s))
    return pl.pallas_call(
        functools.partial(_lru_kernel, t=t),
        grid=(b,),
        in_specs=[
            pl.BlockSpec((1, t, c), lambda i: (i, 0, 0)),
            pl.BlockSpec((1, N_DIR, W_MIX), lambda i: (i, 0, 0)),
            full(N_DIR, LRU_CONV, W_MIX), full(N_DIR, 1, W_MIX),
            full(N_DIR, W_MIX, 2 * W_MIX), full(N_DIR, 1, 2 * W_MIX), full(N_DIR, 1, W_MIX),
        ],
        out_specs=[pl.BlockSpec((1, t, W_MIX), lambda i: (i, 0, 0)),
                   pl.BlockSpec((1, N_DIR, W_MIX), lambda i: (i, 0, 0))],
        out_shape=[jax.ShapeDtypeStruct((b, t, W_MIX), F32), jax.ShapeDtypeStruct((b, N_DIR, W_MIX), F32)],
        scratch_shapes=[pltpu.VMEM((t + 2 * LRU_HALO, W_MIX), F32)],
        compiler_params=_cparams(("arbitrary",)),
        name="lru",
    )(zc, h0, p["lru_cw"], p["lru_cb"], p["lru_wax"], p["lru_bax"], p["lru_lam"])


CV_TT = 256
CV_WIN = 16


def _conv_kernel(z_ref, dw_ref, dwb_ref, lng_ref, lnb_ref, pw_ref, pwb_ref, y_ref, upad, *, t, latent):
    tt = min(CV_TT, t)
    ntile = t // tt
    pad = CV_PAD * GRID_W if latent else CV_WIN
    upad[0:pad, :] = jnp.zeros((pad, W_MIX), F32)
    upad[pad + t:2 * pad + t, :] = jnp.zeros((pad, W_MIX), F32)

    def fill(j, carry):
        base = pl.multiple_of(j * tt, tt)
        z = z_ref[0, pl.ds(base, tt), :]
        upad[pl.ds(base + pad, tt), :] = z[:, 0:W_MIX] * jax.nn.sigmoid(z[:, W_MIX:2 * W_MIX])
        return carry

    lax.fori_loop(0, ntile, fill, 0)

    def tile(j, carry):
        base = pl.multiple_of(j * tt, tt)
        if latent:
            half = W_MIX // 2
            col = _iota((tt, half), 0) % GRID_W
            win = upad[pl.ds(pl.multiple_of(base + pad - CV_WIN, CV_WIN), tt + 2 * CV_WIN), 0:half]
            accw = jnp.zeros((tt, half), F32)
            acch = jnp.zeros((tt, half), F32)
            for tap in range(CV_KERNEL):
                dlt = tap - CV_PAD
                x = win[CV_WIN + dlt:CV_WIN + dlt + tt, :]
                if dlt < 0:
                    x = jnp.where(col >= -dlt, x, 0.0)
                elif dlt > 0:
                    x = jnp.where(col < GRID_W - dlt, x, 0.0)
                accw = accw + dw_ref[tap:tap + 1, 0:half] * x
                rows = pl.ds(pl.multiple_of(base + pad + dlt * GRID_W, GRID_W), tt)
                acch = acch + dw_ref[tap:tap + 1, half:W_MIX] * upad[rows, half:W_MIX]
            u = jnp.concatenate([accw, acch], axis=1)
        else:
            win = upad[pl.ds(base + pad - CV_WIN, tt + 2 * CV_WIN), :]
            u = jnp.zeros((tt, W_MIX), F32)
            for tap in range(CV_KERNEL):
                dlt = tap - CV_PAD
                u = u + dw_ref[tap:tap + 1, :] * win[CV_WIN + dlt:CV_WIN + dlt + tt, :]
        u = u + dwb_ref[...]
        mu = jnp.mean(u, axis=-1, keepdims=True)
        uc = u - mu
        var = jnp.mean(uc * uc, axis=-1, keepdims=True)
        un = uc * lax.rsqrt(var + 1e-5) * lng_ref[...] + lnb_ref[...]
        un = un * jax.nn.sigmoid(un)
        y_ref[0, pl.ds(base, tt), :] = _dot(un, pw_ref[...]) + pwb_ref[...]
        return carry

    lax.fori_loop(0, ntile, tile, 0)


def _conv_mixer(zd, p, latent):
    b, t, c = zd.shape
    pad = CV_PAD * GRID_W if latent else CV_WIN
    full = lambda *s: pl.BlockSpec(s, lambda i: (0,) * len(s))
    vec = full(1, W_MIX)
    return pl.pallas_call(
        functools.partial(_conv_kernel, t=t, latent=latent),
        grid=(b,),
        in_specs=[pl.BlockSpec((1, t, c), lambda i: (i, 0, 0)),
                  full(CV_KERNEL, W_MIX), vec, vec, vec, full(W_MIX, W_MIX), vec],
        out_specs=pl.BlockSpec((1, t, W_MIX), lambda i: (i, 0, 0)),
        out_shape=jax.ShapeDtypeStruct((b, t, W_MIX), F32),
        scratch_shapes=[pltpu.VMEM((t + 2 * pad, W_MIX), F32)],
        compiler_params=_cparams(("arbitrary",)),
        name="conv_grid" if latent else "conv_ctx",
    )(zd, p["cv_dw_w"], p["cv_dw_b"], p["cv_ln_g"], p["cv_ln_b"], p["cv_pw_w"], p["cv_pw_b"])


MOE_TM = 512
MOE_VMEM = VMEM_LIMIT
E_PAD = 128


def _route(sel, scores):
    grp = []
    for g in range(N_GROUPS):
        s = sel[GROUP_SIZE * g:GROUP_SIZE * (g + 1)]
        best_pair = None
        for i in range(GROUP_SIZE):
            for j in range(i + 1, GROUP_SIZE):
                pair = s[i] + s[j]
                best_pair = pair if best_pair is None else jnp.maximum(best_pair, pair)
        grp.append(best_pair)
    best = jnp.zeros_like(grp[0], dtype=jnp.int32)
    top = grp[0]
    for g in range(1, N_GROUPS):
        better = grp[g] > top
        best = jnp.where(better, g, best)
        top = jnp.where(better, grp[g], top)
    neg = jnp.full_like(sel[0], -jnp.inf)
    msel = [jnp.where(best == e // GROUP_SIZE, sel[e], neg) for e in range(N_EXPERTS)]
    picks = []
    for _ in range(2):
        idx = jnp.zeros_like(best)
        top = msel[0]
        for e in range(1, N_EXPERTS):
            better = msel[e] > top
            idx = jnp.where(better, e, idx)
            top = jnp.where(better, msel[e], top)
        picks.append(idx)
        msel = [jnp.where(idx == e, neg, msel[e]) for e in range(N_EXPERTS)]
    chosen = [jnp.where((picks[0] == e) | (picks[1] == e), scores[e], 0.0) for e in range(N_EXPERTS)]
    total = chosen[0]
    for e in range(1, N_EXPERTS):
        total = total + chosen[e]
    return [ch / total for ch in chosen]


def _moe_kernel(x_ref, ya_ref, yb_ref, yc_ref, yd_ref, wo_ref, g1_ref, sh2_ref, sc2_ref, g2_ref, n2_ref,
                wr_ref, br_ref, w13_ref, w2_ref, nf_ref, o_ref, x1_s, h2_s, gt_s, gate_s, acc_s, he_s, *, final_norm):
    grp = pl.program_id(2)

    @pl.when(grp == 0)
    def _():
        y = (_dot(ya_ref[0], wo_ref[0:256, :]) + _dot(yb_ref[0], wo_ref[256:512, :])
             + _dot(yc_ref[0], wo_ref[512:768, :]) + _dot(yd_ref[0], wo_ref[768:1024, :]))
        x1 = x_ref[0] + g1_ref[0] * y
        x1_s[...] = x1
        h2 = x1 * lax.rsqrt(jnp.mean(x1 * x1, axis=-1, keepdims=True) + NORM_EPS) * n2_ref[...]
        h2 = h2 * (1.0 + sc2_ref[0]) + sh2_ref[0]
        h2_s[...] = h2.astype(BF16)
        logits = _dot3_nt(wr_ref[...], h2)
        scores = jax.nn.sigmoid(logits)
        selm = scores + br_ref[...]
        gates = _route([selm[i:i + 1, :] for i in range(N_EXPERTS)],
                       [scores[i:i + 1, :] for i in range(N_EXPERTS)])
        gt_s[...] = jnp.zeros(gt_s.shape, F32)
        for i in range(N_EXPERTS):
            gt_s[i:i + 1, :] = gates[i]
        gate_s[...] = gt_s[...].T
        acc_s[...] = jnp.zeros(acc_s.shape, F32)

    pick = (_iota((E_PAD, GROUP_SIZE * 128), 0)
            == GROUP_SIZE * grp + _iota((E_PAD, GROUP_SIZE * 128), 1) // 128).astype(F32)
    gsel = _dot_x2(gate_s[...], pick)
    h2 = h2_s[...]
    for j in range(GROUP_SIZE):
        hh = jnp.dot(h2, w13_ref[0, :, 2 * D_EXPERT * j:2 * D_EXPERT * (j + 1)], preferred_element_type=F32)
        he = hh[:, 0:D_EXPERT]
        he = he * jax.nn.sigmoid(he) * hh[:, D_EXPERT:2 * D_EXPERT]
        g = gsel[:, 128 * j:128 * (j + 1)]
        he_s[:, D_EXPERT * j:D_EXPERT * (j + 1)] = (he * jnp.concatenate([g, g], axis=1)).astype(BF16)
    acc_s[...] += jnp.dot(he_s[...], w2_ref[0], preferred_element_type=F32)

    @pl.when(grp == N_GROUPS - 1)
    def _():
        x2 = x1_s[...] + g2_ref[0] * acc_s[...]
        if final_norm:
            x2 = x2 * lax.rsqrt(jnp.mean(x2 * x2, axis=-1, keepdims=True) + NORM_EPS) * nf_ref[...]
        o_ref[0] = x2


def _out_moe(x, ys, m, p, norm_f, final_norm):
    b, t, _ = x.shape
    tm = min(MOE_TM, t)
    tok = lambda w_: pl.BlockSpec((1, tm, w_), lambda i, j, e: (i, j, 0))
    mod = lambda k: pl.BlockSpec((1, 1, D_MODEL), lambda i, j, e, k=k: (i, 0, k))
    full = lambda *s: pl.BlockSpec(s, lambda i, j, e: (0,) * len(s))
    return pl.pallas_call(
        functools.partial(_moe_kernel, final_norm=final_norm),
        grid=(b, t // tm, N_GROUPS),
        in_specs=[tok(D_MODEL), tok(W_MIX), tok(W_MIX), tok(W_MIX), tok(W_MIX),
                  full(D_MODEL, D_MODEL), mod(2), mod(3), mod(4), mod(5), full(1, D_MODEL),
                  full(N_EXPERTS, D_MODEL), full(N_EXPERTS, 1),
                  pl.BlockSpec((1, D_MODEL, 2 * D_EXPERT * GROUP_SIZE), lambda i, j, e: (e, 0, 0)),
                  pl.BlockSpec((1, D_EXPERT * GROUP_SIZE, D_MODEL), lambda i, j, e: (e, 0, 0)),
                  full(1, D_MODEL)],
        out_specs=tok(D_MODEL),
        out_shape=jax.ShapeDtypeStruct((b, t, D_MODEL), F32),
        scratch_shapes=[pltpu.VMEM((tm, D_MODEL), F32), pltpu.VMEM((tm, D_MODEL), BF16),
                        pltpu.VMEM((E_PAD, tm), F32), pltpu.VMEM((tm, E_PAD), F32),
                        pltpu.VMEM((tm, D_MODEL), F32), pltpu.VMEM((tm, D_EXPERT * GROUP_SIZE), BF16)],
        compiler_params=_cparams(("arbitrary", "arbitrary", "arbitrary"), MOE_VMEM),
        name="out_moe",
    )(x, *ys, p["w_out"], m, m, m, m, p["norm2_g"], p["w_router_t"], p["b_router"], p["e_w13g"], p["e_w2g"],
      norm_f)


def _gla_pack(z):
    hk = GLA_HK
    lead = z.shape[:-1]
    gl = z[..., 2 * hk + W_MIX:2 * hk + W_MIX + 32]
    return jnp.concatenate([z[..., 0:2 * hk + W_MIX], z[..., 2 * hk + W_MIX + 32:], gl,
                            jnp.zeros(lead + (96,), z.dtype)], -1)


def _gla_state_in(s):
    b = s.shape[0]
    eye = jnp.eye(N_HEADS, dtype=s.dtype)
    return jnp.einsum("bdhkv,hg->bdhvgk", s, eye).reshape(b, N_DIR, W_MIX, GLA_HK)


def _gla_state_out(st):
    b = st.shape[0]
    eye = jnp.eye(N_HEADS, dtype=st.dtype)
    return jnp.einsum("bdhvgk,hg->bdhkv", st.reshape(b, N_DIR, N_HEADS, HEAD, N_HEADS, GLA_DK), eye)


def _rw_state_in(s):
    b = s.shape[0]
    eye = jnp.eye(N_HEADS, dtype=s.dtype)
    return jnp.einsum("bdhvk,hg->bdhvgk", s, eye).reshape(b, N_DIR, W_MIX, W_MIX)


def _rw_state_out(st):
    b = st.shape[0]
    eye = jnp.eye(N_HEADS, dtype=st.dtype)
    return jnp.einsum("bdhvgk,hg->bdhvk", st.reshape(b, N_DIR, N_HEADS, HEAD, N_HEADS, HEAD), eye)


def _prep_params(d):
    L = DEPTH
    z = lambda *s: jnp.zeros(s, F32)
    out = {}
    out["rw_mu"] = jnp.concatenate([d["rw_mu"], z(L, ZA_W - d["rw_mu"].shape[-1])], -1).reshape(L, 1, ZA_W)
    w2, a2 = d["rw_w2"], d["rw_a2"]
    zz = z(L, 32, 256)
    rows = [
        jnp.concatenate([w2[:, 0], zz, zz, zz], -1),
        jnp.concatenate([zz, w2[:, 1], zz, zz], -1),
        jnp.concatenate([zz, zz, a2[:, 0], zz], -1),
        jnp.concatenate([zz, zz, zz, a2[:, 1]], -1),
    ]
    out["rw_wl"] = jnp.concatenate(rows, 1).astype(BF16)
    out["rw_lb"] = jnp.concatenate([d["rw_w0"][:, 0], d["rw_w0"][:, 1], d["rw_a0"][:, 0], d["rw_a0"][:, 1]],
                                   -1).reshape(L, 1, 1024)
    for n in ("rw_kk", "rw_ka", "rw_ln_g", "rw_ln_b"):
        out[n] = d[n].reshape(L, 1, W_MIX)
    out["rw_rk"] = d["rw_rk"].reshape(L, 1, W_MIX)
    out["rw_g2"] = jnp.concatenate([d["rw_g2"], z(L, 64, W_MIX)], 1).astype(BF16)

    gk2 = d["gla_gk2"]
    z16, z96 = z(L, 16, GLA_HK), z(L, 96, GLA_HK)
    out["gla_wg"] = jnp.stack([jnp.concatenate([gk2[:, 0], z16, z96], 1),
                               jnp.concatenate([z16, gk2[:, 1], z96], 1)], 1).astype(BF16)
    out["gla_gb"] = d["gla_gkb"].reshape(L, N_DIR, 1, GLA_HK)
    out["gla_ln_g"] = jnp.tile(d["gla_ln_g"], (1, N_HEADS)).reshape(L, 1, W_MIX)

    eye = jnp.eye(4, dtype=F32)
    bd = lambda w: jnp.einsum("ldgij,gh->ldgihj", w, eye).reshape(L, N_DIR, W_MIX, W_MIX)
    out["lru_cw"] = d["lru_conv_w"]
    out["lru_cb"] = d["lru_conv_b"].reshape(L, N_DIR, 1, W_MIX)
    out["lru_wax"] = jnp.concatenate([bd(d["lru_wa"]), bd(d["lru_wx"])], -1).astype(BF16)
    out["lru_bax"] = jnp.concatenate([d["lru_ba"], d["lru_bx"]], -1).reshape(L, N_DIR, 1, 2 * W_MIX)
    out["lru_lam"] = d["lru_lam"].reshape(L, N_DIR, 1, W_MIX)

    out["cv_dw_w"] = d["cv_dw_w"]
    for n in ("cv_dw_b", "cv_ln_g", "cv_ln_b", "cv_pw_b"):
        out[n] = d[n].reshape(L, 1, W_MIX)
    out["cv_pw_w"] = d["cv_pw_w"].astype(BF16)

    if "w_in" in d:
        w_in = d["w_in"]
        o1, o2, o3 = 960, 1760, 2272
        out["w_in"] = jnp.concatenate([w_in[..., 0:o1], z(L, D_MODEL, ZA_W - o1), _gla_pack(w_in[..., o1:o2]),
                                       w_in[..., o2:o3], w_in[..., o3:]], -1).astype(BF16)
        out["w_out"] = d["w_out"].astype(BF16)
        out["norm1_g"] = d["norm1_g"].reshape(L, 1, D_MODEL)
        out["norm2_g"] = d["norm2_g"].reshape(L, 1, D_MODEL)
        w13 = jnp.concatenate([d["e_w1"].astype(BF16), d["e_w3"].astype(BF16)], -1)
        w13 = w13.reshape(L, N_GROUPS, GROUP_SIZE, D_MODEL, 2 * D_EXPERT).transpose(0, 1, 3, 2, 4)
        out["e_w13g"] = w13.reshape(L, N_GROUPS, D_MODEL, GROUP_SIZE * 2 * D_EXPERT)
        out["e_w2g"] = d["e_w2"].astype(BF16).reshape(L, N_GROUPS, GROUP_SIZE * D_EXPERT, D_MODEL)
    return out


def kernel(x_prompt, x_sample, state_rwkv, state_gla, state_lru, c, c_ctx, norm1_g, norm2_g, norm_f_g, w_ada, b_ada, w_in, w_out, rw_mu, rw_w0, rw_w2, rw_a0, rw_a2, rw_g2, rw_kk, rw_ka, rw_rk, rw_ln_g, rw_ln_b, gla_gk2, gla_gkb, gla_ln_g, lru_conv_w, lru_conv_b, lru_wa, lru_ba, lru_wx, lru_bx, lru_lam, cv_dw_w, cv_dw_b, cv_ln_g, cv_ln_b, cv_pw_w, cv_pw_b, w_router, b_router, e_w1, e_w3, e_w2):
    d = dict(norm1_g=norm1_g, norm2_g=norm2_g, w_in=w_in, w_out=w_out,
             rw_mu=rw_mu, rw_w0=rw_w0, rw_w2=rw_w2, rw_a0=rw_a0, rw_a2=rw_a2, rw_g2=rw_g2, rw_kk=rw_kk,
             rw_ka=rw_ka, rw_rk=rw_rk, rw_ln_g=rw_ln_g, rw_ln_b=rw_ln_b,
             gla_gk2=gla_gk2, gla_gkb=gla_gkb, gla_ln_g=gla_ln_g,
             lru_conv_w=lru_conv_w, lru_conv_b=lru_conv_b, lru_wa=lru_wa, lru_ba=lru_ba, lru_wx=lru_wx,
             lru_bx=lru_bx, lru_lam=lru_lam,
             cv_dw_w=cv_dw_w, cv_dw_b=cv_dw_b, cv_ln_g=cv_ln_g, cv_ln_b=cv_ln_b, cv_pw_w=cv_pw_w, cv_pw_b=cv_pw_b,
             e_w1=e_w1, e_w3=e_w3, e_w2=e_w2)
    P = _prep_params(d)
    shared = {"w_router_t": w_router.T, "b_router": b_router.reshape(N_EXPERTS, 1)}
    norm_f = norm_f_g.reshape(1, D_MODEL)

    bp, tp, _ = x_prompt.shape
    bs, ts, _ = x_sample.shape
    n_mod = 1 + bs
    mod_rows = -(-n_mod // 8) * 8
    cvec = jnp.concatenate([c_ctx[None], c, jnp.zeros((mod_rows - n_mod, D_MODEL), F32)], 0)
    mod = _ada_mod(cvec, w_ada, b_ada)

    tp_flat = min(bp * tp, 1024)
    bp_flat = bp * tp // tp_flat
    xp = x_prompt.reshape(bp_flat, tp_flat, D_MODEL)
    xs = x_sample
    zero_rw = jnp.zeros((bp, N_DIR, W_MIX, W_MIX), F32)
    zero_gla = jnp.zeros((bp, N_DIR, W_MIX, GLA_HK), F32)
    zero_lru = jnp.zeros((bp, N_DIR, W_MIX), F32)

    def layer(x, m, seq_shape, latent, s_rw, s_gla, s_lru, p, last):
        flat_shape = x.shape[:2]
        zs = _in_proj(x, m, p["norm1_g"], p["w_in"])
        za, zb, zc, zd = (z.reshape(seq_shape + (z.shape[-1],)) for z in zs)
        ya, f_rw = _rwkv_mixer(za, s_rw, p, latent)
        yb, f_gla = _gla_mixer(zb, s_gla, p)
        yc, f_lru = _lru_mixer(zc, s_lru, p)
        yd = _conv_mixer(zd, p, latent)
        ys = [y.reshape(flat_shape + (W_MIX,)) for y in (ya, yb, yc, yd)]
        return _out_moe(x, ys, m, p, norm_f, last), f_rw, f_gla, f_lru

    fin_rw, fin_gla, fin_lru = [], [], []
    for l in range(DEPTH):
        p = {n: a[l] for n, a in P.items()}
        p.update(shared)
        last = l == DEPTH - 1
        m_p = jnp.broadcast_to(mod[l, 0:1], (bp_flat, 6 * D_MODEL)).reshape(bp_flat, 1, 6 * D_MODEL)
        xp, f_rw, f_gla, f_lru = layer(xp, m_p, (bp, tp), False, zero_rw, zero_gla, zero_lru, p, last)
        fin_rw.append(_rw_state_out(f_rw))
        fin_gla.append(_gla_state_out(f_gla))
        fin_lru.append(f_lru)
        m_s = mod[l, 1:1 + bs].reshape(bs, 1, 6 * D_MODEL)
        xs, _, _, _ = layer(xs, m_s, (bs, ts), True, _rw_state_in(state_rwkv[:, l]),
                            _gla_state_in(state_gla[:, l]), state_lru[:, l], p, last)
    return (xp.reshape(bp, tp, D_MODEL), xs, jnp.stack(fin_rw, axis=1), jnp.stack(fin_gla, axis=1),
            jnp.stack(fin_lru, axis=1))
```

```python
import functools

import jax
import jax.numpy as jnp
from jax import lax
from jax.experimental import pallas as pl
from jax.experimental.pallas import tpu as pltpu

F32 = jnp.float32
BF16 = jnp.bfloat16

D_MODEL = 1024
DEPTH = 4
GRID_W = 64
N_DIR = 2
W_MIX = 256
N_HEADS = 4
HEAD = W_MIX // N_HEADS
GLA_DK = 32
GLA_HK = N_HEADS * GLA_DK
RW_LN_EPS = 64e-5
NORM_EPS = 1e-6
GLA_GATE_NORM = 16.0
LRU_C = 8.0
LRU_CONV = 4
CV_KERNEL = 31
CV_PAD = (CV_KERNEL - 1) // 2
N_EXPERTS = 16
GROUP_SIZE = 4
N_GROUPS = 4
D_EXPERT = 256
CHUNK = 64
EXP_M_HALF = 0.6065306597126334

ZA_W = 1024
ZB_W = 896
ZC_W = 512
ZD_W = 512
P_PAD = ZA_W + ZB_W + ZC_W + ZD_W

VMEM_LIMIT = 48 * 1024 * 1024


def _cparams(sem, vmem=VMEM_LIMIT):
    return pltpu.CompilerParams(dimension_semantics=sem, vmem_limit_bytes=vmem)


def _dot(a, b):
    return jnp.dot(a.astype(BF16), b.astype(BF16), preferred_element_type=F32)


def _dot_nt(a, b):
    return lax.dot_general(a.astype(BF16), b.astype(BF16), (((1,), (1,)), ((), ())),
                           preferred_element_type=F32)


def _dot_tn(a, b):
    return lax.dot_general(a.astype(BF16), b.astype(BF16), (((0,), (0,)), ((), ())),
                           preferred_element_type=F32)


def _split(a):
    hi = a.astype(BF16)
    lo = (a - hi.astype(F32)).astype(BF16)
    return hi, lo


def _dot_x2(a, b_exact):
    hi, lo = _split(a)
    bb = b_exact.astype(BF16)
    return (jnp.dot(hi, bb, preferred_element_type=F32) + jnp.dot(lo, bb, preferred_element_type=F32))


def _dot_left_x2(a_exact, b):
    hi, lo = _split(b)
    aa = a_exact.astype(BF16)
    return (jnp.dot(aa, hi, preferred_element_type=F32) + jnp.dot(aa, lo, preferred_element_type=F32))


def _dot3(a, b):
    ah, al = _split(a)
    bh, bl = _split(b)
    return (jnp.dot(ah, bh, preferred_element_type=F32) + jnp.dot(ah, bl, preferred_element_type=F32)
            + jnp.dot(al, bh, preferred_element_type=F32))


def _dot3_nt(a, b):
    ah, al = _split(a)
    bh, bl = _split(b)
    dn = (((1,), (1,)), ((), ()))
    return (lax.dot_general(ah, bh, dn, preferred_element_type=F32)
            + lax.dot_general(ah, bl, dn, preferred_element_type=F32)
            + lax.dot_general(al, bh, dn, preferred_element_type=F32))


def _iota(shape, axis):
    return lax.broadcasted_iota(jnp.int32, shape, axis)


def _block_mask(rows, cols, rblk, cblk):
    r = _iota((rows, cols), 0) // rblk
    c = _iota((rows, cols), 1) // cblk
    return (r == c).astype(F32)


def _expand(x, bm):
    xb = x.astype(BF16)
    return jnp.concatenate([xb, xb, xb, xb], axis=0) * bm


def _tri(d, strict):
    t = _iota((CHUNK, CHUNK), 0)
    s = _iota((CHUNK, CHUNK), 1)
    if d == 0:
        m = (s < t) if strict else (s <= t)
    else:
        m = (s > t) if strict else (s >= t)
    return m.astype(F32)


def _tri_wide(d, strict):
    t = _iota((CHUNK, N_HEADS * CHUNK), 0)
    s = _iota((CHUNK, N_HEADS * CHUNK), 1) % CHUNK
    if d == 0:
        m = (s < t) if strict else (s <= t)
    else:
        m = (s > t) if strict else (s >= t)
    return m.astype(F32)


ADA_TN = 1536


def _ada_kernel(c_ref, w_ref, b_ref, o_ref):
    c = c_ref[...]
    s = c * jax.nn.sigmoid(c)
    o_ref[0] = _dot3(s, w_ref[0]) + b_ref[0]


def _ada_mod(cvec, w_ada, b_ada):
    rows = cvec.shape[0]
    n_out = w_ada.shape[-1]
    return pl.pallas_call(
        _ada_kernel,
        grid=(DEPTH, n_out // ADA_TN),
        in_specs=[
            pl.BlockSpec((rows, D_MODEL), lambda l, j: (0, 0)),
            pl.BlockSpec((1, D_MODEL, ADA_TN), lambda l, j: (l, 0, j)),
            pl.BlockSpec((1, 1, ADA_TN), lambda l, j: (l, 0, j)),
        ],
        out_specs=pl.BlockSpec((1, rows, ADA_TN), lambda l, j: (l, 0, j)),
        out_shape=jax.ShapeDtypeStruct((DEPTH, rows, n_out), F32),
        compiler_params=_cparams(("arbitrary", "arbitrary")),
        name="ada_mod",
    )(cvec, w_ada, b_ada.reshape(DEPTH, 1, n_out))


IN_TM = 512


def _in_kernel(x_ref, sh_ref, sc_ref, g_ref, w_ref, za_ref, zb_ref, zc_ref, zd_ref):
    x = x_ref[0]
    h = x * lax.rsqrt(jnp.mean(x * x, axis=-1, keepdims=True) + NORM_EPS) * g_ref[...]
    h = (h * (1.0 + sc_ref[0]) + sh_ref[0]).astype(BF16)
    o0, o1, o2 = ZA_W, ZA_W + ZB_W, ZA_W + ZB_W + ZC_W
    za_ref[0] = jnp.dot(h, w_ref[:, 0:o0], preferred_element_type=F32)
    zb_ref[0] = jnp.dot(h, w_ref[:, o0:o1], preferred_element_type=F32)
    zc_ref[0] = jnp.dot(h, w_ref[:, o1:o2], preferred_element_type=F32)
    zd_ref[0] = jnp.dot(h, w_ref[:, o2:P_PAD], preferred_element_type=F32)


def _in_proj(x, m, g, w):
    b, t, _ = x.shape
    tm = min(IN_TM, t)
    tok = lambda w_: pl.BlockSpec((1, tm, w_), lambda i, j: (i, j, 0))
    mod = lambda k: pl.BlockSpec((1, 1, D_MODEL), lambda i, j, k=k: (i, 0, k))
    return pl.pallas_call(
        _in_kernel,
        grid=(b, t // tm),
        in_specs=[tok(D_MODEL), mod(0), mod(1),
                  pl.BlockSpec((1, D_MODEL), lambda i, j: (0, 0)),
                  pl.BlockSpec((D_MODEL, P_PAD), lambda i, j: (0, 0))],
        out_specs=[tok(ZA_W), tok(ZB_W), tok(ZC_W), tok(ZD_W)],
        out_shape=[jax.ShapeDtypeStruct((b, t, w_), F32) for w_ in (ZA_W, ZB_W, ZC_W, ZD_W)],
        compiler_params=_cparams(("arbitrary", "arbitrary")),
        name="in_proj",
    )(x, m, m, g, w)


SHIFT_TB = 512


def _shift_grid_kernel(z_ref, up_ref, dn_ref, mu_ref, o_ref, *, nblk):
    i = pl.program_id(1)
    z = z_ref[0]
    tb, c = z.shape
    col = _iota((tb, c), 0) % GRID_W
    lane = _iota((tb, c), 1) % 4
    left = jnp.where(col == 0, 0.0, pltpu.roll(z, 1, 0))
    right = jnp.where(col == GRID_W - 1, 0.0, pltpu.roll(z, tb - 1, 0))
    up_halo = jnp.where(i > 0, up_ref[0, 0], 0.0)
    dn_halo = jnp.where(i < nblk - 1, dn_ref[0, 0], 0.0)
    up = jnp.concatenate([up_halo, z[:tb - GRID_W]], axis=0)
    down = jnp.concatenate([z[GRID_W:], dn_halo], axis=0)
    sh = jnp.where(lane == 0, left, jnp.where(lane == 1, right, jnp.where(lane == 2, up, down)))
    o_ref[0] = z + (sh - z) * mu_ref[...]


def _shift_ctx_kernel(z_ref, mu_ref, o_ref):
    z = z_ref[0]
    t, c = z.shape
    row = _iota((t, c), 0)
    lane = _iota((t, c), 1) % 2
    prev = jnp.where(row == 0, 0.0, pltpu.roll(z, 1, 0))
    nxt = jnp.where(row == t - 1, 0.0, pltpu.roll(z, t - 1, 0))
    sh = jnp.where(lane == 0, prev, nxt)
    o_ref[0] = z + (sh - z) * mu_ref[...]


def _rw_shift(za, mu, latent):
    b, t, c = za.shape
    mu_spec = pl.BlockSpec((1, c), lambda *_: (0, 0))
    if not latent:
        return pl.pallas_call(
            _shift_ctx_kernel,
            grid=(b,),
            in_specs=[pl.BlockSpec((1, t, c), lambda i: (i, 0, 0)), mu_spec],
            out_specs=pl.BlockSpec((1, t, c), lambda i: (i, 0, 0)),
            out_shape=jax.ShapeDtypeStruct((b, t, c), F32),
            compiler_params=_cparams(("arbitrary",)),
            name="rw_shift_ctx",
        )(za, mu)
    tb = min(SHIFT_TB, t)
    nblk = t // tb
    rpb = tb // GRID_W
    nrow = t // GRID_W
    z4 = za.reshape(b, nrow, GRID_W, c)
    return pl.pallas_call(
        functools.partial(_shift_grid_kernel, nblk=nblk),
        grid=(b, nblk),
        in_specs=[
            pl.BlockSpec((1, tb, c), lambda i, j: (i, j, 0)),
            pl.BlockSpec((1, 1, GRID_W, c), lambda i, j: (i, jnp.maximum(j * rpb - 1, 0), 0, 0)),
            pl.BlockSpec((1, 1, GRID_W, c), lambda i, j: (i, jnp.minimum(j * rpb + rpb, nrow - 1), 0, 0)),
            mu_spec,
        ],
        out_specs=pl.BlockSpec((1, tb, c), lambda i, j: (i, j, 0)),
        out_shape=jax.ShapeDtypeStruct((b, t, c), F32),
        compiler_params=_cparams(("arbitrary", "arbitrary")),
        name="rw_shift_grid",
    )(za, z4, z4, mu)


SCAN_TB = 512
PREP_GROUP = 4


def _rw_scan_kernel(zf_ref, zb_ref, s0_ref, wl_ref, lb_ref, kkp_ref, kap_ref,
                    yf_ref, yb_ref, st_ref,
                    lw_s, a_s, b_s, kd_s, r2_s, mc_s, ds_s, gt_s, *, nchunk):
    i = pl.program_id(1)

    @pl.when(i == 0)
    def _():
        st_ref[...] = s0_ref[...]

    ones_blk = _block_mask(W_MIX, W_MIX, HEAD, HEAD)
    lane128 = _iota((1, 128), 1)
    z_refs = (zf_ref, zb_ref)

    for d in range(N_DIR):
        k = z_refs[d][0, :, 256:512]
        la = z_refs[d][0, :, 768:896]
        la_t = jnp.where(lane128 < 64, jnp.tanh(la), la)
        wraw = _dot(la_t, wl_ref[:, 256 * d:256 * d + 256]) + lb_ref[:, 256 * d:256 * d + 256]
        araw = _dot(la_t, wl_ref[:, 512 + 256 * d:768 + 256 * d]) + lb_ref[:, 512 + 256 * d:768 + 256 * d]
        lw_s[d] = jax.nn.sigmoid(wraw) * (-EXP_M_HALF)
        icl = jax.nn.sigmoid(araw)
        kkv = k * kkp_ref[...]
        ss = _dot_x2(kkv * kkv, ones_blk)
        kk = kkv * lax.rsqrt(ss + 1e-12)
        a_s[d] = -kk
        b_s[d] = kk * icl
        kd_s[d] = k * (1.0 + (icl - 1.0) * kap_ref[...])

    bm = _block_mask(N_HEADS * CHUNK, W_MIX, CHUNK, HEAD).astype(BF16)
    tri = [_tri(d, False) for d in range(N_DIR)]
    tw_strict = [_tri_wide(d, True) for d in range(N_DIR)]
    tw_incl = [_tri_wide(d, False) for d in range(N_DIR)]
    eye_w = (_iota((CHUNK, N_HEADS * CHUNK), 0) == _iota((CHUNK, N_HEADS * CHUNK), 1) % CHUNK).astype(F32)

    def rows_of(c, n):
        return pl.ds(pl.multiple_of(c * n, n), n)

    group = min(PREP_GROUP, nchunk)
    y_refs = (yf_ref, yb_ref)

    def prep_body(gi, carry):
        ch = [(d, gi * group + j) for j in range(group) for d in range(N_DIR)]
        each = lambda fn, *ls: [fn(*xs) for xs in zip(*ls)]
        ds_ = [d for d, _ in ch]
        load = lambda ref: [ref[d, rows_of(c, CHUNK), :] for d, c in ch]
        lw, a, b, kd = load(lw_s), load(a_s), load(b_s), load(kd_s)
        r = [z_refs[d][0, rows_of(c, CHUNK), 0:256] for d, c in ch]
        v = [z_refs[d][0, rows_of(c, CHUNK), 512:768] for d, c in ch]
        cum = each(lambda d, x: _dot_left_x2(tri[d], x), ds_, lw)
        tot = each(lambda d, x: x[CHUNK - 1:CHUNK, :] if d == 0 else x[0:1, :], ds_, cum)
        g_inv = each(lambda x: jnp.exp(-x), cum)
        g_rem = each(lambda t_, x: jnp.exp(t_ - x), tot, cum)
        at = each(lambda a_, x, l: a_ * jnp.exp(x - l), a, cum, lw)
        rt = each(lambda r_, x: r_ * jnp.exp(x), r, cum)
        x = each(lambda p, q: jnp.concatenate([p, q], axis=0), at, rt)
        gb = each(lambda x_, b_, g: _dot_nt(x_, _expand(b_ * g, bm)), x, b, g_inv)
        gk = each(lambda x_, k_, g: _dot_nt(x_, _expand(k_ * g, bm)), x, kd, g_inv)
        vbd = each(lambda v_: _expand(v_, bm), v)
        a_ab = each(lambda d, g: g[0:CHUNK] * tw_strict[d], ds_, gb)
        av = each(lambda d, g, vb: _dot(g[0:CHUNK] * tw_strict[d], vb), ds_, gk, vbd)
        tw = each(lambda m: eye_w + m, a_ab)
        apow = a_ab
        for _ in range(5):
            apow = each(lambda m: _dot(m, _expand(m, bm)), apow)
            tw = each(lambda t_, m: t_ + _dot(m, _expand(t_, bm)), tw, apow)
        a2 = each(lambda t_, m: _dot(t_, _expand(m, bm)), tw, at)
        u0 = each(lambda t_, m: _dot(t_, _expand(m, bm)), tw, av)
        r_b = each(lambda d, g: g[CHUNK:] * tw_incl[d], ds_, gb)
        r2 = each(lambda rt_, rb, m: rt_ + _dot(rb, _expand(m, bm)), rt, r_b, a2)
        y0 = each(lambda d, rb, u, g, vb: _dot(rb, _expand(u, bm)) + _dot(g[CHUNK:] * tw_incl[d], vb),
                  ds_, r_b, u0, gk, vbd)
        bl = each(lambda b_, g: b_ * g, b, g_rem)
        kl = each(lambda k_, g: k_ * g, kd, g_rem)
        mc = each(lambda m, bl_: _dot_tn(m, bl_) * ones_blk, a2, bl)
        ds0 = each(lambda u, v_, bl_, kl_: _dot_tn(jnp.concatenate([u, v_], axis=0),
                                                   jnp.concatenate([bl_, kl_], axis=0)) * ones_blk,
                   u0, v, bl, kl)
        for n, (d, c) in enumerate(ch):
            rows = rows_of(c, CHUNK)
            y_refs[d][0, rows, :] = y0[n]
            r2_s[d, rows, :] = r2[n].astype(BF16)
            mc_s[d, rows_of(c, W_MIX), :] = mc[n].astype(BF16)
            ds_s[d, rows_of(c, W_MIX), :] = ds0[n]
            gt_s[d, rows_of(c, 8), :] = jnp.broadcast_to(jnp.exp(tot[n]), (8, W_MIX))
        return carry

    lax.fori_loop(0, nchunk // group, prep_body, 0)

    def scan_body(cc, carry):
        cs = [cc, nchunk - 1 - cc]
        s = [st_ref[0, d] for d in range(N_DIR)]
        sb = [x.astype(BF16) for x in s]
        upd = [_dot(sb[d], mc_s[d, rows_of(cs[d], W_MIX), :]) for d in range(N_DIR)]
        ys = [_dot_nt(r2_s[d, rows_of(cs[d], CHUNK), :], sb[d]) for d in range(N_DIR)]
        for d in range(N_DIR):
            st_ref[0, d] = (s[d] * gt_s[d, rows_of(cs[d], 8), :][0:1, :] + upd[d]
                            + ds_s[d, rows_of(cs[d], W_MIX), :])
            y_refs[d][0, rows_of(cs[d], CHUNK), :] += ys[d]
        return carry

    lax.fori_loop(0, nchunk, scan_body, 0)


def _rw_scan(zs, s0, wl, lb, kkp, kap):
    b, t, c = zs.shape
    tb = min(SCAN_TB, t)
    nblk = t // tb
    nchunk = tb // CHUNK
    vec = lambda n: pl.BlockSpec((1, n), lambda i, j: (0, 0))
    st_spec = pl.BlockSpec((1, N_DIR, W_MIX, W_MIX), lambda i, j: (i, 0, 0, 0))
    return pl.pallas_call(
        functools.partial(_rw_scan_kernel, nchunk=nchunk),
        grid=(b, nblk),
        in_specs=[
            pl.BlockSpec((1, tb, c), lambda i, j: (i, j, 0)),
            pl.BlockSpec((1, tb, c), lambda i, j: (i, nblk - 1 - j, 0)),
            st_spec,
            pl.BlockSpec((128, 1024), lambda i, j: (0, 0)),
            vec(1024), vec(W_MIX), vec(W_MIX),
        ],
        out_specs=[
            pl.BlockSpec((1, tb, W_MIX), lambda i, j: (i, j, 0)),
            pl.BlockSpec((1, tb, W_MIX), lambda i, j: (i, nblk - 1 - j, 0)),
            st_spec,
        ],
        out_shape=[
            jax.ShapeDtypeStruct((b, t, W_MIX), F32),
            jax.ShapeDtypeStruct((b, t, W_MIX), F32),
            jax.ShapeDtypeStruct((b, N_DIR, W_MIX, W_MIX), F32),
        ],
        scratch_shapes=[pltpu.VMEM((N_DIR, tb, W_MIX), F32) for _ in range(4)] + [
            pltpu.VMEM((N_DIR, tb, W_MIX), BF16),
            pltpu.VMEM((N_DIR, nchunk * W_MIX, W_MIX), BF16),
            pltpu.VMEM((N_DIR, nchunk * W_MIX, W_MIX), F32),
            pltpu.VMEM((N_DIR, nchunk * 8, W_MIX), F32),
        ],
        compiler_params=_cparams(("arbitrary", "arbitrary")),
        name="rw_scan",
    )(zs, zs, s0, wl, lb, kkp, kap)


EPI_TM = 512


def _rw_epi_kernel(yf_ref, yb_ref, z_ref, lng_ref, lnb_ref, rk_ref, g2_ref, o_ref):
    ones_blk = _block_mask(W_MIX, W_MIX, HEAD, HEAD)
    y = yf_ref[0] + yb_ref[0]
    mu = _dot_x2(y, ones_blk) * (1.0 / HEAD)
    yc = y - mu
    var = _dot_x2(yc * yc, ones_blk) * (1.0 / HEAD)
    yn = yc * lax.rsqrt(var + RW_LN_EPS) * lng_ref[...] + lnb_ref[...]
    z = z_ref[0]
    r = z[:, 0:256]
    k = z[:, 256:512]
    v = z[:, 512:768]
    gl = z[:, 896:1024]
    bonus = _dot_x2(r * k * rk_ref[...], ones_blk) * v
    gate = _dot(jax.nn.sigmoid(gl), g2_ref[...])
    o_ref[0] = (yn + bonus) * gate


def _rw_epi(yf, yb, zs, lng, lnb, rk, g2p):
    b, t, _ = yf.shape
    tm = min(EPI_TM, t)
    tok = lambda w_: pl.BlockSpec((1, tm, w_), lambda i, j: (i, j, 0))
    vec = pl.BlockSpec((1, W_MIX), lambda i, j: (0, 0))
    return pl.pallas_call(
        _rw_epi_kernel,
        grid=(b, t // tm),
        in_specs=[tok(W_MIX), tok(W_MIX), tok(ZA_W), vec, vec, vec,
                  pl.BlockSpec((128, W_MIX), lambda i, j: (0, 0))],
        out_specs=tok(W_MIX),
        out_shape=jax.ShapeDtypeStruct((b, t, W_MIX), F32),
        compiler_params=_cparams(("arbitrary", "arbitrary")),
        name="rw_epi",
    )(yf, yb, zs, lng, lnb, rk, g2p)


def _rwkv_mixer(za, s0_bd, p, latent):
    zs = _rw_shift(za, p["rw_mu"], latent)
    yf, yb, st = _rw_scan(zs, s0_bd, p["rw_wl"], p["rw_lb"], p["rw_kk"], p["rw_ka"])
    ya = _rw_epi(yf, yb, zs, p["rw_ln_g"], p["rw_ln_b"], p["rw_rk"], p["rw_g2"])
    return ya, st


def _gla_scan_kernel(zf_ref, zb_ref, s0_ref, wg_ref, gb_ref, yf_ref, yb_ref, st_ref,
                     la_s, qe_s, ds_s, dec_s, *, nchunk):
    i = pl.program_id(1)

    @pl.when(i == 0)
    def _():
        st_ref[...] = s0_ref[...]

    z_refs = (zf_ref, zb_ref)
    y_refs = (yf_ref, yb_ref)
    for d in range(N_DIR):
        logit = _dot(z_refs[d][0, :, 768:896], wg_ref[d]) + gb_ref[d]
        la_s[d] = jax.nn.log_sigmoid(logit) * (1.0 / GLA_GATE_NORM)

    bm_k = _block_mask(N_HEADS * CHUNK, GLA_HK, CHUNK, GLA_DK).astype(BF16)
    bm_v = _block_mask(N_HEADS * CHUNK, W_MIX, CHUNK, HEAD).astype(BF16)
    bm_s = _block_mask(W_MIX, GLA_HK, HEAD, GLA_DK)
    tri = [_tri(d, False) for d in range(N_DIR)]
    tw_incl = [_tri_wide(d, False) for d in range(N_DIR)]

    def rows_of(c, n):
        return pl.ds(pl.multiple_of(c * n, n), n)

    group = min(PREP_GROUP, nchunk)

    def prep_body(gi, carry):
        ch = [(d, gi * group + j) for j in range(group) for d in range(N_DIR)]
        each = lambda fn, *ls: [fn(*xs) for xs in zip(*ls)]
        ds_ = [d for d, _ in ch]
        la = [la_s[d, rows_of(c, CHUNK), :] for d, c in ch]
        q = [z_refs[d][0, rows_of(c, CHUNK), 0:128] * (GLA_DK ** -0.5) for d, c in ch]
        k = [z_refs[d][0, rows_of(c, CHUNK), 128:256] for d, c in ch]
        v = [z_refs[d][0, rows_of(c, CHUNK), 256:512] for d, c in ch]
        cum = each(lambda d, x: _dot_left_x2(tri[d], x), ds_, la)
        last = each(lambda d, x: x[CHUNK - 1:CHUNK, :] if d == 0 else x[0:1, :], ds_, cum)
        qe = each(lambda q_, x: q_ * jnp.exp(x), q, cum)
        ke = each(lambda k_, x: k_ * jnp.exp(-x), k, cum)
        kl = each(lambda k_, l, x: k_ * jnp.exp(l - x), k, last, cum)
        att = each(lambda d, q_, k_: _dot_nt(q_, _expand(k_, bm_k)) * tw_incl[d], ds_, qe, ke)
        o = each(lambda a_, v_: _dot(a_, _expand(v_, bm_v)), att, v)
        dst = each(lambda v_, k_: _dot_tn(v_, k_) * bm_s, v, kl)
        for n, (d, c) in enumerate(ch):
            y_refs[d][0, rows_of(c, CHUNK), :] = o[n]
            qe_s[d, rows_of(c, CHUNK), :] = qe[n].astype(BF16)
            ds_s[d, rows_of(c, W_MIX), :] = dst[n]
            dec_s[d, rows_of(c, 8), :] = jnp.broadcast_to(jnp.exp(last[n]), (8, GLA_HK))
        return carry

    lax.fori_loop(0, nchunk // group, prep_body, 0)

    def scan_body(cc, carry):
        cs = [cc, nchunk - 1 - cc]
        for d in range(N_DIR):
            s = st_ref[0, d]
            y_refs[d][0, rows_of(cs[d], CHUNK), :] += _dot_nt(qe_s[d, rows_of(cs[d], CHUNK), :], s)
            st_ref[0, d] = s * dec_s[d, rows_of(cs[d], 8), :][0:1, :] + ds_s[d, rows_of(cs[d], W_MIX), :]
        return carry

    lax.fori_loop(0, nchunk, scan_body, 0)


def _gla_scan(zb, s0, wg, gb):
    b, t, c = zb.shape
    tb = min(SCAN_TB, t)
    nblk = t // tb
    st_spec = pl.BlockSpec((1, N_DIR, W_MIX, GLA_HK), lambda i, j: (i, 0, 0, 0))
    nchunk = tb // CHUNK
    return pl.pallas_call(
        functools.partial(_gla_scan_kernel, nchunk=nchunk),
        grid=(b, nblk),
        in_specs=[
            pl.BlockSpec((1, tb, c), lambda i, j: (i, j, 0)),
            pl.BlockSpec((1, tb, c), lambda i, j: (i, nblk - 1 - j, 0)),
            st_spec,
            pl.BlockSpec((N_DIR, 128, GLA_HK), lambda i, j: (0, 0, 0)),
            pl.BlockSpec((N_DIR, 1, GLA_HK), lambda i, j: (0, 0, 0)),
        ],
        out_specs=[
            pl.BlockSpec((1, tb, W_MIX), lambda i, j: (i, j, 0)),
            pl.BlockSpec((1, tb, W_MIX), lambda i, j: (i, nblk - 1 - j, 0)),
            st_spec,
        ],
        out_shape=[
            jax.ShapeDtypeStruct((b, t, W_MIX), F32),
            jax.ShapeDtypeStruct((b, t, W_MIX), F32),
            jax.ShapeDtypeStruct((b, N_DIR, W_MIX, GLA_HK), F32),
        ],
        scratch_shapes=[
            pltpu.VMEM((N_DIR, tb, GLA_HK), F32),
            pltpu.VMEM((N_DIR, tb, GLA_HK), BF16),
            pltpu.VMEM((N_DIR, nchunk * W_MIX, GLA_HK), F32),
            pltpu.VMEM((N_DIR, nchunk * 8, GLA_HK), F32),
        ],
        compiler_params=_cparams(("arbitrary", "arbitrary")),
        name="gla_scan",
    )(zb, zb, s0, wg, gb)


def _gla_epi_kernel(yf_ref, yb_ref, z_ref, g_ref, o_ref):
    ones_blk = _block_mask(W_MIX, W_MIX, HEAD, HEAD)
    y = yf_ref[0] + yb_ref[0]
    ms = _dot_x2(y * y, ones_blk) * (1.0 / HEAD)
    y = y * lax.rsqrt(ms + NORM_EPS) * g_ref[...]
    og = z_ref[0, :, 512:768]
    o_ref[0] = y * (og * jax.nn.sigmoid(og))


def _gla_epi(yf, yb, zb, g):
    b, t, _ = yf.shape
    tm = min(EPI_TM, t)
    tok = lambda w_: pl.BlockSpec((1, tm, w_), lambda i, j: (i, j, 0))
    return pl.pallas_call(
        _gla_epi_kernel,
        grid=(b, t // tm),
        in_specs=[tok(W_MIX), tok(W_MIX), tok(ZB_W), pl.BlockSpec((1, W_MIX), lambda i, j: (0, 0))],
        out_specs=tok(W_MIX),
        out_shape=jax.ShapeDtypeStruct((b, t, W_MIX), F32),
        compiler_params=_cparams(("arbitrary", "arbitrary")),
        name="gla_epi",
    )(yf, yb, zb, g)


def _gla_mixer(zb, s0_bd, p):
    yf, yb, st = _gla_scan(zb, s0_bd, p["gla_wg"], p["gla_gb"])
    return _gla_epi(yf, yb, zb, p["gla_ln_g"]), st


LRU_TT = 256
LRU_HALO = 8


def _lru_kernel(z_ref, h0_ref, cw_ref, cb_ref, wax_ref, bax_ref, lam_ref, y_ref, hf_ref, xpad, *, t):
    tt = min(LRU_TT, t)
    ntile = t // tt
    xpad[0:LRU_HALO, :] = jnp.zeros((LRU_HALO, W_MIX), F32)
    xpad[LRU_HALO + t:2 * LRU_HALO + t, :] = jnp.zeros((LRU_HALO, W_MIX), F32)

    def fill(j, carry):
        base = pl.multiple_of(j * tt, tt)
        xpad[pl.ds(base + LRU_HALO, tt), :] = z_ref[0, pl.ds(base, tt), 0:W_MIX]
        return carry

    lax.fori_loop(0, ntile, fill, 0)
    row = _iota((tt, W_MIX), 0)

    def tile_scan(j, h, d):
        base = pl.multiple_of(j * tt, tt)
        win = xpad[pl.ds(base, tt + 2 * LRU_HALO), :]
        xc = jnp.zeros((tt, W_MIX), F32) + cb_ref[d]
        for tap in range(LRU_CONV):
            off = LRU_HALO - (LRU_CONV - 1) + tap if d == 0 else LRU_HALO + (LRU_CONV - 1) - tap
            xc = xc + cw_ref[d, tap:tap + 1, :] * win[off:off + tt, :]
        g = _dot(xc, wax_ref[d]) + bax_ref[d]
        gr = jax.nn.sigmoid(g[:, 0:W_MIX])
        gi = jax.nn.sigmoid(g[:, W_MIX:2 * W_MIX])
        log_a = -LRU_C * gr * jax.nn.softplus(-lam_ref[d])
        a = jnp.exp(log_a)
        bv = jnp.sqrt(1.0 - jnp.exp(2.0 * log_a)) * gi * xc
        s = 1
        while s < tt:
            if d == 0:
                keep = row >= s
                a_sh = jnp.where(keep, pltpu.roll(a, s, 0), 1.0)
                b_sh = jnp.where(keep, pltpu.roll(bv, s, 0), 0.0)
            else:
                keep = row < tt - s
                a_sh = jnp.where(keep, pltpu.roll(a, tt - s, 0), 1.0)
                b_sh = jnp.where(keep, pltpu.roll(bv, tt - s, 0), 0.0)
            bv = a * b_sh + bv
            a = a * a_sh
            s *= 2
        return a * h + bv, base

    def fwd(j, h):
        ht, base = tile_scan(j, h, 0)
        y_ref[0, pl.ds(base, tt), :] = ht
        return ht[tt - 1:tt, :]

    h_end = lax.fori_loop(0, ntile, fwd, h0_ref[0, 0:1, :])
    hf_ref[0, 0:1, :] = h_end

    def bwd(jj, h):
        j = ntile - 1 - jj
        ht, base = tile_scan(j, h, 1)
        gb = z_ref[0, pl.ds(base, tt), W_MIX:2 * W_MIX]
        y_ref[0, pl.ds(base, tt), :] = (y_ref[0, pl.ds(base, tt), :] + ht) * jax.nn.gelu(gb)
        return ht[0:1, :]

    h_end = lax.fori_loop(0, ntile, bwd, h0_ref[0, 1:2, :])
    hf_ref[0, 1:2, :] = h_end


def _lru_mixer(zc, h0, p):
    b, t, c = zc.shape
    full = lambda *s: pl.BlockSpec(s, lambda i: (0,) * len(s))
    return pl.pallas_call(
        functools.partial(_lru_kernel, t=t),
        grid=(b,),
        in_specs=[
            pl.BlockSpec((1, t, c), lambda i: (i, 0, 0)),
            pl.BlockSpec((1, N_DIR, W_MIX), lambda i: (i, 0, 0)),
            full(N_DIR, LRU_CONV, W_MIX), full(N_DIR, 1, W_MIX),
            full(N_DIR, W_MIX, 2 * W_MIX), full(N_DIR, 1, 2 * W_MIX), full(N_DIR, 1, W_MIX),
        ],
        out_specs=[pl.BlockSpec((1, t, W_MIX), lambda i: (i, 0, 0)),
                   pl.BlockSpec((1, N_DIR, W_MIX), lambda i: (i, 0, 0))],
        out_shape=[jax.ShapeDtypeStruct((b, t, W_MIX), F32), jax.ShapeDtypeStruct((b, N_DIR, W_MIX), F32)],
        scratch_shapes=[pltpu.VMEM((t + 2 * LRU_HALO, W_MIX), F32)],
        compiler_params=_cparams(("arbitrary",)),
        name="lru",
    )(zc, h0, p["lru_cw"], p["lru_cb"], p["lru_wax"], p["lru_bax"], p["lru_lam"])


CV_TT = 256
CV_WIN = 16


def _conv_kernel(z_ref, dw_ref, dwb_ref, lng_ref, lnb_ref, pw_ref, pwb_ref, y_ref, upad, *, t, latent):
    tt = min(CV_TT, t)
    ntile = t // tt
    pad = CV_PAD * GRID_W if latent else CV_WIN
    upad[0:pad, :] = jnp.zeros((pad, W_MIX), F32)
    upad[pad + t:2 * pad + t, :] = jnp.zeros((pad, W_MIX), F32)

    def fill(j, carry):
        base = pl.multiple_of(j * tt, tt)
        z = z_ref[0, pl.ds(base, tt), :]
        upad[pl.ds(base + pad, tt), :] = z[:, 0:W_MIX] * jax.nn.sigmoid(z[:, W_MIX:2 * W_MIX])
        return carry

    lax.fori_loop(0, ntile, fill, 0)

    def tile(j, carry):
        base = pl.multiple_of(j * tt, tt)
        if latent:
            half = W_MIX // 2
            col = _iota((tt, half), 0) % GRID_W
            win = upad[pl.ds(pl.multiple_of(base + pad - CV_WIN, CV_WIN), tt + 2 * CV_WIN), 0:half]
            accw = jnp.zeros((tt, half), F32)
            acch = jnp.zeros((tt, half), F32)
            for tap in range(CV_KERNEL):
                dlt = tap - CV_PAD
                x = win[CV_WIN + dlt:CV_WIN + dlt + tt, :]
                if dlt < 0:
                    x = jnp.where(col >= -dlt, x, 0.0)
                elif dlt > 0:
                    x = jnp.where(col < GRID_W - dlt, x, 0.0)
                accw = accw + dw_ref[tap:tap + 1, 0:half] * x
                rows = pl.ds(pl.multiple_of(base + pad + dlt * GRID_W, GRID_W), tt)
                acch = acch + dw_ref[tap:tap + 1, half:W_MIX] * upad[rows, half:W_MIX]
            u = jnp.concatenate([accw, acch], axis=1)
        else:
            win = upad[pl.ds(base + pad - CV_WIN, tt + 2 * CV_WIN), :]
            u = jnp.zeros((tt, W_MIX), F32)
            for tap in range(CV_KERNEL):
                dlt = tap - CV_PAD
                u = u + dw_ref[tap:tap + 1, :] * win[CV_WIN + dlt:CV_WIN + dlt + tt, :]
        u = u + dwb_ref[...]
        mu = jnp.mean(u, axis=-1, keepdims=True)
        uc = u - mu
        var = jnp.mean(uc * uc, axis=-1, keepdims=True)
        un = uc * lax.rsqrt(var + 1e-5) * lng_ref[...] + lnb_ref[...]
        un = un * jax.nn.sigmoid(un)
        y_ref[0, pl.ds(base, tt), :] = _dot(un, pw_ref[...]) + pwb_ref[...]
        return carry

    lax.fori_loop(0, ntile, tile, 0)


def _conv_mixer(zd, p, latent):
    b, t, c = zd.shape
    pad = CV_PAD * GRID_W if latent else CV_WIN
    full = lambda *s: pl.BlockSpec(s, lambda i: (0,) * len(s))
    vec = full(1, W_MIX)
    return pl.pallas_call(
        functools.partial(_conv_kernel, t=t, latent=latent),
        grid=(b,),
        in_specs=[pl.BlockSpec((1, t, c), lambda i: (i, 0, 0)),
                  full(CV_KERNEL, W_MIX), vec, vec, vec, full(W_MIX, W_MIX), vec],
        out_specs=pl.BlockSpec((1, t, W_MIX), lambda i: (i, 0, 0)),
        out_shape=jax.ShapeDtypeStruct((b, t, W_MIX), F32),
        scratch_shapes=[pltpu.VMEM((t + 2 * pad, W_MIX), F32)],
        compiler_params=_cparams(("arbitrary",)),
        name="conv_grid" if latent else "conv_ctx",
    )(zd, p["cv_dw_w"], p["cv_dw_b"], p["cv_ln_g"], p["cv_ln_b"], p["cv_pw_w"], p["cv_pw_b"])


MOE_TM = 512
MOE_VMEM = VMEM_LIMIT
MOE_RB = 128
E_PAD = 128


def _route(sel, scores):
    grp = []
    for g in range(N_GROUPS):
        s = sel[GROUP_SIZE * g:GROUP_SIZE * (g + 1)]
        best_pair = None
        for i in range(GROUP_SIZE):
            for j in range(i + 1, GROUP_SIZE):
                pair = s[i] + s[j]
                best_pair = pair if best_pair is None else jnp.maximum(best_pair, pair)
        grp.append(best_pair)
    best = jnp.zeros_like(grp[0], dtype=jnp.int32)
    top = grp[0]
    for g in range(1, N_GROUPS):
        better = grp[g] > top
        best = jnp.where(better, g, best)
        top = jnp.where(better, grp[g], top)
    neg = jnp.full_like(sel[0], -jnp.inf)
    msel = [jnp.where(best == e // GROUP_SIZE, sel[e], neg) for e in range(N_EXPERTS)]
    picks = []
    for _ in range(2):
        idx = jnp.zeros_like(best)
        top = msel[0]
        for e in range(1, N_EXPERTS):
            better = msel[e] > top
            idx = jnp.where(better, e, idx)
            top = jnp.where(better, msel[e], top)
        picks.append(idx)
        msel = [jnp.where(idx == e, neg, msel[e]) for e in range(N_EXPERTS)]
    chosen = [jnp.where((picks[0] == e) | (picks[1] == e), scores[e], 0.0) for e in range(N_EXPERTS)]
    total = chosen[0]
    for e in range(1, N_EXPERTS):
        total = total + chosen[e]
    return [ch / total for ch in chosen], best


def _moe_kernel(x_ref, ya_ref, yb_ref, yc_ref, yd_ref, wo_ref, g1_ref, sh2_ref, sc2_ref, g2_ref, n2_ref,
                wr_ref, br_ref, w13_ref, w2_ref, nf_ref, o_ref,
                x1_s, h2_s, gt_s, gate_s, acc_s, he_s, og_s, pmt_s, seg_s, *, final_norm):
    grp = pl.program_id(2)
    tm = x1_s.shape[0]

    @pl.when(grp == 0)
    def _():
        y = (_dot(ya_ref[0], wo_ref[0:256, :]) + _dot(yb_ref[0], wo_ref[256:512, :])
             + _dot(yc_ref[0], wo_ref[512:768, :]) + _dot(yd_ref[0], wo_ref[768:1024, :]))
        x1 = x_ref[0] + g1_ref[0] * y
        x1_s[...] = x1
        h2 = x1 * lax.rsqrt(jnp.mean(x1 * x1, axis=-1, keepdims=True) + NORM_EPS) * n2_ref[...]
        h2 = h2 * (1.0 + sc2_ref[0]) + sh2_ref[0]
        logits = _dot3_nt(wr_ref[...], h2)
        scores = jax.nn.sigmoid(logits)
        selm = scores + br_ref[...]
        gates, best = _route([selm[i:i + 1, :] for i in range(N_EXPERTS)],
                             [scores[i:i + 1, :] for i in range(N_EXPERTS)])
        gt_s[...] = jnp.zeros(gt_s.shape, F32)
        for i in range(N_EXPERTS):
            gt_s[i:i + 1, :] = gates[i]

        og = [jnp.where(best == g, 1.0, 0.0) for g in range(N_GROUPS)]
        og_s[...] = jnp.zeros(og_s.shape, F32)
        start = jnp.int32(0)
        starts = []
        for g in range(N_GROUPS):
            og_s[g:g + 1, :] = og[g]
            starts.append(start)
            seg_s[g] = start
            start = start + jnp.sum(og[g]).astype(jnp.int32)
            seg_s[N_GROUPS + g] = start
        before = jnp.where(_iota((tm, tm), 0) < _iota((tm, tm), 1), 1.0, 0.0).astype(BF16)
        rank = jnp.dot(og_s[...].astype(BF16), before, preferred_element_type=F32)
        pos = og[0] * (starts[0].astype(F32) + rank[0:1, :])
        for g in range(1, N_GROUPS):
            pos = pos + og[g] * (starts[g].astype(F32) + rank[g:g + 1, :])
        gt_s[N_EXPERTS:N_EXPERTS + 1, :] = pos
        gate_tok = gt_s[...].T
        slot_l = _iota((tm, tm), 1).astype(F32)
        slot_s = _iota((tm, tm), 0).astype(F32)
        pmt_s[...] = jnp.where(slot_l == gate_tok[:, N_EXPERTS:N_EXPERTS + 1], 1.0, 0.0).astype(BF16)
        pm = jnp.where(slot_s == pos, 1.0, 0.0).astype(BF16)
        h2_s[...] = jnp.dot(pm, h2.astype(BF16), preferred_element_type=F32).astype(BF16)
        gate_s[...] = _dot_left_x2(pm, gate_tok)
        acc_s[...] = jnp.zeros(acc_s.shape, F32)

    pick = (_iota((E_PAD, GROUP_SIZE * 128), 0)
            == GROUP_SIZE * grp + _iota((E_PAD, GROUP_SIZE * 128), 1) // 128).astype(F32)

    def block(rb, carry):
        rows = pl.ds(pl.multiple_of(rb * MOE_RB, MOE_RB), MOE_RB)
        gsel = _dot_x2(gate_s[rows, :], pick)
        h2 = h2_s[rows, :]
        for j in range(GROUP_SIZE):
            hh = jnp.dot(h2, w13_ref[0, :, 2 * D_EXPERT * j:2 * D_EXPERT * (j + 1)], preferred_element_type=F32)
            he = hh[:, 0:D_EXPERT]
            he = he * jax.nn.sigmoid(he) * hh[:, D_EXPERT:2 * D_EXPERT]
            g = gsel[:, 128 * j:128 * (j + 1)]
            he_s[rows, D_EXPERT * j:D_EXPERT * (j + 1)] = (he * jnp.concatenate([g, g], axis=1)).astype(BF16)
        acc_s[rows, :] += jnp.dot(he_s[rows, :], w2_ref[0], preferred_element_type=F32)
        return carry

    first = seg_s[grp] // MOE_RB
    last = (seg_s[N_GROUPS + grp] + (MOE_RB - 1)) // MOE_RB
    lax.fori_loop(first, last, block, 0)

    @pl.when(grp == N_GROUPS - 1)
    def _():
        x2 = x1_s[...] + g2_ref[0] * _dot_left_x2(pmt_s[...], acc_s[...])
        if final_norm:
            x2 = x2 * lax.rsqrt(jnp.mean(x2 * x2, axis=-1, keepdims=True) + NORM_EPS) * nf_ref[...]
        o_ref[0] = x2


def _out_moe(x, ys, m, p, norm_f, final_norm):
    b, t, _ = x.shape
    tm = min(MOE_TM, t)
    tok = lambda w_: pl.BlockSpec((1, tm, w_), lambda i, j, e: (i, j, 0))
    mod = lambda k: pl.BlockSpec((1, 1, D_MODEL), lambda i, j, e, k=k: (i, 0, k))
    full = lambda *s: pl.BlockSpec(s, lambda i, j, e: (0,) * len(s))
    return pl.pallas_call(
        functools.partial(_moe_kernel, final_norm=final_norm),
        grid=(b, t // tm, N_GROUPS),
        in_specs=[tok(D_MODEL), tok(W_MIX), tok(W_MIX), tok(W_MIX), tok(W_MIX),
                  full(D_MODEL, D_MODEL), mod(2), mod(3), mod(4), mod(5), full(1, D_MODEL),
                  full(N_EXPERTS, D_MODEL), full(N_EXPERTS, 1),
                  pl.BlockSpec((1, D_MODEL, 2 * D_EXPERT * GROUP_SIZE), lambda i, j, e: (e, 0, 0)),
                  pl.BlockSpec((1, D_EXPERT * GROUP_SIZE, D_MODEL), lambda i, j, e: (e, 0, 0)),
                  full(1, D_MODEL)],
        out_specs=tok(D_MODEL),
        out_shape=jax.ShapeDtypeStruct((b, t, D_MODEL), F32),
        scratch_shapes=[pltpu.VMEM((tm, D_MODEL), F32), pltpu.VMEM((tm, D_MODEL), BF16),
                        pltpu.VMEM((E_PAD, tm), F32), pltpu.VMEM((tm, E_PAD), F32),
                        pltpu.VMEM((tm, D_MODEL), F32), pltpu.VMEM((tm, D_EXPERT * GROUP_SIZE), BF16),
                        pltpu.VMEM((8, tm), F32), pltpu.VMEM((tm, tm), BF16),
                        pltpu.SMEM((2 * N_GROUPS,), jnp.int32)],
        compiler_params=_cparams(("arbitrary", "arbitrary", "arbitrary"), MOE_VMEM),
        name="out_moe",
    )(x, *ys, p["w_out"], m, m, m, m, p["norm2_g"], p["w_router_t"], p["b_router"], p["e_w13g"], p["e_w2g"],
      norm_f)


def _gla_pack(z):
    hk = GLA_HK
    lead = z.shape[:-1]
    gl = z[..., 2 * hk + W_MIX:2 * hk + W_MIX + 32]
    return jnp.concatenate([z[..., 0:2 * hk + W_MIX], z[..., 2 * hk + W_MIX + 32:], gl,
                            jnp.zeros(lead + (96,), z.dtype)], -1)


def _gla_state_in(s):
    b = s.shape[0]
    eye = jnp.eye(N_HEADS, dtype=s.dtype)
    return jnp.einsum("bdhkv,hg->bdhvgk", s, eye).reshape(b, N_DIR, W_MIX, GLA_HK)


def _gla_state_out(st):
    b = st.shape[0]
    eye = jnp.eye(N_HEADS, dtype=st.dtype)
    return jnp.einsum("bdhvgk,hg->bdhkv", st.reshape(b, N_DIR, N_HEADS, HEAD, N_HEADS, GLA_DK), eye)


def _rw_state_in(s):
    b = s.shape[0]
    eye = jnp.eye(N_HEADS, dtype=s.dtype)
    return jnp.einsum("bdhvk,hg->bdhvgk", s, eye).reshape(b, N_DIR, W_MIX, W_MIX)


def _rw_state_out(st):
    b = st.shape[0]
    eye = jnp.eye(N_HEADS, dtype=st.dtype)
    return jnp.einsum("bdhvgk,hg->bdhvk", st.reshape(b, N_DIR, N_HEADS, HEAD, N_HEADS, HEAD), eye)


def _prep_params(d):
    L = DEPTH
    z = lambda *s: jnp.zeros(s, F32)
    out = {}
    out["rw_mu"] = jnp.concatenate([d["rw_mu"], z(L, ZA_W - d["rw_mu"].shape[-1])], -1).reshape(L, 1, ZA_W)
    w2, a2 = d["rw_w2"], d["rw_a2"]
    zz = z(L, 32, 256)
    rows = [
        jnp.concatenate([w2[:, 0], zz, zz, zz], -1),
        jnp.concatenate([zz, w2[:, 1], zz, zz], -1),
        jnp.concatenate([zz, zz, a2[:, 0], zz], -1),
        jnp.concatenate([zz, zz, zz, a2[:, 1]], -1),
    ]
    out["rw_wl"] = jnp.concatenate(rows, 1).astype(BF16)
    out["rw_lb"] = jnp.concatenate([d["rw_w0"][:, 0], d["rw_w0"][:, 1], d["rw_a0"][:, 0], d["rw_a0"][:, 1]],
                                   -1).reshape(L, 1, 1024)
    for n in ("rw_kk", "rw_ka", "rw_ln_g", "rw_ln_b"):
        out[n] = d[n].reshape(L, 1, W_MIX)
    out["rw_rk"] = d["rw_rk"].reshape(L, 1, W_MIX)
    out["rw_g2"] = jnp.concatenate([d["rw_g2"], z(L, 64, W_MIX)], 1).astype(BF16)

    gk2 = d["gla_gk2"]
    z16, z96 = z(L, 16, GLA_HK), z(L, 96, GLA_HK)
    out["gla_wg"] = jnp.stack([jnp.concatenate([gk2[:, 0], z16, z96], 1),
                               jnp.concatenate([z16, gk2[:, 1], z96], 1)], 1).astype(BF16)
    out["gla_gb"] = d["gla_gkb"].reshape(L, N_DIR, 1, GLA_HK)
    out["gla_ln_g"] = jnp.tile(d["gla_ln_g"], (1, N_HEADS)).reshape(L, 1, W_MIX)

    eye = jnp.eye(4, dtype=F32)
    bd = lambda w: jnp.einsum("ldgij,gh->ldgihj", w, eye).reshape(L, N_DIR, W_MIX, W_MIX)
    out["lru_cw"] = d["lru_conv_w"]
    out["lru_cb"] = d["lru_conv_b"].reshape(L, N_DIR, 1, W_MIX)
    out["lru_wax"] = jnp.concatenate([bd(d["lru_wa"]), bd(d["lru_wx"])], -1).astype(BF16)
    out["lru_bax"] = jnp.concatenate([d["lru_ba"], d["lru_bx"]], -1).reshape(L, N_DIR, 1, 2 * W_MIX)
    out["lru_lam"] = d["lru_lam"].reshape(L, N_DIR, 1, W_MIX)

    out["cv_dw_w"] = d["cv_dw_w"]
    for n in ("cv_dw_b", "cv_ln_g", "cv_ln_b", "cv_pw_b"):
        out[n] = d[n].reshape(L, 1, W_MIX)
    out["cv_pw_w"] = d["cv_pw_w"].astype(BF16)

    if "w_in" in d:
        w_in = d["w_in"]
        o1, o2, o3 = 960, 1760, 2272
        out["w_in"] = jnp.concatenate([w_in[..., 0:o1], z(L, D_MODEL, ZA_W - o1), _gla_pack(w_in[..., o1:o2]),
                                       w_in[..., o2:o3], w_in[..., o3:]], -1).astype(BF16)
        out["w_out"] = d["w_out"].astype(BF16)
        out["norm1_g"] = d["norm1_g"].reshape(L, 1, D_MODEL)
        out["norm2_g"] = d["norm2_g"].reshape(L, 1, D_MODEL)
        w13 = jnp.concatenate([d["e_w1"].astype(BF16), d["e_w3"].astype(BF16)], -1)
        w13 = w13.reshape(L, N_GROUPS, GROUP_SIZE, D_MODEL, 2 * D_EXPERT).transpose(0, 1, 3, 2, 4)
        out["e_w13g"] = w13.reshape(L, N_GROUPS, D_MODEL, GROUP_SIZE * 2 * D_EXPERT)
        out["e_w2g"] = d["e_w2"].astype(BF16).reshape(L, N_GROUPS, GROUP_SIZE * D_EXPERT, D_MODEL)
    return out


def kernel(x_prompt, x_sample, state_rwkv, state_gla, state_lru, c, c_ctx, norm1_g, norm2_g, norm_f_g, w_ada, b_ada, w_in, w_out, rw_mu, rw_w0, rw_w2, rw_a0, rw_a2, rw_g2, rw_kk, rw_ka, rw_rk, rw_ln_g, rw_ln_b, gla_gk2, gla_gkb, gla_ln_g, lru_conv_w, lru_conv_b, lru_wa, lru_ba, lru_wx, lru_bx, lru_lam, cv_dw_w, cv_dw_b, cv_ln_g, cv_ln_b, cv_pw_w, cv_pw_b, w_router, b_router, e_w1, e_w3, e_w2):
    d = dict(norm1_g=norm1_g, norm2_g=norm2_g, w_in=w_in, w_out=w_out,
             rw_mu=rw_mu, rw_w0=rw_w0, rw_w2=rw_w2, rw_a0=rw_a0, rw_a2=rw_a2, rw_g2=rw_g2, rw_kk=rw_kk,
             rw_ka=rw_ka, rw_rk=rw_rk, rw_ln_g=rw_ln_g, rw_ln_b=rw_ln_b,
             gla_gk2=gla_gk2, gla_gkb=gla_gkb, gla_ln_g=gla_ln_g,
             lru_conv_w=lru_conv_w, lru_conv_b=lru_conv_b, lru_wa=lru_wa, lru_ba=lru_ba, lru_wx=lru_wx,
             lru_bx=lru_bx, lru_lam=lru_lam,
             cv_dw_w=cv_dw_w, cv_dw_b=cv_dw_b, cv_ln_g=cv_ln_g, cv_ln_b=cv_ln_b, cv_pw_w=cv_pw_w, cv_pw_b=cv_pw_b,
             e_w1=e_w1, e_w3=e_w3, e_w2=e_w2)
    P = _prep_params(d)
    shared = {"w_router_t": w_router.T, "b_router": b_router.reshape(N_EXPERTS, 1)}
    norm_f = norm_f_g.reshape(1, D_MODEL)

    bp, tp, _ = x_prompt.shape
    bs, ts, _ = x_sample.shape
    n_mod = 1 + bs
    mod_rows = -(-n_mod // 8) * 8
    cvec = jnp.concatenate([c_ctx[None], c, jnp.zeros((mod_rows - n_mod, D_MODEL), F32)], 0)
    mod = _ada_mod(cvec, w_ada, b_ada)

    tp_flat = min(bp * tp, 1024)
    bp_flat = bp * tp // tp_flat
    xp = x_prompt.reshape(bp_flat, tp_flat, D_MODEL)
    xs = x_sample
    zero_rw = jnp.zeros((bp, N_DIR, W_MIX, W_MIX), F32)
    zero_gla = jnp.zeros((bp, N_DIR, W_MIX, GLA_HK), F32)
    zero_lru = jnp.zeros((bp, N_DIR, W_MIX), F32)

    def layer(x, m, seq_shape, latent, s_rw, s_gla, s_lru, p, last):
        flat_shape = x.shape[:2]
        zs = _in_proj(x, m, p["norm1_g"], p["w_in"])
        za, zb, zc, zd = (z.reshape(seq_shape + (z.shape[-1],)) for z in zs)
        ya, f_rw = _rwkv_mixer(za, s_rw, p, latent)
        yb, f_gla = _gla_mixer(zb, s_gla, p)
        yc, f_lru = _lru_mixer(zc, s_lru, p)
        yd = _conv_mixer(zd, p, latent)
        ys = [y.reshape(flat_shape + (W_MIX,)) for y in (ya, yb, yc, yd)]
        return _out_moe(x, ys, m, p, norm_f, last), f_rw, f_gla, f_lru

    fin_rw, fin_gla, fin_lru = [], [], []
    for l in range(DEPTH):
        p = {n: a[l] for n, a in P.items()}
        p.update(shared)
        last = l == DEPTH - 1
        m_p = jnp.broadcast_to(mod[l, 0:1], (bp_flat, 6 * D_MODEL)).reshape(bp_flat, 1, 6 * D_MODEL)
        xp, f_rw, f_gla, f_lru = layer(xp, m_p, (bp, tp), False, zero_rw, zero_gla, zero_lru, p, last)
        fin_rw.append(_rw_state_out(f_rw))
        fin_gla.append(_gla_state_out(f_gla))
        fin_lru.append(f_lru)
        m_s = mod[l, 1:1 + bs].reshape(bs, 1, 6 * D_MODEL)
        xs, _, _, _ = layer(xs, m_s, (bs, ts), True, _rw_state_in(state_rwkv[:, l]),
                            _gla_state_in(state_gla[:, l]), state_lru[:, l], p, last)
    return (xp.reshape(bp, tp, D_MODEL), xs, jnp.stack(fin_rw, axis=1), jnp.stack(fin_gla, axis=1),
            jnp.stack(fin_lru, axis=1))
```

```python
import functools

import jax
import jax.numpy as jnp
from jax import lax
from jax.experimental import pallas as pl
from jax.experimental.pallas import tpu as pltpu

F32 = jnp.float32
BF16 = jnp.bfloat16
ACT = jnp.bfloat16

D_MODEL = 1024
DEPTH = 4
GRID_W = 64
N_DIR = 2
W_MIX = 256
N_HEADS = 4
HEAD = W_MIX // N_HEADS
GLA_DK = 32
GLA_HK = N_HEADS * GLA_DK
RW_LN_EPS = 64e-5
NORM_EPS = 1e-6
GLA_GATE_NORM = 16.0
LRU_C = 8.0
LRU_CONV = 4
CV_KERNEL = 31
CV_PAD = (CV_KERNEL - 1) // 2
N_EXPERTS = 16
GROUP_SIZE = 4
N_GROUPS = 4
D_EXPERT = 256
CHUNK = 64
EXP_M_HALF = 0.6065306597126334

ZA_W = 1024
ZB_W = 896
ZC_W = 512
ZD_W = 512
P_PAD = ZA_W + ZB_W + ZC_W + ZD_W

VMEM_LIMIT = 48 * 1024 * 1024


def _cparams(sem, vmem=VMEM_LIMIT):
    return pltpu.CompilerParams(dimension_semantics=sem, vmem_limit_bytes=vmem)


def _dot(a, b):
    return jnp.dot(a.astype(BF16), b.astype(BF16), preferred_element_type=F32)


def _dot_nt(a, b):
    return lax.dot_general(a.astype(BF16), b.astype(BF16), (((1,), (1,)), ((), ())),
                           preferred_element_type=F32)


def _dot_tn(a, b):
    return lax.dot_general(a.astype(BF16), b.astype(BF16), (((0,), (0,)), ((), ())),
                           preferred_element_type=F32)


def _split(a):
    hi = a.astype(BF16)
    lo = (a - hi.astype(F32)).astype(BF16)
    return hi, lo


def _dot_x2(a, b_exact):
    hi, lo = _split(a)
    bb = b_exact.astype(BF16)
    return (jnp.dot(hi, bb, preferred_element_type=F32) + jnp.dot(lo, bb, preferred_element_type=F32))


def _dot_left_x2(a_exact, b):
    hi, lo = _split(b)
    aa = a_exact.astype(BF16)
    return (jnp.dot(aa, hi, preferred_element_type=F32) + jnp.dot(aa, lo, preferred_element_type=F32))


def _dot3(a, b):
    ah, al = _split(a)
    bh, bl = _split(b)
    return (jnp.dot(ah, bh, preferred_element_type=F32) + jnp.dot(ah, bl, preferred_element_type=F32)
            + jnp.dot(al, bh, preferred_element_type=F32))


def _dot3_nt(a, b):
    ah, al = _split(a)
    bh, bl = _split(b)
    dn = (((1,), (1,)), ((), ()))
    return (lax.dot_general(ah, bh, dn, preferred_element_type=F32)
            + lax.dot_general(ah, bl, dn, preferred_element_type=F32)
            + lax.dot_general(al, bh, dn, preferred_element_type=F32))


def _iota(shape, axis):
    return lax.broadcasted_iota(jnp.int32, shape, axis)


def _block_mask(rows, cols, rblk, cblk):
    r = _iota((rows, cols), 0) // rblk
    c = _iota((rows, cols), 1) // cblk
    return (r == c).astype(F32)


def _expand(x, bm):
    xb = x.astype(BF16)
    return jnp.concatenate([xb, xb, xb, xb], axis=0) * bm


def _tri(d, strict):
    t = _iota((CHUNK, CHUNK), 0)
    s = _iota((CHUNK, CHUNK), 1)
    if d == 0:
        m = (s < t) if strict else (s <= t)
    else:
        m = (s > t) if strict else (s >= t)
    return m.astype(F32)


def _tri_wide(d, strict):
    t = _iota((CHUNK, N_HEADS * CHUNK), 0)
    s = _iota((CHUNK, N_HEADS * CHUNK), 1) % CHUNK
    if d == 0:
        m = (s < t) if strict else (s <= t)
    else:
        m = (s > t) if strict else (s >= t)
    return m.astype(F32)


ADA_TN = 1536


def _ada_kernel(c_ref, w_ref, b_ref, o_ref):
    c = c_ref[...]
    s = c * jax.nn.sigmoid(c)
    o_ref[0] = _dot3(s, w_ref[0]) + b_ref[0]


def _ada_mod(cvec, w_ada, b_ada):
    rows = cvec.shape[0]
    n_out = w_ada.shape[-1]
    return pl.pallas_call(
        _ada_kernel,
        grid=(DEPTH, n_out // ADA_TN),
        in_specs=[
            pl.BlockSpec((rows, D_MODEL), lambda l, j: (0, 0)),
            pl.BlockSpec((1, D_MODEL, ADA_TN), lambda l, j: (l, 0, j)),
            pl.BlockSpec((1, 1, ADA_TN), lambda l, j: (l, 0, j)),
        ],
        out_specs=pl.BlockSpec((1, rows, ADA_TN), lambda l, j: (l, 0, j)),
        out_shape=jax.ShapeDtypeStruct((DEPTH, rows, n_out), F32),
        compiler_params=_cparams(("arbitrary", "arbitrary")),
        name="ada_mod",
    )(cvec, w_ada, b_ada.reshape(DEPTH, 1, n_out))


IN_TM = 512


def _in_kernel(x_ref, sh_ref, sc_ref, g_ref, w_ref, za_ref, zb_ref, zc_ref, zd_ref):
    x = x_ref[0]
    h = x * lax.rsqrt(jnp.mean(x * x, axis=-1, keepdims=True) + NORM_EPS) * g_ref[...]
    h = (h * (1.0 + sc_ref[0]) + sh_ref[0]).astype(BF16)
    o0, o1, o2 = ZA_W, ZA_W + ZB_W, ZA_W + ZB_W + ZC_W
    za_ref[0] = jnp.dot(h, w_ref[:, 0:o0], preferred_element_type=F32).astype(ACT)
    zb_ref[0] = jnp.dot(h, w_ref[:, o0:o1], preferred_element_type=F32).astype(ACT)
    zc_ref[0] = jnp.dot(h, w_ref[:, o1:o2], preferred_element_type=F32).astype(ACT)
    zd_ref[0] = jnp.dot(h, w_ref[:, o2:P_PAD], preferred_element_type=F32).astype(ACT)


def _in_proj(x, m, g, w):
    b, t, _ = x.shape
    tm = min(IN_TM, t)
    tok = lambda w_: pl.BlockSpec((1, tm, w_), lambda i, j: (i, j, 0))
    mod = lambda k: pl.BlockSpec((1, 1, D_MODEL), lambda i, j, k=k: (i, 0, k))
    return pl.pallas_call(
        _in_kernel,
        grid=(b, t // tm),
        in_specs=[tok(D_MODEL), mod(0), mod(1),
                  pl.BlockSpec((1, D_MODEL), lambda i, j: (0, 0)),
                  pl.BlockSpec((D_MODEL, P_PAD), lambda i, j: (0, 0))],
        out_specs=[tok(ZA_W), tok(ZB_W), tok(ZC_W), tok(ZD_W)],
        out_shape=[jax.ShapeDtypeStruct((b, t, w_), ACT) for w_ in (ZA_W, ZB_W, ZC_W, ZD_W)],
        compiler_params=_cparams(("arbitrary", "arbitrary")),
        name="in_proj",
    )(x, m, m, g, w)


SHIFT_TB = 512


def _shift_grid_kernel(z_ref, up_ref, dn_ref, mu_ref, o_ref, *, nblk):
    i = pl.program_id(1)
    z = z_ref[0].astype(F32)
    tb, c = z.shape
    col = _iota((tb, c), 0) % GRID_W
    lane = _iota((tb, c), 1) % 4
    left = jnp.where(col == 0, 0.0, pltpu.roll(z, 1, 0))
    right = jnp.where(col == GRID_W - 1, 0.0, pltpu.roll(z, tb - 1, 0))
    up_halo = jnp.where(i > 0, up_ref[0, 0].astype(F32), 0.0)
    dn_halo = jnp.where(i < nblk - 1, dn_ref[0, 0].astype(F32), 0.0)
    up = jnp.concatenate([up_halo, z[:tb - GRID_W]], axis=0)
    down = jnp.concatenate([z[GRID_W:], dn_halo], axis=0)
    sh = jnp.where(lane == 0, left, jnp.where(lane == 1, right, jnp.where(lane == 2, up, down)))
    o_ref[0] = (z + (sh - z) * mu_ref[...]).astype(ACT)


def _shift_ctx_kernel(z_ref, mu_ref, o_ref):
    z = z_ref[0].astype(F32)
    t, c = z.shape
    row = _iota((t, c), 0)
    lane = _iota((t, c), 1) % 2
    prev = jnp.where(row == 0, 0.0, pltpu.roll(z, 1, 0))
    nxt = jnp.where(row == t - 1, 0.0, pltpu.roll(z, t - 1, 0))
    sh = jnp.where(lane == 0, prev, nxt)
    o_ref[0] = (z + (sh - z) * mu_ref[...]).astype(ACT)


def _rw_shift(za, mu, latent):
    b, t, c = za.shape
    mu_spec = pl.BlockSpec((1, c), lambda *_: (0, 0))
    if not latent:
        return pl.pallas_call(
            _shift_ctx_kernel,
            grid=(b,),
            in_specs=[pl.BlockSpec((1, t, c), lambda i: (i, 0, 0)), mu_spec],
            out_specs=pl.BlockSpec((1, t, c), lambda i: (i, 0, 0)),
            out_shape=jax.ShapeDtypeStruct((b, t, c), ACT),
            compiler_params=_cparams(("arbitrary",)),
            name="rw_shift_ctx",
        )(za, mu)
    tb = min(SHIFT_TB, t)
    nblk = t // tb
    rpb = tb // GRID_W
    nrow = t // GRID_W
    z4 = za.reshape(b, nrow, GRID_W, c)
    return pl.pallas_call(
        functools.partial(_shift_grid_kernel, nblk=nblk),
        grid=(b, nblk),
        in_specs=[
            pl.BlockSpec((1, tb, c), lambda i, j: (i, j, 0)),
            pl.BlockSpec((1, 1, GRID_W, c), lambda i, j: (i, jnp.maximum(j * rpb - 1, 0), 0, 0)),
            pl.BlockSpec((1, 1, GRID_W, c), lambda i, j: (i, jnp.minimum(j * rpb + rpb, nrow - 1), 0, 0)),
            mu_spec,
        ],
        out_specs=pl.BlockSpec((1, tb, c), lambda i, j: (i, j, 0)),
        out_shape=jax.ShapeDtypeStruct((b, t, c), ACT),
        compiler_params=_cparams(("arbitrary", "arbitrary")),
        name="rw_shift_grid",
    )(za, z4, z4, mu)


SCAN_TB = 512
PREP_GROUP = 4


def _rw_scan_kernel(zf_ref, zb_ref, s0_ref, wl_ref, lb_ref, kkp_ref, kap_ref,
                    yf_ref, yb_ref, st_ref,
                    lw_s, a_s, b_s, kd_s, r2_s, mc_s, ds_s, gt_s, *, nchunk):
    i = pl.program_id(1)

    @pl.when(i == 0)
    def _():
        st_ref[...] = s0_ref[...]

    ones_blk = _block_mask(W_MIX, W_MIX, HEAD, HEAD)
    lane128 = _iota((1, 128), 1)
    z_refs = (zf_ref, zb_ref)

    for d in range(N_DIR):
        k = z_refs[d][0, :, 256:512].astype(F32)
        la = z_refs[d][0, :, 768:896].astype(F32)
        la_t = jnp.where(lane128 < 64, jnp.tanh(la), la)
        wraw = _dot(la_t, wl_ref[:, 256 * d:256 * d + 256]) + lb_ref[:, 256 * d:256 * d + 256]
        araw = _dot(la_t, wl_ref[:, 512 + 256 * d:768 + 256 * d]) + lb_ref[:, 512 + 256 * d:768 + 256 * d]
        lw_s[d] = jax.nn.sigmoid(wraw) * (-EXP_M_HALF)
        icl = jax.nn.sigmoid(araw)
        kkv = k * kkp_ref[...]
        ss = _dot_x2(kkv * kkv, ones_blk)
        kk = kkv * lax.rsqrt(ss + 1e-12)
        a_s[d] = -kk
        b_s[d] = kk * icl
        kd_s[d] = k * (1.0 + (icl - 1.0) * kap_ref[...])

    bm = _block_mask(N_HEADS * CHUNK, W_MIX, CHUNK, HEAD).astype(BF16)
    tri = [_tri(d, False) for d in range(N_DIR)]
    tw_strict = [_tri_wide(d, True) for d in range(N_DIR)]
    tw_incl = [_tri_wide(d, False) for d in range(N_DIR)]
    eye_w = (_iota((CHUNK, N_HEADS * CHUNK), 0) == _iota((CHUNK, N_HEADS * CHUNK), 1) % CHUNK).astype(F32)

    def rows_of(c, n):
        return pl.ds(pl.multiple_of(c * n, n), n)

    group = min(PREP_GROUP, nchunk)
    y_refs = (yf_ref, yb_ref)

    def prep_body(gi, carry):
        ch = [(d, gi * group + j) for j in range(group) for d in range(N_DIR)]
        each = lambda fn, *ls: [fn(*xs) for xs in zip(*ls)]
        ds_ = [d for d, _ in ch]
        load = lambda ref: [ref[d, rows_of(c, CHUNK), :] for d, c in ch]
        lw, a, b, kd = load(lw_s), load(a_s), load(b_s), load(kd_s)
        r = [z_refs[d][0, rows_of(c, CHUNK), 0:256].astype(F32) for d, c in ch]
        v = [z_refs[d][0, rows_of(c, CHUNK), 512:768].astype(F32) for d, c in ch]
        cum = each(lambda d, x: _dot_left_x2(tri[d], x), ds_, lw)
        tot = each(lambda d, x: x[CHUNK - 1:CHUNK, :] if d == 0 else x[0:1, :], ds_, cum)
        g_inv = each(lambda x: jnp.exp(-x), cum)
        g_rem = each(lambda t_, x: jnp.exp(t_ - x), tot, cum)
        at = each(lambda a_, x, l: a_ * jnp.exp(x - l), a, cum, lw)
        rt = each(lambda r_, x: r_ * jnp.exp(x), r, cum)
        x = each(lambda p, q: jnp.concatenate([p, q], axis=0), at, rt)
        gb = each(lambda x_, b_, g: _dot_nt(x_, _expand(b_ * g, bm)), x, b, g_inv)
        gk = each(lambda x_, k_, g: _dot_nt(x_, _expand(k_ * g, bm)), x, kd, g_inv)
        vbd = each(lambda v_: _expand(v_, bm), v)
        a_ab = each(lambda d, g: g[0:CHUNK] * tw_strict[d], ds_, gb)
        av = each(lambda d, g, vb: _dot(g[0:CHUNK] * tw_strict[d], vb), ds_, gk, vbd)
        tw = each(lambda m: eye_w + m, a_ab)
        apow = a_ab
        for _ in range(5):
            apow = each(lambda m: _dot(m, _expand(m, bm)), apow)
            tw = each(lambda t_, m: t_ + _dot(m, _expand(t_, bm)), tw, apow)
        a2 = each(lambda t_, m: _dot(t_, _expand(m, bm)), tw, at)
        u0 = each(lambda t_, m: _dot(t_, _expand(m, bm)), tw, av)
        r_b = each(lambda d, g: g[CHUNK:] * tw_incl[d], ds_, gb)
        r2 = each(lambda rt_, rb, m: rt_ + _dot(rb, _expand(m, bm)), rt, r_b, a2)
        y0 = each(lambda d, rb, u, g, vb: _dot(rb, _expand(u, bm)) + _dot(g[CHUNK:] * tw_incl[d], vb),
                  ds_, r_b, u0, gk, vbd)
        bl = each(lambda b_, g: b_ * g, b, g_rem)
        kl = each(lambda k_, g: k_ * g, kd, g_rem)
        mc = each(lambda m, bl_: _dot_tn(m, bl_) * ones_blk, a2, bl)
        ds0 = each(lambda u, v_, bl_, kl_: _dot_tn(jnp.concatenate([u, v_], axis=0),
                                                   jnp.concatenate([bl_, kl_], axis=0)) * ones_blk,
                   u0, v, bl, kl)
        for n, (d, c) in enumerate(ch):
            rows = rows_of(c, CHUNK)
            y_refs[d][0, rows, :] = y0[n]
            r2_s[d, rows, :] = r2[n].astype(BF16)
            mc_s[d, rows_of(c, W_MIX), :] = mc[n].astype(BF16)
            ds_s[d, rows_of(c, W_MIX), :] = ds0[n]
            gt_s[d, rows_of(c, 8), :] = jnp.broadcast_to(jnp.exp(tot[n]), (8, W_MIX))
        return carry

    lax.fori_loop(0, nchunk // group, prep_body, 0)

    def scan_body(cc, carry):
        cs = [cc, nchunk - 1 - cc]
        s = [st_ref[0, d] for d in range(N_DIR)]
        sb = [x.astype(BF16) for x in s]
        upd = [_dot(sb[d], mc_s[d, rows_of(cs[d], W_MIX), :]) for d in range(N_DIR)]
        ys = [_dot_nt(r2_s[d, rows_of(cs[d], CHUNK), :], sb[d]) for d in range(N_DIR)]
        for d in range(N_DIR):
            st_ref[0, d] = (s[d] * gt_s[d, rows_of(cs[d], 8), :][0:1, :] + upd[d]
                            + ds_s[d, rows_of(cs[d], W_MIX), :])
            y_refs[d][0, rows_of(cs[d], CHUNK), :] += ys[d]
        return carry

    lax.fori_loop(0, nchunk, scan_body, 0)


def _rw_scan(zs, s0, wl, lb, kkp, kap):
    b, t, c = zs.shape
    tb = min(SCAN_TB, t)
    nblk = t // tb
    nchunk = tb // CHUNK
    vec = lambda n: pl.BlockSpec((1, n), lambda i, j: (0, 0))
    st_spec = pl.BlockSpec((1, N_DIR, W_MIX, W_MIX), lambda i, j: (i, 0, 0, 0))
    return pl.pallas_call(
        functools.partial(_rw_scan_kernel, nchunk=nchunk),
        grid=(b, nblk),
        in_specs=[
            pl.BlockSpec((1, tb, c), lambda i, j: (i, j, 0)),
            pl.BlockSpec((1, tb, c), lambda i, j: (i, nblk - 1 - j, 0)),
            st_spec,
            pl.BlockSpec((128, 1024), lambda i, j: (0, 0)),
            vec(1024), vec(W_MIX), vec(W_MIX),
        ],
        out_specs=[
            pl.BlockSpec((1, tb, W_MIX), lambda i, j: (i, j, 0)),
            pl.BlockSpec((1, tb, W_MIX), lambda i, j: (i, nblk - 1 - j, 0)),
            st_spec,
        ],
        out_shape=[
            jax.ShapeDtypeStruct((b, t, W_MIX), F32),
            jax.ShapeDtypeStruct((b, t, W_MIX), F32),
            jax.ShapeDtypeStruct((b, N_DIR, W_MIX, W_MIX), F32),
        ],
        scratch_shapes=[pltpu.VMEM((N_DIR, tb, W_MIX), F32) for _ in range(4)] + [
            pltpu.VMEM((N_DIR, tb, W_MIX), BF16),
            pltpu.VMEM((N_DIR, nchunk * W_MIX, W_MIX), BF16),
            pltpu.VMEM((N_DIR, nchunk * W_MIX, W_MIX), F32),
            pltpu.VMEM((N_DIR, nchunk * 8, W_MIX), F32),
        ],
        compiler_params=_cparams(("arbitrary", "arbitrary")),
        name="rw_scan",
    )(zs, zs, s0, wl, lb, kkp, kap)


EPI_TM = 512


def _rw_epi_kernel(yf_ref, yb_ref, z_ref, lng_ref, lnb_ref, rk_ref, g2_ref, o_ref):
    ones_blk = _block_mask(W_MIX, W_MIX, HEAD, HEAD)
    y = yf_ref[0] + yb_ref[0]
    mu = _dot_x2(y, ones_blk) * (1.0 / HEAD)
    yc = y - mu
    var = _dot_x2(yc * yc, ones_blk) * (1.0 / HEAD)
    yn = yc * lax.rsqrt(var + RW_LN_EPS) * lng_ref[...] + lnb_ref[...]
    r = z_ref[0, :, 0:256].astype(F32)
    k = z_ref[0, :, 256:512].astype(F32)
    v = z_ref[0, :, 512:768].astype(F32)
    gl = z_ref[0, :, 896:1024].astype(F32)
    bonus = _dot_x2(r * k * rk_ref[...], ones_blk) * v
    gate = _dot(jax.nn.sigmoid(gl), g2_ref[...])
    o_ref[0] = (yn + bonus) * gate


def _rw_epi(yf, yb, zs, lng, lnb, rk, g2p):
    b, t, _ = yf.shape
    tm = min(EPI_TM, t)
    tok = lambda w_: pl.BlockSpec((1, tm, w_), lambda i, j: (i, j, 0))
    vec = pl.BlockSpec((1, W_MIX), lambda i, j: (0, 0))
    return pl.pallas_call(
        _rw_epi_kernel,
        grid=(b, t // tm),
        in_specs=[tok(W_MIX), tok(W_MIX), tok(ZA_W), vec, vec, vec,
                  pl.BlockSpec((128, W_MIX), lambda i, j: (0, 0))],
        out_specs=tok(W_MIX),
        out_shape=jax.ShapeDtypeStruct((b, t, W_MIX), F32),
        compiler_params=_cparams(("arbitrary", "arbitrary")),
        name="rw_epi",
    )(yf, yb, zs, lng, lnb, rk, g2p)


def _rwkv_mixer(za, s0_bd, p, latent):
    zs = _rw_shift(za, p["rw_mu"], latent)
    yf, yb, st = _rw_scan(zs, s0_bd, p["rw_wl"], p["rw_lb"], p["rw_kk"], p["rw_ka"])
    ya = _rw_epi(yf, yb, zs, p["rw_ln_g"], p["rw_ln_b"], p["rw_rk"], p["rw_g2"])
    return ya, st


def _gla_scan_kernel(zf_ref, zb_ref, s0_ref, wg_ref, gb_ref, yf_ref, yb_ref, st_ref,
                     la_s, qe_s, ds_s, dec_s, *, nchunk):
    i = pl.program_id(1)

    @pl.when(i == 0)
    def _():
        st_ref[...] = s0_ref[...]

    z_refs = (zf_ref, zb_ref)
    y_refs = (yf_ref, yb_ref)
    for d in range(N_DIR):
        logit = _dot(z_refs[d][0, :, 768:896], wg_ref[d]) + gb_ref[d]
        la_s[d] = jax.nn.log_sigmoid(logit) * (1.0 / GLA_GATE_NORM)

    bm_k = _block_mask(N_HEADS * CHUNK, GLA_HK, CHUNK, GLA_DK).astype(BF16)
    bm_v = _block_mask(N_HEADS * CHUNK, W_MIX, CHUNK, HEAD).astype(BF16)
    bm_s = _block_mask(W_MIX, GLA_HK, HEAD, GLA_DK)
    tri = [_tri(d, False) for d in range(N_DIR)]
    tw_incl = [_tri_wide(d, False) for d in range(N_DIR)]

    def rows_of(c, n):
        return pl.ds(pl.multiple_of(c * n, n), n)

    group = min(PREP_GROUP, nchunk)

    def prep_body(gi, carry):
        ch = [(d, gi * group + j) for j in range(group) for d in range(N_DIR)]
        each = lambda fn, *ls: [fn(*xs) for xs in zip(*ls)]
        ds_ = [d for d, _ in ch]
        la = [la_s[d, rows_of(c, CHUNK), :] for d, c in ch]
        q = [z_refs[d][0, rows_of(c, CHUNK), 0:128].astype(F32) * (GLA_DK ** -0.5) for d, c in ch]
        k = [z_refs[d][0, rows_of(c, CHUNK), 128:256].astype(F32) for d, c in ch]
        v = [z_refs[d][0, rows_of(c, CHUNK), 256:512].astype(F32) for d, c in ch]
        cum = each(lambda d, x: _dot_left_x2(tri[d], x), ds_, la)
        last = each(lambda d, x: x[CHUNK - 1:CHUNK, :] if d == 0 else x[0:1, :], ds_, cum)
        qe = each(lambda q_, x: q_ * jnp.exp(x), q, cum)
        ke = each(lambda k_, x: k_ * jnp.exp(-x), k, cum)
        kl = each(lambda k_, l, x: k_ * jnp.exp(l - x), k, last, cum)
        att = each(lambda d, q_, k_: _dot_nt(q_, _expand(k_, bm_k)) * tw_incl[d], ds_, qe, ke)
        o = each(lambda a_, v_: _dot(a_, _expand(v_, bm_v)), att, v)
        dst = each(lambda v_, k_: _dot_tn(v_, k_) * bm_s, v, kl)
        for n, (d, c) in enumerate(ch):
            y_refs[d][0, rows_of(c, CHUNK), :] = o[n]
            qe_s[d, rows_of(c, CHUNK), :] = qe[n].astype(BF16)
            ds_s[d, rows_of(c, W_MIX), :] = dst[n]
            dec_s[d, rows_of(c, 8), :] = jnp.broadcast_to(jnp.exp(last[n]), (8, GLA_HK))
        return carry

    lax.fori_loop(0, nchunk // group, prep_body, 0)

    def scan_body(cc, carry):
        cs = [cc, nchunk - 1 - cc]
        for d in range(N_DIR):
            s = st_ref[0, d]
            y_refs[d][0, rows_of(cs[d], CHUNK), :] += _dot_nt(qe_s[d, rows_of(cs[d], CHUNK), :], s)
            st_ref[0, d] = s * dec_s[d, rows_of(cs[d], 8), :][0:1, :] + ds_s[d, rows_of(cs[d], W_MIX), :]
        return carry

    lax.fori_loop(0, nchunk, scan_body, 0)


def _gla_scan(zb, s0, wg, gb):
    b, t, c = zb.shape
    tb = min(SCAN_TB, t)
    nblk = t // tb
    st_spec = pl.BlockSpec((1, N_DIR, W_MIX, GLA_HK), lambda i, j: (i, 0, 0, 0))
    nchunk = tb // CHUNK
    return pl.pallas_call(
        functools.partial(_gla_scan_kernel, nchunk=nchunk),
        grid=(b, nblk),
        in_specs=[
            pl.BlockSpec((1, tb, c), lambda i, j: (i, j, 0)),
            pl.BlockSpec((1, tb, c), lambda i, j: (i, nblk - 1 - j, 0)),
            st_spec,
            pl.BlockSpec((N_DIR, 128, GLA_HK), lambda i, j: (0, 0, 0)),
            pl.BlockSpec((N_DIR, 1, GLA_HK), lambda i, j: (0, 0, 0)),
        ],
        out_specs=[
            pl.BlockSpec((1, tb, W_MIX), lambda i, j: (i, j, 0)),
            pl.BlockSpec((1, tb, W_MIX), lambda i, j: (i, nblk - 1 - j, 0)),
            st_spec,
        ],
        out_shape=[
            jax.ShapeDtypeStruct((b, t, W_MIX), F32),
            jax.ShapeDtypeStruct((b, t, W_MIX), F32),
            jax.ShapeDtypeStruct((b, N_DIR, W_MIX, GLA_HK), F32),
        ],
        scratch_shapes=[
            pltpu.VMEM((N_DIR, tb, GLA_HK), F32),
            pltpu.VMEM((N_DIR, tb, GLA_HK), BF16),
            pltpu.VMEM((N_DIR, nchunk * W_MIX, GLA_HK), F32),
            pltpu.VMEM((N_DIR, nchunk * 8, GLA_HK), F32),
        ],
        compiler_params=_cparams(("arbitrary", "arbitrary")),
        name="gla_scan",
    )(zb, zb, s0, wg, gb)


def _gla_epi_kernel(yf_ref, yb_ref, z_ref, g_ref, o_ref):
    ones_blk = _block_mask(W_MIX, W_MIX, HEAD, HEAD)
    y = yf_ref[0] + yb_ref[0]
    ms = _dot_x2(y * y, ones_blk) * (1.0 / HEAD)
    y = y * lax.rsqrt(ms + NORM_EPS) * g_ref[...]
    og = z_ref[0, :, 512:768].astype(F32)
    o_ref[0] = y * (og * jax.nn.sigmoid(og))


def _gla_epi(yf, yb, zb, g):
    b, t, _ = yf.shape
    tm = min(EPI_TM, t)
    tok = lambda w_: pl.BlockSpec((1, tm, w_), lambda i, j: (i, j, 0))
    return pl.pallas_call(
        _gla_epi_kernel,
        grid=(b, t // tm),
        in_specs=[tok(W_MIX), tok(W_MIX), tok(ZB_W), pl.BlockSpec((1, W_MIX), lambda i, j: (0, 0))],
        out_specs=tok(W_MIX),
        out_shape=jax.ShapeDtypeStruct((b, t, W_MIX), F32),
        compiler_params=_cparams(("arbitrary", "arbitrary")),
        name="gla_epi",
    )(yf, yb, zb, g)


def _gla_mixer(zb, s0_bd, p):
    yf, yb, st = _gla_scan(zb, s0_bd, p["gla_wg"], p["gla_gb"])
    return _gla_epi(yf, yb, zb, p["gla_ln_g"]), st


LRU_TT = 256
LRU_HALO = 8


def _lru_kernel(z_ref, h0_ref, cw_ref, cb_ref, wax_ref, bax_ref, lam_ref, y_ref, hf_ref, xpad, *, t):
    tt = min(LRU_TT, t)
    ntile = t // tt
    xpad[0:LRU_HALO, :] = jnp.zeros((LRU_HALO, W_MIX), F32)
    xpad[LRU_HALO + t:2 * LRU_HALO + t, :] = jnp.zeros((LRU_HALO, W_MIX), F32)

    def fill(j, carry):
        base = pl.multiple_of(j * tt, tt)
        xpad[pl.ds(base + LRU_HALO, tt), :] = z_ref[0, pl.ds(base, tt), 0:W_MIX].astype(F32)
        return carry

    lax.fori_loop(0, ntile, fill, 0)
    row = _iota((tt, W_MIX), 0)

    def tile_scan(j, h, d):
        base = pl.multiple_of(j * tt, tt)
        win = xpad[pl.ds(base, tt + 2 * LRU_HALO), :]
        xc = jnp.zeros((tt, W_MIX), F32) + cb_ref[d]
        for tap in range(LRU_CONV):
            off = LRU_HALO - (LRU_CONV - 1) + tap if d == 0 else LRU_HALO + (LRU_CONV - 1) - tap
            xc = xc + cw_ref[d, tap:tap + 1, :] * win[off:off + tt, :]
        g = _dot(xc, wax_ref[d]) + bax_ref[d]
        gr = jax.nn.sigmoid(g[:, 0:W_MIX])
        gi = jax.nn.sigmoid(g[:, W_MIX:2 * W_MIX])
        log_a = -LRU_C * gr * jax.nn.softplus(-lam_ref[d])
        a = jnp.exp(log_a)
        bv = jnp.sqrt(1.0 - jnp.exp(2.0 * log_a)) * gi * xc
        s = 1
        while s < tt:
            if d == 0:
                keep = row >= s
                a_sh = jnp.where(keep, pltpu.roll(a, s, 0), 1.0)
                b_sh = jnp.where(keep, pltpu.roll(bv, s, 0), 0.0)
            else:
                keep = row < tt - s
                a_sh = jnp.where(keep, pltpu.roll(a, tt - s, 0), 1.0)
                b_sh = jnp.where(keep, pltpu.roll(bv, tt - s, 0), 0.0)
            bv = a * b_sh + bv
            a = a * a_sh
            s *= 2
        return a * h + bv, base

    def fwd(j, h):
        ht, base = tile_scan(j, h, 0)
        y_ref[0, pl.ds(base, tt), :] = ht
        return ht[tt - 1:tt, :]

    h_end = lax.fori_loop(0, ntile, fwd, h0_ref[0, 0:1, :])
    hf_ref[0, 0:1, :] = h_end

    def bwd(jj, h):
        j = ntile - 1 - jj
        ht, base = tile_scan(j, h, 1)
        gb = z_ref[0, pl.ds(base, tt), W_MIX:2 * W_MIX].astype(F32)
        y_ref[0, pl.ds(base, tt), :] = (y_ref[0, pl.ds(base, tt), :] + ht) * jax.nn.gelu(gb)
        return ht[0:1, :]

    h_end = lax.fori_loop(0, ntile, bwd, h0_ref[0, 1:2, :])
    hf_ref[0, 1:2, :] = h_end


def _lru_mixer(zc, h0, p):
    b, t, c = zc.shape
    full = lambda *s: pl.BlockSpec(s, lambda i: (0,) * len(s))
    return pl.pallas_call(
        functools.partial(_lru_kernel, t=t),
        grid=(b,),
        in_specs=[
            pl.BlockSpec((1, t, c), lambda i: (i, 0, 0)),
            pl.BlockSpec((1, N_DIR, W_MIX), lambda i: (i, 0, 0)),
            full(N_DIR, LRU_CONV, W_MIX), full(N_DIR, 1, W_MIX),
            full(N_DIR, W_MIX, 2 * W_MIX), full(N_DIR, 1, 2 * W_MIX), full(N_DIR, 1, W_MIX),
        ],
        out_specs=[pl.BlockSpec((1, t, W_MIX), lambda i: (i, 0, 0)),
                   pl.BlockSpec((1, N_DIR, W_MIX), lambda i: (i, 0, 0))],
        out_shape=[jax.ShapeDtypeStruct((b, t, W_MIX), F32), jax.ShapeDtypeStruct((b, N_DIR, W_MIX), F32)],
        scratch_shapes=[pltpu.VMEM((t + 2 * LRU_HALO, W_MIX), F32)],
        compiler_params=_cparams(("arbitrary",)),
        name="lru",
    )(zc, h0, p["lru_cw"], p["lru_cb"], p["lru_wax"], p["lru_bax"], p["lru_lam"])


CV_TT = 256
CV_WIN = 16


def _conv_kernel(z_ref, dw_ref, dwb_ref, lng_ref, lnb_ref, pw_ref, pwb_ref, y_ref, upad, *, t, latent):
    tt = min(CV_TT, t)
    ntile = t // tt
    pad = CV_PAD * GRID_W if latent else CV_WIN
    upad[0:pad, :] = jnp.zeros((pad, W_MIX), F32)
    upad[pad + t:2 * pad + t, :] = jnp.zeros((pad, W_MIX), F32)

    def fill(j, carry):
        base = pl.multiple_of(j * tt, tt)
        z = z_ref[0, pl.ds(base, tt), :].astype(F32)
        upad[pl.ds(base + pad, tt), :] = z[:, 0:W_MIX] * jax.nn.sigmoid(z[:, W_MIX:2 * W_MIX])
        return carry

    lax.fori_loop(0, ntile, fill, 0)

    def tile(j, carry):
        base = pl.multiple_of(j * tt, tt)
        if latent:
            half = W_MIX // 2
            col = _iota((tt, half), 0) % GRID_W
            win = upad[pl.ds(pl.multiple_of(base + pad - CV_WIN, CV_WIN), tt + 2 * CV_WIN), 0:half]
            accw = jnp.zeros((tt, half), F32)
            acch = jnp.zeros((tt, half), F32)
            for tap in range(CV_KERNEL):
                dlt = tap - CV_PAD
                x = win[CV_WIN + dlt:CV_WIN + dlt + tt, :]
                if dlt < 0:
                    x = jnp.where(col >= -dlt, x, 0.0)
                elif dlt > 0:
                    x = jnp.where(col < GRID_W - dlt, x, 0.0)
                accw = accw + dw_ref[tap:tap + 1, 0:half] * x
                rows = pl.ds(pl.multiple_of(base + pad + dlt * GRID_W, GRID_W), tt)
                acch = acch + dw_ref[tap:tap + 1, half:W_MIX] * upad[rows, half:W_MIX]
            u = jnp.concatenate([accw, acch], axis=1)
        else:
            win = upad[pl.ds(base + pad - CV_WIN, tt + 2 * CV_WIN), :]
            u = jnp.zeros((tt, W_MIX), F32)
            for tap in range(CV_KERNEL):
                dlt = tap - CV_PAD
                u = u + dw_ref[tap:tap + 1, :] * win[CV_WIN + dlt:CV_WIN + dlt + tt, :]
        u = u + dwb_ref[...]
        mu = jnp.mean(u, axis=-1, keepdims=True)
        uc = u - mu
        var = jnp.mean(uc * uc, axis=-1, keepdims=True)
        un = uc * lax.rsqrt(var + 1e-5) * lng_ref[...] + lnb_ref[...]
        un = un * jax.nn.sigmoid(un)
        y_ref[0, pl.ds(base, tt), :] = _dot(un, pw_ref[...]) + pwb_ref[...]
        return carry

    lax.fori_loop(0, ntile, tile, 0)


def _conv_mixer(zd, p, latent):
    b, t, c = zd.shape
    pad = CV_PAD * GRID_W if latent else CV_WIN
    full = lambda *s: pl.BlockSpec(s, lambda i: (0,) * len(s))
    vec = full(1, W_MIX)
    return pl.pallas_call(
        functools.partial(_conv_kernel, t=t, latent=latent),
        grid=(b,),
        in_specs=[pl.BlockSpec((1, t, c), lambda i: (i, 0, 0)),
                  full(CV_KERNEL, W_MIX), vec, vec, vec, full(W_MIX, W_MIX), vec],
        out_specs=pl.BlockSpec((1, t, W_MIX), lambda i: (i, 0, 0)),
        out_shape=jax.ShapeDtypeStruct((b, t, W_MIX), F32),
        scratch_shapes=[pltpu.VMEM((t + 2 * pad, W_MIX), F32)],
        compiler_params=_cparams(("arbitrary",)),
        name="conv_grid" if latent else "conv_ctx",
    )(zd, p["cv_dw_w"], p["cv_dw_b"], p["cv_ln_g"], p["cv_ln_b"], p["cv_pw_w"], p["cv_pw_b"])


MOE_TM = 512
MOE_VMEM = VMEM_LIMIT
MOE_RB = 128
E_PAD = 128


def _route(sel, scores):
    grp = []
    for g in range(N_GROUPS):
        s = sel[GROUP_SIZE * g:GROUP_SIZE * (g + 1)]
        best_pair = None
        for i in range(GROUP_SIZE):
            for j in range(i + 1, GROUP_SIZE):
                pair = s[i] + s[j]
                best_pair = pair if best_pair is None else jnp.maximum(best_pair, pair)
        grp.append(best_pair)
    best = jnp.zeros_like(grp[0], dtype=jnp.int32)
    top = grp[0]
    for g in range(1, N_GROUPS):
        better = grp[g] > top
        best = jnp.where(better, g, best)
        top = jnp.where(better, grp[g], top)
    neg = jnp.full_like(sel[0], -jnp.inf)
    msel = [jnp.where(best == e // GROUP_SIZE, sel[e], neg) for e in range(N_EXPERTS)]
    picks = []
    for _ in range(2):
        idx = jnp.zeros_like(best)
        top = msel[0]
        for e in range(1, N_EXPERTS):
            better = msel[e] > top
            idx = jnp.where(better, e, idx)
            top = jnp.where(better, msel[e], top)
        picks.append(idx)
        msel = [jnp.where(idx == e, neg, msel[e]) for e in range(N_EXPERTS)]
    chosen = [jnp.where((picks[0] == e) | (picks[1] == e), scores[e], 0.0) for e in range(N_EXPERTS)]
    total = chosen[0]
    for e in range(1, N_EXPERTS):
        total = total + chosen[e]
    return [ch / total for ch in chosen], best


def _moe_kernel(x_ref, ya_ref, yb_ref, yc_ref, yd_ref, wo_ref, g1_ref, sh2_ref, sc2_ref, g2_ref, n2_ref,
                wr_ref, br_ref, w1_ref, w3_ref, w2_ref, nf_ref, o_ref,
                x1_s, h2_s, gt_s, gate_s, acc_s, he_s, og_s, pmt_s, seg_s, *, final_norm):
    grp = pl.program_id(2)
    tm = x1_s.shape[0]

    @pl.when(grp == 0)
    def _():
        y = (_dot(ya_ref[0], wo_ref[0:256, :]) + _dot(yb_ref[0], wo_ref[256:512, :])
             + _dot(yc_ref[0], wo_ref[512:768, :]) + _dot(yd_ref[0], wo_ref[768:1024, :]))
        x1 = x_ref[0] + g1_ref[0] * y
        x1_s[...] = x1
        h2 = x1 * lax.rsqrt(jnp.mean(x1 * x1, axis=-1, keepdims=True) + NORM_EPS) * n2_ref[...]
        h2 = h2 * (1.0 + sc2_ref[0]) + sh2_ref[0]
        logits = _dot3_nt(wr_ref[...], h2)
        scores = jax.nn.sigmoid(logits)
        selm = scores + br_ref[...]
        gates, best = _route([selm[i:i + 1, :] for i in range(N_EXPERTS)],
                             [scores[i:i + 1, :] for i in range(N_EXPERTS)])
        gt_s[...] = jnp.zeros(gt_s.shape, F32)
        for i in range(N_EXPERTS):
            gt_s[i:i + 1, :] = gates[i]

        og = [jnp.where(best == g, 1.0, 0.0) for g in range(N_GROUPS)]
        og_s[...] = jnp.zeros(og_s.shape, F32)
        start = jnp.int32(0)
        starts = []
        for g in range(N_GROUPS):
            og_s[g:g + 1, :] = og[g]
            starts.append(start)
            seg_s[g] = start
            start = start + jnp.sum(og[g]).astype(jnp.int32)
            seg_s[N_GROUPS + g] = start
        before = jnp.where(_iota((tm, tm), 0) < _iota((tm, tm), 1), 1.0, 0.0).astype(BF16)
        rank = jnp.dot(og_s[...].astype(BF16), before, preferred_element_type=F32)
        pos = og[0] * (starts[0].astype(F32) + rank[0:1, :])
        for g in range(1, N_GROUPS):
            pos = pos + og[g] * (starts[g].astype(F32) + rank[g:g + 1, :])
        gt_s[N_EXPERTS:N_EXPERTS + 1, :] = pos
        gate_tok = gt_s[...].T
        slot_l = _iota((tm, tm), 1).astype(F32)
        slot_s = _iota((tm, tm), 0).astype(F32)
        pmt_s[...] = jnp.where(slot_l == gate_tok[:, N_EXPERTS:N_EXPERTS + 1], 1.0, 0.0).astype(BF16)
        pm = jnp.where(slot_s == pos, 1.0, 0.0).astype(BF16)
        h2_s[...] = jnp.dot(pm, h2.astype(BF16), preferred_element_type=F32).astype(BF16)
        gate_s[...] = _dot_left_x2(pm, gate_tok)
        acc_s[...] = jnp.zeros(acc_s.shape, F32)

    pick = (_iota((E_PAD, GROUP_SIZE * 128), 0)
            == GROUP_SIZE * grp + _iota((E_PAD, GROUP_SIZE * 128), 1) // 128).astype(F32)

    def block(rb, carry):
        rows = pl.ds(pl.multiple_of(rb * MOE_RB, MOE_RB), MOE_RB)
        gsel = _dot_x2(gate_s[rows, :], pick)
        h2 = h2_s[rows, :]
        for j in range(GROUP_SIZE):
            he = jnp.dot(h2, w1_ref[j], preferred_element_type=F32)
            he = he * jax.nn.sigmoid(he) * jnp.dot(h2, w3_ref[j], preferred_element_type=F32)
            g = gsel[:, 128 * j:128 * (j + 1)]
            he_s[rows, D_EXPERT * j:D_EXPERT * (j + 1)] = (he * jnp.concatenate([g, g], axis=1)).astype(BF16)
        w2g = w2_ref[...].reshape(GROUP_SIZE * D_EXPERT, D_MODEL)
        acc_s[rows, :] += jnp.dot(he_s[rows, :], w2g, preferred_element_type=F32)
        return carry

    first = seg_s[grp] // MOE_RB
    last = (seg_s[N_GROUPS + grp] + (MOE_RB - 1)) // MOE_RB
    lax.fori_loop(first, last, block, 0)

    @pl.when(grp == N_GROUPS - 1)
    def _():
        x2 = x1_s[...] + g2_ref[0] * _dot_left_x2(pmt_s[...], acc_s[...])
        if final_norm:
            x2 = x2 * lax.rsqrt(jnp.mean(x2 * x2, axis=-1, keepdims=True) + NORM_EPS) * nf_ref[...]
        o_ref[0] = x2


def _out_moe(x, ys, m, p, norm_f, final_norm):
    b, t, _ = x.shape
    tm = min(MOE_TM, t)
    tok = lambda w_: pl.BlockSpec((1, tm, w_), lambda i, j, e: (i, j, 0))
    mod = lambda k: pl.BlockSpec((1, 1, D_MODEL), lambda i, j, e, k=k: (i, 0, k))
    full = lambda *s: pl.BlockSpec(s, lambda i, j, e: (0,) * len(s))
    return pl.pallas_call(
        functools.partial(_moe_kernel, final_norm=final_norm),
        grid=(b, t // tm, N_GROUPS),
        in_specs=[tok(D_MODEL), tok(W_MIX), tok(W_MIX), tok(W_MIX), tok(W_MIX),
                  full(D_MODEL, D_MODEL), mod(2), mod(3), mod(4), mod(5), full(1, D_MODEL),
                  full(N_EXPERTS, D_MODEL), full(N_EXPERTS, 1),
                  pl.BlockSpec((GROUP_SIZE, D_MODEL, D_EXPERT), lambda i, j, e: (e, 0, 0)),
                  pl.BlockSpec((GROUP_SIZE, D_MODEL, D_EXPERT), lambda i, j, e: (e, 0, 0)),
                  pl.BlockSpec((GROUP_SIZE, D_EXPERT, D_MODEL), lambda i, j, e: (e, 0, 0)),
                  full(1, D_MODEL)],
        out_specs=tok(D_MODEL),
        out_shape=jax.ShapeDtypeStruct((b, t, D_MODEL), F32),
        scratch_shapes=[pltpu.VMEM((tm, D_MODEL), F32), pltpu.VMEM((tm, D_MODEL), BF16),
                        pltpu.VMEM((E_PAD, tm), F32), pltpu.VMEM((tm, E_PAD), F32),
                        pltpu.VMEM((tm, D_MODEL), F32), pltpu.VMEM((tm, D_EXPERT * GROUP_SIZE), BF16),
                        pltpu.VMEM((8, tm), F32), pltpu.VMEM((tm, tm), BF16),
                        pltpu.SMEM((2 * N_GROUPS,), jnp.int32)],
        compiler_params=_cparams(("arbitrary", "arbitrary", "arbitrary"), MOE_VMEM),
        name="out_moe",
    )(x, *ys, p["w_out"], m, m, m, m, p["norm2_g"], p["w_router_t"], p["b_router"], p["e_w1"], p["e_w3"], p["e_w2"],
      norm_f)


def _gla_pack(z):
    hk = GLA_HK
    lead = z.shape[:-1]
    gl = z[..., 2 * hk + W_MIX:2 * hk + W_MIX + 32]
    return jnp.concatenate([z[..., 0:2 * hk + W_MIX], z[..., 2 * hk + W_MIX + 32:], gl,
                            jnp.zeros(lead + (96,), z.dtype)], -1)


def _gla_state_in(s):
    b = s.shape[0]
    eye = jnp.eye(N_HEADS, dtype=s.dtype)
    return jnp.einsum("bdhkv,hg->bdhvgk", s, eye).reshape(b, N_DIR, W_MIX, GLA_HK)


def _gla_state_out(st):
    b = st.shape[0]
    eye = jnp.eye(N_HEADS, dtype=st.dtype)
    return jnp.einsum("bdhvgk,hg->bdhkv", st.reshape(b, N_DIR, N_HEADS, HEAD, N_HEADS, GLA_DK), eye)


def _rw_state_in(s):
    b = s.shape[0]
    eye = jnp.eye(N_HEADS, dtype=s.dtype)
    return jnp.einsum("bdhvk,hg->bdhvgk", s, eye).reshape(b, N_DIR, W_MIX, W_MIX)


def _rw_state_out(st):
    b = st.shape[0]
    eye = jnp.eye(N_HEADS, dtype=st.dtype)
    return jnp.einsum("bdhvgk,hg->bdhvk", st.reshape(b, N_DIR, N_HEADS, HEAD, N_HEADS, HEAD), eye)


def _prep_params(d):
    L = DEPTH
    z = lambda *s: jnp.zeros(s, F32)
    out = {}
    out["rw_mu"] = jnp.concatenate([d["rw_mu"], z(L, ZA_W - d["rw_mu"].shape[-1])], -1).reshape(L, 1, ZA_W)
    w2, a2 = d["rw_w2"], d["rw_a2"]
    zz = z(L, 32, 256)
    rows = [
        jnp.concatenate([w2[:, 0], zz, zz, zz], -1),
        jnp.concatenate([zz, w2[:, 1], zz, zz], -1),
        jnp.concatenate([zz, zz, a2[:, 0], zz], -1),
        jnp.concatenate([zz, zz, zz, a2[:, 1]], -1),
    ]
    out["rw_wl"] = jnp.concatenate(rows, 1).astype(BF16)
    out["rw_lb"] = jnp.concatenate([d["rw_w0"][:, 0], d["rw_w0"][:, 1], d["rw_a0"][:, 0], d["rw_a0"][:, 1]],
                                   -1).reshape(L, 1, 1024)
    for n in ("rw_kk", "rw_ka", "rw_ln_g", "rw_ln_b"):
        out[n] = d[n].reshape(L, 1, W_MIX)
    out["rw_rk"] = d["rw_rk"].reshape(L, 1, W_MIX)
    out["rw_g2"] = jnp.concatenate([d["rw_g2"], z(L, 64, W_MIX)], 1).astype(BF16)

    gk2 = d["gla_gk2"]
    z16, z96 = z(L, 16, GLA_HK), z(L, 96, GLA_HK)
    out["gla_wg"] = jnp.stack([jnp.concatenate([gk2[:, 0], z16, z96], 1),
                               jnp.concatenate([z16, gk2[:, 1], z96], 1)], 1).astype(BF16)
    out["gla_gb"] = d["gla_gkb"].reshape(L, N_DIR, 1, GLA_HK)
    out["gla_ln_g"] = jnp.tile(d["gla_ln_g"], (1, N_HEADS)).reshape(L, 1, W_MIX)

    eye = jnp.eye(4, dtype=F32)
    bd = lambda w: jnp.einsum("ldgij,gh->ldgihj", w, eye).reshape(L, N_DIR, W_MIX, W_MIX)
    out["lru_cw"] = d["lru_conv_w"]
    out["lru_cb"] = d["lru_conv_b"].reshape(L, N_DIR, 1, W_MIX)
    out["lru_wax"] = jnp.concatenate([bd(d["lru_wa"]), bd(d["lru_wx"])], -1).astype(BF16)
    out["lru_bax"] = jnp.concatenate([d["lru_ba"], d["lru_bx"]], -1).reshape(L, N_DIR, 1, 2 * W_MIX)
    out["lru_lam"] = d["lru_lam"].reshape(L, N_DIR, 1, W_MIX)

    out["cv_dw_w"] = d["cv_dw_w"]
    for n in ("cv_dw_b", "cv_ln_g", "cv_ln_b", "cv_pw_b"):
        out[n] = d[n].reshape(L, 1, W_MIX)
    out["cv_pw_w"] = d["cv_pw_w"].astype(BF16)

    if "w_in" in d:
        w_in = d["w_in"]
        o1, o2, o3 = 960, 1760, 2272
        out["w_in"] = jnp.concatenate([w_in[..., 0:o1], z(L, D_MODEL, ZA_W - o1), _gla_pack(w_in[..., o1:o2]),
                                       w_in[..., o2:o3], w_in[..., o3:]], -1).astype(BF16)
        out["w_out"] = d["w_out"].astype(BF16)
        out["norm1_g"] = d["norm1_g"].reshape(L, 1, D_MODEL)
        out["norm2_g"] = d["norm2_g"].reshape(L, 1, D_MODEL)
        for n in ("e_w1", "e_w3", "e_w2"):
            out[n] = d[n].astype(BF16)
    return out


def kernel(x_prompt, x_sample, state_rwkv, state_gla, state_lru, c, c_ctx, norm1_g, norm2_g, norm_f_g, w_ada, b_ada, w_in, w_out, rw_mu, rw_w0, rw_w2, rw_a0, rw_a2, rw_g2, rw_kk, rw_ka, rw_rk, rw_ln_g, rw_ln_b, gla_gk2, gla_gkb, gla_ln_g, lru_conv_w, lru_conv_b, lru_wa, lru_ba, lru_wx, lru_bx, lru_lam, cv_dw_w, cv_dw_b, cv_ln_g, cv_ln_b, cv_pw_w, cv_pw_b, w_router, b_router, e_w1, e_w3, e_w2):
    d = dict(norm1_g=norm1_g, norm2_g=norm2_g, w_in=w_in, w_out=w_out,
             rw_mu=rw_mu, rw_w0=rw_w0, rw_w2=rw_w2, rw_a0=rw_a0, rw_a2=rw_a2, rw_g2=rw_g2, rw_kk=rw_kk,
             rw_ka=rw_ka, rw_rk=rw_rk, rw_ln_g=rw_ln_g, rw_ln_b=rw_ln_b,
             gla_gk2=gla_gk2, gla_gkb=gla_gkb, gla_ln_g=gla_ln_g,
             lru_conv_w=lru_conv_w, lru_conv_b=lru_conv_b, lru_wa=lru_wa, lru_ba=lru_ba, lru_wx=lru_wx,
             lru_bx=lru_bx, lru_lam=lru_lam,
             cv_dw_w=cv_dw_w, cv_dw_b=cv_dw_b, cv_ln_g=cv_ln_g, cv_ln_b=cv_ln_b, cv_pw_w=cv_pw_w, cv_pw_b=cv_pw_b,
             e_w1=e_w1, e_w3=e_w3, e_w2=e_w2)
    P = _prep_params(d)
    shared = {"w_router_t": w_router.T, "b_router": b_router.reshape(N_EXPERTS, 1)}
    norm_f = norm_f_g.reshape(1, D_MODEL)

    bp, tp, _ = x_prompt.shape
    bs, ts, _ = x_sample.shape
    n_mod = 1 + bs
    mod_rows = -(-n_mod // 8) * 8
    cvec = jnp.concatenate([c_ctx[None], c, jnp.zeros((mod_rows - n_mod, D_MODEL), F32)], 0)
    mod = _ada_mod(cvec, w_ada, b_ada)

    tp_flat = min(bp * tp, 1024)
    bp_flat = bp * tp // tp_flat
    xp = x_prompt.reshape(bp_flat, tp_flat, D_MODEL)
    xs = x_sample
    zero_rw = jnp.zeros((bp, N_DIR, W_MIX, W_MIX), F32)
    zero_gla = jnp.zeros((bp, N_DIR, W_MIX, GLA_HK), F32)
    zero_lru = jnp.zeros((bp, N_DIR, W_MIX), F32)

    def layer(x, m, seq_shape, latent, s_rw, s_gla, s_lru, p, last):
        flat_shape = x.shape[:2]
        zs = _in_proj(x, m, p["norm1_g"], p["w_in"])
        za, zb, zc, zd = (z.reshape(seq_shape + (z.shape[-1],)) for z in zs)
        ya, f_rw = _rwkv_mixer(za, s_rw, p, latent)
        yb, f_gla = _gla_mixer(zb, s_gla, p)
        yc, f_lru = _lru_mixer(zc, s_lru, p)
        yd = _conv_mixer(zd, p, latent)
        ys = [y.reshape(flat_shape + (W_MIX,)) for y in (ya, yb, yc, yd)]
        return _out_moe(x, ys, m, p, norm_f, last), f_rw, f_gla, f_lru

    fin_rw, fin_gla, fin_lru = [], [], []
    for l in range(DEPTH):
        p = {n: a[l] for n, a in P.items()}
        p.update(shared)
        last = l == DEPTH - 1
        m_p = jnp.broadcast_to(mod[l, 0:1], (bp_flat, 6 * D_MODEL)).reshape(bp_flat, 1, 6 * D_MODEL)
        xp, f_rw, f_gla, f_lru = layer(xp, m_p, (bp, tp), False, zero_rw, zero_gla, zero_lru, p, last)
        fin_rw.append(_rw_state_out(f_rw))
        fin_gla.append(_gla_state_out(f_gla))
        fin_lru.append(f_lru)
        m_s = mod[l, 1:1 + bs].reshape(bs, 1, 6 * D_MODEL)
        xs, _, _, _ = layer(xs, m_s, (bs, ts), True, _rw_state_in(state_rwkv[:, l]),
                            _gla_state_in(state_gla[:, l]), state_lru[:, l], p, last)
    return (xp.reshape(bp, tp, D_MODEL), xs, jnp.stack(fin_rw, axis=1), jnp.stack(fin_gla, axis=1),
            jnp.stack(fin_lru, axis=1))
```

```python
import functools

import jax
import jax.numpy as jnp
from jax import lax
from jax.experimental import pallas as pl
from jax.experimental.pallas import tpu as pltpu

F32 = jnp.float32
BF16 = jnp.bfloat16
ACT = jnp.bfloat16

D_MODEL = 1024
DEPTH = 4
GRID_W = 64
N_DIR = 2
W_MIX = 256
N_HEADS = 4
HEAD = W_MIX // N_HEADS
GLA_DK = 32
GLA_HK = N_HEADS * GLA_DK
RW_LN_EPS = 64e-5
NORM_EPS = 1e-6
GLA_GATE_NORM = 16.0
LRU_C = 8.0
LRU_CONV = 4
CV_KERNEL = 31
CV_PAD = (CV_KERNEL - 1) // 2
N_EXPERTS = 16
GROUP_SIZE = 4
N_GROUPS = 4
D_EXPERT = 256
CHUNK = 64
EXP_M_HALF = 0.6065306597126334

ZA_W = 1024
ZB_W = 896
ZC_W = 512
ZD_W = 512
P_PAD = ZA_W + ZB_W + ZC_W + ZD_W

VMEM_LIMIT = 48 * 1024 * 1024


def _cparams(sem, vmem=VMEM_LIMIT):
    return pltpu.CompilerParams(dimension_semantics=sem, vmem_limit_bytes=vmem)


def _dot(a, b):
    return jnp.dot(a.astype(BF16), b.astype(BF16), preferred_element_type=F32)


def _dot_nt(a, b):
    return lax.dot_general(a.astype(BF16), b.astype(BF16), (((1,), (1,)), ((), ())),
                           preferred_element_type=F32)


def _dot_tn(a, b):
    return lax.dot_general(a.astype(BF16), b.astype(BF16), (((0,), (0,)), ((), ())),
                           preferred_element_type=F32)


def _split(a):
    hi = a.astype(BF16)
    lo = (a - hi.astype(F32)).astype(BF16)
    return hi, lo


def _dot_x2(a, b_exact):
    hi, lo = _split(a)
    bb = b_exact.astype(BF16)
    return (jnp.dot(hi, bb, preferred_element_type=F32) + jnp.dot(lo, bb, preferred_element_type=F32))


def _dot_left_x2(a_exact, b):
    hi, lo = _split(b)
    aa = a_exact.astype(BF16)
    return (jnp.dot(aa, hi, preferred_element_type=F32) + jnp.dot(aa, lo, preferred_element_type=F32))


def _dot3(a, b):
    ah, al = _split(a)
    bh, bl = _split(b)
    return (jnp.dot(ah, bh, preferred_element_type=F32) + jnp.dot(ah, bl, preferred_element_type=F32)
            + jnp.dot(al, bh, preferred_element_type=F32))


def _dot3_nt(a, b):
    ah, al = _split(a)
    bh, bl = _split(b)
    dn = (((1,), (1,)), ((), ()))
    return (lax.dot_general(ah, bh, dn, preferred_element_type=F32)
            + lax.dot_general(ah, bl, dn, preferred_element_type=F32)
            + lax.dot_general(al, bh, dn, preferred_element_type=F32))


def _iota(shape, axis):
    return lax.broadcasted_iota(jnp.int32, shape, axis)


def _block_mask(rows, cols, rblk, cblk):
    r = _iota((rows, cols), 0) // rblk
    c = _iota((rows, cols), 1) // cblk
    return (r == c).astype(F32)


def _expand(x, bm):
    xb = x.astype(BF16)
    return jnp.concatenate([xb, xb, xb, xb], axis=0) * bm


def _tri(d, strict):
    t = _iota((CHUNK, CHUNK), 0)
    s = _iota((CHUNK, CHUNK), 1)
    if d == 0:
        m = (s < t) if strict else (s <= t)
    else:
        m = (s > t) if strict else (s >= t)
    return m.astype(F32)


def _tri_wide(d, strict):
    t = _iota((CHUNK, N_HEADS * CHUNK), 0)
    s = _iota((CHUNK, N_HEADS * CHUNK), 1) % CHUNK
    if d == 0:
        m = (s < t) if strict else (s <= t)
    else:
        m = (s > t) if strict else (s >= t)
    return m.astype(F32)


ADA_TN = 1536


def _ada_kernel(c_ref, w_ref, b_ref, o_ref):
    c = c_ref[...]
    s = c * jax.nn.sigmoid(c)
    o_ref[0] = _dot3(s, w_ref[0]) + b_ref[0]


def _ada_mod(cvec, w_ada, b_ada):
    rows = cvec.shape[0]
    n_out = w_ada.shape[-1]
    return pl.pallas_call(
        _ada_kernel,
        grid=(DEPTH, n_out // ADA_TN),
        in_specs=[
            pl.BlockSpec((rows, D_MODEL), lambda l, j: (0, 0)),
            pl.BlockSpec((1, D_MODEL, ADA_TN), lambda l, j: (l, 0, j)),
            pl.BlockSpec((1, 1, ADA_TN), lambda l, j: (l, 0, j)),
        ],
        out_specs=pl.BlockSpec((1, rows, ADA_TN), lambda l, j: (l, 0, j)),
        out_shape=jax.ShapeDtypeStruct((DEPTH, rows, n_out), F32),
        compiler_params=_cparams(("arbitrary", "arbitrary")),
        name="ada_mod",
    )(cvec, w_ada, b_ada.reshape(DEPTH, 1, n_out))


IN_TM = 512


def _in_kernel(x_ref, sh_ref, sc_ref, g_ref, w_ref, za_ref, zb_ref, zc_ref, zd_ref):
    x = x_ref[0]
    h = x * lax.rsqrt(jnp.mean(x * x, axis=-1, keepdims=True) + NORM_EPS) * g_ref[...]
    h = (h * (1.0 + sc_ref[0]) + sh_ref[0]).astype(BF16)
    o0, o1, o2 = ZA_W, ZA_W + ZB_W, ZA_W + ZB_W + ZC_W
    za_ref[0] = jnp.dot(h, w_ref[:, 0:o0], preferred_element_type=F32).astype(ACT)
    zb_ref[0] = jnp.dot(h, w_ref[:, o0:o1], preferred_element_type=F32).astype(ACT)
    zc_ref[0] = jnp.dot(h, w_ref[:, o1:o2], preferred_element_type=F32).astype(ACT)
    zd_ref[0] = jnp.dot(h, w_ref[:, o2:P_PAD], preferred_element_type=F32).astype(ACT)


def _in_proj(x, m, g, w):
    b, t, _ = x.shape
    tm = min(IN_TM, t)
    tok = lambda w_: pl.BlockSpec((1, tm, w_), lambda i, j: (i, j, 0))
    mod = lambda k: pl.BlockSpec((1, 1, D_MODEL), lambda i, j, k=k: (i, 0, k))
    return pl.pallas_call(
        _in_kernel,
        grid=(b, t // tm),
        in_specs=[tok(D_MODEL), mod(0), mod(1),
                  pl.BlockSpec((1, D_MODEL), lambda i, j: (0, 0)),
                  pl.BlockSpec((D_MODEL, P_PAD), lambda i, j: (0, 0))],
        out_specs=[tok(ZA_W), tok(ZB_W), tok(ZC_W), tok(ZD_W)],
        out_shape=[jax.ShapeDtypeStruct((b, t, w_), ACT) for w_ in (ZA_W, ZB_W, ZC_W, ZD_W)],
        compiler_params=_cparams(("arbitrary", "arbitrary")),
        name="in_proj",
    )(x, m, m, g, w)


SHIFT_TB = 512


def _shift_grid_kernel(z_ref, up_ref, dn_ref, mu_ref, o_ref, *, nblk):
    i = pl.program_id(1)
    z = z_ref[0].astype(F32)
    tb, c = z.shape
    col = _iota((tb, c), 0) % GRID_W
    lane = _iota((tb, c), 1) % 4
    left = jnp.where(col == 0, 0.0, pltpu.roll(z, 1, 0))
    right = jnp.where(col == GRID_W - 1, 0.0, pltpu.roll(z, tb - 1, 0))
    up_halo = jnp.where(i > 0, up_ref[0, 0].astype(F32), 0.0)
    dn_halo = jnp.where(i < nblk - 1, dn_ref[0, 0].astype(F32), 0.0)
    up = jnp.concatenate([up_halo, z[:tb - GRID_W]], axis=0)
    down = jnp.concatenate([z[GRID_W:], dn_halo], axis=0)
    sh = jnp.where(lane == 0, left, jnp.where(lane == 1, right, jnp.where(lane == 2, up, down)))
    o_ref[0] = (z + (sh - z) * mu_ref[...]).astype(ACT)


def _shift_ctx_kernel(z_ref, mu_ref, o_ref):
    z = z_ref[0].astype(F32)
    t, c = z.shape
    row = _iota((t, c), 0)
    lane = _iota((t, c), 1) % 2
    prev = jnp.where(row == 0, 0.0, pltpu.roll(z, 1, 0))
    nxt = jnp.where(row == t - 1, 0.0, pltpu.roll(z, t - 1, 0))
    sh = jnp.where(lane == 0, prev, nxt)
    o_ref[0] = (z + (sh - z) * mu_ref[...]).astype(ACT)


def _rw_shift(za, mu, latent):
    b, t, c = za.shape
    mu_spec = pl.BlockSpec((1, c), lambda *_: (0, 0))
    if not latent:
        return pl.pallas_call(
            _shift_ctx_kernel,
            grid=(b,),
            in_specs=[pl.BlockSpec((1, t, c), lambda i: (i, 0, 0)), mu_spec],
            out_specs=pl.BlockSpec((1, t, c), lambda i: (i, 0, 0)),
            out_shape=jax.ShapeDtypeStruct((b, t, c), ACT),
            compiler_params=_cparams(("arbitrary",)),
            name="rw_shift_ctx",
        )(za, mu)
    tb = min(SHIFT_TB, t)
    nblk = t // tb
    rpb = tb // GRID_W
    nrow = t // GRID_W
    z4 = za.reshape(b, nrow, GRID_W, c)
    return pl.pallas_call(
        functools.partial(_shift_grid_kernel, nblk=nblk),
        grid=(b, nblk),
        in_specs=[
            pl.BlockSpec((1, tb, c), lambda i, j: (i, j, 0)),
            pl.BlockSpec((1, 1, GRID_W, c), lambda i, j: (i, jnp.maximum(j * rpb - 1, 0), 0, 0)),
            pl.BlockSpec((1, 1, GRID_W, c), lambda i, j: (i, jnp.minimum(j * rpb + rpb, nrow - 1), 0, 0)),
            mu_spec,
        ],
        out_specs=pl.BlockSpec((1, tb, c), lambda i, j: (i, j, 0)),
        out_shape=jax.ShapeDtypeStruct((b, t, c), ACT),
        compiler_params=_cparams(("arbitrary", "arbitrary")),
        name="rw_shift_grid",
    )(za, z4, z4, mu)


SCAN_TB = 512
PREP_SPLIT = 2
PREP_LOCKSTEP = 4
PREP_SKEW = 3
PREP_GROUP = 8


def _rw_scan_kernel(zf_ref, zb_ref, s0_ref, wl_ref, lb_ref, kkp_ref, kap_ref,
                    y_ref, st_ref,
                    lw_s, a_s, b_s, kd_s, r2_s, mc_s, ds_s, gt_s, *, nchunk, nblk):
    i = pl.program_id(1)

    @pl.when(i == 0)
    def _():
        st_ref[...] = s0_ref[...]
        y_ref[...] = jnp.zeros(y_ref.shape, F32)

    def y_rows(d, c):
        blk = i if d == 0 else nblk - 1 - i
        return pl.ds(pl.multiple_of((blk * nchunk + c) * CHUNK, CHUNK), CHUNK)

    ones_blk = _block_mask(W_MIX, W_MIX, HEAD, HEAD)
    lane128 = _iota((1, 128), 1)
    z_refs = (zf_ref, zb_ref)

    for d in range(N_DIR):
        k = z_refs[d][0, :, 256:512].astype(F32)
        la = z_refs[d][0, :, 768:896].astype(F32)
        la_t = jnp.where(lane128 < 64, jnp.tanh(la), la)
        wraw = _dot(la_t, wl_ref[:, 256 * d:256 * d + 256]) + lb_ref[:, 256 * d:256 * d + 256]
        araw = _dot(la_t, wl_ref[:, 512 + 256 * d:768 + 256 * d]) + lb_ref[:, 512 + 256 * d:768 + 256 * d]
        lw_s[d] = jax.nn.sigmoid(wraw) * (-EXP_M_HALF)
        icl = jax.nn.sigmoid(araw)
        kkv = k * kkp_ref[...]
        ss = _dot_x2(kkv * kkv, ones_blk)
        kk = kkv * lax.rsqrt(ss + 1e-12)
        a_s[d] = -kk
        b_s[d] = kk * icl
        kd_s[d] = k * (1.0 + (icl - 1.0) * kap_ref[...])

    bm = _block_mask(N_HEADS * CHUNK, W_MIX, CHUNK, HEAD).astype(BF16)
    tri = [_tri(d, False) for d in range(N_DIR)]
    tw_strict = [_tri_wide(d, True) for d in range(N_DIR)]
    tw_incl = [_tri_wide(d, False) for d in range(N_DIR)]
    eye_w = (_iota((CHUNK, N_HEADS * CHUNK), 0) == _iota((CHUNK, N_HEADS * CHUNK), 1) % CHUNK).astype(F32)

    def rows_of(c, n):
        return pl.ds(pl.multiple_of(c * n, n), n)

    group = min(PREP_GROUP, nchunk)

    each = lambda fn, *ls: [fn(*xs) for xs in zip(*ls)]

    def prep_stages(ch):
        ds_ = [d for d, _ in ch]
        st = {}

        def front():
            load = lambda ref: [ref[d, rows_of(c, CHUNK), :] for d, c in ch]
            lw, a, st["b"], st["kd"] = load(lw_s), load(a_s), load(b_s), load(kd_s)
            r = [z_refs[d][0, rows_of(c, CHUNK), 0:256].astype(F32) for d, c in ch]
            st["v"] = [z_refs[d][0, rows_of(c, CHUNK), 512:768].astype(F32) for d, c in ch]
            cum = each(lambda d, x: _dot_left_x2(tri[d], x), ds_, lw)
            st["tot"] = each(lambda d, x: x[CHUNK - 1:CHUNK, :] if d == 0 else x[0:1, :], ds_, cum)
            st["g_inv"] = each(lambda x: jnp.exp(-x), cum)
            st["g_rem"] = each(lambda t_, x: jnp.exp(t_ - x), st["tot"], cum)
            st["at"] = each(lambda a_, x, l: a_ * jnp.exp(x - l), a, cum, lw)
            st["rt"] = each(lambda r_, x: r_ * jnp.exp(x), r, cum)

        def gram():
            x = each(lambda p, q: jnp.concatenate([p, q], axis=0), st["at"], st["rt"])
            gb = each(lambda x_, b_, g: _dot_nt(x_, _expand(b_ * g, bm)), x, st["b"], st["g_inv"])
            gk = each(lambda x_, k_, g: _dot_nt(x_, _expand(k_ * g, bm)), x, st["kd"], st["g_inv"])
            st["vbd"] = each(lambda v_: _expand(v_, bm), st["v"])
            st["apow"] = each(lambda d, g: g[0:CHUNK] * tw_strict[d], ds_, gb)
            st["av"] = each(lambda d, g, vb: _dot(g[0:CHUNK] * tw_strict[d], vb), ds_, gk, st["vbd"])
            st["tw"] = each(lambda m: eye_w + m, st["apow"])
            st["r_b"] = each(lambda d, g: g[CHUNK:] * tw_incl[d], ds_, gb)
            st["r_k"] = each(lambda d, g: g[CHUNK:] * tw_incl[d], ds_, gk)

        def double():
            st["apow"] = each(lambda m: _dot(m, _expand(m, bm)), st["apow"])
            st["tw"] = each(lambda t_, m: t_ + _dot(m, _expand(t_, bm)), st["tw"], st["apow"])

        def apply():
            st["a2"] = each(lambda t_, m: _dot(t_, _expand(m, bm)), st["tw"], st["at"])
            st["u0"] = each(lambda t_, m: _dot(t_, _expand(m, bm)), st["tw"], st["av"])

        def finish():
            r2 = each(lambda rt_, rb, m: rt_ + _dot(rb, _expand(m, bm)), st["rt"], st["r_b"], st["a2"])
            y0 = each(lambda rb, u, rk, vb: _dot(rb, _expand(u, bm)) + _dot(rk, vb),
                      st["r_b"], st["u0"], st["r_k"], st["vbd"])
            bl = each(lambda b_, g: b_ * g, st["b"], st["g_rem"])
            kl = each(lambda k_, g: k_ * g, st["kd"], st["g_rem"])
            mc = each(lambda m, bl_: _dot_tn(m, bl_) * ones_blk, st["a2"], bl)
            ds0 = each(lambda u, v_, bl_, kl_: _dot_tn(jnp.concatenate([u, v_], axis=0),
                                                       jnp.concatenate([bl_, kl_], axis=0)) * ones_blk,
                       st["u0"], st["v"], bl, kl)
            for n, (d, c) in enumerate(ch):
                rows = rows_of(c, CHUNK)
                y_ref[0, y_rows(d, c), :] += y0[n]
                r2_s[d, rows, :] = r2[n].astype(BF16)
                mc_s[d, rows_of(c, W_MIX), :] = mc[n].astype(BF16)
                ds_s[d, rows_of(c, W_MIX), :] = ds0[n]
                gt_s[d, rows_of(c, 8), :] = jnp.broadcast_to(jnp.exp(st["tot"][n]), (8, W_MIX))

        return [front, gram] + [double] * 5 + [apply, finish]

    def prep_body(gi, carry):
        half = group // PREP_SPLIT if group >= PREP_SPLIT * PREP_LOCKSTEP else group
        halves = [prep_stages([(d, gi * group + j) for j in range(h0, min(h0 + half, group)) for d in range(N_DIR)])
                  for h0 in range(0, group, half)]
        n_stage = len(halves[0])
        for step in range(n_stage + PREP_SKEW * (len(halves) - 1)):
            for hi, stages in enumerate(halves):
                k = step - PREP_SKEW * hi
                if 0 <= k < n_stage:
                    stages[k]()
        return carry

    lax.fori_loop(0, nchunk // group, prep_body, 0)

    def scan_body(cc, carry):
        cs = [cc, nchunk - 1 - cc]
        s = [st_ref[0, d] for d in range(N_DIR)]
        sb = [x.astype(BF16) for x in s]
        upd = [_dot(sb[d], mc_s[d, rows_of(cs[d], W_MIX), :]) for d in range(N_DIR)]
        ys = [_dot_nt(r2_s[d, rows_of(cs[d], CHUNK), :], sb[d]) for d in range(N_DIR)]
        for d in range(N_DIR):
            st_ref[0, d] = (s[d] * gt_s[d, rows_of(cs[d], 8), :][0:1, :] + upd[d]
                            + ds_s[d, rows_of(cs[d], W_MIX), :])
            y_ref[0, y_rows(d, cs[d]), :] += ys[d]
        return carry

    lax.fori_loop(0, nchunk, scan_body, 0)


def _rw_scan(zs, s0, wl, lb, kkp, kap):
    b, t, c = zs.shape
    tb = min(SCAN_TB, t)
    nblk = t // tb
    nchunk = tb // CHUNK
    vec = lambda n: pl.BlockSpec((1, n), lambda i, j: (0, 0))
    st_spec = pl.BlockSpec((1, N_DIR, W_MIX, W_MIX), lambda i, j: (i, 0, 0, 0))
    return pl.pallas_call(
        functools.partial(_rw_scan_kernel, nchunk=nchunk, nblk=nblk),
        grid=(b, nblk),
        in_specs=[
            pl.BlockSpec((1, tb, c), lambda i, j: (i, j, 0)),
            pl.BlockSpec((1, tb, c), lambda i, j: (i, nblk - 1 - j, 0)),
            st_spec,
            pl.BlockSpec((128, 1024), lambda i, j: (0, 0)),
            vec(1024), vec(W_MIX), vec(W_MIX),
        ],
        out_specs=[pl.BlockSpec((1, t, W_MIX), lambda i, j: (i, 0, 0)), st_spec],
        out_shape=[
            jax.ShapeDtypeStruct((b, t, W_MIX), F32),
            jax.ShapeDtypeStruct((b, N_DIR, W_MIX, W_MIX), F32),
        ],
        scratch_shapes=[pltpu.VMEM((N_DIR, tb, W_MIX), F32) for _ in range(4)] + [
            pltpu.VMEM((N_DIR, tb, W_MIX), BF16),
            pltpu.VMEM((N_DIR, nchunk * W_MIX, W_MIX), BF16),
            pltpu.VMEM((N_DIR, nchunk * W_MIX, W_MIX), F32),
            pltpu.VMEM((N_DIR, nchunk * 8, W_MIX), F32),
        ],
        compiler_params=_cparams(("arbitrary", "arbitrary")),
        name="rw_scan",
    )(zs, zs, s0, wl, lb, kkp, kap)


EPI_TM = 512


def _rw_epi_kernel(y_ref, z_ref, lng_ref, lnb_ref, rk_ref, g2_ref, o_ref):
    ones_blk = _block_mask(W_MIX, W_MIX, HEAD, HEAD)
    y = y_ref[0]
    mu = _dot_x2(y, ones_blk) * (1.0 / HEAD)
    yc = y - mu
    var = _dot_x2(yc * yc, ones_blk) * (1.0 / HEAD)
    yn = yc * lax.rsqrt(var + RW_LN_EPS) * lng_ref[...] + lnb_ref[...]
    r = z_ref[0, :, 0:256].astype(F32)
    k = z_ref[0, :, 256:512].astype(F32)
    v = z_ref[0, :, 512:768].astype(F32)
    gl = z_ref[0, :, 896:1024].astype(F32)
    bonus = _dot_x2(r * k * rk_ref[...], ones_blk) * v
    gate = _dot(jax.nn.sigmoid(gl), g2_ref[...])
    o_ref[0] = ((yn + bonus) * gate).astype(ACT)


def _rw_epi(y, zs, lng, lnb, rk, g2p):
    b, t, _ = y.shape
    tm = min(EPI_TM, t)
    tok = lambda w_: pl.BlockSpec((1, tm, w_), lambda i, j: (i, j, 0))
    vec = pl.BlockSpec((1, W_MIX), lambda i, j: (0, 0))
    return pl.pallas_call(
        _rw_epi_kernel,
        grid=(b, t // tm),
        in_specs=[tok(W_MIX), tok(ZA_W), vec, vec, vec,
                  pl.BlockSpec((128, W_MIX), lambda i, j: (0, 0))],
        out_specs=tok(W_MIX),
        out_shape=jax.ShapeDtypeStruct((b, t, W_MIX), ACT),
        compiler_params=_cparams(("arbitrary", "arbitrary")),
        name="rw_epi",
    )(y, zs, lng, lnb, rk, g2p)


def _rwkv_mixer(za, s0_bd, p, latent):
    zs = _rw_shift(za, p["rw_mu"], latent)
    y, st = _rw_scan(zs, s0_bd, p["rw_wl"], p["rw_lb"], p["rw_kk"], p["rw_ka"])
    ya = _rw_epi(y, zs, p["rw_ln_g"], p["rw_ln_b"], p["rw_rk"], p["rw_g2"])
    return ya, st


def _gla_scan_kernel(zf_ref, zb_ref, s0_ref, wg_ref, gb_ref, y_ref, st_ref,
                     la_s, qe_s, ds_s, dec_s, *, nchunk, nblk):
    i = pl.program_id(1)

    @pl.when(i == 0)
    def _():
        st_ref[...] = s0_ref[...]
        y_ref[...] = jnp.zeros(y_ref.shape, F32)

    def y_rows(d, c):
        blk = i if d == 0 else nblk - 1 - i
        return pl.ds(pl.multiple_of((blk * nchunk + c) * CHUNK, CHUNK), CHUNK)

    z_refs = (zf_ref, zb_ref)
    for d in range(N_DIR):
        logit = _dot(z_refs[d][0, :, 768:896], wg_ref[d]) + gb_ref[d]
        la_s[d] = jax.nn.log_sigmoid(logit) * (1.0 / GLA_GATE_NORM)

    bm_k = _block_mask(N_HEADS * CHUNK, GLA_HK, CHUNK, GLA_DK).astype(BF16)
    bm_v = _block_mask(N_HEADS * CHUNK, W_MIX, CHUNK, HEAD).astype(BF16)
    bm_s = _block_mask(W_MIX, GLA_HK, HEAD, GLA_DK)
    tri = [_tri(d, False) for d in range(N_DIR)]
    tw_incl = [_tri_wide(d, False) for d in range(N_DIR)]

    def rows_of(c, n):
        return pl.ds(pl.multiple_of(c * n, n), n)

    group = min(PREP_GROUP, nchunk)

    def prep_body(gi, carry):
        ch = [(d, gi * group + j) for j in range(group) for d in range(N_DIR)]
        each = lambda fn, *ls: [fn(*xs) for xs in zip(*ls)]
        ds_ = [d for d, _ in ch]
        la = [la_s[d, rows_of(c, CHUNK), :] for d, c in ch]
        q = [z_refs[d][0, rows_of(c, CHUNK), 0:128].astype(F32) * (GLA_DK ** -0.5) for d, c in ch]
        k = [z_refs[d][0, rows_of(c, CHUNK), 128:256].astype(F32) for d, c in ch]
        v = [z_refs[d][0, rows_of(c, CHUNK), 256:512].astype(F32) for d, c in ch]
        cum = each(lambda d, x: _dot_left_x2(tri[d], x), ds_, la)
        last = each(lambda d, x: x[CHUNK - 1:CHUNK, :] if d == 0 else x[0:1, :], ds_, cum)
        qe = each(lambda q_, x: q_ * jnp.exp(x), q, cum)
        ke = each(lambda k_, x: k_ * jnp.exp(-x), k, cum)
        kl = each(lambda k_, l, x: k_ * jnp.exp(l - x), k, last, cum)
        att = each(lambda d, q_, k_: _dot_nt(q_, _expand(k_, bm_k)) * tw_incl[d], ds_, qe, ke)
        o = each(lambda a_, v_: _dot(a_, _expand(v_, bm_v)), att, v)
        dst = each(lambda v_, k_: _dot_tn(v_, k_) * bm_s, v, kl)
        for n, (d, c) in enumerate(ch):
            y_ref[0, y_rows(d, c), :] += o[n]
            qe_s[d, rows_of(c, CHUNK), :] = qe[n].astype(BF16)
            ds_s[d, rows_of(c, W_MIX), :] = dst[n]
            dec_s[d, rows_of(c, 8), :] = jnp.broadcast_to(jnp.exp(last[n]), (8, GLA_HK))
        return carry

    lax.fori_loop(0, nchunk // group, prep_body, 0)

    def scan_body(cc, carry):
        cs = [cc, nchunk - 1 - cc]
        for d in range(N_DIR):
            s = st_ref[0, d]
            y_ref[0, y_rows(d, cs[d]), :] += _dot_nt(qe_s[d, rows_of(cs[d], CHUNK), :], s)
            st_ref[0, d] = s * dec_s[d, rows_of(cs[d], 8), :][0:1, :] + ds_s[d, rows_of(cs[d], W_MIX), :]
        return carry

    lax.fori_loop(0, nchunk, scan_body, 0)


def _gla_scan(zb, s0, wg, gb):
    b, t, c = zb.shape
    tb = min(SCAN_TB, t)
    nblk = t // tb
    st_spec = pl.BlockSpec((1, N_DIR, W_MIX, GLA_HK), lambda i, j: (i, 0, 0, 0))
    nchunk = tb // CHUNK
    return pl.pallas_call(
        functools.partial(_gla_scan_kernel, nchunk=nchunk, nblk=nblk),
        grid=(b, nblk),
        in_specs=[
            pl.BlockSpec((1, tb, c), lambda i, j: (i, j, 0)),
            pl.BlockSpec((1, tb, c), lambda i, j: (i, nblk - 1 - j, 0)),
            st_spec,
            pl.BlockSpec((N_DIR, 128, GLA_HK), lambda i, j: (0, 0, 0)),
            pl.BlockSpec((N_DIR, 1, GLA_HK), lambda i, j: (0, 0, 0)),
        ],
        out_specs=[pl.BlockSpec((1, t, W_MIX), lambda i, j: (i, 0, 0)), st_spec],
        out_shape=[
            jax.ShapeDtypeStruct((b, t, W_MIX), F32),
            jax.ShapeDtypeStruct((b, N_DIR, W_MIX, GLA_HK), F32),
        ],
        scratch_shapes=[
            pltpu.VMEM((N_DIR, tb, GLA_HK), F32),
            pltpu.VMEM((N_DIR, tb, GLA_HK), BF16),
            pltpu.VMEM((N_DIR, nchunk * W_MIX, GLA_HK), F32),
            pltpu.VMEM((N_DIR, nchunk * 8, GLA_HK), F32),
        ],
        compiler_params=_cparams(("arbitrary", "arbitrary")),
        name="gla_scan",
    )(zb, zb, s0, wg, gb)


def _gla_epi_kernel(y_ref, z_ref, g_ref, o_ref):
    ones_blk = _block_mask(W_MIX, W_MIX, HEAD, HEAD)
    y = y_ref[0]
    ms = _dot_x2(y * y, ones_blk) * (1.0 / HEAD)
    y = y * lax.rsqrt(ms + NORM_EPS) * g_ref[...]
    og = z_ref[0].astype(F32)
    o_ref[0] = (y * (og * jax.nn.sigmoid(og))).astype(ACT)


def _gla_epi(y, zb, g):
    b, t, _ = y.shape
    tm = min(EPI_TM, t)
    tok = lambda w_: pl.BlockSpec((1, tm, w_), lambda i, j: (i, j, 0))
    return pl.pallas_call(
        _gla_epi_kernel,
        grid=(b, t // tm),
        in_specs=[tok(W_MIX),
                  pl.BlockSpec((1, tm, W_MIX), lambda i, j: (i, j, 2)),
                  pl.BlockSpec((1, W_MIX), lambda i, j: (0, 0))],
        out_specs=tok(W_MIX),
        out_shape=jax.ShapeDtypeStruct((b, t, W_MIX), ACT),
        compiler_params=_cparams(("arbitrary", "arbitrary")),
        name="gla_epi",
    )(y, zb, g)


def _gla_mixer(zb, s0_bd, p):
    y, st = _gla_scan(zb, s0_bd, p["gla_wg"], p["gla_gb"])
    return _gla_epi(y, zb, p["gla_ln_g"]), st


LRU_TT = 256
LRU_HALO = 8


def _lru_kernel(z_ref, h0_ref, cw_ref, cb_ref, wax_ref, bax_ref, lam_ref, y_ref, hf_ref, xpad, hfwd, *, t):
    tt = min(LRU_TT, t)
    ntile = t // tt
    xpad[0:LRU_HALO, :] = jnp.zeros((LRU_HALO, W_MIX), F32)
    xpad[LRU_HALO + t:2 * LRU_HALO + t, :] = jnp.zeros((LRU_HALO, W_MIX), F32)

    def fill(j, carry):
        base = pl.multiple_of(j * tt, tt)
        xpad[pl.ds(base + LRU_HALO, tt), :] = z_ref[0, pl.ds(base, tt), 0:W_MIX].astype(F32)
        return carry

    lax.fori_loop(0, ntile, fill, 0)
    row = _iota((tt, W_MIX), 0)

    def tile_scan(j, h, d):
        base = pl.multiple_of(j * tt, tt)
        win = xpad[pl.ds(base, tt + 2 * LRU_HALO), :]
        xc = jnp.zeros((tt, W_MIX), F32) + cb_ref[d]
        for tap in range(LRU_CONV):
            off = LRU_HALO - (LRU_CONV - 1) + tap if d == 0 else LRU_HALO + (LRU_CONV - 1) - tap
            xc = xc + cw_ref[d, tap:tap + 1, :] * win[off:off + tt, :]
        g = _dot(xc, wax_ref[d]) + bax_ref[d]
        gr = jax.nn.sigmoid(g[:, 0:W_MIX])
        gi = jax.nn.sigmoid(g[:, W_MIX:2 * W_MIX])
        log_a = -LRU_C * gr * jax.nn.softplus(-lam_ref[d])
        a = jnp.exp(log_a)
        bv = jnp.sqrt(1.0 - jnp.exp(2.0 * log_a)) * gi * xc
        s = 1
        while s < tt:
            if d == 0:
                keep = row >= s
                a_sh = jnp.where(keep, pltpu.roll(a, s, 0), 1.0)
                b_sh = jnp.where(keep, pltpu.roll(bv, s, 0), 0.0)
            else:
                keep = row < tt - s
                a_sh = jnp.where(keep, pltpu.roll(a, tt - s, 0), 1.0)
                b_sh = jnp.where(keep, pltpu.roll(bv, tt - s, 0), 0.0)
            bv = a * b_sh + bv
            a = a * a_sh
            s *= 2
        return a * h + bv, base

    def fwd(j, h):
        ht, base = tile_scan(j, h, 0)
        hfwd[pl.ds(base, tt), :] = ht
        return ht[tt - 1:tt, :]

    h_end = lax.fori_loop(0, ntile, fwd, h0_ref[0, 0:1, :])
    hf_ref[0, 0:1, :] = h_end

    def bwd(jj, h):
        j = ntile - 1 - jj
        ht, base = tile_scan(j, h, 1)
        gb = z_ref[0, pl.ds(base, tt), W_MIX:2 * W_MIX].astype(F32)
        y_ref[0, pl.ds(base, tt), :] = ((hfwd[pl.ds(base, tt), :] + ht) * jax.nn.gelu(gb)).astype(ACT)
        return ht[0:1, :]

    h_end = lax.fori_loop(0, ntile, bwd, h0_ref[0, 1:2, :])
    hf_ref[0, 1:2, :] = h_end


def _lru_mixer(zc, h0, p):
    b, t, c = zc.shape
    full = lambda *s: pl.BlockSpec(s, lambda i: (0,) * len(s))
    return pl.pallas_call(
        functools.partial(_lru_kernel, t=t),
        grid=(b,),
        in_specs=[
            pl.BlockSpec((1, t, c), lambda i: (i, 0, 0)),
            pl.BlockSpec((1, N_DIR, W_MIX), lambda i: (i, 0, 0)),
            full(N_DIR, LRU_CONV, W_MIX), full(N_DIR, 1, W_MIX),
            full(N_DIR, W_MIX, 2 * W_MIX), full(N_DIR, 1, 2 * W_MIX), full(N_DIR, 1, W_MIX),
        ],
        out_specs=[pl.BlockSpec((1, t, W_MIX), lambda i: (i, 0, 0)),
                   pl.BlockSpec((1, N_DIR, W_MIX), lambda i: (i, 0, 0))],
        out_shape=[jax.ShapeDtypeStruct((b, t, W_MIX), ACT), jax.ShapeDtypeStruct((b, N_DIR, W_MIX), F32)],
        scratch_shapes=[pltpu.VMEM((t + 2 * LRU_HALO, W_MIX), F32), pltpu.VMEM((t, W_MIX), F32)],
        compiler_params=_cparams(("arbitrary",)),
        name="lru",
    )(zc, h0, p["lru_cw"], p["lru_cb"], p["lru_wax"], p["lru_bax"], p["lru_lam"])


CV_TT = 256
CV_WIN = 16


def _conv_kernel(z_ref, dw_ref, dwb_ref, lng_ref, lnb_ref, pw_ref, pwb_ref, y_ref, upad, *, t, latent):
    tt = min(CV_TT, t)
    ntile = t // tt
    pad = CV_PAD * GRID_W if latent else CV_WIN
    upad[0:pad, :] = jnp.zeros((pad, W_MIX), F32)
    upad[pad + t:2 * pad + t, :] = jnp.zeros((pad, W_MIX), F32)

    def fill(j, carry):
        base = pl.multiple_of(j * tt, tt)
        z = z_ref[0, pl.ds(base, tt), :].astype(F32)
        upad[pl.ds(base + pad, tt), :] = z[:, 0:W_MIX] * jax.nn.sigmoid(z[:, W_MIX:2 * W_MIX])
        return carry

    lax.fori_loop(0, ntile, fill, 0)

    def tile(j, carry):
        base = pl.multiple_of(j * tt, tt)
        if latent:
            half = W_MIX // 2
            col = _iota((tt, half), 0) % GRID_W
            win = upad[pl.ds(pl.multiple_of(base + pad - CV_WIN, CV_WIN), tt + 2 * CV_WIN), 0:half]
            accw = jnp.zeros((tt, half), F32)
            acch = jnp.zeros((tt, half), F32)
            for tap in range(CV_KERNEL):
                dlt = tap - CV_PAD
                x = win[CV_WIN + dlt:CV_WIN + dlt + tt, :]
                if dlt < 0:
                    x = jnp.where(col >= -dlt, x, 0.0)
                elif dlt > 0:
                    x = jnp.where(col < GRID_W - dlt, x, 0.0)
                accw = accw + dw_ref[tap:tap + 1, 0:half] * x
                rows = pl.ds(pl.multiple_of(base + pad + dlt * GRID_W, GRID_W), tt)
                acch = acch + dw_ref[tap:tap + 1, half:W_MIX] * upad[rows, half:W_MIX]
            u = jnp.concatenate([accw, acch], axis=1)
        else:
            win = upad[pl.ds(base + pad - CV_WIN, tt + 2 * CV_WIN), :]
            u = jnp.zeros((tt, W_MIX), F32)
            for tap in range(CV_KERNEL):
                dlt = tap - CV_PAD
                u = u + dw_ref[tap:tap + 1, :] * win[CV_WIN + dlt:CV_WIN + dlt + tt, :]
        u = u + dwb_ref[...]
        mu = jnp.mean(u, axis=-1, keepdims=True)
        uc = u - mu
        var = jnp.mean(uc * uc, axis=-1, keepdims=True)
        un = uc * lax.rsqrt(var + 1e-5) * lng_ref[...] + lnb_ref[...]
        un = un * jax.nn.sigmoid(un)
        y_ref[0, pl.ds(base, tt), :] = (_dot(un, pw_ref[...]) + pwb_ref[...]).astype(ACT)
        return carry

    lax.fori_loop(0, ntile, tile, 0)


def _conv_mixer(zd, p, latent):
    b, t, c = zd.shape
    pad = CV_PAD * GRID_W if latent else CV_WIN
    full = lambda *s: pl.BlockSpec(s, lambda i: (0,) * len(s))
    vec = full(1, W_MIX)
    return pl.pallas_call(
        functools.partial(_conv_kernel, t=t, latent=latent),
        grid=(b,),
        in_specs=[pl.BlockSpec((1, t, c), lambda i: (i, 0, 0)),
                  full(CV_KERNEL, W_MIX), vec, vec, vec, full(W_MIX, W_MIX), vec],
        out_specs=pl.BlockSpec((1, t, W_MIX), lambda i: (i, 0, 0)),
        out_shape=jax.ShapeDtypeStruct((b, t, W_MIX), ACT),
        scratch_shapes=[pltpu.VMEM((t + 2 * pad, W_MIX), F32)],
        compiler_params=_cparams(("arbitrary",)),
        name="conv_grid" if latent else "conv_ctx",
    )(zd, p["cv_dw_w"], p["cv_dw_b"], p["cv_ln_g"], p["cv_ln_b"], p["cv_pw_w"], p["cv_pw_b"])


MOE_TM = 512
MOE_VMEM = VMEM_LIMIT
MOE_RB = 128
E_PAD = 128


def _route(sel, scores):
    grp = []
    for g in range(N_GROUPS):
        s = sel[GROUP_SIZE * g:GROUP_SIZE * (g + 1)]
        best_pair = None
        for i in range(GROUP_SIZE):
            for j in range(i + 1, GROUP_SIZE):
                pair = s[i] + s[j]
                best_pair = pair if best_pair is None else jnp.maximum(best_pair, pair)
        grp.append(best_pair)
    best = jnp.zeros_like(grp[0], dtype=jnp.int32)
    top = grp[0]
    for g in range(1, N_GROUPS):
        better = grp[g] > top
        best = jnp.where(better, g, best)
        top = jnp.where(better, grp[g], top)
    neg = jnp.full_like(sel[0], -jnp.inf)
    msel = [jnp.where(best == e // GROUP_SIZE, sel[e], neg) for e in range(N_EXPERTS)]
    picks = []
    for _ in range(2):
        idx = jnp.zeros_like(best)
        top = msel[0]
        for e in range(1, N_EXPERTS):
            better = msel[e] > top
            idx = jnp.where(better, e, idx)
            top = jnp.where(better, msel[e], top)
        picks.append(idx)
        msel = [jnp.where(idx == e, neg, msel[e]) for e in range(N_EXPERTS)]
    chosen = [jnp.where((picks[0] == e) | (picks[1] == e), scores[e], 0.0) for e in range(N_EXPERTS)]
    total = chosen[0]
    for e in range(1, N_EXPERTS):
        total = total + chosen[e]
    return [ch / total for ch in chosen], best


def _moe_kernel(x_ref, ya_ref, yb_ref, yc_ref, yd_ref, wo_ref, g1_ref, sh2_ref, sc2_ref, g2_ref, n2_ref,
                wr_ref, br_ref, w1_ref, w3_ref, w2_ref, nf_ref, o_ref,
                x1_s, h2_s, gt_s, gate_s, acc_s, he_s, og_s, pmt_s, seg_s, *, final_norm):
    grp = pl.program_id(2)
    tm = x1_s.shape[0]

    @pl.when(grp == 0)
    def _():
        y = (_dot(ya_ref[0], wo_ref[0:256, :]) + _dot(yb_ref[0], wo_ref[256:512, :])
             + _dot(yc_ref[0], wo_ref[512:768, :]) + _dot(yd_ref[0], wo_ref[768:1024, :]))
        x1 = x_ref[0] + g1_ref[0] * y
        x1_s[...] = x1
        h2 = x1 * lax.rsqrt(jnp.mean(x1 * x1, axis=-1, keepdims=True) + NORM_EPS) * n2_ref[...]
        h2 = h2 * (1.0 + sc2_ref[0]) + sh2_ref[0]
        logits = _dot3_nt(wr_ref[...], h2)
        scores = jax.nn.sigmoid(logits)
        selm = scores + br_ref[...]
        gates, best = _route([selm[i:i + 1, :] for i in range(N_EXPERTS)],
                             [scores[i:i + 1, :] for i in range(N_EXPERTS)])
        gt_s[...] = jnp.zeros(gt_s.shape, F32)
        for i in range(N_EXPERTS):
            gt_s[i:i + 1, :] = gates[i]

        og = [jnp.where(best == g, 1.0, 0.0) for g in range(N_GROUPS)]
        og_s[...] = jnp.zeros(og_s.shape, F32)
        start = jnp.int32(0)
        starts = []
        for g in range(N_GROUPS):
            og_s[g:g + 1, :] = og[g]
            starts.append(start)
            seg_s[g] = start
            start = start + jnp.sum(og[g]).astype(jnp.int32)
            seg_s[N_GROUPS + g] = start
        before = jnp.where(_iota((tm, tm), 0) < _iota((tm, tm), 1), 1.0, 0.0).astype(BF16)
        rank = jnp.dot(og_s[...].astype(BF16), before, preferred_element_type=F32)
        pos = og[0] * (starts[0].astype(F32) + rank[0:1, :])
        for g in range(1, N_GROUPS):
            pos = pos + og[g] * (starts[g].astype(F32) + rank[g:g + 1, :])
        gt_s[N_EXPERTS:N_EXPERTS + 1, :] = pos
        gate_tok = gt_s[...].T
        slot_l = _iota((tm, tm), 1).astype(F32)
        slot_s = _iota((tm, tm), 0).astype(F32)
        pmt_s[...] = jnp.where(slot_l == gate_tok[:, N_EXPERTS:N_EXPERTS + 1], 1.0, 0.0).astype(BF16)
        pm = jnp.where(slot_s == pos, 1.0, 0.0).astype(BF16)
        h2_s[...] = jnp.dot(pm, h2.astype(BF16), preferred_element_type=F32).astype(BF16)
        gate_s[...] = _dot_left_x2(pm, gate_tok)
        acc_s[...] = jnp.zeros(acc_s.shape, F32)

    pick = (_iota((E_PAD, GROUP_SIZE * 128), 0)
            == GROUP_SIZE * grp + _iota((E_PAD, GROUP_SIZE * 128), 1) // 128).astype(F32)

    def block(rb, carry):
        rows = pl.ds(pl.multiple_of(rb * MOE_RB, MOE_RB), MOE_RB)
        gsel = _dot_x2(gate_s[rows, :], pick)
        h2 = h2_s[rows, :]
        for j in range(GROUP_SIZE):
            he = jnp.dot(h2, w1_ref[j], preferred_element_type=F32)
            he = he * jax.nn.sigmoid(he) * jnp.dot(h2, w3_ref[j], preferred_element_type=F32)
            g = gsel[:, 128 * j:128 * (j + 1)]
            he_s[rows, D_EXPERT * j:D_EXPERT * (j + 1)] = (he * jnp.concatenate([g, g], axis=1)).astype(BF16)
        w2g = w2_ref[...].reshape(GROUP_SIZE * D_EXPERT, D_MODEL)
        acc_s[rows, :] += jnp.dot(he_s[rows, :], w2g, preferred_element_type=F32)
        return carry

    first = seg_s[grp] // MOE_RB
    last = (seg_s[N_GROUPS + grp] + (MOE_RB - 1)) // MOE_RB
    lax.fori_loop(first, last, block, 0)

    @pl.when(grp == N_GROUPS - 1)
    def _():
        x2 = x1_s[...] + g2_ref[0] * _dot_left_x2(pmt_s[...], acc_s[...])
        if final_norm:
            x2 = x2 * lax.rsqrt(jnp.mean(x2 * x2, axis=-1, keepdims=True) + NORM_EPS) * nf_ref[...]
        o_ref[0] = x2


def _out_moe(x, ys, m, p, norm_f, final_norm):
    b, t, _ = x.shape
    tm = min(MOE_TM, t)
    tok = lambda w_: pl.BlockSpec((1, tm, w_), lambda i, j, e: (i, j, 0))
    mod = lambda k: pl.BlockSpec((1, 1, D_MODEL), lambda i, j, e, k=k: (i, 0, k))
    full = lambda *s: pl.BlockSpec(s, lambda i, j, e: (0,) * len(s))
    return pl.pallas_call(
        functools.partial(_moe_kernel, final_norm=final_norm),
        grid=(b, t // tm, N_GROUPS),
        in_specs=[tok(D_MODEL), tok(W_MIX), tok(W_MIX), tok(W_MIX), tok(W_MIX),
                  full(D_MODEL, D_MODEL), mod(2), mod(3), mod(4), mod(5), full(1, D_MODEL),
                  full(N_EXPERTS, D_MODEL), full(N_EXPERTS, 1),
                  pl.BlockSpec((GROUP_SIZE, D_MODEL, D_EXPERT), lambda i, j, e: (e, 0, 0)),
                  pl.BlockSpec((GROUP_SIZE, D_MODEL, D_EXPERT), lambda i, j, e: (e, 0, 0)),
                  pl.BlockSpec((GROUP_SIZE, D_EXPERT, D_MODEL), lambda i, j, e: (e, 0, 0)),
                  full(1, D_MODEL)],
        out_specs=tok(D_MODEL),
        out_shape=jax.ShapeDtypeStruct((b, t, D_MODEL), F32),
        scratch_shapes=[pltpu.VMEM((tm, D_MODEL), F32), pltpu.VMEM((tm, D_MODEL), BF16),
                        pltpu.VMEM((E_PAD, tm), F32), pltpu.VMEM((tm, E_PAD), F32),
                        pltpu.VMEM((tm, D_MODEL), F32), pltpu.VMEM((tm, D_EXPERT * GROUP_SIZE), BF16),
                        pltpu.VMEM((8, tm), F32), pltpu.VMEM((tm, tm), BF16),
                        pltpu.SMEM((2 * N_GROUPS,), jnp.int32)],
        compiler_params=_cparams(("arbitrary", "arbitrary", "arbitrary"), MOE_VMEM),
        name="out_moe",
    )(x, *ys, p["w_out"], m, m, m, m, p["norm2_g"], p["w_router_t"], p["b_router"], p["e_w1"], p["e_w3"], p["e_w2"],
      norm_f)


def _gla_pack(z):
    hk = GLA_HK
    lead = z.shape[:-1]
    gl = z[..., 2 * hk + W_MIX:2 * hk + W_MIX + 32]
    return jnp.concatenate([z[..., 0:2 * hk + W_MIX], z[..., 2 * hk + W_MIX + 32:], gl,
                            jnp.zeros(lead + (96,), z.dtype)], -1)


def _gla_state_in(s):
    b = s.shape[0]
    eye = jnp.eye(N_HEADS, dtype=s.dtype)
    return jnp.einsum("bdhkv,hg->bdhvgk", s, eye).reshape(b, N_DIR, W_MIX, GLA_HK)


def _gla_state_out(st):
    b = st.shape[0]
    eye = jnp.eye(N_HEADS, dtype=st.dtype)
    return jnp.einsum("bdhvgk,hg->bdhkv", st.reshape(b, N_DIR, N_HEADS, HEAD, N_HEADS, GLA_DK), eye)


def _rw_state_in(s):
    b = s.shape[0]
    eye = jnp.eye(N_HEADS, dtype=s.dtype)
    return jnp.einsum("bdhvk,hg->bdhvgk", s, eye).reshape(b, N_DIR, W_MIX, W_MIX)


def _rw_state_out(st):
    b = st.shape[0]
    eye = jnp.eye(N_HEADS, dtype=st.dtype)
    return jnp.einsum("bdhvgk,hg->bdhvk", st.reshape(b, N_DIR, N_HEADS, HEAD, N_HEADS, HEAD), eye)


def _prep_params(d):
    L = DEPTH
    z = lambda *s: jnp.zeros(s, F32)
    out = {}
    out["rw_mu"] = jnp.concatenate([d["rw_mu"], z(L, ZA_W - d["rw_mu"].shape[-1])], -1).reshape(L, 1, ZA_W)
    w2, a2 = d["rw_w2"], d["rw_a2"]
    zz = z(L, 32, 256)
    rows = [
        jnp.concatenate([w2[:, 0], zz, zz, zz], -1),
        jnp.concatenate([zz, w2[:, 1], zz, zz], -1),
        jnp.concatenate([zz, zz, a2[:, 0], zz], -1),
        jnp.concatenate([zz, zz, zz, a2[:, 1]], -1),
    ]
    out["rw_wl"] = jnp.concatenate(rows, 1).astype(BF16)
    out["rw_lb"] = jnp.concatenate([d["rw_w0"][:, 0], d["rw_w0"][:, 1], d["rw_a0"][:, 0], d["rw_a0"][:, 1]],
                                   -1).reshape(L, 1, 1024)
    for n in ("rw_kk", "rw_ka", "rw_ln_g", "rw_ln_b"):
        out[n] = d[n].reshape(L, 1, W_MIX)
    out["rw_rk"] = d["rw_rk"].reshape(L, 1, W_MIX)
    out["rw_g2"] = jnp.concatenate([d["rw_g2"], z(L, 64, W_MIX)], 1).astype(BF16)

    gk2 = d["gla_gk2"]
    z16, z96 = z(L, 16, GLA_HK), z(L, 96, GLA_HK)
    out["gla_wg"] = jnp.stack([jnp.concatenate([gk2[:, 0], z16, z96], 1),
                               jnp.concatenate([z16, gk2[:, 1], z96], 1)], 1).astype(BF16)
    out["gla_gb"] = d["gla_gkb"].reshape(L, N_DIR, 1, GLA_HK)
    out["gla_ln_g"] = jnp.tile(d["gla_ln_g"], (1, N_HEADS)).reshape(L, 1, W_MIX)

    eye = jnp.eye(4, dtype=F32)
    bd = lambda w: jnp.einsum("ldgij,gh->ldgihj", w, eye).reshape(L, N_DIR, W_MIX, W_MIX)
    out["lru_cw"] = d["lru_conv_w"]
    out["lru_cb"] = d["lru_conv_b"].reshape(L, N_DIR, 1, W_MIX)
    out["lru_wax"] = jnp.concatenate([bd(d["lru_wa"]), bd(d["lru_wx"])], -1).astype(BF16)
    out["lru_bax"] = jnp.concatenate([d["lru_ba"], d["lru_bx"]], -1).reshape(L, N_DIR, 1, 2 * W_MIX)
    out["lru_lam"] = d["lru_lam"].reshape(L, N_DIR, 1, W_MIX)

    out["cv_dw_w"] = d["cv_dw_w"]
    for n in ("cv_dw_b", "cv_ln_g", "cv_ln_b", "cv_pw_b"):
        out[n] = d[n].reshape(L, 1, W_MIX)
    out["cv_pw_w"] = d["cv_pw_w"].astype(BF16)

    if "w_in" in d:
        w_in = d["w_in"]
        o1, o2, o3 = 960, 1760, 2272
        out["w_in"] = jnp.concatenate([w_in[..., 0:o1], z(L, D_MODEL, ZA_W - o1), _gla_pack(w_in[..., o1:o2]),
                                       w_in[..., o2:o3], w_in[..., o3:]], -1).astype(BF16)
        out["w_out"] = d["w_out"].astype(BF16)
        out["norm1_g"] = d["norm1_g"].reshape(L, 1, D_MODEL)
        out["norm2_g"] = d["norm2_g"].reshape(L, 1, D_MODEL)
        for n in ("e_w1", "e_w3", "e_w2"):
            out[n] = d[n].astype(BF16)
    return out


def kernel(x_prompt, x_sample, state_rwkv, state_gla, state_lru, c, c_ctx, norm1_g, norm2_g, norm_f_g, w_ada, b_ada, w_in, w_out, rw_mu, rw_w0, rw_w2, rw_a0, rw_a2, rw_g2, rw_kk, rw_ka, rw_rk, rw_ln_g, rw_ln_b, gla_gk2, gla_gkb, gla_ln_g, lru_conv_w, lru_conv_b, lru_wa, lru_ba, lru_wx, lru_bx, lru_lam, cv_dw_w, cv_dw_b, cv_ln_g, cv_ln_b, cv_pw_w, cv_pw_b, w_router, b_router, e_w1, e_w3, e_w2):
    d = dict(norm1_g=norm1_g, norm2_g=norm2_g, w_in=w_in, w_out=w_out,
             rw_mu=rw_mu, rw_w0=rw_w0, rw_w2=rw_w2, rw_a0=rw_a0, rw_a2=rw_a2, rw_g2=rw_g2, rw_kk=rw_kk,
             rw_ka=rw_ka, rw_rk=rw_rk, rw_ln_g=rw_ln_g, rw_ln_b=rw_ln_b,
             gla_gk2=gla_gk2, gla_gkb=gla_gkb, gla_ln_g=gla_ln_g,
             lru_conv_w=lru_conv_w, lru_conv_b=lru_conv_b, lru_wa=lru_wa, lru_ba=lru_ba, lru_wx=lru_wx,
             lru_bx=lru_bx, lru_lam=lru_lam,
             cv_dw_w=cv_dw_w, cv_dw_b=cv_dw_b, cv_ln_g=cv_ln_g, cv_ln_b=cv_ln_b, cv_pw_w=cv_pw_w, cv_pw_b=cv_pw_b,
             e_w1=e_w1, e_w3=e_w3, e_w2=e_w2)
    P = _prep_params(d)
    shared = {"w_router_t": w_router.T, "b_router": b_router.reshape(N_EXPERTS, 1)}
    norm_f = norm_f_g.reshape(1, D_MODEL)

    bp, tp, _ = x_prompt.shape
    bs, ts, _ = x_sample.shape
    n_mod = 1 + bs
    mod_rows = -(-n_mod // 8) * 8
    cvec = jnp.concatenate([c_ctx[None], c, jnp.zeros((mod_rows - n_mod, D_MODEL), F32)], 0)
    mod = _ada_mod(cvec, w_ada, b_ada)

    tp_flat = min(bp * tp, 1024)
    bp_flat = bp * tp // tp_flat
    xp = x_prompt.reshape(bp_flat, tp_flat, D_MODEL)
    xs = x_sample
    zero_rw = jnp.zeros((bp, N_DIR, W_MIX, W_MIX), F32)
    zero_gla = jnp.zeros((bp, N_DIR, W_MIX, GLA_HK), F32)
    zero_lru = jnp.zeros((bp, N_DIR, W_MIX), F32)

    def layer(x, m, seq_shape, latent, s_rw, s_gla, s_lru, p, last):
        flat_shape = x.shape[:2]
        zs = _in_proj(x, m, p["norm1_g"], p["w_in"])
        za, zb, zc, zd = (z.reshape(seq_shape + (z.shape[-1],)) for z in zs)
        ya, f_rw = _rwkv_mixer(za, s_rw, p, latent)
        yb, f_gla = _gla_mixer(zb, s_gla, p)
        yc, f_lru = _lru_mixer(zc, s_lru, p)
        yd = _conv_mixer(zd, p, latent)
        ys = [y.reshape(flat_shape + (W_MIX,)) for y in (ya, yb, yc, yd)]
        return _out_moe(x, ys, m, p, norm_f, last), f_rw, f_gla, f_lru

    fin_rw, fin_gla, fin_lru = [], [], []
    for l in range(DEPTH):
        p = {n: a[l] for n, a in P.items()}
        p.update(shared)
        last = l == DEPTH - 1
        m_p = jnp.broadcast_to(mod[l, 0:1], (bp_flat, 6 * D_MODEL)).reshape(bp_flat, 1, 6 * D_MODEL)
        xp, f_rw, f_gla, f_lru = layer(xp, m_p, (bp, tp), False, zero_rw, zero_gla, zero_lru, p, last)
        fin_rw.append(_rw_state_out(f_rw))
        fin_gla.append(_gla_state_out(f_gla))
        fin_lru.append(f_lru)
        m_s = mod[l, 1:1 + bs].reshape(bs, 1, 6 * D_MODEL)
        xs, _, _, _ = layer(xs, m_s, (bs, ts), True, _rw_state_in(state_rwkv[:, l]),
                            _gla_state_in(state_gla[:, l]), state_lru[:, l], p, last)
    return (xp.reshape(bp, tp, D_MODEL), xs, jnp.stack(fin_rw, axis=1), jnp.stack(fin_gla, axis=1),
            jnp.stack(fin_lru, axis=1))
```

```python
import functools

import jax
import jax.numpy as jnp
from jax import lax
from jax.experimental import pallas as pl
from jax.experimental.pallas import tpu as pltpu

F32 = jnp.float32
BF16 = jnp.bfloat16
ACT = jnp.bfloat16

D_MODEL = 1024
DEPTH = 4
GRID_W = 64
N_DIR = 2
W_MIX = 256
N_HEADS = 4
HEAD = W_MIX // N_HEADS
GLA_DK = 32
GLA_HK = N_HEADS * GLA_DK
RW_LN_EPS = 64e-5
NORM_EPS = 1e-6
GLA_GATE_NORM = 16.0
LRU_C = 8.0
LRU_CONV = 4
CV_KERNEL = 31
CV_PAD = (CV_KERNEL - 1) // 2
N_EXPERTS = 16
GROUP_SIZE = 4
N_GROUPS = 4
D_EXPERT = 256
CHUNK = 64
EXP_M_HALF = 0.6065306597126334

ZA_W = 1024
ZB_W = 896
ZC_W = 512
ZD_W = 512
P_PAD = ZA_W + ZB_W + ZC_W + ZD_W

VMEM_LIMIT = 48 * 1024 * 1024


def _cparams(sem, vmem=VMEM_LIMIT):
    return pltpu.CompilerParams(dimension_semantics=sem, vmem_limit_bytes=vmem)


def _dot(a, b):
    return jnp.dot(a.astype(BF16), b.astype(BF16), preferred_element_type=F32)


def _dot_nt(a, b):
    return lax.dot_general(a.astype(BF16), b.astype(BF16), (((1,), (1,)), ((), ())),
                           preferred_element_type=F32)


def _dot_tn(a, b):
    return lax.dot_general(a.astype(BF16), b.astype(BF16), (((0,), (0,)), ((), ())),
                           preferred_element_type=F32)


def _split(a):
    hi = a.astype(BF16)
    lo = (a - hi.astype(F32)).astype(BF16)
    return hi, lo


def _dot_x2(a, b_exact):
    hi, lo = _split(a)
    bb = b_exact.astype(BF16)
    return (jnp.dot(hi, bb, preferred_element_type=F32) + jnp.dot(lo, bb, preferred_element_type=F32))


def _dot_left_x2(a_exact, b):
    hi, lo = _split(b)
    aa = a_exact.astype(BF16)
    return (jnp.dot(aa, hi, preferred_element_type=F32) + jnp.dot(aa, lo, preferred_element_type=F32))


def _dot3(a, b):
    ah, al = _split(a)
    bh, bl = _split(b)
    return (jnp.dot(ah, bh, preferred_element_type=F32) + jnp.dot(ah, bl, preferred_element_type=F32)
            + jnp.dot(al, bh, preferred_element_type=F32))


def _dot3_nt(a, b):
    ah, al = _split(a)
    bh, bl = _split(b)
    dn = (((1,), (1,)), ((), ()))
    return (lax.dot_general(ah, bh, dn, preferred_element_type=F32)
            + lax.dot_general(ah, bl, dn, preferred_element_type=F32)
            + lax.dot_general(al, bh, dn, preferred_element_type=F32))


def _iota(shape, axis):
    return lax.broadcasted_iota(jnp.int32, shape, axis)


def _block_mask(rows, cols, rblk, cblk):
    r = _iota((rows, cols), 0) // rblk
    c = _iota((rows, cols), 1) // cblk
    return (r == c).astype(F32)


def _expand(x, bm):
    xb = x.astype(BF16)
    return jnp.concatenate([xb, xb, xb, xb], axis=0) * bm


def _tri(d, strict):
    t = _iota((CHUNK, CHUNK), 0)
    s = _iota((CHUNK, CHUNK), 1)
    if d == 0:
        m = (s < t) if strict else (s <= t)
    else:
        m = (s > t) if strict else (s >= t)
    return m.astype(F32)


def _tri_wide(d, strict):
    t = _iota((CHUNK, N_HEADS * CHUNK), 0)
    s = _iota((CHUNK, N_HEADS * CHUNK), 1) % CHUNK
    if d == 0:
        m = (s < t) if strict else (s <= t)
    else:
        m = (s > t) if strict else (s >= t)
    return m.astype(F32)


ADA_TN = 1536


def _ada_kernel(c_ref, w_ref, b_ref, o_ref):
    c = c_ref[...]
    s = c * jax.nn.sigmoid(c)
    o_ref[0] = _dot3(s, w_ref[0]) + b_ref[0]


def _ada_mod(cvec, w_ada, b_ada):
    rows = cvec.shape[0]
    n_out = w_ada.shape[-1]
    return pl.pallas_call(
        _ada_kernel,
        grid=(DEPTH, n_out // ADA_TN),
        in_specs=[
            pl.BlockSpec((rows, D_MODEL), lambda l, j: (0, 0)),
            pl.BlockSpec((1, D_MODEL, ADA_TN), lambda l, j: (l, 0, j)),
            pl.BlockSpec((1, 1, ADA_TN), lambda l, j: (l, 0, j)),
        ],
        out_specs=pl.BlockSpec((1, rows, ADA_TN), lambda l, j: (l, 0, j)),
        out_shape=jax.ShapeDtypeStruct((DEPTH, rows, n_out), F32),
        compiler_params=_cparams(("arbitrary", "arbitrary")),
        name="ada_mod",
    )(cvec, w_ada, b_ada.reshape(DEPTH, 1, n_out))


IN_TM = 512


def _in_kernel(x_ref, sh_ref, sc_ref, g_ref, w_ref, za_ref, zb_ref, zc_ref, zd_ref):
    x = x_ref[0]
    h = x * lax.rsqrt(jnp.mean(x * x, axis=-1, keepdims=True) + NORM_EPS) * g_ref[...]
    h = (h * (1.0 + sc_ref[0]) + sh_ref[0]).astype(BF16)
    o0, o1, o2 = ZA_W, ZA_W + ZB_W, ZA_W + ZB_W + ZC_W
    za_ref[0] = jnp.dot(h, w_ref[:, 0:o0], preferred_element_type=F32).astype(ACT)
    zb_ref[0] = jnp.dot(h, w_ref[:, o0:o1], preferred_element_type=F32).astype(ACT)
    zc_ref[0] = jnp.dot(h, w_ref[:, o1:o2], preferred_element_type=F32).astype(ACT)
    zd_ref[0] = jnp.dot(h, w_ref[:, o2:P_PAD], preferred_element_type=F32).astype(ACT)


def _in_proj(x, m, g, w):
    b, t, _ = x.shape
    tm = min(IN_TM, t)
    tok = lambda w_: pl.BlockSpec((1, tm, w_), lambda i, j: (i, j, 0))
    mod = lambda k: pl.BlockSpec((1, 1, D_MODEL), lambda i, j, k=k: (i, 0, k))
    return pl.pallas_call(
        _in_kernel,
        grid=(b, t // tm),
        in_specs=[tok(D_MODEL), mod(0), mod(1),
                  pl.BlockSpec((1, D_MODEL), lambda i, j: (0, 0)),
                  pl.BlockSpec((D_MODEL, P_PAD), lambda i, j: (0, 0))],
        out_specs=[tok(ZA_W), tok(ZB_W), tok(ZC_W), tok(ZD_W)],
        out_shape=[jax.ShapeDtypeStruct((b, t, w_), ACT) for w_ in (ZA_W, ZB_W, ZC_W, ZD_W)],
        compiler_params=_cparams(("arbitrary", "arbitrary")),
        name="in_proj",
    )(x, m, m, g, w)


SHIFT_TB = 512


def _shift_grid_kernel(z_ref, up_ref, dn_ref, mu_ref, o_ref, *, nblk):
    i = pl.program_id(1)
    z = z_ref[0].astype(F32)
    tb, c = z.shape
    col = _iota((tb, c), 0) % GRID_W
    lane = _iota((tb, c), 1) % 4
    left = jnp.where(col == 0, 0.0, pltpu.roll(z, 1, 0))
    right = jnp.where(col == GRID_W - 1, 0.0, pltpu.roll(z, tb - 1, 0))
    up_halo = jnp.where(i > 0, up_ref[0, 0].astype(F32), 0.0)
    dn_halo = jnp.where(i < nblk - 1, dn_ref[0, 0].astype(F32), 0.0)
    up = jnp.concatenate([up_halo, z[:tb - GRID_W]], axis=0)
    down = jnp.concatenate([z[GRID_W:], dn_halo], axis=0)
    sh = jnp.where(lane == 0, left, jnp.where(lane == 1, right, jnp.where(lane == 2, up, down)))
    o_ref[0] = (z + (sh - z) * mu_ref[...]).astype(ACT)


def _shift_ctx_kernel(z_ref, mu_ref, o_ref):
    z = z_ref[0].astype(F32)
    t, c = z.shape
    row = _iota((t, c), 0)
    lane = _iota((t, c), 1) % 2
    prev = jnp.where(row == 0, 0.0, pltpu.roll(z, 1, 0))
    nxt = jnp.where(row == t - 1, 0.0, pltpu.roll(z, t - 1, 0))
    sh = jnp.where(lane == 0, prev, nxt)
    o_ref[0] = (z + (sh - z) * mu_ref[...]).astype(ACT)


def _rw_shift(za, mu, latent):
    b, t, c = za.shape
    mu_spec = pl.BlockSpec((1, c), lambda *_: (0, 0))
    if not latent:
        return pl.pallas_call(
            _shift_ctx_kernel,
            grid=(b,),
            in_specs=[pl.BlockSpec((1, t, c), lambda i: (i, 0, 0)), mu_spec],
            out_specs=pl.BlockSpec((1, t, c), lambda i: (i, 0, 0)),
            out_shape=jax.ShapeDtypeStruct((b, t, c), ACT),
            compiler_params=_cparams(("arbitrary",)),
            name="rw_shift_ctx",
        )(za, mu)
    tb = min(SHIFT_TB, t)
    nblk = t // tb
    rpb = tb // GRID_W
    nrow = t // GRID_W
    z4 = za.reshape(b, nrow, GRID_W, c)
    return pl.pallas_call(
        functools.partial(_shift_grid_kernel, nblk=nblk),
        grid=(b, nblk),
        in_specs=[
            pl.BlockSpec((1, tb, c), lambda i, j: (i, j, 0)),
            pl.BlockSpec((1, 1, GRID_W, c), lambda i, j: (i, jnp.maximum(j * rpb - 1, 0), 0, 0)),
            pl.BlockSpec((1, 1, GRID_W, c), lambda i, j: (i, jnp.minimum(j * rpb + rpb, nrow - 1), 0, 0)),
            mu_spec,
        ],
        out_specs=pl.BlockSpec((1, tb, c), lambda i, j: (i, j, 0)),
        out_shape=jax.ShapeDtypeStruct((b, t, c), ACT),
        compiler_params=_cparams(("arbitrary", "arbitrary")),
        name="rw_shift_grid",
    )(za, z4, z4, mu)


SCAN_TB = 512
PREP_SPLIT = 2
PREP_LOCKSTEP = 4
PREP_SKEW = 3
PREP_GROUP = 8


def _rw_scan_kernel(z_ref, s0_ref, wl_ref, lb_ref, kkp_ref, kap_ref, lng_ref, lnb_ref, rk_ref, g2_ref,
                    o_ref, st_ref,
                    y_ref, lw_s, a_s, b_s, kd_s, r2_s, mc_s, ds_s, gt_s, *, nchunk, nblk):
    i = pl.program_id(1)
    tb = nchunk * CHUNK

    @pl.when(i == 0)
    def _():
        st_ref[...] = s0_ref[...]
        y_ref[...] = jnp.zeros(y_ref.shape, F32)

    def y_rows(d, c):
        blk = i if d == 0 else nblk - 1 - i
        return pl.ds(pl.multiple_of((blk * nchunk + c) * CHUNK, CHUNK), CHUNK)

    def blk_rows(d):
        blk = i if d == 0 else nblk - 1 - i
        return pl.ds(pl.multiple_of(blk * tb, tb), tb)

    ones_blk = _block_mask(W_MIX, W_MIX, HEAD, HEAD)
    lane128 = _iota((1, 128), 1)

    for d in range(N_DIR):
        k = z_ref[0, blk_rows(d), 256:512].astype(F32)
        la = z_ref[0, blk_rows(d), 768:896].astype(F32)
        la_t = jnp.where(lane128 < 64, jnp.tanh(la), la)
        wraw = _dot(la_t, wl_ref[:, 256 * d:256 * d + 256]) + lb_ref[:, 256 * d:256 * d + 256]
        araw = _dot(la_t, wl_ref[:, 512 + 256 * d:768 + 256 * d]) + lb_ref[:, 512 + 256 * d:768 + 256 * d]
        lw_s[d] = jax.nn.sigmoid(wraw) * (-EXP_M_HALF)
        icl = jax.nn.sigmoid(araw)
        kkv = k * kkp_ref[...]
        ss = _dot_x2(kkv * kkv, ones_blk)
        kk = kkv * lax.rsqrt(ss + 1e-12)
        a_s[d] = -kk
        b_s[d] = kk * icl
        kd_s[d] = k * (1.0 + (icl - 1.0) * kap_ref[...])

    bm = _block_mask(N_HEADS * CHUNK, W_MIX, CHUNK, HEAD).astype(BF16)
    tri = [_tri(d, False) for d in range(N_DIR)]
    tw_strict = [_tri_wide(d, True) for d in range(N_DIR)]
    tw_incl = [_tri_wide(d, False) for d in range(N_DIR)]
    eye_w = (_iota((CHUNK, N_HEADS * CHUNK), 0) == _iota((CHUNK, N_HEADS * CHUNK), 1) % CHUNK).astype(F32)

    def rows_of(c, n):
        return pl.ds(pl.multiple_of(c * n, n), n)

    group = min(PREP_GROUP, nchunk)

    each = lambda fn, *ls: [fn(*xs) for xs in zip(*ls)]

    def prep_stages(ch):
        ds_ = [d for d, _ in ch]
        st = {}

        def front():
            load = lambda ref: [ref[d, rows_of(c, CHUNK), :] for d, c in ch]
            lw, a, st["b"], st["kd"] = load(lw_s), load(a_s), load(b_s), load(kd_s)
            r = [z_ref[0, y_rows(d, c), 0:256].astype(F32) for d, c in ch]
            st["v"] = [z_ref[0, y_rows(d, c), 512:768].astype(F32) for d, c in ch]
            cum = each(lambda d, x: _dot_left_x2(tri[d], x), ds_, lw)
            st["tot"] = each(lambda d, x: x[CHUNK - 1:CHUNK, :] if d == 0 else x[0:1, :], ds_, cum)
            st["g_inv"] = each(lambda x: jnp.exp(-x), cum)
            st["g_rem"] = each(lambda t_, x: jnp.exp(t_ - x), st["tot"], cum)
            st["at"] = each(lambda a_, x, l: a_ * jnp.exp(x - l), a, cum, lw)
            st["rt"] = each(lambda r_, x: r_ * jnp.exp(x), r, cum)

        def gram():
            x = each(lambda p, q: jnp.concatenate([p, q], axis=0), st["at"], st["rt"])
            gb = each(lambda x_, b_, g: _dot_nt(x_, _expand(b_ * g, bm)), x, st["b"], st["g_inv"])
            gk = each(lambda x_, k_, g: _dot_nt(x_, _expand(k_ * g, bm)), x, st["kd"], st["g_inv"])
            st["vbd"] = each(lambda v_: _expand(v_, bm), st["v"])
            st["apow"] = each(lambda d, g: g[0:CHUNK] * tw_strict[d], ds_, gb)
            st["av"] = each(lambda d, g, vb: _dot(g[0:CHUNK] * tw_strict[d], vb), ds_, gk, st["vbd"])
            st["tw"] = each(lambda m: eye_w + m, st["apow"])
            st["r_b"] = each(lambda d, g: g[CHUNK:] * tw_incl[d], ds_, gb)
            st["r_k"] = each(lambda d, g: g[CHUNK:] * tw_incl[d], ds_, gk)

        def double():
            st["apow"] = each(lambda m: _dot(m, _expand(m, bm)), st["apow"])
            st["tw"] = each(lambda t_, m: t_ + _dot(m, _expand(t_, bm)), st["tw"], st["apow"])

        def apply():
            st["a2"] = each(lambda t_, m: _dot(t_, _expand(m, bm)), st["tw"], st["at"])
            st["u0"] = each(lambda t_, m: _dot(t_, _expand(m, bm)), st["tw"], st["av"])

        def finish():
            r2 = each(lambda rt_, rb, m: rt_ + _dot(rb, _expand(m, bm)), st["rt"], st["r_b"], st["a2"])
            y0 = each(lambda rb, u, rk, vb: _dot(rb, _expand(u, bm)) + _dot(rk, vb),
                      st["r_b"], st["u0"], st["r_k"], st["vbd"])
            bl = each(lambda b_, g: b_ * g, st["b"], st["g_rem"])
            kl = each(lambda k_, g: k_ * g, st["kd"], st["g_rem"])
            mc = each(lambda m, bl_: _dot_tn(m, bl_) * ones_blk, st["a2"], bl)
            ds0 = each(lambda u, v_, bl_, kl_: _dot_tn(jnp.concatenate([u, v_], axis=0),
                                                       jnp.concatenate([bl_, kl_], axis=0)) * ones_blk,
                       st["u0"], st["v"], bl, kl)
            for n, (d, c) in enumerate(ch):
                rows = rows_of(c, CHUNK)
                y_ref[0, y_rows(d, c), :] += y0[n]
                r2_s[d, rows, :] = r2[n].astype(BF16)
                mc_s[d, rows_of(c, W_MIX), :] = mc[n].astype(BF16)
                ds_s[d, rows_of(c, W_MIX), :] = ds0[n]
                gt_s[d, rows_of(c, 8), :] = jnp.broadcast_to(jnp.exp(st["tot"][n]), (8, W_MIX))

        return [front, gram] + [double] * 5 + [apply, finish]

    def prep_body(gi, carry):
        half = group // PREP_SPLIT if group >= PREP_SPLIT * PREP_LOCKSTEP else group
        halves = [prep_stages([(d, gi * group + j) for j in range(h0, min(h0 + half, group)) for d in range(N_DIR)])
                  for h0 in range(0, group, half)]
        n_stage = len(halves[0])
        for step in range(n_stage + PREP_SKEW * (len(halves) - 1)):
            for hi, stages in enumerate(halves):
                k = step - PREP_SKEW * hi
                if 0 <= k < n_stage:
                    stages[k]()
        return carry

    lax.fori_loop(0, nchunk // group, prep_body, 0)

    def scan_body(cc, carry):
        cs = [cc, nchunk - 1 - cc]
        s = [st_ref[0, d] for d in range(N_DIR)]
        sb = [x.astype(BF16) for x in s]
        upd = [_dot(sb[d], mc_s[d, rows_of(cs[d], W_MIX), :]) for d in range(N_DIR)]
        ys = [_dot_nt(r2_s[d, rows_of(cs[d], CHUNK), :], sb[d]) for d in range(N_DIR)]
        for d in range(N_DIR):
            st_ref[0, d] = (s[d] * gt_s[d, rows_of(cs[d], 8), :][0:1, :] + upd[d]
                            + ds_s[d, rows_of(cs[d], W_MIX), :])
            y_ref[0, y_rows(d, cs[d]), :] += ys[d]
        return carry

    lax.fori_loop(0, nchunk, scan_body, 0)

    @pl.when(i == nblk - 1)
    def _():
        def epilogue(j, carry):
            rows = pl.ds(pl.multiple_of(j * tb, tb), tb)
            y = y_ref[0, rows, :]
            mu = _dot_x2(y, ones_blk) * (1.0 / HEAD)
            yc = y - mu
            var = _dot_x2(yc * yc, ones_blk) * (1.0 / HEAD)
            yn = yc * lax.rsqrt(var + RW_LN_EPS) * lng_ref[...] + lnb_ref[...]
            r = z_ref[0, rows, 0:256].astype(F32)
            k = z_ref[0, rows, 256:512].astype(F32)
            v = z_ref[0, rows, 512:768].astype(F32)
            gl = z_ref[0, rows, 896:1024].astype(F32)
            bonus = _dot_x2(r * k * rk_ref[...], ones_blk) * v
            gate = _dot(jax.nn.sigmoid(gl), g2_ref[...])
            o_ref[0, rows, :] = ((yn + bonus) * gate).astype(ACT)
            return carry

        lax.fori_loop(0, nblk, epilogue, 0)


def _rw_scan(zs, s0, p):
    b, t, c = zs.shape
    tb = min(SCAN_TB, t)
    nblk = t // tb
    nchunk = tb // CHUNK
    vec = lambda n: pl.BlockSpec((1, n), lambda i, j: (0, 0))
    seq = lambda w_: pl.BlockSpec((1, t, w_), lambda i, j: (i, 0, 0))
    st_spec = pl.BlockSpec((1, N_DIR, W_MIX, W_MIX), lambda i, j: (i, 0, 0, 0))
    return pl.pallas_call(
        functools.partial(_rw_scan_kernel, nchunk=nchunk, nblk=nblk),
        grid=(b, nblk),
        in_specs=[
            seq(c), st_spec,
            pl.BlockSpec((128, 1024), lambda i, j: (0, 0)),
            vec(1024), vec(W_MIX), vec(W_MIX), vec(W_MIX), vec(W_MIX), vec(W_MIX),
            pl.BlockSpec((128, W_MIX), lambda i, j: (0, 0)),
        ],
        out_specs=[seq(W_MIX), st_spec],
        out_shape=[
            jax.ShapeDtypeStruct((b, t, W_MIX), ACT),
            jax.ShapeDtypeStruct((b, N_DIR, W_MIX, W_MIX), F32),
        ],
        scratch_shapes=[pltpu.VMEM((1, t, W_MIX), F32)]
        + [pltpu.VMEM((N_DIR, tb, W_MIX), F32) for _ in range(4)] + [
            pltpu.VMEM((N_DIR, tb, W_MIX), BF16),
            pltpu.VMEM((N_DIR, nchunk * W_MIX, W_MIX), BF16),
            pltpu.VMEM((N_DIR, nchunk * W_MIX, W_MIX), F32),
            pltpu.VMEM((N_DIR, nchunk * 8, W_MIX), F32),
        ],
        compiler_params=_cparams(("arbitrary", "arbitrary")),
        name="rw_scan",
    )(zs, s0, p["rw_wl"], p["rw_lb"], p["rw_kk"], p["rw_ka"], p["rw_ln_g"], p["rw_ln_b"], p["rw_rk"], p["rw_g2"])


def _rwkv_mixer(za, s0_bd, p, latent):
    zs = _rw_shift(za, p["rw_mu"], latent)
    return _rw_scan(zs, s0_bd, p)


def _gla_scan_kernel(z_ref, s0_ref, wg_ref, gb_ref, lng_ref, o_ref, st_ref,
                     y_ref, la_s, qe_s, ds_s, dec_s, *, nchunk, nblk):
    i = pl.program_id(1)
    tb = nchunk * CHUNK

    @pl.when(i == 0)
    def _():
        st_ref[...] = s0_ref[...]
        y_ref[...] = jnp.zeros(y_ref.shape, F32)

    def y_rows(d, c):
        blk = i if d == 0 else nblk - 1 - i
        return pl.ds(pl.multiple_of((blk * nchunk + c) * CHUNK, CHUNK), CHUNK)

    for d in range(N_DIR):
        blk = i if d == 0 else nblk - 1 - i
        logit = _dot(z_ref[0, pl.ds(pl.multiple_of(blk * tb, tb), tb), 768:896], wg_ref[d]) + gb_ref[d]
        la_s[d] = jax.nn.log_sigmoid(logit) * (1.0 / GLA_GATE_NORM)

    bm_k = _block_mask(N_HEADS * CHUNK, GLA_HK, CHUNK, GLA_DK).astype(BF16)
    bm_v = _block_mask(N_HEADS * CHUNK, W_MIX, CHUNK, HEAD).astype(BF16)
    bm_s = _block_mask(W_MIX, GLA_HK, HEAD, GLA_DK)
    tri = [_tri(d, False) for d in range(N_DIR)]
    tw_incl = [_tri_wide(d, False) for d in range(N_DIR)]

    def rows_of(c, n):
        return pl.ds(pl.multiple_of(c * n, n), n)

    group = min(PREP_GROUP, nchunk)

    def prep_body(gi, carry):
        ch = [(d, gi * group + j) for j in range(group) for d in range(N_DIR)]
        each = lambda fn, *ls: [fn(*xs) for xs in zip(*ls)]
        ds_ = [d for d, _ in ch]
        la = [la_s[d, rows_of(c, CHUNK), :] for d, c in ch]
        q = [z_ref[0, y_rows(d, c), 0:128].astype(F32) * (GLA_DK ** -0.5) for d, c in ch]
        k = [z_ref[0, y_rows(d, c), 128:256].astype(F32) for d, c in ch]
        v = [z_ref[0, y_rows(d, c), 256:512].astype(F32) for d, c in ch]
        cum = each(lambda d, x: _dot_left_x2(tri[d], x), ds_, la)
        last = each(lambda d, x: x[CHUNK - 1:CHUNK, :] if d == 0 else x[0:1, :], ds_, cum)
        qe = each(lambda q_, x: q_ * jnp.exp(x), q, cum)
        ke = each(lambda k_, x: k_ * jnp.exp(-x), k, cum)
        kl = each(lambda k_, l, x: k_ * jnp.exp(l - x), k, last, cum)
        att = each(lambda d, q_, k_: _dot_nt(q_, _expand(k_, bm_k)) * tw_incl[d], ds_, qe, ke)
        o = each(lambda a_, v_: _dot(a_, _expand(v_, bm_v)), att, v)
        dst = each(lambda v_, k_: _dot_tn(v_, k_) * bm_s, v, kl)
        for n, (d, c) in enumerate(ch):
            y_ref[0, y_rows(d, c), :] += o[n]
            qe_s[d, rows_of(c, CHUNK), :] = qe[n].astype(BF16)
            ds_s[d, rows_of(c, W_MIX), :] = dst[n]
            dec_s[d, rows_of(c, 8), :] = jnp.broadcast_to(jnp.exp(last[n]), (8, GLA_HK))
        return carry

    lax.fori_loop(0, nchunk // group, prep_body, 0)

    def scan_body(cc, carry):
        cs = [cc, nchunk - 1 - cc]
        for d in range(N_DIR):
            s = st_ref[0, d]
            y_ref[0, y_rows(d, cs[d]), :] += _dot_nt(qe_s[d, rows_of(cs[d], CHUNK), :], s)
            st_ref[0, d] = s * dec_s[d, rows_of(cs[d], 8), :][0:1, :] + ds_s[d, rows_of(cs[d], W_MIX), :]
        return carry

    lax.fori_loop(0, nchunk, scan_body, 0)

    @pl.when(i == nblk - 1)
    def _():
        ones_blk = _block_mask(W_MIX, W_MIX, HEAD, HEAD)

        def epilogue(j, carry):
            rows = pl.ds(pl.multiple_of(j * tb, tb), tb)
            y = y_ref[0, rows, :]
            ms = _dot_x2(y * y, ones_blk) * (1.0 / HEAD)
            y = y * lax.rsqrt(ms + NORM_EPS) * lng_ref[...]
            og = z_ref[0, rows, 512:768].astype(F32)
            o_ref[0, rows, :] = (y * (og * jax.nn.sigmoid(og))).astype(ACT)
            return carry

        lax.fori_loop(0, nblk, epilogue, 0)


def _gla_mixer(zb, s0, p):
    b, t, c = zb.shape
    tb = min(SCAN_TB, t)
    nblk = t // tb
    seq = lambda w_: pl.BlockSpec((1, t, w_), lambda i, j: (i, 0, 0))
    st_spec = pl.BlockSpec((1, N_DIR, W_MIX, GLA_HK), lambda i, j: (i, 0, 0, 0))
    nchunk = tb // CHUNK
    return pl.pallas_call(
        functools.partial(_gla_scan_kernel, nchunk=nchunk, nblk=nblk),
        grid=(b, nblk),
        in_specs=[
            seq(c), st_spec,
            pl.BlockSpec((N_DIR, 128, GLA_HK), lambda i, j: (0, 0, 0)),
            pl.BlockSpec((N_DIR, 1, GLA_HK), lambda i, j: (0, 0, 0)),
            pl.BlockSpec((1, W_MIX), lambda i, j: (0, 0)),
        ],
        out_specs=[seq(W_MIX), st_spec],
        out_shape=[
            jax.ShapeDtypeStruct((b, t, W_MIX), ACT),
            jax.ShapeDtypeStruct((b, N_DIR, W_MIX, GLA_HK), F32),
        ],
        scratch_shapes=[
            pltpu.VMEM((1, t, W_MIX), F32),
            pltpu.VMEM((N_DIR, tb, GLA_HK), F32),
            pltpu.VMEM((N_DIR, tb, GLA_HK), BF16),
            pltpu.VMEM((N_DIR, nchunk * W_MIX, GLA_HK), F32),
            pltpu.VMEM((N_DIR, nchunk * 8, GLA_HK), F32),
        ],
        compiler_params=_cparams(("arbitrary", "arbitrary")),
        name="gla_scan",
    )(zb, s0, p["gla_wg"], p["gla_gb"], p["gla_ln_g"])


LRU_TT = 256
LRU_HALO = 8
LRU_SUB = 8


def _lru_kernel(z_ref, h0_ref, cw_ref, cb_ref, wax_ref, bax_ref, lam_ref, y_ref, hf_ref, xpad, hfwd, *, t):
    tt = min(LRU_TT, t)
    ntile = t // tt
    xpad[0:LRU_HALO, :] = jnp.zeros((LRU_HALO, W_MIX), F32)
    xpad[LRU_HALO + t:2 * LRU_HALO + t, :] = jnp.zeros((LRU_HALO, W_MIX), F32)

    def fill(j, carry):
        base = pl.multiple_of(j * tt, tt)
        xpad[pl.ds(base + LRU_HALO, tt), :] = z_ref[0, pl.ds(base, tt), 0:W_MIX].astype(F32)
        return carry

    lax.fori_loop(0, ntile, fill, 0)
    sub = _iota((tt // LRU_SUB, LRU_SUB, W_MIX), 1)

    def tile_scan(j, h, d):
        base = pl.multiple_of(j * tt, tt)
        win = xpad[pl.ds(base, tt + 2 * LRU_HALO), :]
        xc = jnp.zeros((tt, W_MIX), F32) + cb_ref[d]
        for tap in range(LRU_CONV):
            off = LRU_HALO - (LRU_CONV - 1) + tap if d == 0 else LRU_HALO + (LRU_CONV - 1) - tap
            xc = xc + cw_ref[d, tap:tap + 1, :] * win[off:off + tt, :]
        g = _dot(xc, wax_ref[d]) + bax_ref[d]
        gr = jax.nn.sigmoid(g[:, 0:W_MIX])
        gi = jax.nn.sigmoid(g[:, W_MIX:2 * W_MIX])
        log_a = -LRU_C * gr * jax.nn.softplus(-lam_ref[d])
        a = jnp.exp(log_a)
        bv = jnp.sqrt(1.0 - jnp.exp(2.0 * log_a)) * gi * xc
        ngrp = tt // LRU_SUB
        a = a.reshape(ngrp, LRU_SUB, W_MIX)
        bv = bv.reshape(ngrp, LRU_SUB, W_MIX)
        s = 1
        while s < LRU_SUB:
            shift = s if d == 0 else LRU_SUB - s
            keep = (sub >= s) if d == 0 else (sub < LRU_SUB - s)
            a_sh = jnp.where(keep, pltpu.roll(a, shift, 1), 1.0)
            b_sh = jnp.where(keep, pltpu.roll(bv, shift, 1), 0.0)
            bv = a * b_sh + bv
            a = a * a_sh
            s *= 2
        out = [None] * ngrp
        for gg in range(ngrp):
            g = gg if d == 0 else ngrp - 1 - gg
            hg = a[g] * h + bv[g]
            out[g] = hg
            h = hg[LRU_SUB - 1:LRU_SUB, :] if d == 0 else hg[0:1, :]
        return jnp.concatenate(out, axis=0), h, base

    def fwd(j, h):
        ht, h, base = tile_scan(j, h, 0)
        hfwd[pl.ds(base, tt), :] = ht
        return h

    h_end = lax.fori_loop(0, ntile, fwd, h0_ref[0, 0:1, :])
    hf_ref[0, 0:1, :] = h_end

    def bwd(jj, h):
        j = ntile - 1 - jj
        ht, h, base = tile_scan(j, h, 1)
        gb = z_ref[0, pl.ds(base, tt), W_MIX:2 * W_MIX].astype(F32)
        y_ref[0, pl.ds(base, tt), :] = ((hfwd[pl.ds(base, tt), :] + ht) * jax.nn.gelu(gb)).astype(ACT)
        return h

    h_end = lax.fori_loop(0, ntile, bwd, h0_ref[0, 1:2, :])
    hf_ref[0, 1:2, :] = h_end


def _lru_mixer(zc, h0, p):
    b, t, c = zc.shape
    full = lambda *s: pl.BlockSpec(s, lambda i: (0,) * len(s))
    return pl.pallas_call(
        functools.partial(_lru_kernel, t=t),
        grid=(b,),
        in_specs=[
            pl.BlockSpec((1, t, c), lambda i: (i, 0, 0)),
            pl.BlockSpec((1, N_DIR, W_MIX), lambda i: (i, 0, 0)),
            full(N_DIR, LRU_CONV, W_MIX), full(N_DIR, 1, W_MIX),
            full(N_DIR, W_MIX, 2 * W_MIX), full(N_DIR, 1, 2 * W_MIX), full(N_DIR, 1, W_MIX),
        ],
        out_specs=[pl.BlockSpec((1, t, W_MIX), lambda i: (i, 0, 0)),
                   pl.BlockSpec((1, N_DIR, W_MIX), lambda i: (i, 0, 0))],
        out_shape=[jax.ShapeDtypeStruct((b, t, W_MIX), ACT), jax.ShapeDtypeStruct((b, N_DIR, W_MIX), F32)],
        scratch_shapes=[pltpu.VMEM((t + 2 * LRU_HALO, W_MIX), F32), pltpu.VMEM((t, W_MIX), F32)],
        compiler_params=_cparams(("arbitrary",)),
        name="lru",
    )(zc, h0, p["lru_cw"], p["lru_cb"], p["lru_wax"], p["lru_bax"], p["lru_lam"])


CV_TT = 256
CV_WIN = 16


def _conv_kernel(z_ref, dw_ref, dwb_ref, lng_ref, lnb_ref, pw_ref, pwb_ref, y_ref, upad, *, t, latent):
    tt = min(CV_TT, t)
    ntile = t // tt
    pad = CV_PAD * GRID_W if latent else CV_WIN
    upad[0:pad, :] = jnp.zeros((pad, W_MIX), F32)
    upad[pad + t:2 * pad + t, :] = jnp.zeros((pad, W_MIX), F32)

    def fill(j, carry):
        base = pl.multiple_of(j * tt, tt)
        z = z_ref[0, pl.ds(base, tt), :].astype(F32)
        upad[pl.ds(base + pad, tt), :] = z[:, 0:W_MIX] * jax.nn.sigmoid(z[:, W_MIX:2 * W_MIX])
        return carry

    lax.fori_loop(0, ntile, fill, 0)

    def tile(j, carry):
        base = pl.multiple_of(j * tt, tt)
        if latent:
            half = W_MIX // 2
            col = _iota((tt, half), 0) % GRID_W
            win = upad[pl.ds(pl.multiple_of(base + pad - CV_WIN, CV_WIN), tt + 2 * CV_WIN), 0:half]
            accw = jnp.zeros((tt, half), F32)
            acch = jnp.zeros((tt, half), F32)
            for tap in range(CV_KERNEL):
                dlt = tap - CV_PAD
                x = win[CV_WIN + dlt:CV_WIN + dlt + tt, :]
                if dlt < 0:
                    x = jnp.where(col >= -dlt, x, 0.0)
                elif dlt > 0:
                    x = jnp.where(col < GRID_W - dlt, x, 0.0)
                accw = accw + dw_ref[tap:tap + 1, 0:half] * x
                rows = pl.ds(pl.multiple_of(base + pad + dlt * GRID_W, GRID_W), tt)
                acch = acch + dw_ref[tap:tap + 1, half:W_MIX] * upad[rows, half:W_MIX]
            u = jnp.concatenate([accw, acch], axis=1)
        else:
            win = upad[pl.ds(base + pad - CV_WIN, tt + 2 * CV_WIN), :]
            u = jnp.zeros((tt, W_MIX), F32)
            for tap in range(CV_KERNEL):
                dlt = tap - CV_PAD
                u = u + dw_ref[tap:tap + 1, :] * win[CV_WIN + dlt:CV_WIN + dlt + tt, :]
        u = u + dwb_ref[...]
        mu = jnp.mean(u, axis=-1, keepdims=True)
        uc = u - mu
        var = jnp.mean(uc * uc, axis=-1, keepdims=True)
        un = uc * lax.rsqrt(var + 1e-5) * lng_ref[...] + lnb_ref[...]
        un = un * jax.nn.sigmoid(un)
        y_ref[0, pl.ds(base, tt), :] = (_dot(un, pw_ref[...]) + pwb_ref[...]).astype(ACT)
        return carry

    lax.fori_loop(0, ntile, tile, 0)


def _conv_mixer(zd, p, latent):
    b, t, c = zd.shape
    pad = CV_PAD * GRID_W if latent else CV_WIN
    full = lambda *s: pl.BlockSpec(s, lambda i: (0,) * len(s))
    vec = full(1, W_MIX)
    return pl.pallas_call(
        functools.partial(_conv_kernel, t=t, latent=latent),
        grid=(b,),
        in_specs=[pl.BlockSpec((1, t, c), lambda i: (i, 0, 0)),
                  full(CV_KERNEL, W_MIX), vec, vec, vec, full(W_MIX, W_MIX), vec],
        out_specs=pl.BlockSpec((1, t, W_MIX), lambda i: (i, 0, 0)),
        out_shape=jax.ShapeDtypeStruct((b, t, W_MIX), ACT),
        scratch_shapes=[pltpu.VMEM((t + 2 * pad, W_MIX), F32)],
        compiler_params=_cparams(("arbitrary",)),
        name="conv_grid" if latent else "conv_ctx",
    )(zd, p["cv_dw_w"], p["cv_dw_b"], p["cv_ln_g"], p["cv_ln_b"], p["cv_pw_w"], p["cv_pw_b"])


MOE_TM = 512
MOE_VMEM = VMEM_LIMIT
MOE_RB = 128
E_PAD = 128


def _route(sel, scores):
    grp = []
    for g in range(N_GROUPS):
        s = sel[GROUP_SIZE * g:GROUP_SIZE * (g + 1)]
        best_pair = None
        for i in range(GROUP_SIZE):
            for j in range(i + 1, GROUP_SIZE):
                pair = s[i] + s[j]
                best_pair = pair if best_pair is None else jnp.maximum(best_pair, pair)
        grp.append(best_pair)
    best = jnp.zeros_like(grp[0], dtype=jnp.int32)
    top = grp[0]
    for g in range(1, N_GROUPS):
        better = grp[g] > top
        best = jnp.where(better, g, best)
        top = jnp.where(better, grp[g], top)
    neg = jnp.full_like(sel[0], -jnp.inf)
    msel = [jnp.where(best == e // GROUP_SIZE, sel[e], neg) for e in range(N_EXPERTS)]
    picks = []
    for _ in range(2):
        idx = jnp.zeros_like(best)
        top = msel[0]
        for e in range(1, N_EXPERTS):
            better = msel[e] > top
            idx = jnp.where(better, e, idx)
            top = jnp.where(better, msel[e], top)
        picks.append(idx)
        msel = [jnp.where(idx == e, neg, msel[e]) for e in range(N_EXPERTS)]
    chosen = [jnp.where((picks[0] == e) | (picks[1] == e), scores[e], 0.0) for e in range(N_EXPERTS)]
    total = chosen[0]
    for e in range(1, N_EXPERTS):
        total = total + chosen[e]
    return [ch / total for ch in chosen], best


def _moe_kernel(x_ref, ya_ref, yb_ref, yc_ref, yd_ref, wo_ref, g1_ref, sh2_ref, sc2_ref, g2_ref, n2_ref,
                wr_ref, br_ref, w1_ref, w3_ref, w2_ref, nf_ref, o_ref,
                x1_s, h2_s, gt_s, gate_s, acc_s, he_s, og_s, pmt_s, seg_s, *, final_norm):
    grp = pl.program_id(2)
    tm = x1_s.shape[0]

    @pl.when(grp == 0)
    def _():
        y = (_dot(ya_ref[0], wo_ref[0:256, :]) + _dot(yb_ref[0], wo_ref[256:512, :])
             + _dot(yc_ref[0], wo_ref[512:768, :]) + _dot(yd_ref[0], wo_ref[768:1024, :]))
        x1 = x_ref[0] + g1_ref[0] * y
        x1_s[...] = x1
        h2 = x1 * lax.rsqrt(jnp.mean(x1 * x1, axis=-1, keepdims=True) + NORM_EPS) * n2_ref[...]
        h2 = h2 * (1.0 + sc2_ref[0]) + sh2_ref[0]
        logits = _dot3_nt(wr_ref[...], h2)
        scores = jax.nn.sigmoid(logits)
        selm = scores + br_ref[...]
        gates, best = _route([selm[i:i + 1, :] for i in range(N_EXPERTS)],
                             [scores[i:i + 1, :] for i in range(N_EXPERTS)])
        gt_s[...] = jnp.zeros(gt_s.shape, F32)
        for i in range(N_EXPERTS):
            gt_s[i:i + 1, :] = gates[i]

        og = [jnp.where(best == g, 1.0, 0.0) for g in range(N_GROUPS)]
        og_s[...] = jnp.zeros(og_s.shape, F32)
        start = jnp.int32(0)
        starts = []
        for g in range(N_GROUPS):
            og_s[g:g + 1, :] = og[g]
            starts.append(start)
            seg_s[g] = start
            start = start + jnp.sum(og[g]).astype(jnp.int32)
            seg_s[N_GROUPS + g] = start
        before = jnp.where(_iota((tm, tm), 0) < _iota((tm, tm), 1), 1.0, 0.0).astype(BF16)
        rank = jnp.dot(og_s[...].astype(BF16), before, preferred_element_type=F32)
        pos = og[0] * (starts[0].astype(F32) + rank[0:1, :])
        for g in range(1, N_GROUPS):
            pos = pos + og[g] * (starts[g].astype(F32) + rank[g:g + 1, :])
        gt_s[N_EXPERTS:N_EXPERTS + 1, :] = pos
        gate_tok = gt_s[...].T
        slot_l = _iota((tm, tm), 1).astype(F32)
        slot_s = _iota((tm, tm), 0).astype(F32)
        pmt_s[...] = jnp.where(slot_l == gate_tok[:, N_EXPERTS:N_EXPERTS + 1], 1.0, 0.0).astype(BF16)
        pm = jnp.where(slot_s == pos, 1.0, 0.0).astype(BF16)
        h2_s[...] = jnp.dot(pm, h2.astype(BF16), preferred_element_type=F32).astype(BF16)
        gate_s[...] = _dot_left_x2(pm, gate_tok)
        acc_s[...] = jnp.zeros(acc_s.shape, F32)

    pick = (_iota((E_PAD, GROUP_SIZE * 128), 0)
            == GROUP_SIZE * grp + _iota((E_PAD, GROUP_SIZE * 128), 1) // 128).astype(F32)

    def block(rb, carry):
        rows = pl.ds(pl.multiple_of(rb * MOE_RB, MOE_RB), MOE_RB)
        gsel = _dot_x2(gate_s[rows, :], pick)
        h2 = h2_s[rows, :]
        for j in range(GROUP_SIZE):
            he = jnp.dot(h2, w1_ref[j], preferred_element_type=F32)
            he = he * jax.nn.sigmoid(he) * jnp.dot(h2, w3_ref[j], preferred_element_type=F32)
            g = gsel[:, 128 * j:128 * (j + 1)]
            he_s[rows, D_EXPERT * j:D_EXPERT * (j + 1)] = (he * jnp.concatenate([g, g], axis=1)).astype(BF16)
        w2g = w2_ref[...].reshape(GROUP_SIZE * D_EXPERT, D_MODEL)
        acc_s[rows, :] += jnp.dot(he_s[rows, :], w2g, preferred_element_type=F32)
        return carry

    first = seg_s[grp] // MOE_RB
    last = (seg_s[N_GROUPS + grp] + (MOE_RB - 1)) // MOE_RB
    lax.fori_loop(first, last, block, 0)

    @pl.when(grp == N_GROUPS - 1)
    def _():
        x2 = x1_s[...] + g2_ref[0] * _dot_left_x2(pmt_s[...], acc_s[...])
        if final_norm:
            x2 = x2 * lax.rsqrt(jnp.mean(x2 * x2, axis=-1, keepdims=True) + NORM_EPS) * nf_ref[...]
        o_ref[0] = x2


def _out_moe(x, ys, m, p, norm_f, final_norm):
    b, t, _ = x.shape
    tm = min(MOE_TM, t)
    tok = lambda w_: pl.BlockSpec((1, tm, w_), lambda i, j, e: (i, j, 0))
    mod = lambda k: pl.BlockSpec((1, 1, D_MODEL), lambda i, j, e, k=k: (i, 0, k))
    full = lambda *s: pl.BlockSpec(s, lambda i, j, e: (0,) * len(s))
    return pl.pallas_call(
        functools.partial(_moe_kernel, final_norm=final_norm),
        grid=(b, t // tm, N_GROUPS),
        in_specs=[tok(D_MODEL), tok(W_MIX), tok(W_MIX), tok(W_MIX), tok(W_MIX),
                  full(D_MODEL, D_MODEL), mod(2), mod(3), mod(4), mod(5), full(1, D_MODEL),
                  full(N_EXPERTS, D_MODEL), full(N_EXPERTS, 1),
                  pl.BlockSpec((GROUP_SIZE, D_MODEL, D_EXPERT), lambda i, j, e: (e, 0, 0)),
                  pl.BlockSpec((GROUP_SIZE, D_MODEL, D_EXPERT), lambda i, j, e: (e, 0, 0)),
                  pl.BlockSpec((GROUP_SIZE, D_EXPERT, D_MODEL), lambda i, j, e: (e, 0, 0)),
                  full(1, D_MODEL)],
        out_specs=tok(D_MODEL),
        out_shape=jax.ShapeDtypeStruct((b, t, D_MODEL), F32),
        scratch_shapes=[pltpu.VMEM((tm, D_MODEL), F32), pltpu.VMEM((tm, D_MODEL), BF16),
                        pltpu.VMEM((E_PAD, tm), F32), pltpu.VMEM((tm, E_PAD), F32),
                        pltpu.VMEM((tm, D_MODEL), F32), pltpu.VMEM((tm, D_EXPERT * GROUP_SIZE), BF16),
                        pltpu.VMEM((8, tm), F32), pltpu.VMEM((tm, tm), BF16),
                        pltpu.SMEM((2 * N_GROUPS,), jnp.int32)],
        compiler_params=_cparams(("arbitrary", "arbitrary", "arbitrary"), MOE_VMEM),
        name="out_moe",
    )(x, *ys, p["w_out"], m, m, m, m, p["norm2_g"], p["w_router_t"], p["b_router"], p["e_w1"], p["e_w3"], p["e_w2"],
      norm_f)


def _gla_pack(z):
    hk = GLA_HK
    lead = z.shape[:-1]
    gl = z[..., 2 * hk + W_MIX:2 * hk + W_MIX + 32]
    return jnp.concatenate([z[..., 0:2 * hk + W_MIX], z[..., 2 * hk + W_MIX + 32:], gl,
                            jnp.zeros(lead + (96,), z.dtype)], -1)


def _gla_state_in(s):
    b = s.shape[0]
    eye = jnp.eye(N_HEADS, dtype=s.dtype)
    return jnp.einsum("bdhkv,hg->bdhvgk", s, eye).reshape(b, N_DIR, W_MIX, GLA_HK)


def _gla_state_out(st):
    b = st.shape[0]
    eye = jnp.eye(N_HEADS, dtype=st.dtype)
    return jnp.einsum("bdhvgk,hg->bdhkv", st.reshape(b, N_DIR, N_HEADS, HEAD, N_HEADS, GLA_DK), eye)


def _rw_state_in(s):
    b = s.shape[0]
    eye = jnp.eye(N_HEADS, dtype=s.dtype)
    return jnp.einsum("bdhvk,hg->bdhvgk", s, eye).reshape(b, N_DIR, W_MIX, W_MIX)


def _rw_state_out(st):
    b = st.shape[0]
    eye = jnp.eye(N_HEADS, dtype=st.dtype)
    return jnp.einsum("bdhvgk,hg->bdhvk", st.reshape(b, N_DIR, N_HEADS, HEAD, N_HEADS, HEAD), eye)


def _prep_params(d):
    L = DEPTH
    z = lambda *s: jnp.zeros(s, F32)
    out = {}
    out["rw_mu"] = jnp.concatenate([d["rw_mu"], z(L, ZA_W - d["rw_mu"].shape[-1])], -1).reshape(L, 1, ZA_W)
    w2, a2 = d["rw_w2"], d["rw_a2"]
    zz = z(L, 32, 256)
    rows = [
        jnp.concatenate([w2[:, 0], zz, zz, zz], -1),
        jnp.concatenate([zz, w2[:, 1], zz, zz], -1),
        jnp.concatenate([zz, zz, a2[:, 0], zz], -1),
        jnp.concatenate([zz, zz, zz, a2[:, 1]], -1),
    ]
    out["rw_wl"] = jnp.concatenate(rows, 1).astype(BF16)
    out["rw_lb"] = jnp.concatenate([d["rw_w0"][:, 0], d["rw_w0"][:, 1], d["rw_a0"][:, 0], d["rw_a0"][:, 1]],
                                   -1).reshape(L, 1, 1024)
    for n in ("rw_kk", "rw_ka", "rw_ln_g", "rw_ln_b"):
        out[n] = d[n].reshape(L, 1, W_MIX)
    out["rw_rk"] = d["rw_rk"].reshape(L, 1, W_MIX)
    out["rw_g2"] = jnp.concatenate([d["rw_g2"], z(L, 64, W_MIX)], 1).astype(BF16)

    gk2 = d["gla_gk2"]
    z16, z96 = z(L, 16, GLA_HK), z(L, 96, GLA_HK)
    out["gla_wg"] = jnp.stack([jnp.concatenate([gk2[:, 0], z16, z96], 1),
                               jnp.concatenate([z16, gk2[:, 1], z96], 1)], 1).astype(BF16)
    out["gla_gb"] = d["gla_gkb"].reshape(L, N_DIR, 1, GLA_HK)
    out["gla_ln_g"] = jnp.tile(d["gla_ln_g"], (1, N_HEADS)).reshape(L, 1, W_MIX)

    eye = jnp.eye(4, dtype=F32)
    bd = lambda w: jnp.einsum("ldgij,gh->ldgihj", w, eye).reshape(L, N_DIR, W_MIX, W_MIX)
    out["lru_cw"] = d["lru_conv_w"]
    out["lru_cb"] = d["lru_conv_b"].reshape(L, N_DIR, 1, W_MIX)
    out["lru_wax"] = jnp.concatenate([bd(d["lru_wa"]), bd(d["lru_wx"])], -1).astype(BF16)
    out["lru_bax"] = jnp.concatenate([d["lru_ba"], d["lru_bx"]], -1).reshape(L, N_DIR, 1, 2 * W_MIX)
    out["lru_lam"] = d["lru_lam"].reshape(L, N_DIR, 1, W_MIX)

    out["cv_dw_w"] = d["cv_dw_w"]
    for n in ("cv_dw_b", "cv_ln_g", "cv_ln_b", "cv_pw_b"):
        out[n] = d[n].reshape(L, 1, W_MIX)
    out["cv_pw_w"] = d["cv_pw_w"].astype(BF16)

    if "w_in" in d:
        w_in = d["w_in"]
        o1, o2, o3 = 960, 1760, 2272
        out["w_in"] = jnp.concatenate([w_in[..., 0:o1], z(L, D_MODEL, ZA_W - o1), _gla_pack(w_in[..., o1:o2]),
                                       w_in[..., o2:o3], w_in[..., o3:]], -1).astype(BF16)
        out["w_out"] = d["w_out"].astype(BF16)
        out["norm1_g"] = d["norm1_g"].reshape(L, 1, D_MODEL)
        out["norm2_g"] = d["norm2_g"].reshape(L, 1, D_MODEL)
        for n in ("e_w1", "e_w3", "e_w2"):
            out[n] = d[n].astype(BF16)
    return out


def kernel(x_prompt, x_sample, state_rwkv, state_gla, state_lru, c, c_ctx, norm1_g, norm2_g, norm_f_g, w_ada, b_ada, w_in, w_out, rw_mu, rw_w0, rw_w2, rw_a0, rw_a2, rw_g2, rw_kk, rw_ka, rw_rk, rw_ln_g, rw_ln_b, gla_gk2, gla_gkb, gla_ln_g, lru_conv_w, lru_conv_b, lru_wa, lru_ba, lru_wx, lru_bx, lru_lam, cv_dw_w, cv_dw_b, cv_ln_g, cv_ln_b, cv_pw_w, cv_pw_b, w_router, b_router, e_w1, e_w3, e_w2):
    d = dict(norm1_g=norm1_g, norm2_g=norm2_g, w_in=w_in, w_out=w_out,
             rw_mu=rw_mu, rw_w0=rw_w0, rw_w2=rw_w2, rw_a0=rw_a0, rw_a2=rw_a2, rw_g2=rw_g2, rw_kk=rw_kk,
             rw_ka=rw_ka, rw_rk=rw_rk, rw_ln_g=rw_ln_g, rw_ln_b=rw_ln_b,
             gla_gk2=gla_gk2, gla_gkb=gla_gkb, gla_ln_g=gla_ln_g,
             lru_conv_w=lru_conv_w, lru_conv_b=lru_conv_b, lru_wa=lru_wa, lru_ba=lru_ba, lru_wx=lru_wx,
             lru_bx=lru_bx, lru_lam=lru_lam,
             cv_dw_w=cv_dw_w, cv_dw_b=cv_dw_b, cv_ln_g=cv_ln_g, cv_ln_b=cv_ln_b, cv_pw_w=cv_pw_w, cv_pw_b=cv_pw_b,
             e_w1=e_w1, e_w3=e_w3, e_w2=e_w2)
    P = _prep_params(d)
    shared = {"w_router_t": w_router.T, "b_router": b_router.reshape(N_EXPERTS, 1)}
    norm_f = norm_f_g.reshape(1, D_MODEL)

    bp, tp, _ = x_prompt.shape
    bs, ts, _ = x_sample.shape
    n_mod = 1 + bs
    mod_rows = -(-n_mod // 8) * 8
    cvec = jnp.concatenate([c_ctx[None], c, jnp.zeros((mod_rows - n_mod, D_MODEL), F32)], 0)
    mod = _ada_mod(cvec, w_ada, b_ada)

    tp_flat = min(bp * tp, 1024)
    bp_flat = bp * tp // tp_flat
    xp = x_prompt.reshape(bp_flat, tp_flat, D_MODEL)
    xs = x_sample
    zero_rw = jnp.zeros((bp, N_DIR, W_MIX, W_MIX), F32)
    zero_gla = jnp.zeros((bp, N_DIR, W_MIX, GLA_HK), F32)
    zero_lru = jnp.zeros((bp, N_DIR, W_MIX), F32)

    def layer(x, m, seq_shape, latent, s_rw, s_gla, s_lru, p, last):
        flat_shape = x.shape[:2]
        zs = _in_proj(x, m, p["norm1_g"], p["w_in"])
        za, zb, zc, zd = (z.reshape(seq_shape + (z.shape[-1],)) for z in zs)
        ya, f_rw = _rwkv_mixer(za, s_rw, p, latent)
        yb, f_gla = _gla_mixer(zb, s_gla, p)
        yc, f_lru = _lru_mixer(zc, s_lru, p)
        yd = _conv_mixer(zd, p, latent)
        ys = [y.reshape(flat_shape + (W_MIX,)) for y in (ya, yb, yc, yd)]
        return _out_moe(x, ys, m, p, norm_f, last), f_rw, f_gla, f_lru

    fin_rw, fin_gla, fin_lru = [], [], []
    for l in range(DEPTH):
        p = {n: a[l] for n, a in P.items()}
        p.update(shared)
        last = l == DEPTH - 1
        m_p = jnp.broadcast_to(mod[l, 0:1], (bp_flat, 6 * D_MODEL)).reshape(bp_flat, 1, 6 * D_MODEL)
        xp, f_rw, f_gla, f_lru = layer(xp, m_p, (bp, tp), False, zero_rw, zero_gla, zero_lru, p, last)
        fin_rw.append(_rw_state_out(f_rw))
        fin_gla.append(_gla_state_out(f_gla))
        fin_lru.append(f_lru)
        m_s = mod[l, 1:1 + bs].reshape(bs, 1, 6 * D_MODEL)
        xs, _, _, _ = layer(xs, m_s, (bs, ts), True, _rw_state_in(state_rwkv[:, l]),
                            _gla_state_in(state_gla[:, l]), state_lru[:, l], p, last)
    return (xp.reshape(bp, tp, D_MODEL), xs, jnp.stack(fin_rw, axis=1), jnp.stack(fin_gla, axis=1),
            jnp.stack(fin_lru, axis=1))
```

```python
import functools

import jax
import jax.numpy as jnp
from jax import lax
from jax.experimental import pallas as pl
from jax.experimental.pallas import tpu as pltpu

F32 = jnp.float32
BF16 = jnp.bfloat16
ACT = jnp.bfloat16

D_MODEL = 1024
DEPTH = 4
GRID_W = 64
N_DIR = 2
W_MIX = 256
N_HEADS = 4
HEAD = W_MIX // N_HEADS
GLA_DK = 32
GLA_HK = N_HEADS * GLA_DK
RW_LN_EPS = 64e-5
NORM_EPS = 1e-6
GLA_GATE_NORM = 16.0
LRU_C = 8.0
LRU_CONV = 4
CV_KERNEL = 31
CV_PAD = (CV_KERNEL - 1) // 2
N_EXPERTS = 16
GROUP_SIZE = 4
N_GROUPS = 4
D_EXPERT = 256
CHUNK = 64
EXP_M_HALF = 0.6065306597126334

LANE = 128
SUBLANE = 8

RW_COLS = 960
GLA_COLS = 800
ZA_W = 1024
ZB_W = 896
ZC_W = 512
ZD_W = 512
P_PAD = ZA_W + ZB_W + ZC_W + ZD_W
RW_R, RW_K, RW_V = slice(0, 256), slice(256, 512), slice(512, 768)
RW_LORA = slice(768, 896)
RW_GATE = slice(896, 1024)
GLA_Q, GLA_K, GLA_V = slice(0, 128), slice(128, 256), slice(256, 512)
GLA_OG = slice(512, 768)
GLA_GATE = slice(768, 896)

VMEM_LIMIT = 48 * 1024 * 1024


def _cparams(sem, vmem=VMEM_LIMIT):
    return pltpu.CompilerParams(dimension_semantics=sem, vmem_limit_bytes=vmem)


def _dot(a, b):
    return jnp.dot(a.astype(BF16), b.astype(BF16), preferred_element_type=F32)


def _dot_nt(a, b):
    return lax.dot_general(a.astype(BF16), b.astype(BF16), (((1,), (1,)), ((), ())),
                           preferred_element_type=F32)


def _dot_tn(a, b):
    return lax.dot_general(a.astype(BF16), b.astype(BF16), (((0,), (0,)), ((), ())),
                           preferred_element_type=F32)


def _split(a):
    hi = a.astype(BF16)
    lo = (a - hi.astype(F32)).astype(BF16)
    return hi, lo


def _dot_x2(a, b_exact):
    hi, lo = _split(a)
    bb = b_exact.astype(BF16)
    return (jnp.dot(hi, bb, preferred_element_type=F32) + jnp.dot(lo, bb, preferred_element_type=F32))


def _dot_left_x2(a_exact, b):
    hi, lo = _split(b)
    aa = a_exact.astype(BF16)
    return (jnp.dot(aa, hi, preferred_element_type=F32) + jnp.dot(aa, lo, preferred_element_type=F32))


def _dot3(a, b):
    ah, al = _split(a)
    bh, bl = _split(b)
    return (jnp.dot(ah, bh, preferred_element_type=F32) + jnp.dot(ah, bl, preferred_element_type=F32)
            + jnp.dot(al, bh, preferred_element_type=F32))


def _dot3_nt(a, b):
    ah, al = _split(a)
    bh, bl = _split(b)
    dn = (((1,), (1,)), ((), ()))
    return (lax.dot_general(ah, bh, dn, preferred_element_type=F32)
            + lax.dot_general(ah, bl, dn, preferred_element_type=F32)
            + lax.dot_general(al, bh, dn, preferred_element_type=F32))


def _iota(shape, axis):
    return lax.broadcasted_iota(jnp.int32, shape, axis)


def _block_mask(rows, cols, rblk, cblk):
    r = _iota((rows, cols), 0) // rblk
    c = _iota((rows, cols), 1) // cblk
    return (r == c).astype(F32)


def _expand(x, bm):
    xb = x.astype(BF16)
    return jnp.concatenate([xb, xb, xb, xb], axis=0) * bm


def _tri(d, strict):
    t = _iota((CHUNK, CHUNK), 0)
    s = _iota((CHUNK, CHUNK), 1)
    if d == 0:
        m = (s < t) if strict else (s <= t)
    else:
        m = (s > t) if strict else (s >= t)
    return m.astype(F32)


def _tri_wide(d, strict):
    t = _iota((CHUNK, N_HEADS * CHUNK), 0)
    s = _iota((CHUNK, N_HEADS * CHUNK), 1) % CHUNK
    if d == 0:
        m = (s < t) if strict else (s <= t)
    else:
        m = (s > t) if strict else (s >= t)
    return m.astype(F32)


ADA_TN = 1536


def _ada_kernel(c_ref, w_ref, b_ref, o_ref):
    c = c_ref[...]
    s = c * jax.nn.sigmoid(c)
    o_ref[0] = _dot3(s, w_ref[0]) + b_ref[0]


def _ada_mod(cvec, w_ada, b_ada):
    rows = cvec.shape[0]
    n_out = w_ada.shape[-1]
    return pl.pallas_call(
        _ada_kernel,
        grid=(DEPTH, n_out // ADA_TN),
        in_specs=[
            pl.BlockSpec((rows, D_MODEL), lambda l, j: (0, 0)),
            pl.BlockSpec((1, D_MODEL, ADA_TN), lambda l, j: (l, 0, j)),
            pl.BlockSpec((1, 1, ADA_TN), lambda l, j: (l, 0, j)),
        ],
        out_specs=pl.BlockSpec((1, rows, ADA_TN), lambda l, j: (l, 0, j)),
        out_shape=jax.ShapeDtypeStruct((DEPTH, rows, n_out), F32),
        compiler_params=_cparams(("arbitrary", "arbitrary")),
        name="ada_mod",
    )(cvec, w_ada, b_ada.reshape(DEPTH, 1, n_out))


IN_TM = 512


def _in_kernel(x_ref, sh_ref, sc_ref, g_ref, w_ref, za_ref, zb_ref, zc_ref, zd_ref):
    x = x_ref[0]
    h = x * lax.rsqrt(jnp.mean(x * x, axis=-1, keepdims=True) + NORM_EPS) * g_ref[...]
    h = (h * (1.0 + sc_ref[0]) + sh_ref[0]).astype(BF16)
    o0, o1, o2 = ZA_W, ZA_W + ZB_W, ZA_W + ZB_W + ZC_W
    za_ref[0] = jnp.dot(h, w_ref[:, 0:o0], preferred_element_type=F32).astype(ACT)
    zb_ref[0] = jnp.dot(h, w_ref[:, o0:o1], preferred_element_type=F32).astype(ACT)
    zc_ref[0] = jnp.dot(h, w_ref[:, o1:o2], preferred_element_type=F32).astype(ACT)
    zd_ref[0] = jnp.dot(h, w_ref[:, o2:P_PAD], preferred_element_type=F32).astype(ACT)


def _in_proj(x, m, g, w):
    b, t, _ = x.shape
    tm = min(IN_TM, t)
    tok = lambda w_: pl.BlockSpec((1, tm, w_), lambda i, j: (i, j, 0))
    mod = lambda k: pl.BlockSpec((1, 1, D_MODEL), lambda i, j, k=k: (i, 0, k))
    return pl.pallas_call(
        _in_kernel,
        grid=(b, t // tm),
        in_specs=[tok(D_MODEL), mod(0), mod(1),
                  pl.BlockSpec((1, D_MODEL), lambda i, j: (0, 0)),
                  pl.BlockSpec((D_MODEL, P_PAD), lambda i, j: (0, 0))],
        out_specs=[tok(ZA_W), tok(ZB_W), tok(ZC_W), tok(ZD_W)],
        out_shape=[jax.ShapeDtypeStruct((b, t, w_), ACT) for w_ in (ZA_W, ZB_W, ZC_W, ZD_W)],
        compiler_params=_cparams(("arbitrary", "arbitrary")),
        name="in_proj",
    )(x, m, m, g, w)


SHIFT_TB = 512


def _shift_grid_kernel(z_ref, up_ref, dn_ref, mu_ref, o_ref, *, nblk):
    i = pl.program_id(1)
    z = z_ref[0].astype(F32)
    tb, c = z.shape
    col = _iota((tb, c), 0) % GRID_W
    lane = _iota((tb, c), 1) % 4
    left = jnp.where(col == 0, 0.0, pltpu.roll(z, 1, 0))
    right = jnp.where(col == GRID_W - 1, 0.0, pltpu.roll(z, tb - 1, 0))
    up_halo = jnp.where(i > 0, up_ref[0, 0].astype(F32), 0.0)
    dn_halo = jnp.where(i < nblk - 1, dn_ref[0, 0].astype(F32), 0.0)
    up = jnp.concatenate([up_halo, z[:tb - GRID_W]], axis=0)
    down = jnp.concatenate([z[GRID_W:], dn_halo], axis=0)
    sh = jnp.where(lane == 0, left, jnp.where(lane == 1, right, jnp.where(lane == 2, up, down)))
    o_ref[0] = (z + (sh - z) * mu_ref[...]).astype(ACT)


def _shift_ctx_kernel(z_ref, mu_ref, o_ref):
    z = z_ref[0].astype(F32)
    t, c = z.shape
    row = _iota((t, c), 0)
    lane = _iota((t, c), 1) % 2
    prev = jnp.where(row == 0, 0.0, pltpu.roll(z, 1, 0))
    nxt = jnp.where(row == t - 1, 0.0, pltpu.roll(z, t - 1, 0))
    sh = jnp.where(lane == 0, prev, nxt)
    o_ref[0] = (z + (sh - z) * mu_ref[...]).astype(ACT)


def _rw_shift(za, mu, latent):
    b, t, c = za.shape
    mu_spec = pl.BlockSpec((1, c), lambda *_: (0, 0))
    if not latent:
        return pl.pallas_call(
            _shift_ctx_kernel,
            grid=(b,),
            in_specs=[pl.BlockSpec((1, t, c), lambda i: (i, 0, 0)), mu_spec],
            out_specs=pl.BlockSpec((1, t, c), lambda i: (i, 0, 0)),
            out_shape=jax.ShapeDtypeStruct((b, t, c), ACT),
            compiler_params=_cparams(("arbitrary",)),
            name="rw_shift_ctx",
        )(za, mu)
    tb = min(SHIFT_TB, t)
    nblk = t // tb
    rpb = tb // GRID_W
    nrow = t // GRID_W
    z4 = za.reshape(b, nrow, GRID_W, c)
    return pl.pallas_call(
        functools.partial(_shift_grid_kernel, nblk=nblk),
        grid=(b, nblk),
        in_specs=[
            pl.BlockSpec((1, tb, c), lambda i, j: (i, j, 0)),
            pl.BlockSpec((1, 1, GRID_W, c), lambda i, j: (i, jnp.maximum(j * rpb - 1, 0), 0, 0)),
            pl.BlockSpec((1, 1, GRID_W, c), lambda i, j: (i, jnp.minimum(j * rpb + rpb, nrow - 1), 0, 0)),
            mu_spec,
        ],
        out_specs=pl.BlockSpec((1, tb, c), lambda i, j: (i, j, 0)),
        out_shape=jax.ShapeDtypeStruct((b, t, c), ACT),
        compiler_params=_cparams(("arbitrary", "arbitrary")),
        name="rw_shift_grid",
    )(za, z4, z4, mu)


SCAN_TB = 512
PREP_SPLIT = 2
PREP_LOCKSTEP = 4
PREP_SKEW = 3
PREP_GROUP = 8


def _rw_scan_kernel(z_ref, s0_ref, wl_ref, lb_ref, kkp_ref, kap_ref, lng_ref, lnb_ref, rk_ref, g2_ref,
                    o_ref, st_ref,
                    y_ref, lw_s, a_s, b_s, kd_s, r2_s, mc_s, ds_s, gt_s, *, nchunk, nblk):
    i = pl.program_id(1)
    tb = nchunk * CHUNK

    @pl.when(i == 0)
    def _():
        st_ref[...] = s0_ref[...]
        y_ref[...] = jnp.zeros(y_ref.shape, F32)

    def y_rows(d, c):
        blk = i if d == 0 else nblk - 1 - i
        return pl.ds(pl.multiple_of((blk * nchunk + c) * CHUNK, CHUNK), CHUNK)

    def blk_rows(d):
        blk = i if d == 0 else nblk - 1 - i
        return pl.ds(pl.multiple_of(blk * tb, tb), tb)

    ones_blk = _block_mask(W_MIX, W_MIX, HEAD, HEAD)
    is_decay_lora = _iota((1, LANE), 1) < LANE // 2

    for d in range(N_DIR):
        k = z_ref[0, blk_rows(d), RW_K].astype(F32)
        la = z_ref[0, blk_rows(d), RW_LORA].astype(F32)
        la_t = jnp.where(is_decay_lora, jnp.tanh(la), la)
        w_cols = slice(W_MIX * d, W_MIX * (d + 1))
        a_cols = slice(W_MIX * (N_DIR + d), W_MIX * (N_DIR + d + 1))
        wraw = _dot(la_t, wl_ref[:, w_cols]) + lb_ref[:, w_cols]
        araw = _dot(la_t, wl_ref[:, a_cols]) + lb_ref[:, a_cols]
        lw_s[d] = jax.nn.sigmoid(wraw) * (-EXP_M_HALF)
        icl = jax.nn.sigmoid(araw)
        kkv = k * kkp_ref[...]
        ss = _dot_x2(kkv * kkv, ones_blk)
        kk = kkv * lax.rsqrt(ss + 1e-12)
        a_s[d] = -kk
        b_s[d] = kk * icl
        kd_s[d] = k * (1.0 + (icl - 1.0) * kap_ref[...])

    bm = _block_mask(N_HEADS * CHUNK, W_MIX, CHUNK, HEAD).astype(BF16)
    tri = [_tri(d, False) for d in range(N_DIR)]
    tw_strict = [_tri_wide(d, True) for d in range(N_DIR)]
    tw_incl = [_tri_wide(d, False) for d in range(N_DIR)]
    eye_w = (_iota((CHUNK, N_HEADS * CHUNK), 0) == _iota((CHUNK, N_HEADS * CHUNK), 1) % CHUNK).astype(F32)

    def rows_of(c, n):
        return pl.ds(pl.multiple_of(c * n, n), n)

    group = min(PREP_GROUP, nchunk)

    each = lambda fn, *ls: [fn(*xs) for xs in zip(*ls)]

    def prep_stages(ch):
        ds_ = [d for d, _ in ch]
        st = {}

        def front():
            load = lambda ref: [ref[d, rows_of(c, CHUNK), :] for d, c in ch]
            lw, a, st["b"], st["kd"] = load(lw_s), load(a_s), load(b_s), load(kd_s)
            r = [z_ref[0, y_rows(d, c), RW_R].astype(F32) for d, c in ch]
            st["v"] = [z_ref[0, y_rows(d, c), RW_V].astype(F32) for d, c in ch]
            cum = each(lambda d, x: _dot_left_x2(tri[d], x), ds_, lw)
            st["tot"] = each(lambda d, x: x[CHUNK - 1:CHUNK, :] if d == 0 else x[0:1, :], ds_, cum)
            st["g_inv"] = each(lambda x: jnp.exp(-x), cum)
            st["g_rem"] = each(lambda t_, x: jnp.exp(t_ - x), st["tot"], cum)
            st["at"] = each(lambda a_, x, l: a_ * jnp.exp(x - l), a, cum, lw)
            st["rt"] = each(lambda r_, x: r_ * jnp.exp(x), r, cum)

        def gram():
            x = each(lambda p, q: jnp.concatenate([p, q], axis=0), st["at"], st["rt"])
            gb = each(lambda x_, b_, g: _dot_nt(x_, _expand(b_ * g, bm)), x, st["b"], st["g_inv"])
            gk = each(lambda x_, k_, g: _dot_nt(x_, _expand(k_ * g, bm)), x, st["kd"], st["g_inv"])
            st["vbd"] = each(lambda v_: _expand(v_, bm), st["v"])
            st["apow"] = each(lambda d, g: g[0:CHUNK] * tw_strict[d], ds_, gb)
            st["av"] = each(lambda d, g, vb: _dot(g[0:CHUNK] * tw_strict[d], vb), ds_, gk, st["vbd"])
            st["tw"] = each(lambda m: eye_w + m, st["apow"])
            st["r_b"] = each(lambda d, g: g[CHUNK:] * tw_incl[d], ds_, gb)
            st["r_k"] = each(lambda d, g: g[CHUNK:] * tw_incl[d], ds_, gk)

        def double():
            st["apow"] = each(lambda m: _dot(m, _expand(m, bm)), st["apow"])
            st["tw"] = each(lambda t_, m: t_ + _dot(m, _expand(t_, bm)), st["tw"], st["apow"])

        def apply():
            st["a2"] = each(lambda t_, m: _dot(t_, _expand(m, bm)), st["tw"], st["at"])
            st["u0"] = each(lambda t_, m: _dot(t_, _expand(m, bm)), st["tw"], st["av"])

        def finish():
            r2 = each(lambda rt_, rb, m: rt_ + _dot(rb, _expand(m, bm)), st["rt"], st["r_b"], st["a2"])
            y0 = each(lambda rb, u, rk, vb: _dot(rb, _expand(u, bm)) + _dot(rk, vb),
                      st["r_b"], st["u0"], st["r_k"], st["vbd"])
            bl = each(lambda b_, g: b_ * g, st["b"], st["g_rem"])
            kl = each(lambda k_, g: k_ * g, st["kd"], st["g_rem"])
            mc = each(lambda m, bl_: _dot_tn(m, bl_) * ones_blk, st["a2"], bl)
            ds0 = each(lambda u, v_, bl_, kl_: _dot_tn(jnp.concatenate([u, v_], axis=0),
                                                       jnp.concatenate([bl_, kl_], axis=0)) * ones_blk,
                       st["u0"], st["v"], bl, kl)
            for n, (d, c) in enumerate(ch):
                rows = rows_of(c, CHUNK)
                y_ref[0, y_rows(d, c), :] += y0[n]
                r2_s[d, rows, :] = r2[n].astype(BF16)
                mc_s[d, rows_of(c, W_MIX), :] = mc[n].astype(BF16)
                ds_s[d, rows_of(c, W_MIX), :] = ds0[n]
                gt_s[d, rows_of(c, SUBLANE), :] = jnp.broadcast_to(jnp.exp(st["tot"][n]), (SUBLANE, W_MIX))

        return [front, gram] + [double] * 5 + [apply, finish]

    def prep_body(gi, carry):
        half = group // PREP_SPLIT if group >= PREP_SPLIT * PREP_LOCKSTEP else group
        halves = [prep_stages([(d, gi * group + j) for j in range(h0, min(h0 + half, group)) for d in range(N_DIR)])
                  for h0 in range(0, group, half)]
        n_stage = len(halves[0])
        for step in range(n_stage + PREP_SKEW * (len(halves) - 1)):
            for hi, stages in enumerate(halves):
                k = step - PREP_SKEW * hi
                if 0 <= k < n_stage:
                    stages[k]()
        return carry

    lax.fori_loop(0, nchunk // group, prep_body, 0)

    def scan_body(cc, carry):
        cs = [cc, nchunk - 1 - cc]
        s = [st_ref[0, d] for d in range(N_DIR)]
        sb = [x.astype(BF16) for x in s]
        upd = [_dot(sb[d], mc_s[d, rows_of(cs[d], W_MIX), :]) for d in range(N_DIR)]
        ys = [_dot_nt(r2_s[d, rows_of(cs[d], CHUNK), :], sb[d]) for d in range(N_DIR)]
        for d in range(N_DIR):
            st_ref[0, d] = (s[d] * gt_s[d, rows_of(cs[d], SUBLANE), :][0:1, :] + upd[d]
                            + ds_s[d, rows_of(cs[d], W_MIX), :])
            y_ref[0, y_rows(d, cs[d]), :] += ys[d]
        return carry

    lax.fori_loop(0, nchunk, scan_body, 0)

    @pl.when(i == nblk - 1)
    def _():
        def epilogue(j, carry):
            rows = pl.ds(pl.multiple_of(j * tb, tb), tb)
            y = y_ref[0, rows, :]
            mu = _dot_x2(y, ones_blk) * (1.0 / HEAD)
            yc = y - mu
            var = _dot_x2(yc * yc, ones_blk) * (1.0 / HEAD)
            yn = yc * lax.rsqrt(var + RW_LN_EPS) * lng_ref[...] + lnb_ref[...]
            r = z_ref[0, rows, RW_R].astype(F32)
            k = z_ref[0, rows, RW_K].astype(F32)
            v = z_ref[0, rows, RW_V].astype(F32)
            gl = z_ref[0, rows, RW_GATE].astype(F32)
            bonus = _dot_x2(r * k * rk_ref[...], ones_blk) * v
            gate = _dot(jax.nn.sigmoid(gl), g2_ref[...])
            o_ref[0, rows, :] = ((yn + bonus) * gate).astype(ACT)
            return carry

        lax.fori_loop(0, nblk, epilogue, 0)


def _rw_scan(zs, s0, p):
    b, t, c = zs.shape
    tb = min(SCAN_TB, t)
    nblk = t // tb
    nchunk = tb // CHUNK
    vec = lambda n: pl.BlockSpec((1, n), lambda i, j: (0, 0))
    seq = lambda w_: pl.BlockSpec((1, t, w_), lambda i, j: (i, 0, 0))
    st_spec = pl.BlockSpec((1, N_DIR, W_MIX, W_MIX), lambda i, j: (i, 0, 0, 0))
    return pl.pallas_call(
        functools.partial(_rw_scan_kernel, nchunk=nchunk, nblk=nblk),
        grid=(b, nblk),
        in_specs=[
            seq(c), st_spec,
            pl.BlockSpec((LANE, 2 * N_DIR * W_MIX), lambda i, j: (0, 0)),
            vec(1024), vec(W_MIX), vec(W_MIX), vec(W_MIX), vec(W_MIX), vec(W_MIX),
            pl.BlockSpec((LANE, W_MIX), lambda i, j: (0, 0)),
        ],
        out_specs=[seq(W_MIX), st_spec],
        out_shape=[
            jax.ShapeDtypeStruct((b, t, W_MIX), ACT),
            jax.ShapeDtypeStruct((b, N_DIR, W_MIX, W_MIX), F32),
        ],
        scratch_shapes=[pltpu.VMEM((1, t, W_MIX), F32)]
        + [pltpu.VMEM((N_DIR, tb, W_MIX), F32) for _ in range(4)] + [
            pltpu.VMEM((N_DIR, tb, W_MIX), BF16),
            pltpu.VMEM((N_DIR, nchunk * W_MIX, W_MIX), BF16),
            pltpu.VMEM((N_DIR, nchunk * W_MIX, W_MIX), F32),
            pltpu.VMEM((N_DIR, nchunk * 8, W_MIX), F32),
        ],
        compiler_params=_cparams(("arbitrary", "arbitrary")),
        name="rw_scan",
    )(zs, s0, p["rw_wl"], p["rw_lb"], p["rw_kk"], p["rw_ka"], p["rw_ln_g"], p["rw_ln_b"], p["rw_rk"], p["rw_g2"])


def _rwkv_mixer(za, s0_bd, p, latent):
    zs = _rw_shift(za, p["rw_mu"], latent)
    return _rw_scan(zs, s0_bd, p)


def _gla_scan_kernel(z_ref, s0_ref, wg_ref, gb_ref, lng_ref, o_ref, st_ref,
                     y_ref, la_s, qe_s, ds_s, dec_s, *, nchunk, nblk):
    i = pl.program_id(1)
    tb = nchunk * CHUNK

    @pl.when(i == 0)
    def _():
        st_ref[...] = s0_ref[...]
        y_ref[...] = jnp.zeros(y_ref.shape, F32)

    def y_rows(d, c):
        blk = i if d == 0 else nblk - 1 - i
        return pl.ds(pl.multiple_of((blk * nchunk + c) * CHUNK, CHUNK), CHUNK)

    for d in range(N_DIR):
        blk = i if d == 0 else nblk - 1 - i
        logit = _dot(z_ref[0, pl.ds(pl.multiple_of(blk * tb, tb), tb), GLA_GATE], wg_ref[d]) + gb_ref[d]
        la_s[d] = jax.nn.log_sigmoid(logit) * (1.0 / GLA_GATE_NORM)

    bm_k = _block_mask(N_HEADS * CHUNK, GLA_HK, CHUNK, GLA_DK).astype(BF16)
    bm_v = _block_mask(N_HEADS * CHUNK, W_MIX, CHUNK, HEAD).astype(BF16)
    bm_s = _block_mask(W_MIX, GLA_HK, HEAD, GLA_DK)
    tri = [_tri(d, False) for d in range(N_DIR)]
    tw_incl = [_tri_wide(d, False) for d in range(N_DIR)]

    def rows_of(c, n):
        return pl.ds(pl.multiple_of(c * n, n), n)

    group = min(PREP_GROUP, nchunk)

    def prep_body(gi, carry):
        ch = [(d, gi * group + j) for j in range(group) for d in range(N_DIR)]
        each = lambda fn, *ls: [fn(*xs) for xs in zip(*ls)]
        ds_ = [d for d, _ in ch]
        la = [la_s[d, rows_of(c, CHUNK), :] for d, c in ch]
        q = [z_ref[0, y_rows(d, c), GLA_Q].astype(F32) * (GLA_DK ** -0.5) for d, c in ch]
        k = [z_ref[0, y_rows(d, c), GLA_K].astype(F32) for d, c in ch]
        v = [z_ref[0, y_rows(d, c), GLA_V].astype(F32) for d, c in ch]
        cum = each(lambda d, x: _dot_left_x2(tri[d], x), ds_, la)
        last = each(lambda d, x: x[CHUNK - 1:CHUNK, :] if d == 0 else x[0:1, :], ds_, cum)
        qe = each(lambda q_, x: q_ * jnp.exp(x), q, cum)
        ke = each(lambda k_, x: k_ * jnp.exp(-x), k, cum)
        kl = each(lambda k_, l, x: k_ * jnp.exp(l - x), k, last, cum)
        att = each(lambda d, q_, k_: _dot_nt(q_, _expand(k_, bm_k)) * tw_incl[d], ds_, qe, ke)
        o = each(lambda a_, v_: _dot(a_, _expand(v_, bm_v)), att, v)
        dst = each(lambda v_, k_: _dot_tn(v_, k_) * bm_s, v, kl)
        for n, (d, c) in enumerate(ch):
            y_ref[0, y_rows(d, c), :] += o[n]
            qe_s[d, rows_of(c, CHUNK), :] = qe[n].astype(BF16)
            ds_s[d, rows_of(c, W_MIX), :] = dst[n]
            dec_s[d, rows_of(c, SUBLANE), :] = jnp.broadcast_to(jnp.exp(last[n]), (SUBLANE, GLA_HK))
        return carry

    lax.fori_loop(0, nchunk // group, prep_body, 0)

    def scan_body(cc, carry):
        cs = [cc, nchunk - 1 - cc]
        for d in range(N_DIR):
            s = st_ref[0, d]
            y_ref[0, y_rows(d, cs[d]), :] += _dot_nt(qe_s[d, rows_of(cs[d], CHUNK), :], s)
            st_ref[0, d] = s * dec_s[d, rows_of(cs[d], SUBLANE), :][0:1, :] + ds_s[d, rows_of(cs[d], W_MIX), :]
        return carry

    lax.fori_loop(0, nchunk, scan_body, 0)

    @pl.when(i == nblk - 1)
    def _():
        ones_blk = _block_mask(W_MIX, W_MIX, HEAD, HEAD)

        def epilogue(j, carry):
            rows = pl.ds(pl.multiple_of(j * tb, tb), tb)
            y = y_ref[0, rows, :]
            ms = _dot_x2(y * y, ones_blk) * (1.0 / HEAD)
            y = y * lax.rsqrt(ms + NORM_EPS) * lng_ref[...]
            og = z_ref[0, rows, GLA_OG].astype(F32)
            o_ref[0, rows, :] = (y * (og * jax.nn.sigmoid(og))).astype(ACT)
            return carry

        lax.fori_loop(0, nblk, epilogue, 0)


def _gla_mixer(zb, s0, p):
    b, t, c = zb.shape
    tb = min(SCAN_TB, t)
    nblk = t // tb
    seq = lambda w_: pl.BlockSpec((1, t, w_), lambda i, j: (i, 0, 0))
    st_spec = pl.BlockSpec((1, N_DIR, W_MIX, GLA_HK), lambda i, j: (i, 0, 0, 0))
    nchunk = tb // CHUNK
    return pl.pallas_call(
        functools.partial(_gla_scan_kernel, nchunk=nchunk, nblk=nblk),
        grid=(b, nblk),
        in_specs=[
            seq(c), st_spec,
            pl.BlockSpec((N_DIR, LANE, GLA_HK), lambda i, j: (0, 0, 0)),
            pl.BlockSpec((N_DIR, 1, GLA_HK), lambda i, j: (0, 0, 0)),
            pl.BlockSpec((1, W_MIX), lambda i, j: (0, 0)),
        ],
        out_specs=[seq(W_MIX), st_spec],
        out_shape=[
            jax.ShapeDtypeStruct((b, t, W_MIX), ACT),
            jax.ShapeDtypeStruct((b, N_DIR, W_MIX, GLA_HK), F32),
        ],
        scratch_shapes=[
            pltpu.VMEM((1, t, W_MIX), F32),
            pltpu.VMEM((N_DIR, tb, GLA_HK), F32),
            pltpu.VMEM((N_DIR, tb, GLA_HK), BF16),
            pltpu.VMEM((N_DIR, nchunk * W_MIX, GLA_HK), F32),
            pltpu.VMEM((N_DIR, nchunk * 8, GLA_HK), F32),
        ],
        compiler_params=_cparams(("arbitrary", "arbitrary")),
        name="gla_scan",
    )(zb, s0, p["gla_wg"], p["gla_gb"], p["gla_ln_g"])


LRU_TT = 256
LRU_HALO = 8
LRU_SUB = 8


def _lru_kernel(z_ref, h0_ref, cw_ref, cb_ref, wax_ref, bax_ref, lam_ref, y_ref, hf_ref, xpad, hfwd, *, t):
    tt = min(LRU_TT, t)
    ntile = t // tt
    xpad[0:LRU_HALO, :] = jnp.zeros((LRU_HALO, W_MIX), F32)
    xpad[LRU_HALO + t:2 * LRU_HALO + t, :] = jnp.zeros((LRU_HALO, W_MIX), F32)

    def fill(j, carry):
        base = pl.multiple_of(j * tt, tt)
        xpad[pl.ds(base + LRU_HALO, tt), :] = z_ref[0, pl.ds(base, tt), 0:W_MIX].astype(F32)
        return carry

    lax.fori_loop(0, ntile, fill, 0)
    sub = _iota((tt // LRU_SUB, LRU_SUB, W_MIX), 1)

    def tile_scan(j, h, d):
        base = pl.multiple_of(j * tt, tt)
        win = xpad[pl.ds(base, tt + 2 * LRU_HALO), :]
        xc = jnp.zeros((tt, W_MIX), F32) + cb_ref[d]
        for tap in range(LRU_CONV):
            off = LRU_HALO - (LRU_CONV - 1) + tap if d == 0 else LRU_HALO + (LRU_CONV - 1) - tap
            xc = xc + cw_ref[d, tap:tap + 1, :] * win[off:off + tt, :]
        g = _dot(xc, wax_ref[d]) + bax_ref[d]
        gr = jax.nn.sigmoid(g[:, 0:W_MIX])
        gi = jax.nn.sigmoid(g[:, W_MIX:2 * W_MIX])
        log_a = -LRU_C * gr * jax.nn.softplus(-lam_ref[d])
        a = jnp.exp(log_a)
        bv = jnp.sqrt(1.0 - jnp.exp(2.0 * log_a)) * gi * xc
        ngrp = tt // LRU_SUB
        a = a.reshape(ngrp, LRU_SUB, W_MIX)
        bv = bv.reshape(ngrp, LRU_SUB, W_MIX)
        s = 1
        while s < LRU_SUB:
            shift = s if d == 0 else LRU_SUB - s
            keep = (sub >= s) if d == 0 else (sub < LRU_SUB - s)
            a_sh = jnp.where(keep, pltpu.roll(a, shift, 1), 1.0)
            b_sh = jnp.where(keep, pltpu.roll(bv, shift, 1), 0.0)
            bv = a * b_sh + bv
            a = a * a_sh
            s *= 2
        out = [None] * ngrp
        for gg in range(ngrp):
            g = gg if d == 0 else ngrp - 1 - gg
            hg = a[g] * h + bv[g]
            out[g] = hg
            h = hg[LRU_SUB - 1:LRU_SUB, :] if d == 0 else hg[0:1, :]
        return jnp.concatenate(out, axis=0), h, base

    def fwd(j, h):
        ht, h, base = tile_scan(j, h, 0)
        hfwd[pl.ds(base, tt), :] = ht
        return h

    h_end = lax.fori_loop(0, ntile, fwd, h0_ref[0, 0:1, :])
    hf_ref[0, 0:1, :] = h_end

    def bwd(jj, h):
        j = ntile - 1 - jj
        ht, h, base = tile_scan(j, h, 1)
        gb = z_ref[0, pl.ds(base, tt), W_MIX:2 * W_MIX].astype(F32)
        y_ref[0, pl.ds(base, tt), :] = ((hfwd[pl.ds(base, tt), :] + ht) * jax.nn.gelu(gb)).astype(ACT)
        return h

    h_end = lax.fori_loop(0, ntile, bwd, h0_ref[0, 1:2, :])
    hf_ref[0, 1:2, :] = h_end


def _lru_mixer(zc, h0, p):
    b, t, c = zc.shape
    full = lambda *s: pl.BlockSpec(s, lambda i: (0,) * len(s))
    return pl.pallas_call(
        functools.partial(_lru_kernel, t=t),
        grid=(b,),
        in_specs=[
            pl.BlockSpec((1, t, c), lambda i: (i, 0, 0)),
            pl.BlockSpec((1, N_DIR, W_MIX), lambda i: (i, 0, 0)),
            full(N_DIR, LRU_CONV, W_MIX), full(N_DIR, 1, W_MIX),
            full(N_DIR, W_MIX, 2 * W_MIX), full(N_DIR, 1, 2 * W_MIX), full(N_DIR, 1, W_MIX),
        ],
        out_specs=[pl.BlockSpec((1, t, W_MIX), lambda i: (i, 0, 0)),
                   pl.BlockSpec((1, N_DIR, W_MIX), lambda i: (i, 0, 0))],
        out_shape=[jax.ShapeDtypeStruct((b, t, W_MIX), ACT), jax.ShapeDtypeStruct((b, N_DIR, W_MIX), F32)],
        scratch_shapes=[pltpu.VMEM((t + 2 * LRU_HALO, W_MIX), F32), pltpu.VMEM((t, W_MIX), F32)],
        compiler_params=_cparams(("arbitrary",)),
        name="lru",
    )(zc, h0, p["lru_cw"], p["lru_cb"], p["lru_wax"], p["lru_bax"], p["lru_lam"])


CV_TT = 256
CV_WIN = 16


def _conv_kernel(z_ref, dw_ref, dwb_ref, lng_ref, lnb_ref, pw_ref, pwb_ref, y_ref, upad, *, t, latent):
    tt = min(CV_TT, t)
    ntile = t // tt
    pad = CV_PAD * GRID_W if latent else CV_WIN
    upad[0:pad, :] = jnp.zeros((pad, W_MIX), F32)
    upad[pad + t:2 * pad + t, :] = jnp.zeros((pad, W_MIX), F32)

    def fill(j, carry):
        base = pl.multiple_of(j * tt, tt)
        z = z_ref[0, pl.ds(base, tt), :].astype(F32)
        upad[pl.ds(base + pad, tt), :] = z[:, 0:W_MIX] * jax.nn.sigmoid(z[:, W_MIX:2 * W_MIX])
        return carry

    lax.fori_loop(0, ntile, fill, 0)

    def tile(j, carry):
        base = pl.multiple_of(j * tt, tt)
        if latent:
            half = W_MIX // 2
            col = _iota((tt, half), 0) % GRID_W
            win = upad[pl.ds(pl.multiple_of(base + pad - CV_WIN, CV_WIN), tt + 2 * CV_WIN), 0:half]
            accw = jnp.zeros((tt, half), F32)
            acch = jnp.zeros((tt, half), F32)
            for tap in range(CV_KERNEL):
                dlt = tap - CV_PAD
                x = win[CV_WIN + dlt:CV_WIN + dlt + tt, :]
                if dlt < 0:
                    x = jnp.where(col >= -dlt, x, 0.0)
                elif dlt > 0:
                    x = jnp.where(col < GRID_W - dlt, x, 0.0)
                accw = accw + dw_ref[tap:tap + 1, 0:half] * x
                rows = pl.ds(pl.multiple_of(base + pad + dlt * GRID_W, GRID_W), tt)
                acch = acch + dw_ref[tap:tap + 1, half:W_MIX] * upad[rows, half:W_MIX]
            u = jnp.concatenate([accw, acch], axis=1)
        else:
            win = upad[pl.ds(base + pad - CV_WIN, tt + 2 * CV_WIN), :]
            u = jnp.zeros((tt, W_MIX), F32)
            for tap in range(CV_KERNEL):
                dlt = tap - CV_PAD
                u = u + dw_ref[tap:tap + 1, :] * win[CV_WIN + dlt:CV_WIN + dlt + tt, :]
        u = u + dwb_ref[...]
        mu = jnp.mean(u, axis=-1, keepdims=True)
        uc = u - mu
        var = jnp.mean(uc * uc, axis=-1, keepdims=True)
        un = uc * lax.rsqrt(var + 1e-5) * lng_ref[...] + lnb_ref[...]
        un = un * jax.nn.sigmoid(un)
        y_ref[0, pl.ds(base, tt), :] = (_dot(un, pw_ref[...]) + pwb_ref[...]).astype(ACT)
        return carry

    lax.fori_loop(0, ntile, tile, 0)


def _conv_mixer(zd, p, latent):
    b, t, c = zd.shape
    pad = CV_PAD * GRID_W if latent else CV_WIN
    full = lambda *s: pl.BlockSpec(s, lambda i: (0,) * len(s))
    vec = full(1, W_MIX)
    return pl.pallas_call(
        functools.partial(_conv_kernel, t=t, latent=latent),
        grid=(b,),
        in_specs=[pl.BlockSpec((1, t, c), lambda i: (i, 0, 0)),
                  full(CV_KERNEL, W_MIX), vec, vec, vec, full(W_MIX, W_MIX), vec],
        out_specs=pl.BlockSpec((1, t, W_MIX), lambda i: (i, 0, 0)),
        out_shape=jax.ShapeDtypeStruct((b, t, W_MIX), ACT),
        scratch_shapes=[pltpu.VMEM((t + 2 * pad, W_MIX), F32)],
        compiler_params=_cparams(("arbitrary",)),
        name="conv_grid" if latent else "conv_ctx",
    )(zd, p["cv_dw_w"], p["cv_dw_b"], p["cv_ln_g"], p["cv_ln_b"], p["cv_pw_w"], p["cv_pw_b"])


MOE_TM = 512
MOE_RB = 128
E_PAD = LANE


def _route(sel, scores):
    grp = []
    for g in range(N_GROUPS):
        s = sel[GROUP_SIZE * g:GROUP_SIZE * (g + 1)]
        best_pair = None
        for i in range(GROUP_SIZE):
            for j in range(i + 1, GROUP_SIZE):
                pair = s[i] + s[j]
                best_pair = pair if best_pair is None else jnp.maximum(best_pair, pair)
        grp.append(best_pair)
    best = jnp.zeros_like(grp[0], dtype=jnp.int32)
    top = grp[0]
    for g in range(1, N_GROUPS):
        better = grp[g] > top
        best = jnp.where(better, g, best)
        top = jnp.where(better, grp[g], top)
    neg = jnp.full_like(sel[0], -jnp.inf)
    msel = [jnp.where(best == e // GROUP_SIZE, sel[e], neg) for e in range(N_EXPERTS)]
    picks = []
    for _ in range(2):
        idx = jnp.zeros_like(best)
        top = msel[0]
        for e in range(1, N_EXPERTS):
            better = msel[e] > top
            idx = jnp.where(better, e, idx)
            top = jnp.where(better, msel[e], top)
        picks.append(idx)
        msel = [jnp.where(idx == e, neg, msel[e]) for e in range(N_EXPERTS)]
    chosen = [jnp.where((picks[0] == e) | (picks[1] == e), scores[e], 0.0) for e in range(N_EXPERTS)]
    total = chosen[0]
    for e in range(1, N_EXPERTS):
        total = total + chosen[e]
    return [ch / total for ch in chosen], best


def _moe_kernel(x_ref, ya_ref, yb_ref, yc_ref, yd_ref, wo_ref, g1_ref, sh2_ref, sc2_ref, g2_ref, n2_ref,
                wr_ref, br_ref, w1_ref, w3_ref, w2_ref, nf_ref, o_ref,
                x1_s, h2_s, gt_s, gate_s, acc_s, he_s, og_s, pmt_s, seg_s, *, final_norm):
    grp = pl.program_id(2)
    tm = x1_s.shape[0]

    @pl.when(grp == 0)
    def _():
        y = sum(_dot(y_ref[0], wo_ref[W_MIX * m:W_MIX * (m + 1), :])
                for m, y_ref in enumerate((ya_ref, yb_ref, yc_ref, yd_ref)))
        x1 = x_ref[0] + g1_ref[0] * y
        x1_s[...] = x1
        h2 = x1 * lax.rsqrt(jnp.mean(x1 * x1, axis=-1, keepdims=True) + NORM_EPS) * n2_ref[...]
        h2 = h2 * (1.0 + sc2_ref[0]) + sh2_ref[0]
        logits = _dot3_nt(wr_ref[...], h2)
        scores = jax.nn.sigmoid(logits)
        selm = scores + br_ref[...]
        gates, best = _route([selm[i:i + 1, :] for i in range(N_EXPERTS)],
                             [scores[i:i + 1, :] for i in range(N_EXPERTS)])
        gt_s[...] = jnp.zeros(gt_s.shape, F32)
        for i in range(N_EXPERTS):
            gt_s[i:i + 1, :] = gates[i]

        og = [jnp.where(best == g, 1.0, 0.0) for g in range(N_GROUPS)]
        og_s[...] = jnp.zeros(og_s.shape, F32)
        start = jnp.int32(0)
        starts = []
        for g in range(N_GROUPS):
            og_s[g:g + 1, :] = og[g]
            starts.append(start)
            seg_s[g] = start
            start = start + jnp.sum(og[g]).astype(jnp.int32)
            seg_s[N_GROUPS + g] = start
        before = jnp.where(_iota((tm, tm), 0) < _iota((tm, tm), 1), 1.0, 0.0).astype(BF16)
        rank = jnp.dot(og_s[...].astype(BF16), before, preferred_element_type=F32)
        pos = og[0] * (starts[0].astype(F32) + rank[0:1, :])
        for g in range(1, N_GROUPS):
            pos = pos + og[g] * (starts[g].astype(F32) + rank[g:g + 1, :])
        gt_s[N_EXPERTS:N_EXPERTS + 1, :] = pos
        gate_tok = gt_s[...].T
        slot_l = _iota((tm, tm), 1).astype(F32)
        slot_s = _iota((tm, tm), 0).astype(F32)
        pmt_s[...] = jnp.where(slot_l == gate_tok[:, N_EXPERTS:N_EXPERTS + 1], 1.0, 0.0).astype(BF16)
        pm = jnp.where(slot_s == pos, 1.0, 0.0).astype(BF16)
        h2_s[...] = jnp.dot(pm, h2.astype(BF16), preferred_element_type=F32).astype(BF16)
        gate_s[...] = _dot_left_x2(pm, gate_tok)
        acc_s[...] = jnp.zeros(acc_s.shape, F32)

    pick = (_iota((E_PAD, GROUP_SIZE * LANE), 0)
            == GROUP_SIZE * grp + _iota((E_PAD, GROUP_SIZE * LANE), 1) // LANE).astype(F32)

    def block(rb, carry):
        rows = pl.ds(pl.multiple_of(rb * MOE_RB, MOE_RB), MOE_RB)
        gsel = _dot_x2(gate_s[rows, :], pick)
        h2 = h2_s[rows, :]
        for j in range(GROUP_SIZE):
            he = jnp.dot(h2, w1_ref[j], preferred_element_type=F32)
            he = he * jax.nn.sigmoid(he) * jnp.dot(h2, w3_ref[j], preferred_element_type=F32)
            g = gsel[:, LANE * j:LANE * (j + 1)]
            he_s[rows, D_EXPERT * j:D_EXPERT * (j + 1)] = (he * jnp.concatenate([g, g], axis=1)).astype(BF16)
        w2g = w2_ref[...].reshape(GROUP_SIZE * D_EXPERT, D_MODEL)
        acc_s[rows, :] += jnp.dot(he_s[rows, :], w2g, preferred_element_type=F32)
        return carry

    first = seg_s[grp] // MOE_RB
    last = (seg_s[N_GROUPS + grp] + (MOE_RB - 1)) // MOE_RB
    lax.fori_loop(first, last, block, 0)

    @pl.when(grp == N_GROUPS - 1)
    def _():
        x2 = x1_s[...] + g2_ref[0] * _dot_left_x2(pmt_s[...], acc_s[...])
        if final_norm:
            x2 = x2 * lax.rsqrt(jnp.mean(x2 * x2, axis=-1, keepdims=True) + NORM_EPS) * nf_ref[...]
        o_ref[0] = x2


def _out_moe(x, ys, m, p, norm_f, final_norm):
    b, t, _ = x.shape
    tm = min(MOE_TM, t)
    tok = lambda w_: pl.BlockSpec((1, tm, w_), lambda i, j, e: (i, j, 0))
    mod = lambda k: pl.BlockSpec((1, 1, D_MODEL), lambda i, j, e, k=k: (i, 0, k))
    full = lambda *s: pl.BlockSpec(s, lambda i, j, e: (0,) * len(s))
    return pl.pallas_call(
        functools.partial(_moe_kernel, final_norm=final_norm),
        grid=(b, t // tm, N_GROUPS),
        in_specs=[tok(D_MODEL), tok(W_MIX), tok(W_MIX), tok(W_MIX), tok(W_MIX),
                  full(D_MODEL, D_MODEL), mod(2), mod(3), mod(4), mod(5), full(1, D_MODEL),
                  full(N_EXPERTS, D_MODEL), full(N_EXPERTS, 1),
                  pl.BlockSpec((GROUP_SIZE, D_MODEL, D_EXPERT), lambda i, j, e: (e, 0, 0)),
                  pl.BlockSpec((GROUP_SIZE, D_MODEL, D_EXPERT), lambda i, j, e: (e, 0, 0)),
                  pl.BlockSpec((GROUP_SIZE, D_EXPERT, D_MODEL), lambda i, j, e: (e, 0, 0)),
                  full(1, D_MODEL)],
        out_specs=tok(D_MODEL),
        out_shape=jax.ShapeDtypeStruct((b, t, D_MODEL), F32),
        scratch_shapes=[pltpu.VMEM((tm, D_MODEL), F32), pltpu.VMEM((tm, D_MODEL), BF16),
                        pltpu.VMEM((E_PAD, tm), F32), pltpu.VMEM((tm, E_PAD), F32),
                        pltpu.VMEM((tm, D_MODEL), F32), pltpu.VMEM((tm, D_EXPERT * GROUP_SIZE), BF16),
                        pltpu.VMEM((SUBLANE, tm), F32), pltpu.VMEM((tm, tm), BF16),
                        pltpu.SMEM((2 * N_GROUPS,), jnp.int32)],
        compiler_params=_cparams(("arbitrary", "arbitrary", "arbitrary")),
        name="out_moe",
    )(x, *ys, p["w_out"], m, m, m, m, p["norm2_g"], p["w_router_t"], p["b_router"], p["e_w1"], p["e_w3"], p["e_w2"],
      norm_f)


def _gla_pack(z):
    hk = GLA_HK
    lead = z.shape[:-1]
    gl = z[..., 2 * hk + W_MIX:2 * hk + W_MIX + 32]
    return jnp.concatenate([z[..., 0:2 * hk + W_MIX], z[..., 2 * hk + W_MIX + 32:], gl,
                            jnp.zeros(lead + (96,), z.dtype)], -1)


def _gla_state_in(s):
    b = s.shape[0]
    eye = jnp.eye(N_HEADS, dtype=s.dtype)
    return jnp.einsum("bdhkv,hg->bdhvgk", s, eye).reshape(b, N_DIR, W_MIX, GLA_HK)


def _gla_state_out(st):
    b = st.shape[0]
    eye = jnp.eye(N_HEADS, dtype=st.dtype)
    return jnp.einsum("bdhvgk,hg->bdhkv", st.reshape(b, N_DIR, N_HEADS, HEAD, N_HEADS, GLA_DK), eye)


def _rw_state_in(s):
    b = s.shape[0]
    eye = jnp.eye(N_HEADS, dtype=s.dtype)
    return jnp.einsum("bdhvk,hg->bdhvgk", s, eye).reshape(b, N_DIR, W_MIX, W_MIX)


def _rw_state_out(st):
    b = st.shape[0]
    eye = jnp.eye(N_HEADS, dtype=st.dtype)
    return jnp.einsum("bdhvgk,hg->bdhvk", st.reshape(b, N_DIR, N_HEADS, HEAD, N_HEADS, HEAD), eye)


def _prep_params(d):
    L = DEPTH
    z = lambda *s: jnp.zeros(s, F32)
    out = {}
    out["rw_mu"] = jnp.concatenate([d["rw_mu"], z(L, ZA_W - d["rw_mu"].shape[-1])], -1).reshape(L, 1, ZA_W)
    w2, a2 = d["rw_w2"], d["rw_a2"]
    zz = z(L, 32, 256)
    rows = [
        jnp.concatenate([w2[:, 0], zz, zz, zz], -1),
        jnp.concatenate([zz, w2[:, 1], zz, zz], -1),
        jnp.concatenate([zz, zz, a2[:, 0], zz], -1),
        jnp.concatenate([zz, zz, zz, a2[:, 1]], -1),
    ]
    out["rw_wl"] = jnp.concatenate(rows, 1).astype(BF16)
    out["rw_lb"] = jnp.concatenate([d["rw_w0"][:, 0], d["rw_w0"][:, 1], d["rw_a0"][:, 0], d["rw_a0"][:, 1]],
                                   -1).reshape(L, 1, 1024)
    for n in ("rw_kk", "rw_ka", "rw_ln_g", "rw_ln_b"):
        out[n] = d[n].reshape(L, 1, W_MIX)
    out["rw_rk"] = d["rw_rk"].reshape(L, 1, W_MIX)
    out["rw_g2"] = jnp.concatenate([d["rw_g2"], z(L, 64, W_MIX)], 1).astype(BF16)

    gk2 = d["gla_gk2"]
    z16, z96 = z(L, 16, GLA_HK), z(L, 96, GLA_HK)
    out["gla_wg"] = jnp.stack([jnp.concatenate([gk2[:, 0], z16, z96], 1),
                               jnp.concatenate([z16, gk2[:, 1], z96], 1)], 1).astype(BF16)
    out["gla_gb"] = d["gla_gkb"].reshape(L, N_DIR, 1, GLA_HK)
    out["gla_ln_g"] = jnp.tile(d["gla_ln_g"], (1, N_HEADS)).reshape(L, 1, W_MIX)

    eye = jnp.eye(4, dtype=F32)
    bd = lambda w: jnp.einsum("ldgij,gh->ldgihj", w, eye).reshape(L, N_DIR, W_MIX, W_MIX)
    out["lru_cw"] = d["lru_conv_w"]
    out["lru_cb"] = d["lru_conv_b"].reshape(L, N_DIR, 1, W_MIX)
    out["lru_wax"] = jnp.concatenate([bd(d["lru_wa"]), bd(d["lru_wx"])], -1).astype(BF16)
    out["lru_bax"] = jnp.concatenate([d["lru_ba"], d["lru_bx"]], -1).reshape(L, N_DIR, 1, 2 * W_MIX)
    out["lru_lam"] = d["lru_lam"].reshape(L, N_DIR, 1, W_MIX)

    out["cv_dw_w"] = d["cv_dw_w"]
    for n in ("cv_dw_b", "cv_ln_g", "cv_ln_b", "cv_pw_b"):
        out[n] = d[n].reshape(L, 1, W_MIX)
    out["cv_pw_w"] = d["cv_pw_w"].astype(BF16)

    w_in = d["w_in"]
    o1 = RW_COLS
    o2 = o1 + GLA_COLS
    o3 = o2 + ZC_W
    out["w_in"] = jnp.concatenate([w_in[..., 0:o1], z(L, D_MODEL, ZA_W - o1), _gla_pack(w_in[..., o1:o2]),
                                   w_in[..., o2:o3], w_in[..., o3:]], -1).astype(BF16)
    out["w_out"] = d["w_out"].astype(BF16)
    out["norm1_g"] = d["norm1_g"].reshape(L, 1, D_MODEL)
    out["norm2_g"] = d["norm2_g"].reshape(L, 1, D_MODEL)
    for n in ("e_w1", "e_w3", "e_w2"):
        out[n] = d[n].astype(BF16)
    return out


def kernel(x_prompt, x_sample, state_rwkv, state_gla, state_lru, c, c_ctx, norm1_g, norm2_g, norm_f_g, w_ada, b_ada, w_in, w_out, rw_mu, rw_w0, rw_w2, rw_a0, rw_a2, rw_g2, rw_kk, rw_ka, rw_rk, rw_ln_g, rw_ln_b, gla_gk2, gla_gkb, gla_ln_g, lru_conv_w, lru_conv_b, lru_wa, lru_ba, lru_wx, lru_bx, lru_lam, cv_dw_w, cv_dw_b, cv_ln_g, cv_ln_b, cv_pw_w, cv_pw_b, w_router, b_router, e_w1, e_w3, e_w2):
    d = dict(norm1_g=norm1_g, norm2_g=norm2_g, w_in=w_in, w_out=w_out,
             rw_mu=rw_mu, rw_w0=rw_w0, rw_w2=rw_w2, rw_a0=rw_a0, rw_a2=rw_a2, rw_g2=rw_g2, rw_kk=rw_kk,
             rw_ka=rw_ka, rw_rk=rw_rk, rw_ln_g=rw_ln_g, rw_ln_b=rw_ln_b,
             gla_gk2=gla_gk2, gla_gkb=gla_gkb, gla_ln_g=gla_ln_g,
             lru_conv_w=lru_conv_w, lru_conv_b=lru_conv_b, lru_wa=lru_wa, lru_ba=lru_ba, lru_wx=lru_wx,
             lru_bx=lru_bx, lru_lam=lru_lam,
             cv_dw_w=cv_dw_w, cv_dw_b=cv_dw_b, cv_ln_g=cv_ln_g, cv_ln_b=cv_ln_b, cv_pw_w=cv_pw_w, cv_pw_b=cv_pw_b,
             e_w1=e_w1, e_w3=e_w3, e_w2=e_w2)
    P = _prep_params(d)
    shared = {"w_router_t": w_router.T, "b_router": b_router.reshape(N_EXPERTS, 1)}
    norm_f = norm_f_g.reshape(1, D_MODEL)

    bp, tp, _ = x_prompt.shape
    bs, ts, _ = x_sample.shape
    n_mod = 1 + bs
    mod_rows = -(-n_mod // 8) * 8
    cvec = jnp.concatenate([c_ctx[None], c, jnp.zeros((mod_rows - n_mod, D_MODEL), F32)], 0)
    mod = _ada_mod(cvec, w_ada, b_ada)

    tp_flat = min(bp * tp, 1024)
    bp_flat = bp * tp // tp_flat
    xp = x_prompt.reshape(bp_flat, tp_flat, D_MODEL)
    xs = x_sample
    zero_rw = jnp.zeros((bp, N_DIR, W_MIX, W_MIX), F32)
    zero_gla = jnp.zeros((bp, N_DIR, W_MIX, GLA_HK), F32)
    zero_lru = jnp.zeros((bp, N_DIR, W_MIX), F32)

    def layer(x, m, seq_shape, latent, s_rw, s_gla, s_lru, p, last):
        flat_shape = x.shape[:2]
        zs = _in_proj(x, m, p["norm1_g"], p["w_in"])
        za, zb, zc, zd = (z.reshape(seq_shape + (z.shape[-1],)) for z in zs)
        ya, f_rw = _rwkv_mixer(za, s_rw, p, latent)
        yb, f_gla = _gla_mixer(zb, s_gla, p)
        yc, f_lru = _lru_mixer(zc, s_lru, p)
        yd = _conv_mixer(zd, p, latent)
        ys = [y.reshape(flat_shape + (W_MIX,)) for y in (ya, yb, yc, yd)]
        return _out_moe(x, ys, m, p, norm_f, last), f_rw, f_gla, f_lru

    fin_rw, fin_gla, fin_lru = [], [], []
    for l in range(DEPTH):
        p = {n: a[l] for n, a in P.items()}
        p.update(shared)
        last = l == DEPTH - 1
        m_p = jnp.broadcast_to(mod[l, 0:1], (bp_flat, 6 * D_MODEL)).reshape(bp_flat, 1, 6 * D_MODEL)
        xp, f_rw, f_gla, f_lru = layer(xp, m_p, (bp, tp), False, zero_rw, zero_gla, zero_lru, p, last)
        fin_rw.append(_rw_state_out(f_rw))
        fin_gla.append(_gla_state_out(f_gla))
        fin_lru.append(f_lru)
        m_s = mod[l, 1:1 + bs].reshape(bs, 1, 6 * D_MODEL)
        xs, _, _, _ = layer(xs, m_s, (bs, ts), True, _rw_state_in(state_rwkv[:, l]),
                            _gla_state_in(state_gla[:, l]), state_lru[:, l], p, last)
    return (xp.reshape(bp, tp, D_MODEL), xs, jnp.stack(fin_rw, axis=1), jnp.stack(fin_gla, axis=1),
            jnp.stack(fin_lru, axis=1))
```

```python
import functools

import jax
import jax.numpy as jnp
from jax import lax
from jax.experimental import pallas as pl
from jax.experimental.pallas import tpu as pltpu

F32 = jnp.float32
BF16 = jnp.bfloat16
ACT = jnp.bfloat16

D_MODEL = 1024
DEPTH = 4
GRID_W = 64
N_DIR = 2
W_MIX = 256
N_HEADS = 4
HEAD = W_MIX // N_HEADS
GLA_DK = 32
GLA_HK = N_HEADS * GLA_DK
RW_LN_EPS = 64e-5
NORM_EPS = 1e-6
GLA_GATE_NORM = 16.0
LRU_C = 8.0
LRU_CONV = 4
CV_KERNEL = 31
CV_PAD = (CV_KERNEL - 1) // 2
N_EXPERTS = 16
GROUP_SIZE = 4
N_GROUPS = 4
D_EXPERT = 256
CHUNK = 64
EXP_M_HALF = 0.6065306597126334

LANE = 128
SUBLANE = 8

RW_COLS = 960
GLA_COLS = 800
ZA_W = 1024
ZB_W = 896
ZC_W = 512
ZD_W = 512
P_PAD = ZA_W + ZB_W + ZC_W + ZD_W
RW_R, RW_K, RW_V = slice(0, 256), slice(256, 512), slice(512, 768)
RW_LORA = slice(768, 896)
RW_GATE = slice(896, 1024)
GLA_Q, GLA_K, GLA_V = slice(0, 128), slice(128, 256), slice(256, 512)
GLA_OG = slice(512, 768)
GLA_GATE = slice(768, 896)

VMEM_LIMIT = 48 * 1024 * 1024


def _cparams(sem, vmem=VMEM_LIMIT):
    return pltpu.CompilerParams(dimension_semantics=sem, vmem_limit_bytes=vmem)


def _dot(a, b):
    return jnp.dot(a.astype(BF16), b.astype(BF16), preferred_element_type=F32)


def _dot_nt(a, b):
    return lax.dot_general(a.astype(BF16), b.astype(BF16), (((1,), (1,)), ((), ())),
                           preferred_element_type=F32)


def _dot_tn(a, b):
    return lax.dot_general(a.astype(BF16), b.astype(BF16), (((0,), (0,)), ((), ())),
                           preferred_element_type=F32)


def _split(a):
    hi = a.astype(BF16)
    lo = (a - hi.astype(F32)).astype(BF16)
    return hi, lo


def _dot_x2(a, b_exact):
    hi, lo = _split(a)
    bb = b_exact.astype(BF16)
    return (jnp.dot(hi, bb, preferred_element_type=F32) + jnp.dot(lo, bb, preferred_element_type=F32))


def _dot_left_x2(a_exact, b):
    hi, lo = _split(b)
    aa = a_exact.astype(BF16)
    return (jnp.dot(aa, hi, preferred_element_type=F32) + jnp.dot(aa, lo, preferred_element_type=F32))


def _dot3(a, b):
    ah, al = _split(a)
    bh, bl = _split(b)
    return (jnp.dot(ah, bh, preferred_element_type=F32) + jnp.dot(ah, bl, preferred_element_type=F32)
            + jnp.dot(al, bh, preferred_element_type=F32))


def _dot3_nt(a, b):
    ah, al = _split(a)
    bh, bl = _split(b)
    dn = (((1,), (1,)), ((), ()))
    return (lax.dot_general(ah, bh, dn, preferred_element_type=F32)
            + lax.dot_general(ah, bl, dn, preferred_element_type=F32)
            + lax.dot_general(al, bh, dn, preferred_element_type=F32))


def _iota(shape, axis):
    return lax.broadcasted_iota(jnp.int32, shape, axis)


def _block_mask(rows, cols, rblk, cblk):
    r = _iota((rows, cols), 0) // rblk
    c = _iota((rows, cols), 1) // cblk
    return (r == c).astype(F32)


def _expand(x, bm):
    xb = x.astype(BF16)
    return jnp.concatenate([xb, xb, xb, xb], axis=0) * bm


def _tri(d, strict):
    t = _iota((CHUNK, CHUNK), 0)
    s = _iota((CHUNK, CHUNK), 1)
    if d == 0:
        m = (s < t) if strict else (s <= t)
    else:
        m = (s > t) if strict else (s >= t)
    return m.astype(F32)


def _tri_wide(d, strict):
    t = _iota((CHUNK, N_HEADS * CHUNK), 0)
    s = _iota((CHUNK, N_HEADS * CHUNK), 1) % CHUNK
    if d == 0:
        m = (s < t) if strict else (s <= t)
    else:
        m = (s > t) if strict else (s >= t)
    return m.astype(F32)


ADA_TN = 1536


def _ada_kernel(c_ref, w_ref, b_ref, o_ref):
    c = c_ref[...]
    s = c * jax.nn.sigmoid(c)
    o_ref[0] = _dot3(s, w_ref[0]) + b_ref[0]


def _ada_mod(cvec, w_ada, b_ada):
    rows = cvec.shape[0]
    n_out = w_ada.shape[-1]
    return pl.pallas_call(
        _ada_kernel,
        grid=(DEPTH, n_out // ADA_TN),
        in_specs=[
            pl.BlockSpec((rows, D_MODEL), lambda l, j: (0, 0)),
            pl.BlockSpec((1, D_MODEL, ADA_TN), lambda l, j: (l, 0, j)),
            pl.BlockSpec((1, 1, ADA_TN), lambda l, j: (l, 0, j)),
        ],
        out_specs=pl.BlockSpec((1, rows, ADA_TN), lambda l, j: (l, 0, j)),
        out_shape=jax.ShapeDtypeStruct((DEPTH, rows, n_out), F32),
        compiler_params=_cparams(("arbitrary", "arbitrary")),
        name="ada_mod",
    )(cvec, w_ada, b_ada.reshape(DEPTH, 1, n_out))


IN_TM = 512


def _in_kernel(x_ref, sh_ref, sc_ref, g_ref, w_ref, za_ref, zb_ref, zc_ref, zd_ref):
    x = x_ref[0]
    h = x * lax.rsqrt(jnp.mean(x * x, axis=-1, keepdims=True) + NORM_EPS) * g_ref[...]
    h = (h * (1.0 + sc_ref[0]) + sh_ref[0]).astype(BF16)
    o0, o1, o2 = ZA_W, ZA_W + ZB_W, ZA_W + ZB_W + ZC_W
    za_ref[0] = jnp.dot(h, w_ref[:, 0:o0], preferred_element_type=F32).astype(ACT)
    zb_ref[0] = jnp.dot(h, w_ref[:, o0:o1], preferred_element_type=F32).astype(ACT)
    zc_ref[0] = jnp.dot(h, w_ref[:, o1:o2], preferred_element_type=F32).astype(ACT)
    zd_ref[0] = jnp.dot(h, w_ref[:, o2:P_PAD], preferred_element_type=F32).astype(ACT)


def _in_proj(x, m, g, w):
    b, t, _ = x.shape
    tm = min(IN_TM, t)
    tok = lambda w_: pl.BlockSpec((1, tm, w_), lambda i, j: (i, j, 0))
    mod = lambda k: pl.BlockSpec((1, 1, D_MODEL), lambda i, j, k=k: (i, 0, k))
    return pl.pallas_call(
        _in_kernel,
        grid=(b, t // tm),
        in_specs=[tok(D_MODEL), mod(0), mod(1),
                  pl.BlockSpec((1, D_MODEL), lambda i, j: (0, 0)),
                  pl.BlockSpec((D_MODEL, P_PAD), lambda i, j: (0, 0))],
        out_specs=[tok(ZA_W), tok(ZB_W), tok(ZC_W), tok(ZD_W)],
        out_shape=[jax.ShapeDtypeStruct((b, t, w_), ACT) for w_ in (ZA_W, ZB_W, ZC_W, ZD_W)],
        compiler_params=_cparams(("arbitrary", "arbitrary")),
        name="in_proj",
    )(x, m, m, g, w)


SHIFT_TB = 512


def _shift_grid_kernel(z_ref, up_ref, dn_ref, mu_ref, o_ref, *, nblk):
    i = pl.program_id(1)
    z = z_ref[0].astype(F32)
    tb, c = z.shape
    col = _iota((tb, c), 0) % GRID_W
    lane = _iota((tb, c), 1) % 4
    left = jnp.where(col == 0, 0.0, pltpu.roll(z, 1, 0))
    right = jnp.where(col == GRID_W - 1, 0.0, pltpu.roll(z, tb - 1, 0))
    up_halo = jnp.where(i > 0, up_ref[0, 0].astype(F32), 0.0)
    dn_halo = jnp.where(i < nblk - 1, dn_ref[0, 0].astype(F32), 0.0)
    up = jnp.concatenate([up_halo, z[:tb - GRID_W]], axis=0)
    down = jnp.concatenate([z[GRID_W:], dn_halo], axis=0)
    sh = jnp.where(lane == 0, left, jnp.where(lane == 1, right, jnp.where(lane == 2, up, down)))
    o_ref[0] = (z + (sh - z) * mu_ref[...]).astype(ACT)


def _shift_ctx_kernel(z_ref, mu_ref, o_ref):
    z = z_ref[0].astype(F32)
    t, c = z.shape
    row = _iota((t, c), 0)
    lane = _iota((t, c), 1) % 2
    prev = jnp.where(row == 0, 0.0, pltpu.roll(z, 1, 0))
    nxt = jnp.where(row == t - 1, 0.0, pltpu.roll(z, t - 1, 0))
    sh = jnp.where(lane == 0, prev, nxt)
    o_ref[0] = (z + (sh - z) * mu_ref[...]).astype(ACT)


def _rw_shift(za, mu, latent):
    b, t, c = za.shape
    mu_spec = pl.BlockSpec((1, c), lambda *_: (0, 0))
    if not latent:
        return pl.pallas_call(
            _shift_ctx_kernel,
            grid=(b,),
            in_specs=[pl.BlockSpec((1, t, c), lambda i: (i, 0, 0)), mu_spec],
            out_specs=pl.BlockSpec((1, t, c), lambda i: (i, 0, 0)),
            out_shape=jax.ShapeDtypeStruct((b, t, c), ACT),
            compiler_params=_cparams(("arbitrary",)),
            name="rw_shift_ctx",
        )(za, mu)
    tb = min(SHIFT_TB, t)
    nblk = t // tb
    rpb = tb // GRID_W
    nrow = t // GRID_W
    z4 = za.reshape(b, nrow, GRID_W, c)
    return pl.pallas_call(
        functools.partial(_shift_grid_kernel, nblk=nblk),
        grid=(b, nblk),
        in_specs=[
            pl.BlockSpec((1, tb, c), lambda i, j: (i, j, 0)),
            pl.BlockSpec((1, 1, GRID_W, c), lambda i, j: (i, jnp.maximum(j * rpb - 1, 0), 0, 0)),
            pl.BlockSpec((1, 1, GRID_W, c), lambda i, j: (i, jnp.minimum(j * rpb + rpb, nrow - 1), 0, 0)),
            mu_spec,
        ],
        out_specs=pl.BlockSpec((1, tb, c), lambda i, j: (i, j, 0)),
        out_shape=jax.ShapeDtypeStruct((b, t, c), ACT),
        compiler_params=_cparams(("arbitrary", "arbitrary")),
        name="rw_shift_grid",
    )(za, z4, z4, mu)


SCAN_TB = 512
PREP_SPLIT = 2
PREP_LOCKSTEP = 4
PREP_SKEW = 3
PREP_GROUP = 8


def _rw_scan_kernel(z_ref, s0_ref, wl_ref, lb_ref, kkp_ref, kap_ref, lng_ref, lnb_ref, rk_ref, g2_ref,
                    o_ref, st_ref,
                    y_ref, lw_s, a_s, b_s, kd_s, r2_s, mc_s, ds_s, gt_s, *, nchunk, nblk):
    i = pl.program_id(1)
    tb = nchunk * CHUNK

    @pl.when(i == 0)
    def _():
        st_ref[...] = s0_ref[...]
        y_ref[...] = jnp.zeros(y_ref.shape, F32)

    def y_rows(d, c):
        blk = i if d == 0 else nblk - 1 - i
        return pl.ds(pl.multiple_of((blk * nchunk + c) * CHUNK, CHUNK), CHUNK)

    def blk_rows(d):
        blk = i if d == 0 else nblk - 1 - i
        return pl.ds(pl.multiple_of(blk * tb, tb), tb)

    ones_blk = _block_mask(W_MIX, W_MIX, HEAD, HEAD)
    is_decay_lora = _iota((1, LANE), 1) < LANE // 2

    for d in range(N_DIR):
        k = z_ref[0, blk_rows(d), RW_K].astype(F32)
        la = z_ref[0, blk_rows(d), RW_LORA].astype(F32)
        la_t = jnp.where(is_decay_lora, jnp.tanh(la), la)
        w_cols = slice(W_MIX * d, W_MIX * (d + 1))
        a_cols = slice(W_MIX * (N_DIR + d), W_MIX * (N_DIR + d + 1))
        wraw = _dot(la_t, wl_ref[:, w_cols]) + lb_ref[:, w_cols]
        araw = _dot(la_t, wl_ref[:, a_cols]) + lb_ref[:, a_cols]
        lw_s[d] = jax.nn.sigmoid(wraw) * (-EXP_M_HALF)
        icl = jax.nn.sigmoid(araw)
        kkv = k * kkp_ref[...]
        ss = _dot_x2(kkv * kkv, ones_blk)
        kk = kkv * lax.rsqrt(ss + 1e-12)
        a_s[d] = -kk
        b_s[d] = kk * icl
        kd_s[d] = k * (1.0 + (icl - 1.0) * kap_ref[...])

    bm = _block_mask(N_HEADS * CHUNK, W_MIX, CHUNK, HEAD).astype(BF16)
    tri = [_tri(d, False) for d in range(N_DIR)]
    tw_strict = [_tri_wide(d, True) for d in range(N_DIR)]
    tw_incl = [_tri_wide(d, False) for d in range(N_DIR)]
    eye_w = (_iota((CHUNK, N_HEADS * CHUNK), 0) == _iota((CHUNK, N_HEADS * CHUNK), 1) % CHUNK).astype(F32)

    def rows_of(c, n):
        return pl.ds(pl.multiple_of(c * n, n), n)

    group = min(PREP_GROUP, nchunk)

    each = lambda fn, *ls: [fn(*xs) for xs in zip(*ls)]

    def prep_stages(ch):
        ds_ = [d for d, _ in ch]
        st = {}

        def front():
            load = lambda ref: [ref[d, rows_of(c, CHUNK), :] for d, c in ch]
            lw, a, st["b"], st["kd"] = load(lw_s), load(a_s), load(b_s), load(kd_s)
            r = [z_ref[0, y_rows(d, c), RW_R].astype(F32) for d, c in ch]
            st["v"] = [z_ref[0, y_rows(d, c), RW_V].astype(F32) for d, c in ch]
            cum = each(lambda d, x: _dot_left_x2(tri[d], x), ds_, lw)
            st["tot"] = each(lambda d, x: x[CHUNK - 1:CHUNK, :] if d == 0 else x[0:1, :], ds_, cum)
            st["g_inv"] = each(lambda x: jnp.exp(-x), cum)
            st["g_rem"] = each(lambda t_, x: jnp.exp(t_ - x), st["tot"], cum)
            st["at"] = each(lambda a_, x, l: a_ * jnp.exp(x - l), a, cum, lw)
            st["rt"] = each(lambda r_, x: r_ * jnp.exp(x), r, cum)

        def gram():
            x = each(lambda p, q: jnp.concatenate([p, q], axis=0), st["at"], st["rt"])
            gb = each(lambda x_, b_, g: _dot_nt(x_, _expand(b_ * g, bm)), x, st["b"], st["g_inv"])
            gk = each(lambda x_, k_, g: _dot_nt(x_, _expand(k_ * g, bm)), x, st["kd"], st["g_inv"])
            st["vbd"] = each(lambda v_: _expand(v_, bm), st["v"])
            st["apow"] = each(lambda d, g: g[0:CHUNK] * tw_strict[d], ds_, gb)
            st["av"] = each(lambda d, g, vb: _dot(g[0:CHUNK] * tw_strict[d], vb), ds_, gk, st["vbd"])
            st["tw"] = each(lambda m: eye_w + m, st["apow"])
            st["r_b"] = each(lambda d, g: g[CHUNK:] * tw_incl[d], ds_, gb)
            st["r_k"] = each(lambda d, g: g[CHUNK:] * tw_incl[d], ds_, gk)

        def double():
            st["apow"] = each(lambda m: _dot(m, _expand(m, bm)), st["apow"])
            st["tw"] = each(lambda t_, m: t_ + _dot(m, _expand(t_, bm)), st["tw"], st["apow"])

        def apply():
            st["a2"] = each(lambda t_, m: _dot(t_, _expand(m, bm)), st["tw"], st["at"])
            st["u0"] = each(lambda t_, m: _dot(t_, _expand(m, bm)), st["tw"], st["av"])

        def finish():
            r2 = each(lambda rt_, rb, m: rt_ + _dot(rb, _expand(m, bm)), st["rt"], st["r_b"], st["a2"])
            y0 = each(lambda rb, u, rk, vb: _dot(rb, _expand(u, bm)) + _dot(rk, vb),
                      st["r_b"], st["u0"], st["r_k"], st["vbd"])
            bl = each(lambda b_, g: b_ * g, st["b"], st["g_rem"])
            kl = each(lambda k_, g: k_ * g, st["kd"], st["g_rem"])
            mc = each(lambda m, bl_: _dot_tn(m, bl_) * ones_blk, st["a2"], bl)
            ds0 = each(lambda u, v_, bl_, kl_: _dot_tn(jnp.concatenate([u, v_], axis=0),
                                                       jnp.concatenate([bl_, kl_], axis=0)) * ones_blk,
                       st["u0"], st["v"], bl, kl)
            for n, (d, c) in enumerate(ch):
                rows = rows_of(c, CHUNK)
                y_ref[0, y_rows(d, c), :] += y0[n]
                r2_s[d, rows, :] = r2[n].astype(BF16)
                mc_s[d, rows_of(c, W_MIX), :] = mc[n].astype(BF16)
                ds_s[d, rows_of(c, W_MIX), :] = ds0[n]
                gt_s[d, rows_of(c, SUBLANE), :] = jnp.broadcast_to(jnp.exp(st["tot"][n]), (SUBLANE, W_MIX))

        return [front, gram] + [double] * 5 + [apply, finish]

    def prep_body(gi, carry):
        half = group // PREP_SPLIT if group >= PREP_SPLIT * PREP_LOCKSTEP else group
        halves = [prep_stages([(d, gi * group + j) for j in range(h0, min(h0 + half, group)) for d in range(N_DIR)])
                  for h0 in range(0, group, half)]
        n_stage = len(halves[0])
        for step in range(n_stage + PREP_SKEW * (len(halves) - 1)):
            for hi, stages in enumerate(halves):
                k = step - PREP_SKEW * hi
                if 0 <= k < n_stage:
                    stages[k]()
        return carry

    lax.fori_loop(0, nchunk // group, prep_body, 0)

    def scan_body(cc, carry):
        cs = [cc, nchunk - 1 - cc]
        s = [st_ref[0, d] for d in range(N_DIR)]
        sb = [x.astype(BF16) for x in s]
        upd = [_dot(sb[d], mc_s[d, rows_of(cs[d], W_MIX), :]) for d in range(N_DIR)]
        ys = [_dot_nt(r2_s[d, rows_of(cs[d], CHUNK), :], sb[d]) for d in range(N_DIR)]
        for d in range(N_DIR):
            st_ref[0, d] = (s[d] * gt_s[d, rows_of(cs[d], SUBLANE), :][0:1, :] + upd[d]
                            + ds_s[d, rows_of(cs[d], W_MIX), :])
            y_ref[0, y_rows(d, cs[d]), :] += ys[d]
        return carry

    lax.fori_loop(0, nchunk, scan_body, 0)

    @pl.when(i == nblk - 1)
    def _():
        def epilogue(j, carry):
            rows = pl.ds(pl.multiple_of(j * tb, tb), tb)
            y = y_ref[0, rows, :]
            mu = _dot_x2(y, ones_blk) * (1.0 / HEAD)
            yc = y - mu
            var = _dot_x2(yc * yc, ones_blk) * (1.0 / HEAD)
            yn = yc * lax.rsqrt(var + RW_LN_EPS) * lng_ref[...] + lnb_ref[...]
            r = z_ref[0, rows, RW_R].astype(F32)
            k = z_ref[0, rows, RW_K].astype(F32)
            v = z_ref[0, rows, RW_V].astype(F32)
            gl = z_ref[0, rows, RW_GATE].astype(F32)
            bonus = _dot_x2(r * k * rk_ref[...], ones_blk) * v
            gate = _dot(jax.nn.sigmoid(gl), g2_ref[...])
            o_ref[0, rows, :] = ((yn + bonus) * gate).astype(ACT)
            return carry

        lax.fori_loop(0, nblk, epilogue, 0)


def _rw_scan(zs, s0, p):
    b, t, c = zs.shape
    tb = min(SCAN_TB, t)
    nblk = t // tb
    nchunk = tb // CHUNK
    vec = lambda n: pl.BlockSpec((1, n), lambda i, j: (0, 0))
    seq = lambda w_: pl.BlockSpec((1, t, w_), lambda i, j: (i, 0, 0))
    st_spec = pl.BlockSpec((1, N_DIR, W_MIX, W_MIX), lambda i, j: (i, 0, 0, 0))
    return pl.pallas_call(
        functools.partial(_rw_scan_kernel, nchunk=nchunk, nblk=nblk),
        grid=(b, nblk),
        in_specs=[
            seq(c), st_spec,
            pl.BlockSpec((LANE, 2 * N_DIR * W_MIX), lambda i, j: (0, 0)),
            vec(1024), vec(W_MIX), vec(W_MIX), vec(W_MIX), vec(W_MIX), vec(W_MIX),
            pl.BlockSpec((LANE, W_MIX), lambda i, j: (0, 0)),
        ],
        out_specs=[seq(W_MIX), st_spec],
        out_shape=[
            jax.ShapeDtypeStruct((b, t, W_MIX), ACT),
            jax.ShapeDtypeStruct((b, N_DIR, W_MIX, W_MIX), F32),
        ],
        scratch_shapes=[pltpu.VMEM((1, t, W_MIX), F32)]
        + [pltpu.VMEM((N_DIR, tb, W_MIX), F32) for _ in range(4)] + [
            pltpu.VMEM((N_DIR, tb, W_MIX), BF16),
            pltpu.VMEM((N_DIR, nchunk * W_MIX, W_MIX), BF16),
            pltpu.VMEM((N_DIR, nchunk * W_MIX, W_MIX), F32),
            pltpu.VMEM((N_DIR, nchunk * 8, W_MIX), F32),
        ],
        compiler_params=_cparams(("arbitrary", "arbitrary")),
        name="rw_scan",
    )(zs, s0, p["rw_wl"], p["rw_lb"], p["rw_kk"], p["rw_ka"], p["rw_ln_g"], p["rw_ln_b"], p["rw_rk"], p["rw_g2"])


def _rwkv_mixer(za, s0_bd, p, latent):
    zs = _rw_shift(za, p["rw_mu"], latent)
    return _rw_scan(zs, s0_bd, p)


def _gla_scan_kernel(z_ref, s0_ref, wg_ref, gb_ref, lng_ref, o_ref, st_ref,
                     y_ref, la_s, qe_s, ds_s, dec_s, sp_s, *, nchunk, nblk):
    i = pl.program_id(1)
    tb = nchunk * CHUNK

    @pl.when(i == 0)
    def _():
        st_ref[...] = s0_ref[...]
        y_ref[...] = jnp.zeros(y_ref.shape, F32)

    def y_rows(d, c):
        blk = i if d == 0 else nblk - 1 - i
        return pl.ds(pl.multiple_of((blk * nchunk + c) * CHUNK, CHUNK), CHUNK)

    for d in range(N_DIR):
        blk = i if d == 0 else nblk - 1 - i
        logit = _dot(z_ref[0, pl.ds(pl.multiple_of(blk * tb, tb), tb), GLA_GATE], wg_ref[d]) + gb_ref[d]
        la_s[d] = jax.nn.log_sigmoid(logit) * (1.0 / GLA_GATE_NORM)

    bm_k = _block_mask(N_HEADS * CHUNK, GLA_HK, CHUNK, GLA_DK).astype(BF16)
    bm_v = _block_mask(N_HEADS * CHUNK, W_MIX, CHUNK, HEAD).astype(BF16)
    bm_s = _block_mask(W_MIX, GLA_HK, HEAD, GLA_DK)
    tri = [_tri(d, False) for d in range(N_DIR)]
    tw_incl = [_tri_wide(d, False) for d in range(N_DIR)]

    def rows_of(c, n):
        return pl.ds(pl.multiple_of(c * n, n), n)

    group = min(PREP_GROUP, nchunk)

    def prep_body(gi, carry):
        ch = [(d, gi * group + j) for j in range(group) for d in range(N_DIR)]
        each = lambda fn, *ls: [fn(*xs) for xs in zip(*ls)]
        ds_ = [d for d, _ in ch]
        la = [la_s[d, rows_of(c, CHUNK), :] for d, c in ch]
        q = [z_ref[0, y_rows(d, c), GLA_Q].astype(F32) * (GLA_DK ** -0.5) for d, c in ch]
        k = [z_ref[0, y_rows(d, c), GLA_K].astype(F32) for d, c in ch]
        v = [z_ref[0, y_rows(d, c), GLA_V].astype(F32) for d, c in ch]
        cum = each(lambda d, x: _dot_left_x2(tri[d], x), ds_, la)
        last = each(lambda d, x: x[CHUNK - 1:CHUNK, :] if d == 0 else x[0:1, :], ds_, cum)
        qe = each(lambda q_, x: q_ * jnp.exp(x), q, cum)
        ke = each(lambda k_, x: k_ * jnp.exp(-x), k, cum)
        kl = each(lambda k_, l, x: k_ * jnp.exp(l - x), k, last, cum)
        att = each(lambda d, q_, k_: _dot_nt(q_, _expand(k_, bm_k)) * tw_incl[d], ds_, qe, ke)
        o = each(lambda a_, v_: _dot(a_, _expand(v_, bm_v)), att, v)
        dst = each(lambda v_, k_: _dot_tn(v_, k_) * bm_s, v, kl)
        for n, (d, c) in enumerate(ch):
            y_ref[0, y_rows(d, c), :] += o[n]
            qe_s[d, rows_of(c, CHUNK), :] = qe[n].astype(BF16)
            ds_s[d, rows_of(c, W_MIX), :] = dst[n]
            dec_s[d, rows_of(c, SUBLANE), :] = jnp.broadcast_to(jnp.exp(last[n]), (SUBLANE, GLA_HK))
        return carry

    lax.fori_loop(0, nchunk // group, prep_body, 0)

    def state_body(cc, carry):
        cs = [cc, nchunk - 1 - cc]
        for d in range(N_DIR):
            s = st_ref[0, d]
            sp_s[d, rows_of(cs[d], W_MIX), :] = s.astype(BF16)
            st_ref[0, d] = s * dec_s[d, rows_of(cs[d], SUBLANE), :][0:1, :] + ds_s[d, rows_of(cs[d], W_MIX), :]
        return carry

    lax.fori_loop(0, nchunk, state_body, 0)

    def out_body(gi, carry):
        ch = [(d, gi * group + j) for j in range(group) for d in range(N_DIR)]
        ys = [_dot_nt(qe_s[d, rows_of(c, CHUNK), :], sp_s[d, rows_of(c, W_MIX), :]) for d, c in ch]
        for n, (d, c) in enumerate(ch):
            y_ref[0, y_rows(d, c), :] += ys[n]
        return carry

    lax.fori_loop(0, nchunk // group, out_body, 0)

    @pl.when(i == nblk - 1)
    def _():
        ones_blk = _block_mask(W_MIX, W_MIX, HEAD, HEAD)

        def epilogue(j, carry):
            rows = pl.ds(pl.multiple_of(j * tb, tb), tb)
            y = y_ref[0, rows, :]
            ms = _dot_x2(y * y, ones_blk) * (1.0 / HEAD)
            y = y * lax.rsqrt(ms + NORM_EPS) * lng_ref[...]
            og = z_ref[0, rows, GLA_OG].astype(F32)
            o_ref[0, rows, :] = (y * (og * jax.nn.sigmoid(og))).astype(ACT)
            return carry

        lax.fori_loop(0, nblk, epilogue, 0)


def _gla_mixer(zb, s0, p):
    b, t, c = zb.shape
    tb = min(SCAN_TB, t)
    nblk = t // tb
    seq = lambda w_: pl.BlockSpec((1, t, w_), lambda i, j: (i, 0, 0))
    st_spec = pl.BlockSpec((1, N_DIR, W_MIX, GLA_HK), lambda i, j: (i, 0, 0, 0))
    nchunk = tb // CHUNK
    return pl.pallas_call(
        functools.partial(_gla_scan_kernel, nchunk=nchunk, nblk=nblk),
        grid=(b, nblk),
        in_specs=[
            seq(c), st_spec,
            pl.BlockSpec((N_DIR, LANE, GLA_HK), lambda i, j: (0, 0, 0)),
            pl.BlockSpec((N_DIR, 1, GLA_HK), lambda i, j: (0, 0, 0)),
            pl.BlockSpec((1, W_MIX), lambda i, j: (0, 0)),
        ],
        out_specs=[seq(W_MIX), st_spec],
        out_shape=[
            jax.ShapeDtypeStruct((b, t, W_MIX), ACT),
            jax.ShapeDtypeStruct((b, N_DIR, W_MIX, GLA_HK), F32),
        ],
        scratch_shapes=[
            pltpu.VMEM((1, t, W_MIX), F32),
            pltpu.VMEM((N_DIR, tb, GLA_HK), F32),
            pltpu.VMEM((N_DIR, tb, GLA_HK), BF16),
            pltpu.VMEM((N_DIR, nchunk * W_MIX, GLA_HK), F32),
            pltpu.VMEM((N_DIR, nchunk * SUBLANE, GLA_HK), F32),
            pltpu.VMEM((N_DIR, nchunk * W_MIX, GLA_HK), BF16),
        ],
        compiler_params=_cparams(("arbitrary", "arbitrary")),
        name="gla_scan",
    )(zb, s0, p["gla_wg"], p["gla_gb"], p["gla_ln_g"])


LRU_TT = 256
LRU_HALO = 8
LRU_SUB = 8


def _lru_kernel(z_ref, h0_ref, cw_ref, cb_ref, wax_ref, bax_ref, lam_ref, y_ref, hf_ref, xpad, hfwd, *, t):
    tt = min(LRU_TT, t)
    ntile = t // tt
    xpad[0:LRU_HALO, :] = jnp.zeros((LRU_HALO, W_MIX), F32)
    xpad[LRU_HALO + t:2 * LRU_HALO + t, :] = jnp.zeros((LRU_HALO, W_MIX), F32)

    def fill(j, carry):
        base = pl.multiple_of(j * tt, tt)
        xpad[pl.ds(base + LRU_HALO, tt), :] = z_ref[0, pl.ds(base, tt), 0:W_MIX].astype(F32)
        return carry

    lax.fori_loop(0, ntile, fill, 0)
    sub = _iota((tt // LRU_SUB, LRU_SUB, W_MIX), 1)

    def tile_scan(j, h, d):
        base = pl.multiple_of(j * tt, tt)
        win = xpad[pl.ds(base, tt + 2 * LRU_HALO), :]
        xc = jnp.zeros((tt, W_MIX), F32) + cb_ref[d]
        for tap in range(LRU_CONV):
            off = LRU_HALO - (LRU_CONV - 1) + tap if d == 0 else LRU_HALO + (LRU_CONV - 1) - tap
            xc = xc + cw_ref[d, tap:tap + 1, :] * win[off:off + tt, :]
        g = _dot(xc, wax_ref[d]) + bax_ref[d]
        gr = jax.nn.sigmoid(g[:, 0:W_MIX])
        gi = jax.nn.sigmoid(g[:, W_MIX:2 * W_MIX])
        log_a = -LRU_C * gr * jax.nn.softplus(-lam_ref[d])
        a = jnp.exp(log_a)
        bv = jnp.sqrt(1.0 - jnp.exp(2.0 * log_a)) * gi * xc
        ngrp = tt // LRU_SUB
        a = a.reshape(ngrp, LRU_SUB, W_MIX)
        bv = bv.reshape(ngrp, LRU_SUB, W_MIX)
        s = 1
        while s < LRU_SUB:
            shift = s if d == 0 else LRU_SUB - s
            keep = (sub >= s) if d == 0 else (sub < LRU_SUB - s)
            a_sh = jnp.where(keep, pltpu.roll(a, shift, 1), 1.0)
            b_sh = jnp.where(keep, pltpu.roll(bv, shift, 1), 0.0)
            bv = a * b_sh + bv
            a = a * a_sh
            s *= 2
        out = [None] * ngrp
        for gg in range(ngrp):
            g = gg if d == 0 else ngrp - 1 - gg
            hg = a[g] * h + bv[g]
            out[g] = hg
            h = hg[LRU_SUB - 1:LRU_SUB, :] if d == 0 else hg[0:1, :]
        return jnp.concatenate(out, axis=0), h, base

    def fwd(j, h):
        ht, h, base = tile_scan(j, h, 0)
        hfwd[pl.ds(base, tt), :] = ht
        return h

    h_end = lax.fori_loop(0, ntile, fwd, h0_ref[0, 0:1, :])
    hf_ref[0, 0:1, :] = h_end

    def bwd(jj, h):
        j = ntile - 1 - jj
        ht, h, base = tile_scan(j, h, 1)
        gb = z_ref[0, pl.ds(base, tt), W_MIX:2 * W_MIX].astype(F32)
        y_ref[0, pl.ds(base, tt), :] = ((hfwd[pl.ds(base, tt), :] + ht) * jax.nn.gelu(gb)).astype(ACT)
        return h

    h_end = lax.fori_loop(0, ntile, bwd, h0_ref[0, 1:2, :])
    hf_ref[0, 1:2, :] = h_end


def _lru_mixer(zc, h0, p):
    b, t, c = zc.shape
    full = lambda *s: pl.BlockSpec(s, lambda i: (0,) * len(s))
    return pl.pallas_call(
        functools.partial(_lru_kernel, t=t),
        grid=(b,),
        in_specs=[
            pl.BlockSpec((1, t, c), lambda i: (i, 0, 0)),
            pl.BlockSpec((1, N_DIR, W_MIX), lambda i: (i, 0, 0)),
            full(N_DIR, LRU_CONV, W_MIX), full(N_DIR, 1, W_MIX),
            full(N_DIR, W_MIX, 2 * W_MIX), full(N_DIR, 1, 2 * W_MIX), full(N_DIR, 1, W_MIX),
        ],
        out_specs=[pl.BlockSpec((1, t, W_MIX), lambda i: (i, 0, 0)),
                   pl.BlockSpec((1, N_DIR, W_MIX), lambda i: (i, 0, 0))],
        out_shape=[jax.ShapeDtypeStruct((b, t, W_MIX), ACT), jax.ShapeDtypeStruct((b, N_DIR, W_MIX), F32)],
        scratch_shapes=[pltpu.VMEM((t + 2 * LRU_HALO, W_MIX), F32), pltpu.VMEM((t, W_MIX), F32)],
        compiler_params=_cparams(("arbitrary",)),
        name="lru",
    )(zc, h0, p["lru_cw"], p["lru_cb"], p["lru_wax"], p["lru_bax"], p["lru_lam"])


CV_TT = 256
CV_WIN = 16


def _conv_kernel(z_ref, dw_ref, dwb_ref, lng_ref, lnb_ref, pw_ref, pwb_ref, y_ref, upad, *, t, latent):
    tt = min(CV_TT, t)
    ntile = t // tt
    pad = CV_PAD * GRID_W if latent else CV_WIN
    upad[0:pad, :] = jnp.zeros((pad, W_MIX), F32)
    upad[pad + t:2 * pad + t, :] = jnp.zeros((pad, W_MIX), F32)

    def fill(j, carry):
        base = pl.multiple_of(j * tt, tt)
        z = z_ref[0, pl.ds(base, tt), :].astype(F32)
        upad[pl.ds(base + pad, tt), :] = z[:, 0:W_MIX] * jax.nn.sigmoid(z[:, W_MIX:2 * W_MIX])
        return carry

    lax.fori_loop(0, ntile, fill, 0)

    def tile(j, carry):
        base = pl.multiple_of(j * tt, tt)
        if latent:
            half = W_MIX // 2
            col = _iota((tt, half), 0) % GRID_W
            win = upad[pl.ds(pl.multiple_of(base + pad - CV_WIN, CV_WIN), tt + 2 * CV_WIN), 0:half]
            accw = jnp.zeros((tt, half), F32)
            acch = jnp.zeros((tt, half), F32)
            for tap in range(CV_KERNEL):
                dlt = tap - CV_PAD
                x = win[CV_WIN + dlt:CV_WIN + dlt + tt, :]
                if dlt < 0:
                    x = jnp.where(col >= -dlt, x, 0.0)
                elif dlt > 0:
                    x = jnp.where(col < GRID_W - dlt, x, 0.0)
                accw = accw + dw_ref[tap:tap + 1, 0:half] * x
                rows = pl.ds(pl.multiple_of(base + pad + dlt * GRID_W, GRID_W), tt)
                acch = acch + dw_ref[tap:tap + 1, half:W_MIX] * upad[rows, half:W_MIX]
            u = jnp.concatenate([accw, acch], axis=1)
        else:
            win = upad[pl.ds(base + pad - CV_WIN, tt + 2 * CV_WIN), :]
            u = jnp.zeros((tt, W_MIX), F32)
            for tap in range(CV_KERNEL):
                dlt = tap - CV_PAD
                u = u + dw_ref[tap:tap + 1, :] * win[CV_WIN + dlt:CV_WIN + dlt + tt, :]
        u = u + dwb_ref[...]
        mu = jnp.mean(u, axis=-1, keepdims=True)
        uc = u - mu
        var = jnp.mean(uc * uc, axis=-1, keepdims=True)
        un = uc * lax.rsqrt(var + 1e-5) * lng_ref[...] + lnb_ref[...]
        un = un * jax.nn.sigmoid(un)
        y_ref[0, pl.ds(base, tt), :] = (_dot(un, pw_ref[...]) + pwb_ref[...]).astype(ACT)
        return carry

    lax.fori_loop(0, ntile, tile, 0)


def _conv_mixer(zd, p, latent):
    b, t, c = zd.shape
    pad = CV_PAD * GRID_W if latent else CV_WIN
    full = lambda *s: pl.BlockSpec(s, lambda i: (0,) * len(s))
    vec = full(1, W_MIX)
    return pl.pallas_call(
        functools.partial(_conv_kernel, t=t, latent=latent),
        grid=(b,),
        in_specs=[pl.BlockSpec((1, t, c), lambda i: (i, 0, 0)),
                  full(CV_KERNEL, W_MIX), vec, vec, vec, full(W_MIX, W_MIX), vec],
        out_specs=pl.BlockSpec((1, t, W_MIX), lambda i: (i, 0, 0)),
        out_shape=jax.ShapeDtypeStruct((b, t, W_MIX), ACT),
        scratch_shapes=[pltpu.VMEM((t + 2 * pad, W_MIX), F32)],
        compiler_params=_cparams(("arbitrary",)),
        name="conv_grid" if latent else "conv_ctx",
    )(zd, p["cv_dw_w"], p["cv_dw_b"], p["cv_ln_g"], p["cv_ln_b"], p["cv_pw_w"], p["cv_pw_b"])


MOE_TM = 512
MOE_RB = 128
E_PAD = LANE


def _route(sel, scores):
    grp = []
    for g in range(N_GROUPS):
        s = sel[GROUP_SIZE * g:GROUP_SIZE * (g + 1)]
        best_pair = None
        for i in range(GROUP_SIZE):
            for j in range(i + 1, GROUP_SIZE):
                pair = s[i] + s[j]
                best_pair = pair if best_pair is None else jnp.maximum(best_pair, pair)
        grp.append(best_pair)
    best = jnp.zeros_like(grp[0], dtype=jnp.int32)
    top = grp[0]
    for g in range(1, N_GROUPS):
        better = grp[g] > top
        best = jnp.where(better, g, best)
        top = jnp.where(better, grp[g], top)
    neg = jnp.full_like(sel[0], -jnp.inf)
    msel = [jnp.where(best == e // GROUP_SIZE, sel[e], neg) for e in range(N_EXPERTS)]
    picks = []
    for _ in range(2):
        idx = jnp.zeros_like(best)
        top = msel[0]
        for e in range(1, N_EXPERTS):
            better = msel[e] > top
            idx = jnp.where(better, e, idx)
            top = jnp.where(better, msel[e], top)
        picks.append(idx)
        msel = [jnp.where(idx == e, neg, msel[e]) for e in range(N_EXPERTS)]
    chosen = [jnp.where((picks[0] == e) | (picks[1] == e), scores[e], 0.0) for e in range(N_EXPERTS)]
    total = chosen[0]
    for e in range(1, N_EXPERTS):
        total = total + chosen[e]
    return [ch / total for ch in chosen], best


def _moe_kernel(x_ref, ya_ref, yb_ref, yc_ref, yd_ref, wo_ref, g1_ref, sh2_ref, sc2_ref, g2_ref, n2_ref,
                wr_ref, br_ref, w1_ref, w3_ref, w2_ref, nf_ref, o_ref,
                x1_s, h2_s, gt_s, gate_s, acc_s, he_s, og_s, pmt_s, seg_s, *, final_norm):
    grp = pl.program_id(2)
    tm = x1_s.shape[0]

    @pl.when(grp == 0)
    def _():
        y = sum(_dot(y_ref[0], wo_ref[W_MIX * m:W_MIX * (m + 1), :])
                for m, y_ref in enumerate((ya_ref, yb_ref, yc_ref, yd_ref)))
        x1 = x_ref[0] + g1_ref[0] * y
        x1_s[...] = x1
        h2 = x1 * lax.rsqrt(jnp.mean(x1 * x1, axis=-1, keepdims=True) + NORM_EPS) * n2_ref[...]
        h2 = h2 * (1.0 + sc2_ref[0]) + sh2_ref[0]
        logits = _dot3_nt(wr_ref[...], h2)
        scores = jax.nn.sigmoid(logits)
        selm = scores + br_ref[...]
        gates, best = _route([selm[i:i + 1, :] for i in range(N_EXPERTS)],
                             [scores[i:i + 1, :] for i in range(N_EXPERTS)])
        gt_s[...] = jnp.zeros(gt_s.shape, F32)
        for i in range(N_EXPERTS):
            gt_s[i:i + 1, :] = gates[i]

        og = [jnp.where(best == g, 1.0, 0.0) for g in range(N_GROUPS)]
        og_s[...] = jnp.zeros(og_s.shape, F32)
        start = jnp.int32(0)
        starts = []
        for g in range(N_GROUPS):
            og_s[g:g + 1, :] = og[g]
            starts.append(start)
            seg_s[g] = start
            start = start + jnp.sum(og[g]).astype(jnp.int32)
            seg_s[N_GROUPS + g] = start
        before = jnp.where(_iota((tm, tm), 0) < _iota((tm, tm), 1), 1.0, 0.0).astype(BF16)
        rank = jnp.dot(og_s[...].astype(BF16), before, preferred_element_type=F32)
        pos = og[0] * (starts[0].astype(F32) + rank[0:1, :])
        for g in range(1, N_GROUPS):
            pos = pos + og[g] * (starts[g].astype(F32) + rank[g:g + 1, :])
        gt_s[N_EXPERTS:N_EXPERTS + 1, :] = pos
        gate_tok = gt_s[...].T
        slot_l = _iota((tm, tm), 1).astype(F32)
        slot_s = _iota((tm, tm), 0).astype(F32)
        pmt_s[...] = jnp.where(slot_l == gate_tok[:, N_EXPERTS:N_EXPERTS + 1], 1.0, 0.0).astype(BF16)
        pm = jnp.where(slot_s == pos, 1.0, 0.0).astype(BF16)
        h2_s[...] = jnp.dot(pm, h2.astype(BF16), preferred_element_type=F32).astype(BF16)
        gate_s[...] = _dot_left_x2(pm, gate_tok)
        acc_s[...] = jnp.zeros(acc_s.shape, F32)

    pick = (_iota((E_PAD, GROUP_SIZE * LANE), 0)
            == GROUP_SIZE * grp + _iota((E_PAD, GROUP_SIZE * LANE), 1) // LANE).astype(F32)

    def block(rb, carry):
        rows = pl.ds(pl.multiple_of(rb * MOE_RB, MOE_RB), MOE_RB)
        gsel = _dot_x2(gate_s[rows, :], pick)
        h2 = h2_s[rows, :]
        for j in range(GROUP_SIZE):
            he = jnp.dot(h2, w1_ref[j], preferred_element_type=F32)
            he = he * jax.nn.sigmoid(he) * jnp.dot(h2, w3_ref[j], preferred_element_type=F32)
            g = gsel[:, LANE * j:LANE * (j + 1)]
            he_s[rows, D_EXPERT * j:D_EXPERT * (j + 1)] = (he * jnp.concatenate([g, g], axis=1)).astype(BF16)
        w2g = w2_ref[...].reshape(GROUP_SIZE * D_EXPERT, D_MODEL)
        acc_s[rows, :] += jnp.dot(he_s[rows, :], w2g, preferred_element_type=F32)
        return carry

    first = seg_s[grp] // MOE_RB
    last = (seg_s[N_GROUPS + grp] + (MOE_RB - 1)) // MOE_RB
    lax.fori_loop(first, last, block, 0)

    @pl.when(grp == N_GROUPS - 1)
    def _():
        x2 = x1_s[...] + g2_ref[0] * _dot_left_x2(pmt_s[...], acc_s[...])
        if final_norm:
            x2 = x2 * lax.rsqrt(jnp.mean(x2 * x2, axis=-1, keepdims=True) + NORM_EPS) * nf_ref[...]
        o_ref[0] = x2


def _out_moe(x, ys, m, p, norm_f, final_norm):
    b, t, _ = x.shape
    tm = min(MOE_TM, t)
    tok = lambda w_: pl.BlockSpec((1, tm, w_), lambda i, j, e: (i, j, 0))
    mod = lambda k: pl.BlockSpec((1, 1, D_MODEL), lambda i, j, e, k=k: (i, 0, k))
    full = lambda *s: pl.BlockSpec(s, lambda i, j, e: (0,) * len(s))
    return pl.pallas_call(
        functools.partial(_moe_kernel, final_norm=final_norm),
        grid=(b, t // tm, N_GROUPS),
        in_specs=[tok(D_MODEL), tok(W_MIX), tok(W_MIX), tok(W_MIX), tok(W_MIX),
                  full(D_MODEL, D_MODEL), mod(2), mod(3), mod(4), mod(5), full(1, D_MODEL),
                  full(N_EXPERTS, D_MODEL), full(N_EXPERTS, 1),
                  pl.BlockSpec((GROUP_SIZE, D_MODEL, D_EXPERT), lambda i, j, e: (e, 0, 0)),
                  pl.BlockSpec((GROUP_SIZE, D_MODEL, D_EXPERT), lambda i, j, e: (e, 0, 0)),
                  pl.BlockSpec((GROUP_SIZE, D_EXPERT, D_MODEL), lambda i, j, e: (e, 0, 0)),
                  full(1, D_MODEL)],
        out_specs=tok(D_MODEL),
        out_shape=jax.ShapeDtypeStruct((b, t, D_MODEL), F32),
        scratch_shapes=[pltpu.VMEM((tm, D_MODEL), F32), pltpu.VMEM((tm, D_MODEL), BF16),
                        pltpu.VMEM((E_PAD, tm), F32), pltpu.VMEM((tm, E_PAD), F32),
                        pltpu.VMEM((tm, D_MODEL), F32), pltpu.VMEM((tm, D_EXPERT * GROUP_SIZE), BF16),
                        pltpu.VMEM((SUBLANE, tm), F32), pltpu.VMEM((tm, tm), BF16),
                        pltpu.SMEM((2 * N_GROUPS,), jnp.int32)],
        compiler_params=_cparams(("arbitrary", "arbitrary", "arbitrary")),
        name="out_moe",
    )(x, *ys, p["w_out"], m, m, m, m, p["norm2_g"], p["w_router_t"], p["b_router"], p["e_w1"], p["e_w3"], p["e_w2"],
      norm_f)


def _gla_pack(z):
    hk = GLA_HK
    lead = z.shape[:-1]
    gl = z[..., 2 * hk + W_MIX:2 * hk + W_MIX + 32]
    return jnp.concatenate([z[..., 0:2 * hk + W_MIX], z[..., 2 * hk + W_MIX + 32:], gl,
                            jnp.zeros(lead + (96,), z.dtype)], -1)


def _gla_state_in(s):
    b = s.shape[0]
    eye = jnp.eye(N_HEADS, dtype=s.dtype)
    return jnp.einsum("bdhkv,hg->bdhvgk", s, eye).reshape(b, N_DIR, W_MIX, GLA_HK)


def _gla_state_out(st):
    b = st.shape[0]
    eye = jnp.eye(N_HEADS, dtype=st.dtype)
    return jnp.einsum("bdhvgk,hg->bdhkv", st.reshape(b, N_DIR, N_HEADS, HEAD, N_HEADS, GLA_DK), eye)


def _rw_state_in(s):
    b = s.shape[0]
    eye = jnp.eye(N_HEADS, dtype=s.dtype)
    return jnp.einsum("bdhvk,hg->bdhvgk", s, eye).reshape(b, N_DIR, W_MIX, W_MIX)


def _rw_state_out(st):
    b = st.shape[0]
    eye = jnp.eye(N_HEADS, dtype=st.dtype)
    return jnp.einsum("bdhvgk,hg->bdhvk", st.reshape(b, N_DIR, N_HEADS, HEAD, N_HEADS, HEAD), eye)


def _prep_params(d):
    L = DEPTH
    z = lambda *s: jnp.zeros(s, F32)
    out = {}
    out["rw_mu"] = jnp.concatenate([d["rw_mu"], z(L, ZA_W - d["rw_mu"].shape[-1])], -1).reshape(L, 1, ZA_W)
    w2, a2 = d["rw_w2"], d["rw_a2"]
    zz = z(L, 32, 256)
    rows = [
        jnp.concatenate([w2[:, 0], zz, zz, zz], -1),
        jnp.concatenate([zz, w2[:, 1], zz, zz], -1),
        jnp.concatenate([zz, zz, a2[:, 0], zz], -1),
        jnp.concatenate([zz, zz, zz, a2[:, 1]], -1),
    ]
    out["rw_wl"] = jnp.concatenate(rows, 1).astype(BF16)
    out["rw_lb"] = jnp.concatenate([d["rw_w0"][:, 0], d["rw_w0"][:, 1], d["rw_a0"][:, 0], d["rw_a0"][:, 1]],
                                   -1).reshape(L, 1, 1024)
    for n in ("rw_kk", "rw_ka", "rw_ln_g", "rw_ln_b"):
        out[n] = d[n].reshape(L, 1, W_MIX)
    out["rw_rk"] = d["rw_rk"].reshape(L, 1, W_MIX)
    out["rw_g2"] = jnp.concatenate([d["rw_g2"], z(L, 64, W_MIX)], 1).astype(BF16)

    gk2 = d["gla_gk2"]
    z16, z96 = z(L, 16, GLA_HK), z(L, 96, GLA_HK)
    out["gla_wg"] = jnp.stack([jnp.concatenate([gk2[:, 0], z16, z96], 1),
                               jnp.concatenate([z16, gk2[:, 1], z96], 1)], 1).astype(BF16)
    out["gla_gb"] = d["gla_gkb"].reshape(L, N_DIR, 1, GLA_HK)
    out["gla_ln_g"] = jnp.tile(d["gla_ln_g"], (1, N_HEADS)).reshape(L, 1, W_MIX)

    eye = jnp.eye(4, dtype=F32)
    bd = lambda w: jnp.einsum("ldgij,gh->ldgihj", w, eye).reshape(L, N_DIR, W_MIX, W_MIX)
    out["lru_cw"] = d["lru_conv_w"]
    out["lru_cb"] = d["lru_conv_b"].reshape(L, N_DIR, 1, W_MIX)
    out["lru_wax"] = jnp.concatenate([bd(d["lru_wa"]), bd(d["lru_wx"])], -1).astype(BF16)
    out["lru_bax"] = jnp.concatenate([d["lru_ba"], d["lru_bx"]], -1).reshape(L, N_DIR, 1, 2 * W_MIX)
    out["lru_lam"] = d["lru_lam"].reshape(L, N_DIR, 1, W_MIX)

    out["cv_dw_w"] = d["cv_dw_w"]
    for n in ("cv_dw_b", "cv_ln_g", "cv_ln_b", "cv_pw_b"):
        out[n] = d[n].reshape(L, 1, W_MIX)
    out["cv_pw_w"] = d["cv_pw_w"].astype(BF16)

    w_in = d["w_in"]
    o1 = RW_COLS
    o2 = o1 + GLA_COLS
    o3 = o2 + ZC_W
    out["w_in"] = jnp.concatenate([w_in[..., 0:o1], z(L, D_MODEL, ZA_W - o1), _gla_pack(w_in[..., o1:o2]),
                                   w_in[..., o2:o3], w_in[..., o3:]], -1).astype(BF16)
    out["w_out"] = d["w_out"].astype(BF16)
    out["norm1_g"] = d["norm1_g"].reshape(L, 1, D_MODEL)
    out["norm2_g"] = d["norm2_g"].reshape(L, 1, D_MODEL)
    for n in ("e_w1", "e_w3", "e_w2"):
        out[n] = d[n].astype(BF16)
    return out


def kernel(x_prompt, x_sample, state_rwkv, state_gla, state_lru, c, c_ctx, norm1_g, norm2_g, norm_f_g, w_ada, b_ada, w_in, w_out, rw_mu, rw_w0, rw_w2, rw_a0, rw_a2, rw_g2, rw_kk, rw_ka, rw_rk, rw_ln_g, rw_ln_b, gla_gk2, gla_gkb, gla_ln_g, lru_conv_w, lru_conv_b, lru_wa, lru_ba, lru_wx, lru_bx, lru_lam, cv_dw_w, cv_dw_b, cv_ln_g, cv_ln_b, cv_pw_w, cv_pw_b, w_router, b_router, e_w1, e_w3, e_w2):
    d = dict(norm1_g=norm1_g, norm2_g=norm2_g, w_in=w_in, w_out=w_out,
             rw_mu=rw_mu, rw_w0=rw_w0, rw_w2=rw_w2, rw_a0=rw_a0, rw_a2=rw_a2, rw_g2=rw_g2, rw_kk=rw_kk,
             rw_ka=rw_ka, rw_rk=rw_rk, rw_ln_g=rw_ln_g, rw_ln_b=rw_ln_b,
             gla_gk2=gla_gk2, gla_gkb=gla_gkb, gla_ln_g=gla_ln_g,
             lru_conv_w=lru_conv_w, lru_conv_b=lru_conv_b, lru_wa=lru_wa, lru_ba=lru_ba, lru_wx=lru_wx,
             lru_bx=lru_bx, lru_lam=lru_lam,
             cv_dw_w=cv_dw_w, cv_dw_b=cv_dw_b, cv_ln_g=cv_ln_g, cv_ln_b=cv_ln_b, cv_pw_w=cv_pw_w, cv_pw_b=cv_pw_b,
             e_w1=e_w1, e_w3=e_w3, e_w2=e_w2)
    P = _prep_params(d)
    shared = {"w_router_t": w_router.T, "b_router": b_router.reshape(N_EXPERTS, 1)}
    norm_f = norm_f_g.reshape(1, D_MODEL)

    bp, tp, _ = x_prompt.shape
    bs, ts, _ = x_sample.shape
    n_mod = 1 + bs
    mod_rows = -(-n_mod // 8) * 8
    cvec = jnp.concatenate([c_ctx[None], c, jnp.zeros((mod_rows - n_mod, D_MODEL), F32)], 0)
    mod = _ada_mod(cvec, w_ada, b_ada)

    tp_flat = min(bp * tp, 1024)
    bp_flat = bp * tp // tp_flat
    xp = x_prompt.reshape(bp_flat, tp_flat, D_MODEL)
    xs = x_sample
    zero_rw = jnp.zeros((bp, N_DIR, W_MIX, W_MIX), F32)
    zero_gla = jnp.zeros((bp, N_DIR, W_MIX, GLA_HK), F32)
    zero_lru = jnp.zeros((bp, N_DIR, W_MIX), F32)

    def layer(x, m, seq_shape, latent, s_rw, s_gla, s_lru, p, last):
        flat_shape = x.shape[:2]
        zs = _in_proj(x, m, p["norm1_g"], p["w_in"])
        za, zb, zc, zd = (z.reshape(seq_shape + (z.shape[-1],)) for z in zs)
        ya, f_rw = _rwkv_mixer(za, s_rw, p, latent)
        yb, f_gla = _gla_mixer(zb, s_gla, p)
        yc, f_lru = _lru_mixer(zc, s_lru, p)
        yd = _conv_mixer(zd, p, latent)
        ys = [y.reshape(flat_shape + (W_MIX,)) for y in (ya, yb, yc, yd)]
        return _out_moe(x, ys, m, p, norm_f, last), f_rw, f_gla, f_lru

    fin_rw, fin_gla, fin_lru = [], [], []
    for l in range(DEPTH):
        p = {n: a[l] for n, a in P.items()}
        p.update(shared)
        last = l == DEPTH - 1
        m_p = jnp.broadcast_to(mod[l, 0:1], (bp_flat, 6 * D_MODEL)).reshape(bp_flat, 1, 6 * D_MODEL)
        xp, f_rw, f_gla, f_lru = layer(xp, m_p, (bp, tp), False, zero_rw, zero_gla, zero_lru, p, last)
        fin_rw.append(_rw_state_out(f_rw))
        fin_gla.append(_gla_state_out(f_gla))
        fin_lru.append(f_lru)
        m_s = mod[l, 1:1 + bs].reshape(bs, 1, 6 * D_MODEL)
        xs, _, _, _ = layer(xs, m_s, (bs, ts), True, _rw_state_in(state_rwkv[:, l]),
                            _gla_state_in(state_gla[:, l]), state_lru[:, l], p, last)
    return (xp.reshape(bp, tp, D_MODEL), xs, jnp.stack(fin_rw, axis=1), jnp.stack(fin_gla, axis=1),
            jnp.stack(fin_lru, axis=1))
```

```python
import functools

import jax
import jax.numpy as jnp
from jax import lax
from jax.experimental import pallas as pl
from jax.experimental.pallas import tpu as pltpu

F32 = jnp.float32
BF16 = jnp.bfloat16
ACT = jnp.bfloat16

D_MODEL = 1024
DEPTH = 4
GRID_W = 64
N_DIR = 2
W_MIX = 256
N_HEADS = 4
HEAD = W_MIX // N_HEADS
GLA_DK = 32
GLA_HK = N_HEADS * GLA_DK
RW_LN_EPS = 64e-5
NORM_EPS = 1e-6
GLA_GATE_NORM = 16.0
LRU_C = 8.0
LRU_CONV = 4
CV_KERNEL = 31
CV_PAD = (CV_KERNEL - 1) // 2
N_EXPERTS = 16
GROUP_SIZE = 4
N_GROUPS = 4
D_EXPERT = 256
CHUNK = 64
EXP_M_HALF = 0.6065306597126334

LANE = 128
SUBLANE = 8

RW_COLS = 960
GLA_COLS = 800
ZA_W = 1024
ZB_W = 896
ZC_W = 512
ZD_W = 512
P_PAD = ZA_W + ZB_W + ZC_W + ZD_W
RW_R, RW_K, RW_V = slice(0, 256), slice(256, 512), slice(512, 768)
RW_LORA = slice(768, 896)
RW_GATE = slice(896, 1024)
GLA_Q, GLA_K, GLA_V = slice(0, 128), slice(128, 256), slice(256, 512)
GLA_OG = slice(512, 768)
GLA_GATE = slice(768, 896)

VMEM_LIMIT = 48 * 1024 * 1024


def _cparams(sem, vmem=VMEM_LIMIT):
    return pltpu.CompilerParams(dimension_semantics=sem, vmem_limit_bytes=vmem)


def _dot(a, b):
    return jnp.dot(a.astype(BF16), b.astype(BF16), preferred_element_type=F32)


def _dot_nt(a, b):
    return lax.dot_general(a.astype(BF16), b.astype(BF16), (((1,), (1,)), ((), ())),
                           preferred_element_type=F32)


def _dot_tn(a, b):
    return lax.dot_general(a.astype(BF16), b.astype(BF16), (((0,), (0,)), ((), ())),
                           preferred_element_type=F32)


def _split(a):
    hi = a.astype(BF16)
    lo = (a - hi.astype(F32)).astype(BF16)
    return hi, lo


def _dot_x2(a, b_exact):
    hi, lo = _split(a)
    bb = b_exact.astype(BF16)
    return (jnp.dot(hi, bb, preferred_element_type=F32) + jnp.dot(lo, bb, preferred_element_type=F32))


def _dot_left_x2(a_exact, b):
    hi, lo = _split(b)
    aa = a_exact.astype(BF16)
    return (jnp.dot(aa, hi, preferred_element_type=F32) + jnp.dot(aa, lo, preferred_element_type=F32))


def _dot3(a, b):
    ah, al = _split(a)
    bh, bl = _split(b)
    return (jnp.dot(ah, bh, preferred_element_type=F32) + jnp.dot(ah, bl, preferred_element_type=F32)
            + jnp.dot(al, bh, preferred_element_type=F32))


def _dot3_nt(a, b):
    ah, al = _split(a)
    bh, bl = _split(b)
    dn = (((1,), (1,)), ((), ()))
    return (lax.dot_general(ah, bh, dn, preferred_element_type=F32)
            + lax.dot_general(ah, bl, dn, preferred_element_type=F32)
            + lax.dot_general(al, bh, dn, preferred_element_type=F32))


def _iota(shape, axis):
    return lax.broadcasted_iota(jnp.int32, shape, axis)


def _block_mask(rows, cols, rblk, cblk):
    r = _iota((rows, cols), 0) // rblk
    c = _iota((rows, cols), 1) // cblk
    return (r == c).astype(F32)


def _expand(x, bm):
    xb = x.astype(BF16)
    return jnp.concatenate([xb, xb, xb, xb], axis=0) * bm


def _tri(d, strict):
    t = _iota((CHUNK, CHUNK), 0)
    s = _iota((CHUNK, CHUNK), 1)
    if d == 0:
        m = (s < t) if strict else (s <= t)
    else:
        m = (s > t) if strict else (s >= t)
    return m.astype(F32)


def _tri_wide(d, strict):
    t = _iota((CHUNK, N_HEADS * CHUNK), 0)
    s = _iota((CHUNK, N_HEADS * CHUNK), 1) % CHUNK
    if d == 0:
        m = (s < t) if strict else (s <= t)
    else:
        m = (s > t) if strict else (s >= t)
    return m.astype(F32)


ADA_TN = 1536


def _ada_kernel(c_ref, w_ref, b_ref, o_ref):
    c = c_ref[...]
    s = c * jax.nn.sigmoid(c)
    o_ref[0] = _dot3(s, w_ref[0]) + b_ref[0]


def _ada_mod(cvec, w_ada, b_ada):
    rows = cvec.shape[0]
    n_out = w_ada.shape[-1]
    return pl.pallas_call(
        _ada_kernel,
        grid=(DEPTH, n_out // ADA_TN),
        in_specs=[
            pl.BlockSpec((rows, D_MODEL), lambda l, j: (0, 0)),
            pl.BlockSpec((1, D_MODEL, ADA_TN), lambda l, j: (l, 0, j)),
            pl.BlockSpec((1, 1, ADA_TN), lambda l, j: (l, 0, j)),
        ],
        out_specs=pl.BlockSpec((1, rows, ADA_TN), lambda l, j: (l, 0, j)),
        out_shape=jax.ShapeDtypeStruct((DEPTH, rows, n_out), F32),
        compiler_params=_cparams(("arbitrary", "arbitrary")),
        name="ada_mod",
    )(cvec, w_ada, b_ada.reshape(DEPTH, 1, n_out))


IN_TM = 512


def _in_kernel(x_ref, sh_ref, sc_ref, g_ref, w_ref, za_ref, zb_ref, zc_ref, zd_ref):
    x = x_ref[0]
    h = x * lax.rsqrt(jnp.mean(x * x, axis=-1, keepdims=True) + NORM_EPS) * g_ref[...]
    h = (h * (1.0 + sc_ref[0]) + sh_ref[0]).astype(BF16)
    o0, o1, o2 = ZA_W, ZA_W + ZB_W, ZA_W + ZB_W + ZC_W
    za_ref[0] = jnp.dot(h, w_ref[:, 0:o0], preferred_element_type=F32).astype(ACT)
    zb_ref[0] = jnp.dot(h, w_ref[:, o0:o1], preferred_element_type=F32).astype(ACT)
    zc_ref[0] = jnp.dot(h, w_ref[:, o1:o2], preferred_element_type=F32).astype(ACT)
    zd_ref[0] = jnp.dot(h, w_ref[:, o2:P_PAD], preferred_element_type=F32).astype(ACT)


def _in_proj(x, m, g, w):
    b, t, _ = x.shape
    tm = min(IN_TM, t)
    tok = lambda w_: pl.BlockSpec((1, tm, w_), lambda i, j: (i, j, 0))
    mod = lambda k: pl.BlockSpec((1, 1, D_MODEL), lambda i, j, k=k: (i, 0, k))
    return pl.pallas_call(
        _in_kernel,
        grid=(b, t // tm),
        in_specs=[tok(D_MODEL), mod(0), mod(1),
                  pl.BlockSpec((1, D_MODEL), lambda i, j: (0, 0)),
                  pl.BlockSpec((D_MODEL, P_PAD), lambda i, j: (0, 0))],
        out_specs=[tok(ZA_W), tok(ZB_W), tok(ZC_W), tok(ZD_W)],
        out_shape=[jax.ShapeDtypeStruct((b, t, w_), ACT) for w_ in (ZA_W, ZB_W, ZC_W, ZD_W)],
        compiler_params=_cparams(("arbitrary", "arbitrary")),
        name="in_proj",
    )(x, m, m, g, w)


SHIFT_TB = 512


def _shift_grid_kernel(z_ref, up_ref, dn_ref, mu_ref, o_ref, *, nblk):
    i = pl.program_id(1)
    z = z_ref[0].astype(F32)
    tb, c = z.shape
    col = _iota((tb, c), 0) % GRID_W
    lane = _iota((tb, c), 1) % 4
    left = jnp.where(col == 0, 0.0, pltpu.roll(z, 1, 0))
    right = jnp.where(col == GRID_W - 1, 0.0, pltpu.roll(z, tb - 1, 0))
    up_halo = jnp.where(i > 0, up_ref[0, 0].astype(F32), 0.0)
    dn_halo = jnp.where(i < nblk - 1, dn_ref[0, 0].astype(F32), 0.0)
    up = jnp.concatenate([up_halo, z[:tb - GRID_W]], axis=0)
    down = jnp.concatenate([z[GRID_W:], dn_halo], axis=0)
    sh = jnp.where(lane == 0, left, jnp.where(lane == 1, right, jnp.where(lane == 2, up, down)))
    o_ref[0] = (z + (sh - z) * mu_ref[...]).astype(ACT)


def _shift_ctx_kernel(z_ref, mu_ref, o_ref):
    z = z_ref[0].astype(F32)
    t, c = z.shape
    row = _iota((t, c), 0)
    lane = _iota((t, c), 1) % 2
    prev = jnp.where(row == 0, 0.0, pltpu.roll(z, 1, 0))
    nxt = jnp.where(row == t - 1, 0.0, pltpu.roll(z, t - 1, 0))
    sh = jnp.where(lane == 0, prev, nxt)
    o_ref[0] = (z + (sh - z) * mu_ref[...]).astype(ACT)


def _rw_shift(za, mu, latent):
    b, t, c = za.shape
    mu_spec = pl.BlockSpec((1, c), lambda *_: (0, 0))
    if not latent:
        return pl.pallas_call(
            _shift_ctx_kernel,
            grid=(b,),
            in_specs=[pl.BlockSpec((1, t, c), lambda i: (i, 0, 0)), mu_spec],
            out_specs=pl.BlockSpec((1, t, c), lambda i: (i, 0, 0)),
            out_shape=jax.ShapeDtypeStruct((b, t, c), ACT),
            compiler_params=_cparams(("arbitrary",)),
            name="rw_shift_ctx",
        )(za, mu)
    tb = min(SHIFT_TB, t)
    nblk = t // tb
    rpb = tb // GRID_W
    nrow = t // GRID_W
    z4 = za.reshape(b, nrow, GRID_W, c)
    return pl.pallas_call(
        functools.partial(_shift_grid_kernel, nblk=nblk),
        grid=(b, nblk),
        in_specs=[
            pl.BlockSpec((1, tb, c), lambda i, j: (i, j, 0)),
            pl.BlockSpec((1, 1, GRID_W, c), lambda i, j: (i, jnp.maximum(j * rpb - 1, 0), 0, 0)),
            pl.BlockSpec((1, 1, GRID_W, c), lambda i, j: (i, jnp.minimum(j * rpb + rpb, nrow - 1), 0, 0)),
            mu_spec,
        ],
        out_specs=pl.BlockSpec((1, tb, c), lambda i, j: (i, j, 0)),
        out_shape=jax.ShapeDtypeStruct((b, t, c), ACT),
        compiler_params=_cparams(("arbitrary", "arbitrary")),
        name="rw_shift_grid",
    )(za, z4, z4, mu)


SCAN_TB = 512
PREP_SPLIT = 2
PREP_LOCKSTEP = 4
PREP_SKEW = 3
PREP_GROUP = 8


def _rw_scan_kernel(z_ref, s0_ref, wl_ref, lb_ref, kkp_ref, kap_ref, lng_ref, lnb_ref, rk_ref, g2_ref,
                    o_ref, st_ref,
                    y_ref, lw_s, a_s, b_s, kd_s, r2_s, mc_s, ds_s, gt_s, *, nchunk, nblk):
    i = pl.program_id(1)
    tb = nchunk * CHUNK

    @pl.when(i == 0)
    def _():
        st_ref[...] = s0_ref[...]
        y_ref[...] = jnp.zeros(y_ref.shape, F32)

    def y_rows(d, c):
        blk = i if d == 0 else nblk - 1 - i
        return pl.ds(pl.multiple_of((blk * nchunk + c) * CHUNK, CHUNK), CHUNK)

    def blk_rows(d):
        blk = i if d == 0 else nblk - 1 - i
        return pl.ds(pl.multiple_of(blk * tb, tb), tb)

    ones_blk = _block_mask(W_MIX, W_MIX, HEAD, HEAD)
    is_decay_lora = _iota((1, LANE), 1) < LANE // 2

    for d in range(N_DIR):
        k = z_ref[0, blk_rows(d), RW_K].astype(F32)
        la = z_ref[0, blk_rows(d), RW_LORA].astype(F32)
        la_t = jnp.where(is_decay_lora, jnp.tanh(la), la)
        w_cols = slice(W_MIX * d, W_MIX * (d + 1))
        a_cols = slice(W_MIX * (N_DIR + d), W_MIX * (N_DIR + d + 1))
        wraw = _dot(la_t, wl_ref[:, w_cols]) + lb_ref[:, w_cols]
        araw = _dot(la_t, wl_ref[:, a_cols]) + lb_ref[:, a_cols]
        lw_s[d] = jax.nn.sigmoid(wraw) * (-EXP_M_HALF)
        icl = jax.nn.sigmoid(araw)
        kkv = k * kkp_ref[...]
        ss = _dot_x2(kkv * kkv, ones_blk)
        kk = kkv * lax.rsqrt(ss + 1e-12)
        a_s[d] = -kk
        b_s[d] = kk * icl
        kd_s[d] = k * (1.0 + (icl - 1.0) * kap_ref[...])

    bm = _block_mask(N_HEADS * CHUNK, W_MIX, CHUNK, HEAD).astype(BF16)
    tri = [_tri(d, False) for d in range(N_DIR)]
    tw_strict = [_tri_wide(d, True) for d in range(N_DIR)]
    tw_incl = [_tri_wide(d, False) for d in range(N_DIR)]
    eye_w = (_iota((CHUNK, N_HEADS * CHUNK), 0) == _iota((CHUNK, N_HEADS * CHUNK), 1) % CHUNK).astype(F32)

    def rows_of(c, n):
        return pl.ds(pl.multiple_of(c * n, n), n)

    group = min(PREP_GROUP, nchunk)

    each = lambda fn, *ls: [fn(*xs) for xs in zip(*ls)]

    def prep_stages(ch):
        ds_ = [d for d, _ in ch]
        st = {}

        def front():
            load = lambda ref: [ref[d, rows_of(c, CHUNK), :] for d, c in ch]
            lw, a, st["b"], st["kd"] = load(lw_s), load(a_s), load(b_s), load(kd_s)
            r = [z_ref[0, y_rows(d, c), RW_R].astype(F32) for d, c in ch]
            st["v"] = [z_ref[0, y_rows(d, c), RW_V].astype(F32) for d, c in ch]
            cum = each(lambda d, x: _dot_left_x2(tri[d], x), ds_, lw)
            st["tot"] = each(lambda d, x: x[CHUNK - 1:CHUNK, :] if d == 0 else x[0:1, :], ds_, cum)
            st["g_inv"] = each(lambda x: jnp.exp(-x), cum)
            st["g_rem"] = each(lambda t_, x: jnp.exp(t_ - x), st["tot"], cum)
            st["at"] = each(lambda a_, x, l: a_ * jnp.exp(x - l), a, cum, lw)
            st["rt"] = each(lambda r_, x: r_ * jnp.exp(x), r, cum)

        def gram():
            x = each(lambda p, q: jnp.concatenate([p, q], axis=0), st["at"], st["rt"])
            gb = each(lambda x_, b_, g: _dot_nt(x_, _expand(b_ * g, bm)), x, st["b"], st["g_inv"])
            gk = each(lambda x_, k_, g: _dot_nt(x_, _expand(k_ * g, bm)), x, st["kd"], st["g_inv"])
            st["vbd"] = each(lambda v_: _expand(v_, bm), st["v"])
            st["apow"] = each(lambda d, g: g[0:CHUNK] * tw_strict[d], ds_, gb)
            st["av"] = each(lambda d, g, vb: _dot(g[0:CHUNK] * tw_strict[d], vb), ds_, gk, st["vbd"])
            st["tw"] = each(lambda m: eye_w + m, st["apow"])
            st["r_b"] = each(lambda d, g: g[CHUNK:] * tw_incl[d], ds_, gb)
            st["r_k"] = each(lambda d, g: g[CHUNK:] * tw_incl[d], ds_, gk)

        def double():
            st["apow"] = each(lambda m: _dot(m, _expand(m, bm)), st["apow"])
            st["tw"] = each(lambda t_, m: t_ + _dot(m, _expand(t_, bm)), st["tw"], st["apow"])

        def apply():
            st["a2"] = each(lambda t_, m: _dot(t_, _expand(m, bm)), st["tw"], st["at"])
            st["u0"] = each(lambda t_, m: _dot(t_, _expand(m, bm)), st["tw"], st["av"])

        def finish():
            r2 = each(lambda rt_, rb, m: rt_ + _dot(rb, _expand(m, bm)), st["rt"], st["r_b"], st["a2"])
            y0 = each(lambda rb, u, rk, vb: _dot(rb, _expand(u, bm)) + _dot(rk, vb),
                      st["r_b"], st["u0"], st["r_k"], st["vbd"])
            bl = each(lambda b_, g: b_ * g, st["b"], st["g_rem"])
            kl = each(lambda k_, g: k_ * g, st["kd"], st["g_rem"])
            mc = each(lambda m, bl_: _dot_tn(m, bl_) * ones_blk, st["a2"], bl)
            ds0 = each(lambda u, v_, bl_, kl_: _dot_tn(jnp.concatenate([u, v_], axis=0),
                                                       jnp.concatenate([bl_, kl_], axis=0)) * ones_blk,
                       st["u0"], st["v"], bl, kl)
            for n, (d, c) in enumerate(ch):
                rows = rows_of(c, CHUNK)
                y_ref[0, y_rows(d, c), :] += y0[n]
                r2_s[d, rows, :] = r2[n].astype(BF16)
                mc_s[d, rows_of(c, W_MIX), :] = mc[n].astype(BF16)
                ds_s[d, rows_of(c, W_MIX), :] = ds0[n]
                gt_s[d, rows_of(c, SUBLANE), :] = jnp.broadcast_to(jnp.exp(st["tot"][n]), (SUBLANE, W_MIX))

        return [front, gram] + [double] * 5 + [apply, finish]

    def prep_body(gi, carry):
        half = group // PREP_SPLIT if group >= PREP_SPLIT * PREP_LOCKSTEP else group
        halves = [prep_stages([(d, gi * group + j) for j in range(h0, min(h0 + half, group)) for d in range(N_DIR)])
                  for h0 in range(0, group, half)]
        n_stage = len(halves[0])
        for step in range(n_stage + PREP_SKEW * (len(halves) - 1)):
            for hi, stages in enumerate(halves):
                k = step - PREP_SKEW * hi
                if 0 <= k < n_stage:
                    stages[k]()
        return carry

    lax.fori_loop(0, nchunk // group, prep_body, 0)

    def scan_body(cc, carry):
        cs = [cc, nchunk - 1 - cc]
        s = [st_ref[0, d] for d in range(N_DIR)]
        sb = [x.astype(BF16) for x in s]
        upd = [_dot(sb[d], mc_s[d, rows_of(cs[d], W_MIX), :]) for d in range(N_DIR)]
        ys = [_dot_nt(r2_s[d, rows_of(cs[d], CHUNK), :], sb[d]) for d in range(N_DIR)]
        for d in range(N_DIR):
            st_ref[0, d] = (s[d] * gt_s[d, rows_of(cs[d], SUBLANE), :][0:1, :] + upd[d]
                            + ds_s[d, rows_of(cs[d], W_MIX), :])
            y_ref[0, y_rows(d, cs[d]), :] += ys[d]
        return carry

    lax.fori_loop(0, nchunk, scan_body, 0)

    @pl.when(i == nblk - 1)
    def _():
        def epilogue(j, carry):
            rows = pl.ds(pl.multiple_of(j * tb, tb), tb)
            y = y_ref[0, rows, :]
            mu = _dot_x2(y, ones_blk) * (1.0 / HEAD)
            yc = y - mu
            var = _dot_x2(yc * yc, ones_blk) * (1.0 / HEAD)
            yn = yc * lax.rsqrt(var + RW_LN_EPS) * lng_ref[...] + lnb_ref[...]
            r = z_ref[0, rows, RW_R].astype(F32)
            k = z_ref[0, rows, RW_K].astype(F32)
            v = z_ref[0, rows, RW_V].astype(F32)
            gl = z_ref[0, rows, RW_GATE].astype(F32)
            bonus = _dot_x2(r * k * rk_ref[...], ones_blk) * v
            gate = _dot(jax.nn.sigmoid(gl), g2_ref[...])
            o_ref[0, rows, :] = ((yn + bonus) * gate).astype(ACT)
            return carry

        lax.fori_loop(0, nblk, epilogue, 0)


def _rw_scan(zs, s0, p):
    b, t, c = zs.shape
    tb = min(SCAN_TB, t)
    nblk = t // tb
    nchunk = tb // CHUNK
    vec = lambda n: pl.BlockSpec((1, n), lambda i, j: (0, 0))
    seq = lambda w_: pl.BlockSpec((1, t, w_), lambda i, j: (i, 0, 0))
    st_spec = pl.BlockSpec((1, N_DIR, W_MIX, W_MIX), lambda i, j: (i, 0, 0, 0))
    return pl.pallas_call(
        functools.partial(_rw_scan_kernel, nchunk=nchunk, nblk=nblk),
        grid=(b, nblk),
        in_specs=[
            seq(c), st_spec,
            pl.BlockSpec((LANE, 2 * N_DIR * W_MIX), lambda i, j: (0, 0)),
            vec(1024), vec(W_MIX), vec(W_MIX), vec(W_MIX), vec(W_MIX), vec(W_MIX),
            pl.BlockSpec((LANE, W_MIX), lambda i, j: (0, 0)),
        ],
        out_specs=[seq(W_MIX), st_spec],
        out_shape=[
            jax.ShapeDtypeStruct((b, t, W_MIX), ACT),
            jax.ShapeDtypeStruct((b, N_DIR, W_MIX, W_MIX), F32),
        ],
        scratch_shapes=[pltpu.VMEM((1, t, W_MIX), F32)]
        + [pltpu.VMEM((N_DIR, tb, W_MIX), F32) for _ in range(4)] + [
            pltpu.VMEM((N_DIR, tb, W_MIX), BF16),
            pltpu.VMEM((N_DIR, nchunk * W_MIX, W_MIX), BF16),
            pltpu.VMEM((N_DIR, nchunk * W_MIX, W_MIX), F32),
            pltpu.VMEM((N_DIR, nchunk * 8, W_MIX), F32),
        ],
        compiler_params=_cparams(("arbitrary", "arbitrary")),
        name="rw_scan",
    )(zs, s0, p["rw_wl"], p["rw_lb"], p["rw_kk"], p["rw_ka"], p["rw_ln_g"], p["rw_ln_b"], p["rw_rk"], p["rw_g2"])


def _rwkv_mixer(za, s0_bd, p, latent):
    zs = _rw_shift(za, p["rw_mu"], latent)
    return _rw_scan(zs, s0_bd, p)


def _gla_scan_kernel(z_ref, s0_ref, wg_ref, gb_ref, lng_ref, o_ref, st_ref,
                     y_ref, la_s, qe_s, ds_s, dec_s, sp_s, *, nchunk, nblk):
    i = pl.program_id(1)
    tb = nchunk * CHUNK

    @pl.when(i == 0)
    def _():
        st_ref[...] = s0_ref[...]
        y_ref[...] = jnp.zeros(y_ref.shape, F32)

    def y_rows(d, c):
        blk = i if d == 0 else nblk - 1 - i
        return pl.ds(pl.multiple_of((blk * nchunk + c) * CHUNK, CHUNK), CHUNK)

    for d in range(N_DIR):
        blk = i if d == 0 else nblk - 1 - i
        logit = _dot(z_ref[0, pl.ds(pl.multiple_of(blk * tb, tb), tb), GLA_GATE], wg_ref[d]) + gb_ref[d]
        la_s[d] = jax.nn.log_sigmoid(logit) * (1.0 / GLA_GATE_NORM)

    bm_k = _block_mask(N_HEADS * CHUNK, GLA_HK, CHUNK, GLA_DK).astype(BF16)
    bm_v = _block_mask(N_HEADS * CHUNK, W_MIX, CHUNK, HEAD).astype(BF16)
    bm_s = _block_mask(W_MIX, GLA_HK, HEAD, GLA_DK)
    tri = [_tri(d, False) for d in range(N_DIR)]
    tw_incl = [_tri_wide(d, False) for d in range(N_DIR)]

    def rows_of(c, n):
        return pl.ds(pl.multiple_of(c * n, n), n)

    group = min(PREP_GROUP, nchunk)

    def prep_body(gi, carry):
        ch = [(d, gi * group + j) for j in range(group) for d in range(N_DIR)]
        each = lambda fn, *ls: [fn(*xs) for xs in zip(*ls)]
        ds_ = [d for d, _ in ch]
        la = [la_s[d, rows_of(c, CHUNK), :] for d, c in ch]
        q = [z_ref[0, y_rows(d, c), GLA_Q].astype(F32) * (GLA_DK ** -0.5) for d, c in ch]
        k = [z_ref[0, y_rows(d, c), GLA_K].astype(F32) for d, c in ch]
        v = [z_ref[0, y_rows(d, c), GLA_V].astype(F32) for d, c in ch]
        cum = each(lambda d, x: _dot_left_x2(tri[d], x), ds_, la)
        last = each(lambda d, x: x[CHUNK - 1:CHUNK, :] if d == 0 else x[0:1, :], ds_, cum)
        qe = each(lambda q_, x: q_ * jnp.exp(x), q, cum)
        ke = each(lambda k_, x: k_ * jnp.exp(-x), k, cum)
        kl = each(lambda k_, l, x: k_ * jnp.exp(l - x), k, last, cum)
        att = each(lambda d, q_, k_: _dot_nt(q_, _expand(k_, bm_k)) * tw_incl[d], ds_, qe, ke)
        o = each(lambda a_, v_: _dot(a_, _expand(v_, bm_v)), att, v)
        dst = each(lambda v_, k_: _dot_tn(v_, k_) * bm_s, v, kl)
        for n, (d, c) in enumerate(ch):
            y_ref[0, y_rows(d, c), :] += o[n]
            qe_s[d, rows_of(c, CHUNK), :] = qe[n].astype(BF16)
            ds_s[d, rows_of(c, W_MIX), :] = dst[n]
            dec_s[d, rows_of(c, SUBLANE), :] = jnp.broadcast_to(jnp.exp(last[n]), (SUBLANE, GLA_HK))
        return carry

    lax.fori_loop(0, nchunk // group, prep_body, 0)

    def state_body(cc, carry):
        cs = [cc, nchunk - 1 - cc]
        for d in range(N_DIR):
            s = st_ref[0, d]
            sp_s[d, rows_of(cs[d], W_MIX), :] = s.astype(BF16)
            st_ref[0, d] = s * dec_s[d, rows_of(cs[d], SUBLANE), :][0:1, :] + ds_s[d, rows_of(cs[d], W_MIX), :]
        return carry

    lax.fori_loop(0, nchunk, state_body, 0)

    def out_body(gi, carry):
        ch = [(d, gi * group + j) for j in range(group) for d in range(N_DIR)]
        ys = [_dot_nt(qe_s[d, rows_of(c, CHUNK), :], sp_s[d, rows_of(c, W_MIX), :]) for d, c in ch]
        for n, (d, c) in enumerate(ch):
            y_ref[0, y_rows(d, c), :] += ys[n]
        return carry

    lax.fori_loop(0, nchunk // group, out_body, 0)

    @pl.when(i == nblk - 1)
    def _():
        ones_blk = _block_mask(W_MIX, W_MIX, HEAD, HEAD)

        def epilogue(j, carry):
            rows = pl.ds(pl.multiple_of(j * tb, tb), tb)
            y = y_ref[0, rows, :]
            ms = _dot_x2(y * y, ones_blk) * (1.0 / HEAD)
            y = y * lax.rsqrt(ms + NORM_EPS) * lng_ref[...]
            og = z_ref[0, rows, GLA_OG].astype(F32)
            o_ref[0, rows, :] = (y * (og * jax.nn.sigmoid(og))).astype(ACT)
            return carry

        lax.fori_loop(0, nblk, epilogue, 0)


def _gla_mixer(zb, s0, p):
    b, t, c = zb.shape
    tb = min(SCAN_TB, t)
    nblk = t // tb
    seq = lambda w_: pl.BlockSpec((1, t, w_), lambda i, j: (i, 0, 0))
    st_spec = pl.BlockSpec((1, N_DIR, W_MIX, GLA_HK), lambda i, j: (i, 0, 0, 0))
    nchunk = tb // CHUNK
    return pl.pallas_call(
        functools.partial(_gla_scan_kernel, nchunk=nchunk, nblk=nblk),
        grid=(b, nblk),
        in_specs=[
            seq(c), st_spec,
            pl.BlockSpec((N_DIR, LANE, GLA_HK), lambda i, j: (0, 0, 0)),
            pl.BlockSpec((N_DIR, 1, GLA_HK), lambda i, j: (0, 0, 0)),
            pl.BlockSpec((1, W_MIX), lambda i, j: (0, 0)),
        ],
        out_specs=[seq(W_MIX), st_spec],
        out_shape=[
            jax.ShapeDtypeStruct((b, t, W_MIX), ACT),
            jax.ShapeDtypeStruct((b, N_DIR, W_MIX, GLA_HK), F32),
        ],
        scratch_shapes=[
            pltpu.VMEM((1, t, W_MIX), F32),
            pltpu.VMEM((N_DIR, tb, GLA_HK), F32),
            pltpu.VMEM((N_DIR, tb, GLA_HK), BF16),
            pltpu.VMEM((N_DIR, nchunk * W_MIX, GLA_HK), F32),
            pltpu.VMEM((N_DIR, nchunk * SUBLANE, GLA_HK), F32),
            pltpu.VMEM((N_DIR, nchunk * W_MIX, GLA_HK), BF16),
        ],
        compiler_params=_cparams(("arbitrary", "arbitrary")),
        name="gla_scan",
    )(zb, s0, p["gla_wg"], p["gla_gb"], p["gla_ln_g"])


LRU_TT = 256
LRU_HALO = 8
LRU_SUB = 8


def _lru_kernel(z_ref, h0_ref, cw_ref, cb_ref, wax_ref, bax_ref, lam_ref, y_ref, hf_ref, xpad, hfwd, *, t):
    tt = min(LRU_TT, t)
    ntile = t // tt
    xpad[0:LRU_HALO, :] = jnp.zeros((LRU_HALO, W_MIX), F32)
    xpad[LRU_HALO + t:2 * LRU_HALO + t, :] = jnp.zeros((LRU_HALO, W_MIX), F32)

    def fill(j, carry):
        base = pl.multiple_of(j * tt, tt)
        xpad[pl.ds(base + LRU_HALO, tt), :] = z_ref[0, pl.ds(base, tt), 0:W_MIX].astype(F32)
        return carry

    lax.fori_loop(0, ntile, fill, 0)
    sub = _iota((tt // LRU_SUB, LRU_SUB, W_MIX), 1)

    def tile_scan(j, h, d):
        base = pl.multiple_of(j * tt, tt)
        win = xpad[pl.ds(base, tt + 2 * LRU_HALO), :]
        xc = jnp.zeros((tt, W_MIX), F32) + cb_ref[d]
        for tap in range(LRU_CONV):
            off = LRU_HALO - (LRU_CONV - 1) + tap if d == 0 else LRU_HALO + (LRU_CONV - 1) - tap
            xc = xc + cw_ref[d, tap:tap + 1, :] * win[off:off + tt, :]
        g = _dot(xc, wax_ref[d]) + bax_ref[d]
        gr = jax.nn.sigmoid(g[:, 0:W_MIX])
        gi = jax.nn.sigmoid(g[:, W_MIX:2 * W_MIX])
        log_a = -LRU_C * gr * jax.nn.softplus(-lam_ref[d])
        a = jnp.exp(log_a)
        bv = jnp.sqrt(1.0 - jnp.exp(2.0 * log_a)) * gi * xc
        ngrp = tt // LRU_SUB
        a = a.reshape(ngrp, LRU_SUB, W_MIX)
        bv = bv.reshape(ngrp, LRU_SUB, W_MIX)
        s = 1
        while s < LRU_SUB:
            shift = s if d == 0 else LRU_SUB - s
            keep = (sub >= s) if d == 0 else (sub < LRU_SUB - s)
            a_sh = jnp.where(keep, pltpu.roll(a, shift, 1), 1.0)
            b_sh = jnp.where(keep, pltpu.roll(bv, shift, 1), 0.0)
            bv = a * b_sh + bv
            a = a * a_sh
            s *= 2
        out = [None] * ngrp
        for gg in range(ngrp):
            g = gg if d == 0 else ngrp - 1 - gg
            hg = a[g] * h + bv[g]
            out[g] = hg
            h = hg[LRU_SUB - 1:LRU_SUB, :] if d == 0 else hg[0:1, :]
        return jnp.concatenate(out, axis=0), h, base

    def fwd(j, h):
        ht, h, base = tile_scan(j, h, 0)
        hfwd[pl.ds(base, tt), :] = ht
        return h

    h_end = lax.fori_loop(0, ntile, fwd, h0_ref[0, 0:1, :])
    hf_ref[0, 0:1, :] = h_end

    def bwd(jj, h):
        j = ntile - 1 - jj
        ht, h, base = tile_scan(j, h, 1)
        gb = z_ref[0, pl.ds(base, tt), W_MIX:2 * W_MIX].astype(F32)
        y_ref[0, pl.ds(base, tt), :] = ((hfwd[pl.ds(base, tt), :] + ht) * jax.nn.gelu(gb)).astype(ACT)
        return h

    h_end = lax.fori_loop(0, ntile, bwd, h0_ref[0, 1:2, :])
    hf_ref[0, 1:2, :] = h_end


def _lru_mixer(zc, h0, p):
    b, t, c = zc.shape
    full = lambda *s: pl.BlockSpec(s, lambda i: (0,) * len(s))
    return pl.pallas_call(
        functools.partial(_lru_kernel, t=t),
        grid=(b,),
        in_specs=[
            pl.BlockSpec((1, t, c), lambda i: (i, 0, 0)),
            pl.BlockSpec((1, N_DIR, W_MIX), lambda i: (i, 0, 0)),
            full(N_DIR, LRU_CONV, W_MIX), full(N_DIR, 1, W_MIX),
            full(N_DIR, W_MIX, 2 * W_MIX), full(N_DIR, 1, 2 * W_MIX), full(N_DIR, 1, W_MIX),
        ],
        out_specs=[pl.BlockSpec((1, t, W_MIX), lambda i: (i, 0, 0)),
                   pl.BlockSpec((1, N_DIR, W_MIX), lambda i: (i, 0, 0))],
        out_shape=[jax.ShapeDtypeStruct((b, t, W_MIX), ACT), jax.ShapeDtypeStruct((b, N_DIR, W_MIX), F32)],
        scratch_shapes=[pltpu.VMEM((t + 2 * LRU_HALO, W_MIX), F32), pltpu.VMEM((t, W_MIX), F32)],
        compiler_params=_cparams(("arbitrary",)),
        name="lru",
    )(zc, h0, p["lru_cw"], p["lru_cb"], p["lru_wax"], p["lru_bax"], p["lru_lam"])


CV_TT = 256
CV_WIN = 16


def _conv_kernel(z_ref, dw_ref, dwb_ref, lng_ref, lnb_ref, pw_ref, pwb_ref, y_ref, upad, *, t, latent):
    tt = min(CV_TT, t)
    ntile = t // tt
    pad = CV_PAD * GRID_W if latent else CV_WIN
    upad[0:pad, :] = jnp.zeros((pad, W_MIX), F32)
    upad[pad + t:2 * pad + t, :] = jnp.zeros((pad, W_MIX), F32)

    def fill(j, carry):
        base = pl.multiple_of(j * tt, tt)
        z = z_ref[0, pl.ds(base, tt), :].astype(F32)
        upad[pl.ds(base + pad, tt), :] = z[:, 0:W_MIX] * jax.nn.sigmoid(z[:, W_MIX:2 * W_MIX])
        return carry

    lax.fori_loop(0, ntile, fill, 0)

    def tile(j, carry):
        base = pl.multiple_of(j * tt, tt)
        if latent:
            half = W_MIX // 2
            col = _iota((tt, half), 0) % GRID_W
            win = upad[pl.ds(pl.multiple_of(base + pad - CV_WIN, CV_WIN), tt + 2 * CV_WIN), 0:half]
            accw = jnp.zeros((tt, half), F32)
            acch = jnp.zeros((tt, half), F32)
            for tap in range(CV_KERNEL):
                dlt = tap - CV_PAD
                x = win[CV_WIN + dlt:CV_WIN + dlt + tt, :]
                if dlt < 0:
                    x = jnp.where(col >= -dlt, x, 0.0)
                elif dlt > 0:
                    x = jnp.where(col < GRID_W - dlt, x, 0.0)
                accw = accw + dw_ref[tap:tap + 1, 0:half] * x
                rows = pl.ds(pl.multiple_of(base + pad + dlt * GRID_W, GRID_W), tt)
                acch = acch + dw_ref[tap:tap + 1, half:W_MIX] * upad[rows, half:W_MIX]
            u = jnp.concatenate([accw, acch], axis=1)
        else:
            win = upad[pl.ds(base + pad - CV_WIN, tt + 2 * CV_WIN), :]
            u = jnp.zeros((tt, W_MIX), F32)
            for tap in range(CV_KERNEL):
                dlt = tap - CV_PAD
                u = u + dw_ref[tap:tap + 1, :] * win[CV_WIN + dlt:CV_WIN + dlt + tt, :]
        u = u + dwb_ref[...]
        mu = jnp.mean(u, axis=-1, keepdims=True)
        uc = u - mu
        var = jnp.mean(uc * uc, axis=-1, keepdims=True)
        un = uc * lax.rsqrt(var + 1e-5) * lng_ref[...] + lnb_ref[...]
        un = un * jax.nn.sigmoid(un)
        y_ref[0, pl.ds(base, tt), :] = (_dot(un, pw_ref[...]) + pwb_ref[...]).astype(ACT)
        return carry

    lax.fori_loop(0, ntile, tile, 0)


def _conv_mixer(zd, p, latent):
    b, t, c = zd.shape
    pad = CV_PAD * GRID_W if latent else CV_WIN
    full = lambda *s: pl.BlockSpec(s, lambda i: (0,) * len(s))
    vec = full(1, W_MIX)
    return pl.pallas_call(
        functools.partial(_conv_kernel, t=t, latent=latent),
        grid=(b,),
        in_specs=[pl.BlockSpec((1, t, c), lambda i: (i, 0, 0)),
                  full(CV_KERNEL, W_MIX), vec, vec, vec, full(W_MIX, W_MIX), vec],
        out_specs=pl.BlockSpec((1, t, W_MIX), lambda i: (i, 0, 0)),
        out_shape=jax.ShapeDtypeStruct((b, t, W_MIX), ACT),
        scratch_shapes=[pltpu.VMEM((t + 2 * pad, W_MIX), F32)],
        compiler_params=_cparams(("arbitrary",)),
        name="conv_grid" if latent else "conv_ctx",
    )(zd, p["cv_dw_w"], p["cv_dw_b"], p["cv_ln_g"], p["cv_ln_b"], p["cv_pw_w"], p["cv_pw_b"])


MOE_TM = 512
MOE_RB = 128
E_PAD = LANE


def _route(sel, scores):
    grp = []
    for g in range(N_GROUPS):
        s = sel[GROUP_SIZE * g:GROUP_SIZE * (g + 1)]
        best_pair = None
        for i in range(GROUP_SIZE):
            for j in range(i + 1, GROUP_SIZE):
                pair = s[i] + s[j]
                best_pair = pair if best_pair is None else jnp.maximum(best_pair, pair)
        grp.append(best_pair)
    best = jnp.zeros_like(grp[0], dtype=jnp.int32)
    top = grp[0]
    for g in range(1, N_GROUPS):
        better = grp[g] > top
        best = jnp.where(better, g, best)
        top = jnp.where(better, grp[g], top)
    neg = jnp.full_like(sel[0], -jnp.inf)
    msel = [jnp.where(best == e // GROUP_SIZE, sel[e], neg) for e in range(N_EXPERTS)]
    picks = []
    for _ in range(2):
        idx = jnp.zeros_like(best)
        top = msel[0]
        for e in range(1, N_EXPERTS):
            better = msel[e] > top
            idx = jnp.where(better, e, idx)
            top = jnp.where(better, msel[e], top)
        picks.append(idx)
        msel = [jnp.where(idx == e, neg, msel[e]) for e in range(N_EXPERTS)]
    chosen = [jnp.where((picks[0] == e) | (picks[1] == e), scores[e], 0.0) for e in range(N_EXPERTS)]
    total = chosen[0]
    for e in range(1, N_EXPERTS):
        total = total + chosen[e]
    return [ch / total for ch in chosen], best


def _moe_kernel(x_ref, ya_ref, yb_ref, yc_ref, yd_ref, wo_ref, g1_ref, sh2_ref, sc2_ref, g2_ref, n2_ref,
                wr_ref, br_ref, w1_ref, w3_ref, w2_ref, nf_ref, o_ref,
                x1_s, h2_s, gt_s, gate_s, acc_s, he_s, og_s, pmt_s, seg_s, *, final_norm):
    grp = pl.program_id(2)
    tm = x1_s.shape[0]

    @pl.when(grp == 0)
    def _():
        y = sum(_dot(y_ref[0], wo_ref[W_MIX * m:W_MIX * (m + 1), :])
                for m, y_ref in enumerate((ya_ref, yb_ref, yc_ref, yd_ref)))
        x1 = x_ref[0] + g1_ref[0] * y
        x1_s[...] = x1
        h2 = x1 * lax.rsqrt(jnp.mean(x1 * x1, axis=-1, keepdims=True) + NORM_EPS) * n2_ref[...]
        h2 = h2 * (1.0 + sc2_ref[0]) + sh2_ref[0]
        logits = _dot3_nt(wr_ref[...], h2)
        scores = jax.nn.sigmoid(logits)
        selm = scores + br_ref[...]
        gates, best = _route([selm[i:i + 1, :] for i in range(N_EXPERTS)],
                             [scores[i:i + 1, :] for i in range(N_EXPERTS)])
        gt_s[...] = jnp.zeros(gt_s.shape, F32)
        for i in range(N_EXPERTS):
            gt_s[i:i + 1, :] = gates[i]

        og = [jnp.where(best == g, 1.0, 0.0) for g in range(N_GROUPS)]
        og_s[...] = jnp.zeros(og_s.shape, F32)
        start = jnp.int32(0)
        starts = []
        for g in range(N_GROUPS):
            og_s[g:g + 1, :] = og[g]
            starts.append(start)
            seg_s[g] = start
            start = start + jnp.sum(og[g]).astype(jnp.int32)
            seg_s[N_GROUPS + g] = start
        before = jnp.where(_iota((tm, tm), 0) < _iota((tm, tm), 1), 1.0, 0.0).astype(BF16)
        rank = jnp.dot(og_s[...].astype(BF16), before, preferred_element_type=F32)
        pos = og[0] * (starts[0].astype(F32) + rank[0:1, :])
        for g in range(1, N_GROUPS):
            pos = pos + og[g] * (starts[g].astype(F32) + rank[g:g + 1, :])
        gt_s[N_EXPERTS:N_EXPERTS + 1, :] = pos
        gate_tok = gt_s[...].T
        slot_l = _iota((tm, tm), 1).astype(F32)
        slot_s = _iota((tm, tm), 0).astype(F32)
        pmt_s[...] = jnp.where(slot_l == gate_tok[:, N_EXPERTS:N_EXPERTS + 1], 1.0, 0.0).astype(BF16)
        pm = jnp.where(slot_s == pos, 1.0, 0.0).astype(BF16)
        h2_s[...] = jnp.dot(pm, h2.astype(BF16), preferred_element_type=F32).astype(BF16)
        gate_s[...] = _dot_left_x2(pm, gate_tok)
        acc_s[...] = jnp.zeros(acc_s.shape, F32)

    pick = (_iota((E_PAD, GROUP_SIZE * LANE), 0)
            == GROUP_SIZE * grp + _iota((E_PAD, GROUP_SIZE * LANE), 1) // LANE).astype(F32)

    def block(rb, nrows):
        rows = pl.ds(pl.multiple_of(rb * MOE_RB, MOE_RB), nrows)
        gsel = _dot_x2(gate_s[rows, :], pick)
        h2 = h2_s[rows, :]
        for j in range(GROUP_SIZE):
            he = jnp.dot(h2, w1_ref[j], preferred_element_type=F32)
            he = he * jax.nn.sigmoid(he) * jnp.dot(h2, w3_ref[j], preferred_element_type=F32)
            g = gsel[:, LANE * j:LANE * (j + 1)]
            he_s[rows, D_EXPERT * j:D_EXPERT * (j + 1)] = (he * jnp.concatenate([g, g], axis=1)).astype(BF16)
        w2g = w2_ref[...].reshape(GROUP_SIZE * D_EXPERT, D_MODEL)
        acc_s[rows, :] += jnp.dot(he_s[rows, :], w2g, preferred_element_type=F32)

    first = seg_s[grp] // MOE_RB
    last = (seg_s[N_GROUPS + grp] + (MOE_RB - 1)) // MOE_RB

    def pair(pi, carry):
        block(first + 2 * pi, 2 * MOE_RB)
        return carry

    lax.fori_loop(0, (last - first) // 2, pair, 0)

    @pl.when((last - first) % 2 == 1)
    def _():
        block(last - 1, MOE_RB)

    @pl.when(grp == N_GROUPS - 1)
    def _():
        x2 = x1_s[...] + g2_ref[0] * _dot_left_x2(pmt_s[...], acc_s[...])
        if final_norm:
            x2 = x2 * lax.rsqrt(jnp.mean(x2 * x2, axis=-1, keepdims=True) + NORM_EPS) * nf_ref[...]
        o_ref[0] = x2


def _out_moe(x, ys, m, p, norm_f, final_norm):
    b, t, _ = x.shape
    tm = min(MOE_TM, t)
    tok = lambda w_: pl.BlockSpec((1, tm, w_), lambda i, j, e: (i, j, 0))
    mod = lambda k: pl.BlockSpec((1, 1, D_MODEL), lambda i, j, e, k=k: (i, 0, k))
    full = lambda *s: pl.BlockSpec(s, lambda i, j, e: (0,) * len(s))
    return pl.pallas_call(
        functools.partial(_moe_kernel, final_norm=final_norm),
        grid=(b, t // tm, N_GROUPS),
        in_specs=[tok(D_MODEL), tok(W_MIX), tok(W_MIX), tok(W_MIX), tok(W_MIX),
                  full(D_MODEL, D_MODEL), mod(2), mod(3), mod(4), mod(5), full(1, D_MODEL),
                  full(N_EXPERTS, D_MODEL), full(N_EXPERTS, 1),
                  pl.BlockSpec((GROUP_SIZE, D_MODEL, D_EXPERT), lambda i, j, e: (e, 0, 0)),
                  pl.BlockSpec((GROUP_SIZE, D_MODEL, D_EXPERT), lambda i, j, e: (e, 0, 0)),
                  pl.BlockSpec((GROUP_SIZE, D_EXPERT, D_MODEL), lambda i, j, e: (e, 0, 0)),
                  full(1, D_MODEL)],
        out_specs=tok(D_MODEL),
        out_shape=jax.ShapeDtypeStruct((b, t, D_MODEL), F32),
        scratch_shapes=[pltpu.VMEM((tm, D_MODEL), F32), pltpu.VMEM((tm, D_MODEL), BF16),
                        pltpu.VMEM((E_PAD, tm), F32), pltpu.VMEM((tm, E_PAD), F32),
                        pltpu.VMEM((tm, D_MODEL), F32), pltpu.VMEM((tm, D_EXPERT * GROUP_SIZE), BF16),
                        pltpu.VMEM((SUBLANE, tm), F32), pltpu.VMEM((tm, tm), BF16),
                        pltpu.SMEM((2 * N_GROUPS,), jnp.int32)],
        compiler_params=_cparams(("arbitrary", "arbitrary", "arbitrary")),
        name="out_moe",
    )(x, *ys, p["w_out"], m, m, m, m, p["norm2_g"], p["w_router_t"], p["b_router"], p["e_w1"], p["e_w3"], p["e_w2"],
      norm_f)


def _gla_pack(z):
    hk = GLA_HK
    lead = z.shape[:-1]
    gl = z[..., 2 * hk + W_MIX:2 * hk + W_MIX + 32]
    return jnp.concatenate([z[..., 0:2 * hk + W_MIX], z[..., 2 * hk + W_MIX + 32:], gl,
                            jnp.zeros(lead + (96,), z.dtype)], -1)


def _gla_state_in(s):
    b = s.shape[0]
    eye = jnp.eye(N_HEADS, dtype=s.dtype)
    return jnp.einsum("bdhkv,hg->bdhvgk", s, eye).reshape(b, N_DIR, W_MIX, GLA_HK)


def _gla_state_out(st):
    b = st.shape[0]
    eye = jnp.eye(N_HEADS, dtype=st.dtype)
    return jnp.einsum("bdhvgk,hg->bdhkv", st.reshape(b, N_DIR, N_HEADS, HEAD, N_HEADS, GLA_DK), eye)


def _rw_state_in(s):
    b = s.shape[0]
    eye = jnp.eye(N_HEADS, dtype=s.dtype)
    return jnp.einsum("bdhvk,hg->bdhvgk", s, eye).reshape(b, N_DIR, W_MIX, W_MIX)


def _rw_state_out(st):
    b = st.shape[0]
    eye = jnp.eye(N_HEADS, dtype=st.dtype)
    return jnp.einsum("bdhvgk,hg->bdhvk", st.reshape(b, N_DIR, N_HEADS, HEAD, N_HEADS, HEAD), eye)


def _prep_params(d):
    L = DEPTH
    z = lambda *s: jnp.zeros(s, F32)
    out = {}
    out["rw_mu"] = jnp.concatenate([d["rw_mu"], z(L, ZA_W - d["rw_mu"].shape[-1])], -1).reshape(L, 1, ZA_W)
    w2, a2 = d["rw_w2"], d["rw_a2"]
    zz = z(L, 32, 256)
    rows = [
        jnp.concatenate([w2[:, 0], zz, zz, zz], -1),
        jnp.concatenate([zz, w2[:, 1], zz, zz], -1),
        jnp.concatenate([zz, zz, a2[:, 0], zz], -1),
        jnp.concatenate([zz, zz, zz, a2[:, 1]], -1),
    ]
    out["rw_wl"] = jnp.concatenate(rows, 1).astype(BF16)
    out["rw_lb"] = jnp.concatenate([d["rw_w0"][:, 0], d["rw_w0"][:, 1], d["rw_a0"][:, 0], d["rw_a0"][:, 1]],
                                   -1).reshape(L, 1, 1024)
    for n in ("rw_kk", "rw_ka", "rw_ln_g", "rw_ln_b"):
        out[n] = d[n].reshape(L, 1, W_MIX)
    out["rw_rk"] = d["rw_rk"].reshape(L, 1, W_MIX)
    out["rw_g2"] = jnp.concatenate([d["rw_g2"], z(L, 64, W_MIX)], 1).astype(BF16)

    gk2 = d["gla_gk2"]
    z16, z96 = z(L, 16, GLA_HK), z(L, 96, GLA_HK)
    out["gla_wg"] = jnp.stack([jnp.concatenate([gk2[:, 0], z16, z96], 1),
                               jnp.concatenate([z16, gk2[:, 1], z96], 1)], 1).astype(BF16)
    out["gla_gb"] = d["gla_gkb"].reshape(L, N_DIR, 1, GLA_HK)
    out["gla_ln_g"] = jnp.tile(d["gla_ln_g"], (1, N_HEADS)).reshape(L, 1, W_MIX)

    eye = jnp.eye(4, dtype=F32)
    bd = lambda w: jnp.einsum("ldgij,gh->ldgihj", w, eye).reshape(L, N_DIR, W_MIX, W_MIX)
    out["lru_cw"] = d["lru_conv_w"]
    out["lru_cb"] = d["lru_conv_b"].reshape(L, N_DIR, 1, W_MIX)
    out["lru_wax"] = jnp.concatenate([bd(d["lru_wa"]), bd(d["lru_wx"])], -1).astype(BF16)
    out["lru_bax"] = jnp.concatenate([d["lru_ba"], d["lru_bx"]], -1).reshape(L, N_DIR, 1, 2 * W_MIX)
    out["lru_lam"] = d["lru_lam"].reshape(L, N_DIR, 1, W_MIX)

    out["cv_dw_w"] = d["cv_dw_w"]
    for n in ("cv_dw_b", "cv_ln_g", "cv_ln_b", "cv_pw_b"):
        out[n] = d[n].reshape(L, 1, W_MIX)
    out["cv_pw_w"] = d["cv_pw_w"].astype(BF16)

    w_in = d["w_in"]
    o1 = RW_COLS
    o2 = o1 + GLA_COLS
    o3 = o2 + ZC_W
    out["w_in"] = jnp.concatenate([w_in[..., 0:o1], z(L, D_MODEL, ZA_W - o1), _gla_pack(w_in[..., o1:o2]),
                                   w_in[..., o2:o3], w_in[..., o3:]], -1).astype(BF16)
    out["w_out"] = d["w_out"].astype(BF16)
    out["norm1_g"] = d["norm1_g"].reshape(L, 1, D_MODEL)
    out["norm2_g"] = d["norm2_g"].reshape(L, 1, D_MODEL)
    for n in ("e_w1", "e_w3", "e_w2"):
        out[n] = d[n].astype(BF16)
    return out


def kernel(x_prompt, x_sample, state_rwkv, state_gla, state_lru, c, c_ctx, norm1_g, norm2_g, norm_f_g, w_ada, b_ada, w_in, w_out, rw_mu, rw_w0, rw_w2, rw_a0, rw_a2, rw_g2, rw_kk, rw_ka, rw_rk, rw_ln_g, rw_ln_b, gla_gk2, gla_gkb, gla_ln_g, lru_conv_w, lru_conv_b, lru_wa, lru_ba, lru_wx, lru_bx, lru_lam, cv_dw_w, cv_dw_b, cv_ln_g, cv_ln_b, cv_pw_w, cv_pw_b, w_router, b_router, e_w1, e_w3, e_w2):
    d = dict(norm1_g=norm1_g, norm2_g=norm2_g, w_in=w_in, w_out=w_out,
             rw_mu=rw_mu, rw_w0=rw_w0, rw_w2=rw_w2, rw_a0=rw_a0, rw_a2=rw_a2, rw_g2=rw_g2, rw_kk=rw_kk,
             rw_ka=rw_ka, rw_rk=rw_rk, rw_ln_g=rw_ln_g, rw_ln_b=rw_ln_b,
             gla_gk2=gla_gk2, gla_gkb=gla_gkb, gla_ln_g=gla_ln_g,
             lru_conv_w=lru_conv_w, lru_conv_b=lru_conv_b, lru_wa=lru_wa, lru_ba=lru_ba, lru_wx=lru_wx,
             lru_bx=lru_bx, lru_lam=lru_lam,
             cv_dw_w=cv_dw_w, cv_dw_b=cv_dw_b, cv_ln_g=cv_ln_g, cv_ln_b=cv_ln_b, cv_pw_w=cv_pw_w, cv_pw_b=cv_pw_b,
             e_w1=e_w1, e_w3=e_w3, e_w2=e_w2)
    P = _prep_params(d)
    shared = {"w_router_t": w_router.T, "b_router": b_router.reshape(N_EXPERTS, 1)}
    norm_f = norm_f_g.reshape(1, D_MODEL)

    bp, tp, _ = x_prompt.shape
    bs, ts, _ = x_sample.shape
    n_mod = 1 + bs
    mod_rows = -(-n_mod // 8) * 8
    cvec = jnp.concatenate([c_ctx[None], c, jnp.zeros((mod_rows - n_mod, D_MODEL), F32)], 0)
    mod = _ada_mod(cvec, w_ada, b_ada)

    tp_flat = min(bp * tp, 1024)
    bp_flat = bp * tp // tp_flat
    xp = x_prompt.reshape(bp_flat, tp_flat, D_MODEL)
    xs = x_sample
    zero_rw = jnp.zeros((bp, N_DIR, W_MIX, W_MIX), F32)
    zero_gla = jnp.zeros((bp, N_DIR, W_MIX, GLA_HK), F32)
    zero_lru = jnp.zeros((bp, N_DIR, W_MIX), F32)

    def layer(x, m, seq_shape, latent, s_rw, s_gla, s_lru, p, last):
        flat_shape = x.shape[:2]
        zs = _in_proj(x, m, p["norm1_g"], p["w_in"])
        za, zb, zc, zd = (z.reshape(seq_shape + (z.shape[-1],)) for z in zs)
        ya, f_rw = _rwkv_mixer(za, s_rw, p, latent)
        yb, f_gla = _gla_mixer(zb, s_gla, p)
        yc, f_lru = _lru_mixer(zc, s_lru, p)
        yd = _conv_mixer(zd, p, latent)
        ys = [y.reshape(flat_shape + (W_MIX,)) for y in (ya, yb, yc, yd)]
        return _out_moe(x, ys, m, p, norm_f, last), f_rw, f_gla, f_lru

    fin_rw, fin_gla, fin_lru = [], [], []
    for l in range(DEPTH):
        p = {n: a[l] for n, a in P.items()}
        p.update(shared)
        last = l == DEPTH - 1
        m_p = jnp.broadcast_to(mod[l, 0:1], (bp_flat, 6 * D_MODEL)).reshape(bp_flat, 1, 6 * D_MODEL)
        xp, f_rw, f_gla, f_lru = layer(xp, m_p, (bp, tp), False, zero_rw, zero_gla, zero_lru, p, last)
        fin_rw.append(_rw_state_out(f_rw))
        fin_gla.append(_gla_state_out(f_gla))
        fin_lru.append(f_lru)
        m_s = mod[l, 1:1 + bs].reshape(bs, 1, 6 * D_MODEL)
        xs, _, _, _ = layer(xs, m_s, (bs, ts), True, _rw_state_in(state_rwkv[:, l]),
                            _gla_state_in(state_gla[:, l]), state_lru[:, l], p, last)
    return (xp.reshape(bp, tp, D_MODEL), xs, jnp.stack(fin_rw, axis=1), jnp.stack(fin_gla, axis=1),
            jnp.stack(fin_lru, axis=1))
```

```python
import functools

import jax
import jax.numpy as jnp
from jax import lax
from jax.experimental import pallas as pl
from jax.experimental.pallas import tpu as pltpu

F32 = jnp.float32
BF16 = jnp.bfloat16
ACT = jnp.bfloat16

D_MODEL = 1024
DEPTH = 4
GRID_W = 64
N_DIR = 2
W_MIX = 256
N_HEADS = 4
HEAD = W_MIX // N_HEADS
GLA_DK = 32
GLA_HK = N_HEADS * GLA_DK
RW_LN_EPS = 64e-5
NORM_EPS = 1e-6
GLA_GATE_NORM = 16.0
LRU_C = 8.0
LRU_CONV = 4
CV_KERNEL = 31
CV_PAD = (CV_KERNEL - 1) // 2
N_EXPERTS = 16
GROUP_SIZE = 4
N_GROUPS = 4
D_EXPERT = 256
CHUNK = 64
EXP_M_HALF = 0.6065306597126334

LANE = 128
SUBLANE = 8

RW_COLS = 960
GLA_COLS = 800
ZA_W = 1024
ZB_W = 896
ZC_W = 512
ZD_W = 512
P_PAD = ZA_W + ZB_W + ZC_W + ZD_W
RW_R, RW_K, RW_V = slice(0, 256), slice(256, 512), slice(512, 768)
RW_LORA = slice(768, 896)
RW_GATE = slice(896, 1024)
GLA_Q, GLA_K, GLA_V = slice(0, 128), slice(128, 256), slice(256, 512)
GLA_OG = slice(512, 768)
GLA_GATE = slice(768, 896)

VMEM_LIMIT = 48 * 1024 * 1024


def _cparams(sem, vmem=VMEM_LIMIT):
    return pltpu.CompilerParams(dimension_semantics=sem, vmem_limit_bytes=vmem)


def _dot(a, b):
    return jnp.dot(a.astype(BF16), b.astype(BF16), preferred_element_type=F32)


def _dot_nt(a, b):
    return lax.dot_general(a.astype(BF16), b.astype(BF16), (((1,), (1,)), ((), ())),
                           preferred_element_type=F32)


def _dot_tn(a, b):
    return lax.dot_general(a.astype(BF16), b.astype(BF16), (((0,), (0,)), ((), ())),
                           preferred_element_type=F32)


def _split(a):
    hi = a.astype(BF16)
    lo = (a - hi.astype(F32)).astype(BF16)
    return hi, lo


def _dot_x2(a, b_exact):
    hi, lo = _split(a)
    bb = b_exact.astype(BF16)
    return (jnp.dot(hi, bb, preferred_element_type=F32) + jnp.dot(lo, bb, preferred_element_type=F32))


def _dot_left_x2(a_exact, b):
    hi, lo = _split(b)
    aa = a_exact.astype(BF16)
    return (jnp.dot(aa, hi, preferred_element_type=F32) + jnp.dot(aa, lo, preferred_element_type=F32))


def _dot3(a, b):
    ah, al = _split(a)
    bh, bl = _split(b)
    return (jnp.dot(ah, bh, preferred_element_type=F32) + jnp.dot(ah, bl, preferred_element_type=F32)
            + jnp.dot(al, bh, preferred_element_type=F32))


def _dot3_nt(a, b):
    ah, al = _split(a)
    bh, bl = _split(b)
    dn = (((1,), (1,)), ((), ()))
    return (lax.dot_general(ah, bh, dn, preferred_element_type=F32)
            + lax.dot_general(ah, bl, dn, preferred_element_type=F32)
            + lax.dot_general(al, bh, dn, preferred_element_type=F32))


def _iota(shape, axis):
    return lax.broadcasted_iota(jnp.int32, shape, axis)


def _block_mask(rows, cols, rblk, cblk):
    r = _iota((rows, cols), 0) // rblk
    c = _iota((rows, cols), 1) // cblk
    return (r == c).astype(F32)


def _expand(x, bm):
    xb = x.astype(BF16)
    return jnp.concatenate([xb, xb, xb, xb], axis=0) * bm


def _tri(d, strict):
    t = _iota((CHUNK, CHUNK), 0)
    s = _iota((CHUNK, CHUNK), 1)
    if d == 0:
        m = (s < t) if strict else (s <= t)
    else:
        m = (s > t) if strict else (s >= t)
    return m.astype(F32)


def _tri_wide(d, strict):
    t = _iota((CHUNK, N_HEADS * CHUNK), 0)
    s = _iota((CHUNK, N_HEADS * CHUNK), 1) % CHUNK
    if d == 0:
        m = (s < t) if strict else (s <= t)
    else:
        m = (s > t) if strict else (s >= t)
    return m.astype(F32)


ADA_TN = 1536


def _ada_kernel(c_ref, w_ref, b_ref, o_ref):
    c = c_ref[...]
    s = c * jax.nn.sigmoid(c)
    o_ref[0] = _dot3(s, w_ref[0]) + b_ref[0]


def _ada_mod(cvec, w_ada, b_ada):
    rows = cvec.shape[0]
    n_out = w_ada.shape[-1]
    return pl.pallas_call(
        _ada_kernel,
        grid=(DEPTH, n_out // ADA_TN),
        in_specs=[
            pl.BlockSpec((rows, D_MODEL), lambda l, j: (0, 0)),
            pl.BlockSpec((1, D_MODEL, ADA_TN), lambda l, j: (l, 0, j)),
            pl.BlockSpec((1, 1, ADA_TN), lambda l, j: (l, 0, j)),
        ],
        out_specs=pl.BlockSpec((1, rows, ADA_TN), lambda l, j: (l, 0, j)),
        out_shape=jax.ShapeDtypeStruct((DEPTH, rows, n_out), F32),
        compiler_params=_cparams(("arbitrary", "arbitrary")),
        name="ada_mod",
    )(cvec, w_ada, b_ada.reshape(DEPTH, 1, n_out))


IN_TM = 512


def _in_kernel(x_ref, sh_ref, sc_ref, g_ref, w_ref, za_ref, zb_ref, zc_ref, zd_ref):
    x = x_ref[0]
    h = x * lax.rsqrt(jnp.mean(x * x, axis=-1, keepdims=True) + NORM_EPS) * g_ref[...]
    h = (h * (1.0 + sc_ref[0]) + sh_ref[0]).astype(BF16)
    o0, o1, o2 = ZA_W, ZA_W + ZB_W, ZA_W + ZB_W + ZC_W
    za_ref[0] = jnp.dot(h, w_ref[:, 0:o0], preferred_element_type=F32).astype(ACT)
    zb_ref[0] = jnp.dot(h, w_ref[:, o0:o1], preferred_element_type=F32).astype(ACT)
    zc_ref[0] = jnp.dot(h, w_ref[:, o1:o2], preferred_element_type=F32).astype(ACT)
    zd_ref[0] = jnp.dot(h, w_ref[:, o2:P_PAD], preferred_element_type=F32).astype(ACT)


def _in_proj(x, m, g, w):
    b, t, _ = x.shape
    tm = min(IN_TM, t)
    tok = lambda w_: pl.BlockSpec((1, tm, w_), lambda i, j: (i, j, 0))
    mod = lambda k: pl.BlockSpec((1, 1, D_MODEL), lambda i, j, k=k: (i, 0, k))
    return pl.pallas_call(
        _in_kernel,
        grid=(b, t // tm),
        in_specs=[tok(D_MODEL), mod(0), mod(1),
                  pl.BlockSpec((1, D_MODEL), lambda i, j: (0, 0)),
                  pl.BlockSpec((D_MODEL, P_PAD), lambda i, j: (0, 0))],
        out_specs=[tok(ZA_W), tok(ZB_W), tok(ZC_W), tok(ZD_W)],
        out_shape=[jax.ShapeDtypeStruct((b, t, w_), ACT) for w_ in (ZA_W, ZB_W, ZC_W, ZD_W)],
        compiler_params=_cparams(("arbitrary", "arbitrary")),
        name="in_proj",
    )(x, m, m, g, w)


SHIFT_TB = 512


def _row_neighbours(zb, grp):
    t = _iota((2 * grp, grp), 0)
    s = _iota((2 * grp, grp), 1)
    shift = jnp.where((t < grp) & (s == t - 1) | (t >= grp) & (s == t - grp + 1), 1.0, 0.0).astype(BF16)
    res = [jnp.dot(shift, zb[g * grp:(g + 1) * grp].astype(BF16), preferred_element_type=F32)
           for g in range(zb.shape[0] // grp)]
    return (jnp.concatenate([r[:grp] for r in res], axis=0), jnp.concatenate([r[grp:] for r in res], axis=0))


def _shift_grid_kernel(z_ref, up_ref, dn_ref, mu_ref, o_ref, *, nblk):
    i = pl.program_id(1)
    z = z_ref[0].astype(F32)
    tb, c = z.shape
    lane = _iota((tb, c), 1) % 4
    left, right = _row_neighbours(z_ref[0], GRID_W)
    up_halo = jnp.where(i > 0, up_ref[0, 0].astype(F32), 0.0)
    dn_halo = jnp.where(i < nblk - 1, dn_ref[0, 0].astype(F32), 0.0)
    up = jnp.concatenate([up_halo, z[:tb - GRID_W]], axis=0)
    down = jnp.concatenate([z[GRID_W:], dn_halo], axis=0)
    sh = jnp.where(lane == 0, left, jnp.where(lane == 1, right, jnp.where(lane == 2, up, down)))
    o_ref[0] = (z + (sh - z) * mu_ref[...]).astype(ACT)


def _shift_ctx_kernel(z_ref, mu_ref, o_ref):
    z = z_ref[0].astype(F32)
    t, c = z.shape
    lane = _iota((t, c), 1) % 2
    prev, nxt = _row_neighbours(z_ref[0], t)
    sh = jnp.where(lane == 0, prev, nxt)
    o_ref[0] = (z + (sh - z) * mu_ref[...]).astype(ACT)


def _rw_shift(za, mu, latent):
    b, t, c = za.shape
    mu_spec = pl.BlockSpec((1, c), lambda *_: (0, 0))
    if not latent:
        return pl.pallas_call(
            _shift_ctx_kernel,
            grid=(b,),
            in_specs=[pl.BlockSpec((1, t, c), lambda i: (i, 0, 0)), mu_spec],
            out_specs=pl.BlockSpec((1, t, c), lambda i: (i, 0, 0)),
            out_shape=jax.ShapeDtypeStruct((b, t, c), ACT),
            compiler_params=_cparams(("arbitrary",)),
            name="rw_shift_ctx",
        )(za, mu)
    tb = min(SHIFT_TB, t)
    nblk = t // tb
    rpb = tb // GRID_W
    nrow = t // GRID_W
    z4 = za.reshape(b, nrow, GRID_W, c)
    return pl.pallas_call(
        functools.partial(_shift_grid_kernel, nblk=nblk),
        grid=(b, nblk),
        in_specs=[
            pl.BlockSpec((1, tb, c), lambda i, j: (i, j, 0)),
            pl.BlockSpec((1, 1, GRID_W, c), lambda i, j: (i, jnp.maximum(j * rpb - 1, 0), 0, 0)),
            pl.BlockSpec((1, 1, GRID_W, c), lambda i, j: (i, jnp.minimum(j * rpb + rpb, nrow - 1), 0, 0)),
            mu_spec,
        ],
        out_specs=pl.BlockSpec((1, tb, c), lambda i, j: (i, j, 0)),
        out_shape=jax.ShapeDtypeStruct((b, t, c), ACT),
        compiler_params=_cparams(("arbitrary", "arbitrary")),
        name="rw_shift_grid",
    )(za, z4, z4, mu)


SCAN_TB = 512
PREP_SPLIT = 2
PREP_LOCKSTEP = 4
PREP_SKEW = 3
PREP_GROUP = 8


def _rw_scan_kernel(z_ref, s0_ref, wl_ref, lb_ref, kkp_ref, kap_ref, lng_ref, lnb_ref, rk_ref, g2_ref,
                    o_ref, st_ref,
                    y_ref, lw_s, a_s, b_s, kd_s, r2_s, mc_s, ds_s, gt_s, *, nchunk, nblk):
    i = pl.program_id(1)
    tb = nchunk * CHUNK

    @pl.when(i == 0)
    def _():
        st_ref[...] = s0_ref[...]
        y_ref[...] = jnp.zeros(y_ref.shape, F32)

    def y_rows(d, c):
        blk = i if d == 0 else nblk - 1 - i
        return pl.ds(pl.multiple_of((blk * nchunk + c) * CHUNK, CHUNK), CHUNK)

    def blk_rows(d):
        blk = i if d == 0 else nblk - 1 - i
        return pl.ds(pl.multiple_of(blk * tb, tb), tb)

    ones_blk = _block_mask(W_MIX, W_MIX, HEAD, HEAD)
    is_decay_lora = _iota((1, LANE), 1) < LANE // 2

    for d in range(N_DIR):
        k = z_ref[0, blk_rows(d), RW_K].astype(F32)
        la = z_ref[0, blk_rows(d), RW_LORA].astype(F32)
        la_t = jnp.where(is_decay_lora, jnp.tanh(la), la)
        w_cols = slice(W_MIX * d, W_MIX * (d + 1))
        a_cols = slice(W_MIX * (N_DIR + d), W_MIX * (N_DIR + d + 1))
        wraw = _dot(la_t, wl_ref[:, w_cols]) + lb_ref[:, w_cols]
        araw = _dot(la_t, wl_ref[:, a_cols]) + lb_ref[:, a_cols]
        lw_s[d] = jax.nn.sigmoid(wraw) * (-EXP_M_HALF)
        icl = jax.nn.sigmoid(araw)
        kkv = k * kkp_ref[...]
        ss = _dot_x2(kkv * kkv, ones_blk)
        kk = kkv * lax.rsqrt(ss + 1e-12)
        a_s[d] = -kk
        b_s[d] = kk * icl
        kd_s[d] = k * (1.0 + (icl - 1.0) * kap_ref[...])

    bm = _block_mask(N_HEADS * CHUNK, W_MIX, CHUNK, HEAD).astype(BF16)
    tri = [_tri(d, False) for d in range(N_DIR)]
    tw_strict = [_tri_wide(d, True) for d in range(N_DIR)]
    tw_incl = [_tri_wide(d, False) for d in range(N_DIR)]
    eye_w = (_iota((CHUNK, N_HEADS * CHUNK), 0) == _iota((CHUNK, N_HEADS * CHUNK), 1) % CHUNK).astype(F32)

    def rows_of(c, n):
        return pl.ds(pl.multiple_of(c * n, n), n)

    group = min(PREP_GROUP, nchunk)

    each = lambda fn, *ls: [fn(*xs) for xs in zip(*ls)]

    def prep_stages(ch):
        ds_ = [d for d, _ in ch]
        st = {}

        def front():
            load = lambda ref: [ref[d, rows_of(c, CHUNK), :] for d, c in ch]
            lw, a, st["b"], st["kd"] = load(lw_s), load(a_s), load(b_s), load(kd_s)
            r = [z_ref[0, y_rows(d, c), RW_R].astype(F32) for d, c in ch]
            st["v"] = [z_ref[0, y_rows(d, c), RW_V].astype(F32) for d, c in ch]
            cum = each(lambda d, x: _dot_left_x2(tri[d], x), ds_, lw)
            st["tot"] = each(lambda d, x: x[CHUNK - 1:CHUNK, :] if d == 0 else x[0:1, :], ds_, cum)
            st["g_inv"] = each(lambda x: jnp.exp(-x), cum)
            st["g_rem"] = each(lambda t_, x: jnp.exp(t_ - x), st["tot"], cum)
            st["at"] = each(lambda a_, x, l: a_ * jnp.exp(x - l), a, cum, lw)
            st["rt"] = each(lambda r_, x: r_ * jnp.exp(x), r, cum)

        def gram():
            x = each(lambda p, q: jnp.concatenate([p, q], axis=0), st["at"], st["rt"])
            gb = each(lambda x_, b_, g: _dot_nt(x_, _expand(b_ * g, bm)), x, st["b"], st["g_inv"])
            gk = each(lambda x_, k_, g: _dot_nt(x_, _expand(k_ * g, bm)), x, st["kd"], st["g_inv"])
            st["vbd"] = each(lambda v_: _expand(v_, bm), st["v"])
            st["apow"] = each(lambda d, g: g[0:CHUNK] * tw_strict[d], ds_, gb)
            st["av"] = each(lambda d, g, vb: _dot(g[0:CHUNK] * tw_strict[d], vb), ds_, gk, st["vbd"])
            st["tw"] = each(lambda m: eye_w + m, st["apow"])
            st["r_b"] = each(lambda d, g: g[CHUNK:] * tw_incl[d], ds_, gb)
            st["r_k"] = each(lambda d, g: g[CHUNK:] * tw_incl[d], ds_, gk)

        def double():
            st["apow"] = each(lambda m: _dot(m, _expand(m, bm)), st["apow"])
            st["tw"] = each(lambda t_, m: t_ + _dot(m, _expand(t_, bm)), st["tw"], st["apow"])

        def apply():
            st["a2"] = each(lambda t_, m: _dot(t_, _expand(m, bm)), st["tw"], st["at"])
            st["u0"] = each(lambda t_, m: _dot(t_, _expand(m, bm)), st["tw"], st["av"])

        def finish():
            r2 = each(lambda rt_, rb, m: rt_ + _dot(rb, _expand(m, bm)), st["rt"], st["r_b"], st["a2"])
            y0 = each(lambda rb, u, rk, vb: _dot(rb, _expand(u, bm)) + _dot(rk, vb),
                      st["r_b"], st["u0"], st["r_k"], st["vbd"])
            bl = each(lambda b_, g: b_ * g, st["b"], st["g_rem"])
            kl = each(lambda k_, g: k_ * g, st["kd"], st["g_rem"])
            mc = each(lambda m, bl_: _dot_tn(m, bl_) * ones_blk, st["a2"], bl)
            ds0 = each(lambda u, v_, bl_, kl_: _dot_tn(jnp.concatenate([u, v_], axis=0),
                                                       jnp.concatenate([bl_, kl_], axis=0)) * ones_blk,
                       st["u0"], st["v"], bl, kl)
            for n, (d, c) in enumerate(ch):
                rows = rows_of(c, CHUNK)
                y_ref[0, y_rows(d, c), :] += y0[n]
                r2_s[d, rows, :] = r2[n].astype(BF16)
                mc_s[d, rows_of(c, W_MIX), :] = mc[n].astype(BF16)
                ds_s[d, rows_of(c, W_MIX), :] = ds0[n]
                gt_s[d, rows_of(c, SUBLANE), :] = jnp.broadcast_to(jnp.exp(st["tot"][n]), (SUBLANE, W_MIX))

        return [front, gram] + [double] * 5 + [apply, finish]

    def prep_body(gi, carry):
        half = group // PREP_SPLIT if group >= PREP_SPLIT * PREP_LOCKSTEP else group
        halves = [prep_stages([(d, gi * group + j) for j in range(h0, min(h0 + half, group)) for d in range(N_DIR)])
                  for h0 in range(0, group, half)]
        n_stage = len(halves[0])
        for step in range(n_stage + PREP_SKEW * (len(halves) - 1)):
            for hi, stages in enumerate(halves):
                k = step - PREP_SKEW * hi
                if 0 <= k < n_stage:
                    stages[k]()
        return carry

    lax.fori_loop(0, nchunk // group, prep_body, 0)

    def scan_body(cc, carry):
        cs = [cc, nchunk - 1 - cc]
        s = [st_ref[0, d] for d in range(N_DIR)]
        sb = [x.astype(BF16) for x in s]
        upd = [_dot(sb[d], mc_s[d, rows_of(cs[d], W_MIX), :]) for d in range(N_DIR)]
        ys = [_dot_nt(r2_s[d, rows_of(cs[d], CHUNK), :], sb[d]) for d in range(N_DIR)]
        for d in range(N_DIR):
            st_ref[0, d] = (s[d] * gt_s[d, rows_of(cs[d], SUBLANE), :][0:1, :] + upd[d]
                            + ds_s[d, rows_of(cs[d], W_MIX), :])
            y_ref[0, y_rows(d, cs[d]), :] += ys[d]
        return carry

    lax.fori_loop(0, nchunk, scan_body, 0)

    @pl.when(i == nblk - 1)
    def _():
        def epilogue(j, carry):
            rows = pl.ds(pl.multiple_of(j * tb, tb), tb)
            y = y_ref[0, rows, :]
            mu = _dot_x2(y, ones_blk) * (1.0 / HEAD)
            yc = y - mu
            var = _dot_x2(yc * yc, ones_blk) * (1.0 / HEAD)
            yn = yc * lax.rsqrt(var + RW_LN_EPS) * lng_ref[...] + lnb_ref[...]
            r = z_ref[0, rows, RW_R].astype(F32)
            k = z_ref[0, rows, RW_K].astype(F32)
            v = z_ref[0, rows, RW_V].astype(F32)
            gl = z_ref[0, rows, RW_GATE].astype(F32)
            bonus = _dot_x2(r * k * rk_ref[...], ones_blk) * v
            gate = _dot(jax.nn.sigmoid(gl), g2_ref[...])
            o_ref[0, rows, :] = ((yn + bonus) * gate).astype(ACT)
            return carry

        lax.fori_loop(0, nblk, epilogue, 0)


def _rw_scan(zs, s0, p):
    b, t, c = zs.shape
    tb = min(SCAN_TB, t)
    nblk = t // tb
    nchunk = tb // CHUNK
    vec = lambda n: pl.BlockSpec((1, n), lambda i, j: (0, 0))
    seq = lambda w_: pl.BlockSpec((1, t, w_), lambda i, j: (i, 0, 0))
    st_spec = pl.BlockSpec((1, N_DIR, W_MIX, W_MIX), lambda i, j: (i, 0, 0, 0))
    return pl.pallas_call(
        functools.partial(_rw_scan_kernel, nchunk=nchunk, nblk=nblk),
        grid=(b, nblk),
        in_specs=[
            seq(c), st_spec,
            pl.BlockSpec((LANE, 2 * N_DIR * W_MIX), lambda i, j: (0, 0)),
            vec(1024), vec(W_MIX), vec(W_MIX), vec(W_MIX), vec(W_MIX), vec(W_MIX),
            pl.BlockSpec((LANE, W_MIX), lambda i, j: (0, 0)),
        ],
        out_specs=[seq(W_MIX), st_spec],
        out_shape=[
            jax.ShapeDtypeStruct((b, t, W_MIX), ACT),
            jax.ShapeDtypeStruct((b, N_DIR, W_MIX, W_MIX), F32),
        ],
        scratch_shapes=[pltpu.VMEM((1, t, W_MIX), F32)]
        + [pltpu.VMEM((N_DIR, tb, W_MIX), F32) for _ in range(4)] + [
            pltpu.VMEM((N_DIR, tb, W_MIX), BF16),
            pltpu.VMEM((N_DIR, nchunk * W_MIX, W_MIX), BF16),
            pltpu.VMEM((N_DIR, nchunk * W_MIX, W_MIX), F32),
            pltpu.VMEM((N_DIR, nchunk * 8, W_MIX), F32),
        ],
        compiler_params=_cparams(("arbitrary", "arbitrary")),
        name="rw_scan",
    )(zs, s0, p["rw_wl"], p["rw_lb"], p["rw_kk"], p["rw_ka"], p["rw_ln_g"], p["rw_ln_b"], p["rw_rk"], p["rw_g2"])


def _rwkv_mixer(za, s0_bd, p, latent):
    zs = _rw_shift(za, p["rw_mu"], latent)
    return _rw_scan(zs, s0_bd, p)


def _gla_scan_kernel(z_ref, s0_ref, wg_ref, gb_ref, lng_ref, o_ref, st_ref,
                     y_ref, la_s, qe_s, ds_s, dec_s, sp_s, *, nchunk, nblk):
    i = pl.program_id(1)
    tb = nchunk * CHUNK

    @pl.when(i == 0)
    def _():
        st_ref[...] = s0_ref[...]
        y_ref[...] = jnp.zeros(y_ref.shape, F32)

    def y_rows(d, c):
        blk = i if d == 0 else nblk - 1 - i
        return pl.ds(pl.multiple_of((blk * nchunk + c) * CHUNK, CHUNK), CHUNK)

    for d in range(N_DIR):
        blk = i if d == 0 else nblk - 1 - i
        logit = _dot(z_ref[0, pl.ds(pl.multiple_of(blk * tb, tb), tb), GLA_GATE], wg_ref[d]) + gb_ref[d]
        la_s[d] = jax.nn.log_sigmoid(logit) * (1.0 / GLA_GATE_NORM)

    bm_k = _block_mask(N_HEADS * CHUNK, GLA_HK, CHUNK, GLA_DK).astype(BF16)
    bm_v = _block_mask(N_HEADS * CHUNK, W_MIX, CHUNK, HEAD).astype(BF16)
    bm_s = _block_mask(W_MIX, GLA_HK, HEAD, GLA_DK)
    tri = [_tri(d, False) for d in range(N_DIR)]
    tw_incl = [_tri_wide(d, False) for d in range(N_DIR)]

    def rows_of(c, n):
        return pl.ds(pl.multiple_of(c * n, n), n)

    group = min(PREP_GROUP, nchunk)

    def prep_body(gi, carry):
        ch = [(d, gi * group + j) for j in range(group) for d in range(N_DIR)]
        each = lambda fn, *ls: [fn(*xs) for xs in zip(*ls)]
        ds_ = [d for d, _ in ch]
        la = [la_s[d, rows_of(c, CHUNK), :] for d, c in ch]
        q = [z_ref[0, y_rows(d, c), GLA_Q].astype(F32) * (GLA_DK ** -0.5) for d, c in ch]
        k = [z_ref[0, y_rows(d, c), GLA_K].astype(F32) for d, c in ch]
        v = [z_ref[0, y_rows(d, c), GLA_V].astype(F32) for d, c in ch]
        cum = each(lambda d, x: _dot_left_x2(tri[d], x), ds_, la)
        last = each(lambda d, x: x[CHUNK - 1:CHUNK, :] if d == 0 else x[0:1, :], ds_, cum)
        qe = each(lambda q_, x: q_ * jnp.exp(x), q, cum)
        ke = each(lambda k_, x: k_ * jnp.exp(-x), k, cum)
        kl = each(lambda k_, l, x: k_ * jnp.exp(l - x), k, last, cum)
        att = each(lambda d, q_, k_: _dot_nt(q_, _expand(k_, bm_k)) * tw_incl[d], ds_, qe, ke)
        o = each(lambda a_, v_: _dot(a_, _expand(v_, bm_v)), att, v)
        dst = each(lambda v_, k_: _dot_tn(v_, k_) * bm_s, v, kl)
        for n, (d, c) in enumerate(ch):
            y_ref[0, y_rows(d, c), :] += o[n]
            qe_s[d, rows_of(c, CHUNK), :] = qe[n].astype(BF16)
            ds_s[d, rows_of(c, W_MIX), :] = dst[n]
            dec_s[d, rows_of(c, SUBLANE), :] = jnp.broadcast_to(jnp.exp(last[n]), (SUBLANE, GLA_HK))
        return carry

    lax.fori_loop(0, nchunk // group, prep_body, 0)

    def state_body(cc, carry):
        cs = [cc, nchunk - 1 - cc]
        for d in range(N_DIR):
            s = st_ref[0, d]
            sp_s[d, rows_of(cs[d], W_MIX), :] = s.astype(BF16)
            st_ref[0, d] = s * dec_s[d, rows_of(cs[d], SUBLANE), :][0:1, :] + ds_s[d, rows_of(cs[d], W_MIX), :]
        return carry

    lax.fori_loop(0, nchunk, state_body, 0)

    def out_body(gi, carry):
        ch = [(d, gi * group + j) for j in range(group) for d in range(N_DIR)]
        ys = [_dot_nt(qe_s[d, rows_of(c, CHUNK), :], sp_s[d, rows_of(c, W_MIX), :]) for d, c in ch]
        for n, (d, c) in enumerate(ch):
            y_ref[0, y_rows(d, c), :] += ys[n]
        return carry

    lax.fori_loop(0, nchunk // group, out_body, 0)

    @pl.when(i == nblk - 1)
    def _():
        ones_blk = _block_mask(W_MIX, W_MIX, HEAD, HEAD)

        def epilogue(j, carry):
            rows = pl.ds(pl.multiple_of(j * tb, tb), tb)
            y = y_ref[0, rows, :]
            ms = _dot_x2(y * y, ones_blk) * (1.0 / HEAD)
            y = y * lax.rsqrt(ms + NORM_EPS) * lng_ref[...]
            og = z_ref[0, rows, GLA_OG].astype(F32)
            o_ref[0, rows, :] = (y * (og * jax.nn.sigmoid(og))).astype(ACT)
            return carry

        lax.fori_loop(0, nblk, epilogue, 0)


def _gla_mixer(zb, s0, p):
    b, t, c = zb.shape
    tb = min(SCAN_TB, t)
    nblk = t // tb
    seq = lambda w_: pl.BlockSpec((1, t, w_), lambda i, j: (i, 0, 0))
    st_spec = pl.BlockSpec((1, N_DIR, W_MIX, GLA_HK), lambda i, j: (i, 0, 0, 0))
    nchunk = tb // CHUNK
    return pl.pallas_call(
        functools.partial(_gla_scan_kernel, nchunk=nchunk, nblk=nblk),
        grid=(b, nblk),
        in_specs=[
            seq(c), st_spec,
            pl.BlockSpec((N_DIR, LANE, GLA_HK), lambda i, j: (0, 0, 0)),
            pl.BlockSpec((N_DIR, 1, GLA_HK), lambda i, j: (0, 0, 0)),
            pl.BlockSpec((1, W_MIX), lambda i, j: (0, 0)),
        ],
        out_specs=[seq(W_MIX), st_spec],
        out_shape=[
            jax.ShapeDtypeStruct((b, t, W_MIX), ACT),
            jax.ShapeDtypeStruct((b, N_DIR, W_MIX, GLA_HK), F32),
        ],
        scratch_shapes=[
            pltpu.VMEM((1, t, W_MIX), F32),
            pltpu.VMEM((N_DIR, tb, GLA_HK), F32),
            pltpu.VMEM((N_DIR, tb, GLA_HK), BF16),
            pltpu.VMEM((N_DIR, nchunk * W_MIX, GLA_HK), F32),
            pltpu.VMEM((N_DIR, nchunk * SUBLANE, GLA_HK), F32),
            pltpu.VMEM((N_DIR, nchunk * W_MIX, GLA_HK), BF16),
        ],
        compiler_params=_cparams(("arbitrary", "arbitrary")),
        name="gla_scan",
    )(zb, s0, p["gla_wg"], p["gla_gb"], p["gla_ln_g"])


LRU_TT = 256
LRU_HALO = 8
LRU_SUB = 8


def _lru_kernel(z_ref, h0_ref, cw_ref, cb_ref, wax_ref, bax_ref, lam_ref, y_ref, hf_ref, xpad, hfwd, *, t):
    tt = min(LRU_TT, t)
    ntile = t // tt
    xpad[0:LRU_HALO, :] = jnp.zeros((LRU_HALO, W_MIX), F32)
    xpad[LRU_HALO + t:2 * LRU_HALO + t, :] = jnp.zeros((LRU_HALO, W_MIX), F32)

    def fill(j, carry):
        base = pl.multiple_of(j * tt, tt)
        xpad[pl.ds(base + LRU_HALO, tt), :] = z_ref[0, pl.ds(base, tt), 0:W_MIX].astype(F32)
        return carry

    lax.fori_loop(0, ntile, fill, 0)
    sub = _iota((tt // LRU_SUB, LRU_SUB, W_MIX), 1)

    def tile_scan(j, h, d):
        base = pl.multiple_of(j * tt, tt)
        win = xpad[pl.ds(base, tt + 2 * LRU_HALO), :]
        xc = jnp.zeros((tt, W_MIX), F32) + cb_ref[d]
        for tap in range(LRU_CONV):
            off = LRU_HALO - (LRU_CONV - 1) + tap if d == 0 else LRU_HALO + (LRU_CONV - 1) - tap
            xc = xc + cw_ref[d, tap:tap + 1, :] * win[off:off + tt, :]
        g = _dot(xc, wax_ref[d]) + bax_ref[d]
        gr = jax.nn.sigmoid(g[:, 0:W_MIX])
        gi = jax.nn.sigmoid(g[:, W_MIX:2 * W_MIX])
        log_a = -LRU_C * gr * jax.nn.softplus(-lam_ref[d])
        a = jnp.exp(log_a)
        bv = jnp.sqrt(1.0 - jnp.exp(2.0 * log_a)) * gi * xc
        ngrp = tt // LRU_SUB
        a = a.reshape(ngrp, LRU_SUB, W_MIX)
        bv = bv.reshape(ngrp, LRU_SUB, W_MIX)
        s = 1
        while s < LRU_SUB:
            shift = s if d == 0 else LRU_SUB - s
            keep = (sub >= s) if d == 0 else (sub < LRU_SUB - s)
            a_sh = jnp.where(keep, pltpu.roll(a, shift, 1), 1.0)
            b_sh = jnp.where(keep, pltpu.roll(bv, shift, 1), 0.0)
            bv = a * b_sh + bv
            a = a * a_sh
            s *= 2
        out = [None] * ngrp
        for gg in range(ngrp):
            g = gg if d == 0 else ngrp - 1 - gg
            hg = a[g] * h + bv[g]
            out[g] = hg
            h = hg[LRU_SUB - 1:LRU_SUB, :] if d == 0 else hg[0:1, :]
        return jnp.concatenate(out, axis=0), h, base

    def fwd(j, h):
        ht, h, base = tile_scan(j, h, 0)
        hfwd[pl.ds(base, tt), :] = ht
        return h

    h_end = lax.fori_loop(0, ntile, fwd, h0_ref[0, 0:1, :])
    hf_ref[0, 0:1, :] = h_end

    def bwd(jj, h):
        j = ntile - 1 - jj
        ht, h, base = tile_scan(j, h, 1)
        gb = z_ref[0, pl.ds(base, tt), W_MIX:2 * W_MIX].astype(F32)
        y_ref[0, pl.ds(base, tt), :] = ((hfwd[pl.ds(base, tt), :] + ht) * jax.nn.gelu(gb)).astype(ACT)
        return h

    h_end = lax.fori_loop(0, ntile, bwd, h0_ref[0, 1:2, :])
    hf_ref[0, 1:2, :] = h_end


def _lru_mixer(zc, h0, p):
    b, t, c = zc.shape
    full = lambda *s: pl.BlockSpec(s, lambda i: (0,) * len(s))
    return pl.pallas_call(
        functools.partial(_lru_kernel, t=t),
        grid=(b,),
        in_specs=[
            pl.BlockSpec((1, t, c), lambda i: (i, 0, 0)),
            pl.BlockSpec((1, N_DIR, W_MIX), lambda i: (i, 0, 0)),
            full(N_DIR, LRU_CONV, W_MIX), full(N_DIR, 1, W_MIX),
            full(N_DIR, W_MIX, 2 * W_MIX), full(N_DIR, 1, 2 * W_MIX), full(N_DIR, 1, W_MIX),
        ],
        out_specs=[pl.BlockSpec((1, t, W_MIX), lambda i: (i, 0, 0)),
                   pl.BlockSpec((1, N_DIR, W_MIX), lambda i: (i, 0, 0))],
        out_shape=[jax.ShapeDtypeStruct((b, t, W_MIX), ACT), jax.ShapeDtypeStruct((b, N_DIR, W_MIX), F32)],
        scratch_shapes=[pltpu.VMEM((t + 2 * LRU_HALO, W_MIX), F32), pltpu.VMEM((t, W_MIX), F32)],
        compiler_params=_cparams(("arbitrary",)),
        name="lru",
    )(zc, h0, p["lru_cw"], p["lru_cb"], p["lru_wax"], p["lru_bax"], p["lru_lam"])


CV_TT = 256
CV_WIN = 16


def _conv_kernel(z_ref, dw_ref, dwb_ref, lng_ref, lnb_ref, pw_ref, pwb_ref, y_ref, upad, *, t, latent):
    tt = min(CV_TT, t)
    ntile = t // tt
    pad = CV_PAD * GRID_W if latent else CV_WIN
    upad[0:pad, :] = jnp.zeros((pad, W_MIX), F32)
    upad[pad + t:2 * pad + t, :] = jnp.zeros((pad, W_MIX), F32)

    def fill(j, carry):
        base = pl.multiple_of(j * tt, tt)
        z = z_ref[0, pl.ds(base, tt), :].astype(F32)
        upad[pl.ds(base + pad, tt), :] = z[:, 0:W_MIX] * jax.nn.sigmoid(z[:, W_MIX:2 * W_MIX])
        return carry

    lax.fori_loop(0, ntile, fill, 0)

    def tile(j, carry):
        base = pl.multiple_of(j * tt, tt)
        if latent:
            half = W_MIX // 2
            col = _iota((tt, half), 0) % GRID_W
            win = upad[pl.ds(pl.multiple_of(base + pad - CV_WIN, CV_WIN), tt + 2 * CV_WIN), 0:half]
            accw = jnp.zeros((tt, half), F32)
            acch = jnp.zeros((tt, half), F32)
            for tap in range(CV_KERNEL):
                dlt = tap - CV_PAD
                x = win[CV_WIN + dlt:CV_WIN + dlt + tt, :]
                if dlt < 0:
                    x = jnp.where(col >= -dlt, x, 0.0)
                elif dlt > 0:
                    x = jnp.where(col < GRID_W - dlt, x, 0.0)
                accw = accw + dw_ref[tap:tap + 1, 0:half] * x
                rows = pl.ds(pl.multiple_of(base + pad + dlt * GRID_W, GRID_W), tt)
                acch = acch + dw_ref[tap:tap + 1, half:W_MIX] * upad[rows, half:W_MIX]
            u = jnp.concatenate([accw, acch], axis=1)
        else:
            win = upad[pl.ds(base + pad - CV_WIN, tt + 2 * CV_WIN), :]
            u = jnp.zeros((tt, W_MIX), F32)
            for tap in range(CV_KERNEL):
                dlt = tap - CV_PAD
                u = u + dw_ref[tap:tap + 1, :] * win[CV_WIN + dlt:CV_WIN + dlt + tt, :]
        u = u + dwb_ref[...]
        mu = jnp.mean(u, axis=-1, keepdims=True)
        uc = u - mu
        var = jnp.mean(uc * uc, axis=-1, keepdims=True)
        un = uc * lax.rsqrt(var + 1e-5) * lng_ref[...] + lnb_ref[...]
        un = un * jax.nn.sigmoid(un)
        y_ref[0, pl.ds(base, tt), :] = (_dot(un, pw_ref[...]) + pwb_ref[...]).astype(ACT)
        return carry

    lax.fori_loop(0, ntile, tile, 0)


def _conv_mixer(zd, p, latent):
    b, t, c = zd.shape
    pad = CV_PAD * GRID_W if latent else CV_WIN
    full = lambda *s: pl.BlockSpec(s, lambda i: (0,) * len(s))
    vec = full(1, W_MIX)
    return pl.pallas_call(
        functools.partial(_conv_kernel, t=t, latent=latent),
        grid=(b,),
        in_specs=[pl.BlockSpec((1, t, c), lambda i: (i, 0, 0)),
                  full(CV_KERNEL, W_MIX), vec, vec, vec, full(W_MIX, W_MIX), vec],
        out_specs=pl.BlockSpec((1, t, W_MIX), lambda i: (i, 0, 0)),
        out_shape=jax.ShapeDtypeStruct((b, t, W_MIX), ACT),
        scratch_shapes=[pltpu.VMEM((t + 2 * pad, W_MIX), F32)],
        compiler_params=_cparams(("arbitrary",)),
        name="conv_grid" if latent else "conv_ctx",
    )(zd, p["cv_dw_w"], p["cv_dw_b"], p["cv_ln_g"], p["cv_ln_b"], p["cv_pw_w"], p["cv_pw_b"])


MOE_TM = 512
MOE_RB = 128
E_PAD = LANE


def _route(sel, scores):
    grp = []
    for g in range(N_GROUPS):
        s = sel[GROUP_SIZE * g:GROUP_SIZE * (g + 1)]
        best_pair = None
        for i in range(GROUP_SIZE):
            for j in range(i + 1, GROUP_SIZE):
                pair = s[i] + s[j]
                best_pair = pair if best_pair is None else jnp.maximum(best_pair, pair)
        grp.append(best_pair)
    best = jnp.zeros_like(grp[0], dtype=jnp.int32)
    top = grp[0]
    for g in range(1, N_GROUPS):
        better = grp[g] > top
        best = jnp.where(better, g, best)
        top = jnp.where(better, grp[g], top)
    neg = jnp.full_like(sel[0], -jnp.inf)
    msel = [jnp.where(best == e // GROUP_SIZE, sel[e], neg) for e in range(N_EXPERTS)]
    picks = []
    for _ in range(2):
        idx = jnp.zeros_like(best)
        top = msel[0]
        for e in range(1, N_EXPERTS):
            better = msel[e] > top
            idx = jnp.where(better, e, idx)
            top = jnp.where(better, msel[e], top)
        picks.append(idx)
        msel = [jnp.where(idx == e, neg, msel[e]) for e in range(N_EXPERTS)]
    chosen = [jnp.where((picks[0] == e) | (picks[1] == e), scores[e], 0.0) for e in range(N_EXPERTS)]
    total = chosen[0]
    for e in range(1, N_EXPERTS):
        total = total + chosen[e]
    return [ch / total for ch in chosen], best


def _moe_kernel(x_ref, ya_ref, yb_ref, yc_ref, yd_ref, wo_ref, g1_ref, sh2_ref, sc2_ref, g2_ref, n2_ref,
                wr_ref, br_ref, w1_ref, w3_ref, w2_ref, nf_ref, o_ref,
                x1_s, h2_s, gt_s, gate_s, acc_s, he_s, og_s, pmt_s, seg_s, *, final_norm):
    grp = pl.program_id(2)
    tm = x1_s.shape[0]

    @pl.when(grp == 0)
    def _():
        y = sum(_dot(y_ref[0], wo_ref[W_MIX * m:W_MIX * (m + 1), :])
                for m, y_ref in enumerate((ya_ref, yb_ref, yc_ref, yd_ref)))
        x1 = x_ref[0] + g1_ref[0] * y
        x1_s[...] = x1
        h2 = x1 * lax.rsqrt(jnp.mean(x1 * x1, axis=-1, keepdims=True) + NORM_EPS) * n2_ref[...]
        h2 = h2 * (1.0 + sc2_ref[0]) + sh2_ref[0]
        logits = _dot3_nt(wr_ref[...], h2)
        scores = jax.nn.sigmoid(logits)
        selm = scores + br_ref[...]
        gates, best = _route([selm[i:i + 1, :] for i in range(N_EXPERTS)],
                             [scores[i:i + 1, :] for i in range(N_EXPERTS)])
        gt_s[...] = jnp.zeros(gt_s.shape, F32)
        for i in range(N_EXPERTS):
            gt_s[i:i + 1, :] = gates[i]

        og = [jnp.where(best == g, 1.0, 0.0) for g in range(N_GROUPS)]
        og_s[...] = jnp.zeros(og_s.shape, F32)
        start = jnp.int32(0)
        starts = []
        for g in range(N_GROUPS):
            og_s[g:g + 1, :] = og[g]
            starts.append(start)
            seg_s[g] = start
            start = start + jnp.sum(og[g]).astype(jnp.int32)
            seg_s[N_GROUPS + g] = start
        before = jnp.where(_iota((tm, tm), 0) < _iota((tm, tm), 1), 1.0, 0.0).astype(BF16)
        rank = jnp.dot(og_s[...].astype(BF16), before, preferred_element_type=F32)
        pos = og[0] * (starts[0].astype(F32) + rank[0:1, :])
        for g in range(1, N_GROUPS):
            pos = pos + og[g] * (starts[g].astype(F32) + rank[g:g + 1, :])
        gt_s[N_EXPERTS:N_EXPERTS + 1, :] = pos
        gate_tok = gt_s[...].T
        slot_l = _iota((tm, tm), 1).astype(F32)
        slot_s = _iota((tm, tm), 0).astype(F32)
        pmt_s[...] = jnp.where(slot_l == gate_tok[:, N_EXPERTS:N_EXPERTS + 1], 1.0, 0.0).astype(BF16)
        pm = jnp.where(slot_s == pos, 1.0, 0.0).astype(BF16)
        h2_s[...] = jnp.dot(pm, h2.astype(BF16), preferred_element_type=F32).astype(BF16)
        gate_s[...] = _dot_left_x2(pm, gate_tok)
        acc_s[...] = jnp.zeros(acc_s.shape, F32)

    pick = (_iota((E_PAD, GROUP_SIZE * LANE), 0)
            == GROUP_SIZE * grp + _iota((E_PAD, GROUP_SIZE * LANE), 1) // LANE).astype(F32)

    def block(rb, nrows):
        rows = pl.ds(pl.multiple_of(rb * MOE_RB, MOE_RB), nrows)
        gsel = _dot_x2(gate_s[rows, :], pick)
        h2 = h2_s[rows, :]
        for j in range(GROUP_SIZE):
            he = jnp.dot(h2, w1_ref[j], preferred_element_type=F32)
            he = he * jax.nn.sigmoid(he) * jnp.dot(h2, w3_ref[j], preferred_element_type=F32)
            g = gsel[:, LANE * j:LANE * (j + 1)]
            he_s[rows, D_EXPERT * j:D_EXPERT * (j + 1)] = (he * jnp.concatenate([g, g], axis=1)).astype(BF16)
        w2g = w2_ref[...].reshape(GROUP_SIZE * D_EXPERT, D_MODEL)
        acc_s[rows, :] += jnp.dot(he_s[rows, :], w2g, preferred_element_type=F32)

    first = seg_s[grp] // MOE_RB
    last = (seg_s[N_GROUPS + grp] + (MOE_RB - 1)) // MOE_RB

    def pair(pi, carry):
        block(first + 2 * pi, 2 * MOE_RB)
        return carry

    lax.fori_loop(0, (last - first) // 2, pair, 0)

    @pl.when((last - first) % 2 == 1)
    def _():
        block(last - 1, MOE_RB)

    @pl.when(grp == N_GROUPS - 1)
    def _():
        x2 = x1_s[...] + g2_ref[0] * _dot_left_x2(pmt_s[...], acc_s[...])
        if final_norm:
            x2 = x2 * lax.rsqrt(jnp.mean(x2 * x2, axis=-1, keepdims=True) + NORM_EPS) * nf_ref[...]
        o_ref[0] = x2


def _out_moe(x, ys, m, p, norm_f, final_norm):
    b, t, _ = x.shape
    tm = min(MOE_TM, t)
    tok = lambda w_: pl.BlockSpec((1, tm, w_), lambda i, j, e: (i, j, 0))
    mod = lambda k: pl.BlockSpec((1, 1, D_MODEL), lambda i, j, e, k=k: (i, 0, k))
    full = lambda *s: pl.BlockSpec(s, lambda i, j, e: (0,) * len(s))
    return pl.pallas_call(
        functools.partial(_moe_kernel, final_norm=final_norm),
        grid=(b, t // tm, N_GROUPS),
        in_specs=[tok(D_MODEL), tok(W_MIX), tok(W_MIX), tok(W_MIX), tok(W_MIX),
                  full(D_MODEL, D_MODEL), mod(2), mod(3), mod(4), mod(5), full(1, D_MODEL),
                  full(N_EXPERTS, D_MODEL), full(N_EXPERTS, 1),
                  pl.BlockSpec((GROUP_SIZE, D_MODEL, D_EXPERT), lambda i, j, e: (e, 0, 0)),
                  pl.BlockSpec((GROUP_SIZE, D_MODEL, D_EXPERT), lambda i, j, e: (e, 0, 0)),
                  pl.BlockSpec((GROUP_SIZE, D_EXPERT, D_MODEL), lambda i, j, e: (e, 0, 0)),
                  full(1, D_MODEL)],
        out_specs=tok(D_MODEL),
        out_shape=jax.ShapeDtypeStruct((b, t, D_MODEL), F32),
        scratch_shapes=[pltpu.VMEM((tm, D_MODEL), F32), pltpu.VMEM((tm, D_MODEL), BF16),
                        pltpu.VMEM((E_PAD, tm), F32), pltpu.VMEM((tm, E_PAD), F32),
                        pltpu.VMEM((tm, D_MODEL), F32), pltpu.VMEM((tm, D_EXPERT * GROUP_SIZE), BF16),
                        pltpu.VMEM((SUBLANE, tm), F32), pltpu.VMEM((tm, tm), BF16),
                        pltpu.SMEM((2 * N_GROUPS,), jnp.int32)],
        compiler_params=_cparams(("arbitrary", "arbitrary", "arbitrary")),
        name="out_moe",
    )(x, *ys, p["w_out"], m, m, m, m, p["norm2_g"], p["w_router_t"], p["b_router"], p["e_w1"], p["e_w3"], p["e_w2"],
      norm_f)


def _gla_pack(z):
    hk = GLA_HK
    lead = z.shape[:-1]
    gl = z[..., 2 * hk + W_MIX:2 * hk + W_MIX + 32]
    return jnp.concatenate([z[..., 0:2 * hk + W_MIX], z[..., 2 * hk + W_MIX + 32:], gl,
                            jnp.zeros(lead + (96,), z.dtype)], -1)


def _gla_state_in(s):
    b = s.shape[0]
    eye = jnp.eye(N_HEADS, dtype=s.dtype)
    return jnp.einsum("bdhkv,hg->bdhvgk", s, eye).reshape(b, N_DIR, W_MIX, GLA_HK)


def _gla_state_out(st):
    b = st.shape[0]
    eye = jnp.eye(N_HEADS, dtype=st.dtype)
    return jnp.einsum("bdhvgk,hg->bdhkv", st.reshape(b, N_DIR, N_HEADS, HEAD, N_HEADS, GLA_DK), eye)


def _rw_state_in(s):
    b = s.shape[0]
    eye = jnp.eye(N_HEADS, dtype=s.dtype)
    return jnp.einsum("bdhvk,hg->bdhvgk", s, eye).reshape(b, N_DIR, W_MIX, W_MIX)


def _rw_state_out(st):
    b = st.shape[0]
    eye = jnp.eye(N_HEADS, dtype=st.dtype)
    return jnp.einsum("bdhvgk,hg->bdhvk", st.reshape(b, N_DIR, N_HEADS, HEAD, N_HEADS, HEAD), eye)


def _prep_params(d):
    L = DEPTH
    z = lambda *s: jnp.zeros(s, F32)
    out = {}
    out["rw_mu"] = jnp.concatenate([d["rw_mu"], z(L, ZA_W - d["rw_mu"].shape[-1])], -1).reshape(L, 1, ZA_W)
    w2, a2 = d["rw_w2"], d["rw_a2"]
    zz = z(L, 32, 256)
    rows = [
        jnp.concatenate([w2[:, 0], zz, zz, zz], -1),
        jnp.concatenate([zz, w2[:, 1], zz, zz], -1),
        jnp.concatenate([zz, zz, a2[:, 0], zz], -1),
        jnp.concatenate([zz, zz, zz, a2[:, 1]], -1),
    ]
    out["rw_wl"] = jnp.concatenate(rows, 1).astype(BF16)
    out["rw_lb"] = jnp.concatenate([d["rw_w0"][:, 0], d["rw_w0"][:, 1], d["rw_a0"][:, 0], d["rw_a0"][:, 1]],
                                   -1).reshape(L, 1, 1024)
    for n in ("rw_kk", "rw_ka", "rw_ln_g", "rw_ln_b"):
        out[n] = d[n].reshape(L, 1, W_MIX)
    out["rw_rk"] = d["rw_rk"].reshape(L, 1, W_MIX)
    out["rw_g2"] = jnp.concatenate([d["rw_g2"], z(L, 64, W_MIX)], 1).astype(BF16)

    gk2 = d["gla_gk2"]
    z16, z96 = z(L, 16, GLA_HK), z(L, 96, GLA_HK)
    out["gla_wg"] = jnp.stack([jnp.concatenate([gk2[:, 0], z16, z96], 1),
                               jnp.concatenate([z16, gk2[:, 1], z96], 1)], 1).astype(BF16)
    out["gla_gb"] = d["gla_gkb"].reshape(L, N_DIR, 1, GLA_HK)
    out["gla_ln_g"] = jnp.tile(d["gla_ln_g"], (1, N_HEADS)).reshape(L, 1, W_MIX)

    eye = jnp.eye(4, dtype=F32)
    bd = lambda w: jnp.einsum("ldgij,gh->ldgihj", w, eye).reshape(L, N_DIR, W_MIX, W_MIX)
    out["lru_cw"] = d["lru_conv_w"]
    out["lru_cb"] = d["lru_conv_b"].reshape(L, N_DIR, 1, W_MIX)
    out["lru_wax"] = jnp.concatenate([bd(d["lru_wa"]), bd(d["lru_wx"])], -1).astype(BF16)
    out["lru_bax"] = jnp.concatenate([d["lru_ba"], d["lru_bx"]], -1).reshape(L, N_DIR, 1, 2 * W_MIX)
    out["lru_lam"] = d["lru_lam"].reshape(L, N_DIR, 1, W_MIX)

    out["cv_dw_w"] = d["cv_dw_w"]
    for n in ("cv_dw_b", "cv_ln_g", "cv_ln_b", "cv_pw_b"):
        out[n] = d[n].reshape(L, 1, W_MIX)
    out["cv_pw_w"] = d["cv_pw_w"].astype(BF16)

    w_in = d["w_in"]
    o1 = RW_COLS
    o2 = o1 + GLA_COLS
    o3 = o2 + ZC_W
    out["w_in"] = jnp.concatenate([w_in[..., 0:o1], z(L, D_MODEL, ZA_W - o1), _gla_pack(w_in[..., o1:o2]),
                                   w_in[..., o2:o3], w_in[..., o3:]], -1).astype(BF16)
    out["w_out"] = d["w_out"].astype(BF16)
    out["norm1_g"] = d["norm1_g"].reshape(L, 1, D_MODEL)
    out["norm2_g"] = d["norm2_g"].reshape(L, 1, D_MODEL)
    for n in ("e_w1", "e_w3", "e_w2"):
        out[n] = d[n].astype(BF16)
    return out


def kernel(x_prompt, x_sample, state_rwkv, state_gla, state_lru, c, c_ctx, norm1_g, norm2_g, norm_f_g, w_ada, b_ada, w_in, w_out, rw_mu, rw_w0, rw_w2, rw_a0, rw_a2, rw_g2, rw_kk, rw_ka, rw_rk, rw_ln_g, rw_ln_b, gla_gk2, gla_gkb, gla_ln_g, lru_conv_w, lru_conv_b, lru_wa, lru_ba, lru_wx, lru_bx, lru_lam, cv_dw_w, cv_dw_b, cv_ln_g, cv_ln_b, cv_pw_w, cv_pw_b, w_router, b_router, e_w1, e_w3, e_w2):
    d = dict(norm1_g=norm1_g, norm2_g=norm2_g, w_in=w_in, w_out=w_out,
             rw_mu=rw_mu, rw_w0=rw_w0, rw_w2=rw_w2, rw_a0=rw_a0, rw_a2=rw_a2, rw_g2=rw_g2, rw_kk=rw_kk,
             rw_ka=rw_ka, rw_rk=rw_rk, rw_ln_g=rw_ln_g, rw_ln_b=rw_ln_b,
             gla_gk2=gla_gk2, gla_gkb=gla_gkb, gla_ln_g=gla_ln_g,
             lru_conv_w=lru_conv_w, lru_conv_b=lru_conv_b, lru_wa=lru_wa, lru_ba=lru_ba, lru_wx=lru_wx,
             lru_bx=lru_bx, lru_lam=lru_lam,
             cv_dw_w=cv_dw_w, cv_dw_b=cv_dw_b, cv_ln_g=cv_ln_g, cv_ln_b=cv_ln_b, cv_pw_w=cv_pw_w, cv_pw_b=cv_pw_b,
             e_w1=e_w1, e_w3=e_w3, e_w2=e_w2)
    P = _prep_params(d)
    shared = {"w_router_t": w_router.T, "b_router": b_router.reshape(N_EXPERTS, 1)}
    norm_f = norm_f_g.reshape(1, D_MODEL)

    bp, tp, _ = x_prompt.shape
    bs, ts, _ = x_sample.shape
    n_mod = 1 + bs
    mod_rows = -(-n_mod // 8) * 8
    cvec = jnp.concatenate([c_ctx[None], c, jnp.zeros((mod_rows - n_mod, D_MODEL), F32)], 0)
    mod = _ada_mod(cvec, w_ada, b_ada)

    tp_flat = min(bp * tp, 1024)
    bp_flat = bp * tp // tp_flat
    xp = x_prompt.reshape(bp_flat, tp_flat, D_MODEL)
    xs = x_sample
    zero_rw = jnp.zeros((bp, N_DIR, W_MIX, W_MIX), F32)
    zero_gla = jnp.zeros((bp, N_DIR, W_MIX, GLA_HK), F32)
    zero_lru = jnp.zeros((bp, N_DIR, W_MIX), F32)

    def layer(x, m, seq_shape, latent, s_rw, s_gla, s_lru, p, last):
        flat_shape = x.shape[:2]
        zs = _in_proj(x, m, p["norm1_g"], p["w_in"])
        za, zb, zc, zd = (z.reshape(seq_shape + (z.shape[-1],)) for z in zs)
        ya, f_rw = _rwkv_mixer(za, s_rw, p, latent)
        yb, f_gla = _gla_mixer(zb, s_gla, p)
        yc, f_lru = _lru_mixer(zc, s_lru, p)
        yd = _conv_mixer(zd, p, latent)
        ys = [y.reshape(flat_shape + (W_MIX,)) for y in (ya, yb, yc, yd)]
        return _out_moe(x, ys, m, p, norm_f, last), f_rw, f_gla, f_lru

    fin_rw, fin_gla, fin_lru = [], [], []
    for l in range(DEPTH):
        p = {n: a[l] for n, a in P.items()}
        p.update(shared)
        last = l == DEPTH - 1
        m_p = jnp.broadcast_to(mod[l, 0:1], (bp_flat, 6 * D_MODEL)).reshape(bp_flat, 1, 6 * D_MODEL)
        xp, f_rw, f_gla, f_lru = layer(xp, m_p, (bp, tp), False, zero_rw, zero_gla, zero_lru, p, last)
        fin_rw.append(_rw_state_out(f_rw))
        fin_gla.append(_gla_state_out(f_gla))
        fin_lru.append(f_lru)
        m_s = mod[l, 1:1 + bs].reshape(bs, 1, 6 * D_MODEL)
        xs, _, _, _ = layer(xs, m_s, (bs, ts), True, _rw_state_in(state_rwkv[:, l]),
                            _gla_state_in(state_gla[:, l]), state_lru[:, l], p, last)
    return (xp.reshape(bp, tp, D_MODEL), xs, jnp.stack(fin_rw, axis=1), jnp.stack(fin_gla, axis=1),
            jnp.stack(fin_lru, axis=1))
```

```python
import functools

import jax
import jax.numpy as jnp
from jax import lax
from jax.experimental import pallas as pl
from jax.experimental.pallas import tpu as pltpu

F32 = jnp.float32
BF16 = jnp.bfloat16
ACT = jnp.bfloat16

D_MODEL = 1024
DEPTH = 4
GRID_W = 64
N_DIR = 2
W_MIX = 256
N_HEADS = 4
HEAD = W_MIX // N_HEADS
GLA_DK = 32
GLA_HK = N_HEADS * GLA_DK
RW_LN_EPS = 64e-5
NORM_EPS = 1e-6
GLA_GATE_NORM = 16.0
LRU_C = 8.0
LRU_CONV = 4
CV_KERNEL = 31
CV_PAD = (CV_KERNEL - 1) // 2
N_EXPERTS = 16
GROUP_SIZE = 4
N_GROUPS = 4
D_EXPERT = 256
CHUNK = 64
EXP_M_HALF = 0.6065306597126334

LANE = 128
SUBLANE = 8

RW_COLS = 960
GLA_COLS = 800
ZA_W = 1024
ZB_W = 896
ZC_W = 512
ZD_W = 512
P_PAD = ZA_W + ZB_W + ZC_W + ZD_W
RW_R, RW_K, RW_V = slice(0, 256), slice(256, 512), slice(512, 768)
RW_LORA = slice(768, 896)
RW_GATE = slice(896, 1024)
GLA_Q, GLA_K, GLA_V = slice(0, 128), slice(128, 256), slice(256, 512)
GLA_OG = slice(512, 768)
GLA_GATE = slice(768, 896)

VMEM_LIMIT = 48 * 1024 * 1024


def _cparams(sem, vmem=VMEM_LIMIT):
    return pltpu.CompilerParams(dimension_semantics=sem, vmem_limit_bytes=vmem)


def _dot(a, b):
    return jnp.dot(a.astype(BF16), b.astype(BF16), preferred_element_type=F32)


def _dot_nt(a, b):
    return lax.dot_general(a.astype(BF16), b.astype(BF16), (((1,), (1,)), ((), ())),
                           preferred_element_type=F32)


def _dot_tn(a, b):
    return lax.dot_general(a.astype(BF16), b.astype(BF16), (((0,), (0,)), ((), ())),
                           preferred_element_type=F32)


def _split(a):
    hi = a.astype(BF16)
    lo = (a - hi.astype(F32)).astype(BF16)
    return hi, lo


def _dot_x2(a, b_exact):
    hi, lo = _split(a)
    bb = b_exact.astype(BF16)
    return (jnp.dot(hi, bb, preferred_element_type=F32) + jnp.dot(lo, bb, preferred_element_type=F32))


def _dot_left_x2(a_exact, b):
    hi, lo = _split(b)
    aa = a_exact.astype(BF16)
    return (jnp.dot(aa, hi, preferred_element_type=F32) + jnp.dot(aa, lo, preferred_element_type=F32))


def _dot3(a, b):
    ah, al = _split(a)
    bh, bl = _split(b)
    return (jnp.dot(ah, bh, preferred_element_type=F32) + jnp.dot(ah, bl, preferred_element_type=F32)
            + jnp.dot(al, bh, preferred_element_type=F32))


def _dot3_nt(a, b):
    ah, al = _split(a)
    bh, bl = _split(b)
    dn = (((1,), (1,)), ((), ()))
    return (lax.dot_general(ah, bh, dn, preferred_element_type=F32)
            + lax.dot_general(ah, bl, dn, preferred_element_type=F32)
            + lax.dot_general(al, bh, dn, preferred_element_type=F32))


def _iota(shape, axis):
    return lax.broadcasted_iota(jnp.int32, shape, axis)


def _block_mask(rows, cols, rblk, cblk):
    r = _iota((rows, cols), 0) // rblk
    c = _iota((rows, cols), 1) // cblk
    return (r == c).astype(F32)


def _expand(x, bm):
    xb = x.astype(BF16)
    return jnp.concatenate([xb, xb, xb, xb], axis=0) * bm


def _tri(d, strict):
    t = _iota((CHUNK, CHUNK), 0)
    s = _iota((CHUNK, CHUNK), 1)
    if d == 0:
        m = (s < t) if strict else (s <= t)
    else:
        m = (s > t) if strict else (s >= t)
    return m.astype(F32)


def _tri_wide(d, strict):
    t = _iota((CHUNK, N_HEADS * CHUNK), 0)
    s = _iota((CHUNK, N_HEADS * CHUNK), 1) % CHUNK
    if d == 0:
        m = (s < t) if strict else (s <= t)
    else:
        m = (s > t) if strict else (s >= t)
    return m.astype(F32)


ADA_TN = 1536


def _ada_kernel(c_ref, w_ref, b_ref, o_ref):
    c = c_ref[...]
    s = c * jax.nn.sigmoid(c)
    o_ref[0] = _dot3(s, w_ref[0]) + b_ref[0]


def _ada_mod(cvec, w_ada, b_ada):
    rows = cvec.shape[0]
    n_out = w_ada.shape[-1]
    return pl.pallas_call(
        _ada_kernel,
        grid=(DEPTH, n_out // ADA_TN),
        in_specs=[
            pl.BlockSpec((rows, D_MODEL), lambda l, j: (0, 0)),
            pl.BlockSpec((1, D_MODEL, ADA_TN), lambda l, j: (l, 0, j)),
            pl.BlockSpec((1, 1, ADA_TN), lambda l, j: (l, 0, j)),
        ],
        out_specs=pl.BlockSpec((1, rows, ADA_TN), lambda l, j: (l, 0, j)),
        out_shape=jax.ShapeDtypeStruct((DEPTH, rows, n_out), F32),
        compiler_params=_cparams(("arbitrary", "arbitrary")),
        name="ada_mod",
    )(cvec, w_ada, b_ada.reshape(DEPTH, 1, n_out))


IN_TM = 1024


def _in_kernel(x_ref, sh_ref, sc_ref, g_ref, w_ref, za_ref, zb_ref, zc_ref, zd_ref):
    x = x_ref[0]
    h = x * lax.rsqrt(jnp.mean(x * x, axis=-1, keepdims=True) + NORM_EPS) * g_ref[...]
    h = (h * (1.0 + sc_ref[0]) + sh_ref[0]).astype(BF16)
    o0, o1, o2 = ZA_W, ZA_W + ZB_W, ZA_W + ZB_W + ZC_W
    za_ref[0] = jnp.dot(h, w_ref[:, 0:o0], preferred_element_type=F32).astype(ACT)
    zb_ref[0] = jnp.dot(h, w_ref[:, o0:o1], preferred_element_type=F32).astype(ACT)
    zc_ref[0] = jnp.dot(h, w_ref[:, o1:o2], preferred_element_type=F32).astype(ACT)
    zd_ref[0] = jnp.dot(h, w_ref[:, o2:P_PAD], preferred_element_type=F32).astype(ACT)


def _in_proj(x, m, g, w):
    b, t, _ = x.shape
    tm = min(IN_TM, t)
    tok = lambda w_: pl.BlockSpec((1, tm, w_), lambda i, j: (i, j, 0))
    mod = lambda k: pl.BlockSpec((1, 1, D_MODEL), lambda i, j, k=k: (i, 0, k))
    return pl.pallas_call(
        _in_kernel,
        grid=(b, t // tm),
        in_specs=[tok(D_MODEL), mod(0), mod(1),
                  pl.BlockSpec((1, D_MODEL), lambda i, j: (0, 0)),
                  pl.BlockSpec((D_MODEL, P_PAD), lambda i, j: (0, 0), pipeline_mode=pl.Buffered(1))],
        out_specs=[tok(ZA_W), tok(ZB_W), tok(ZC_W), tok(ZD_W)],
        out_shape=[jax.ShapeDtypeStruct((b, t, w_), ACT) for w_ in (ZA_W, ZB_W, ZC_W, ZD_W)],
        compiler_params=_cparams(("arbitrary", "arbitrary")),
        name="in_proj",
    )(x, m, m, g, w)


SHIFT_TB = 512


def _row_neighbours(zb, grp):
    t = _iota((2 * grp, grp), 0)
    s = _iota((2 * grp, grp), 1)
    shift = jnp.where((t < grp) & (s == t - 1) | (t >= grp) & (s == t - grp + 1), 1.0, 0.0).astype(BF16)
    res = [jnp.dot(shift, zb[g * grp:(g + 1) * grp].astype(BF16), preferred_element_type=F32)
           for g in range(zb.shape[0] // grp)]
    return (jnp.concatenate([r[:grp] for r in res], axis=0), jnp.concatenate([r[grp:] for r in res], axis=0))


def _shift_grid_kernel(z_ref, up_ref, dn_ref, mu_ref, o_ref, *, nblk):
    i = pl.program_id(1)
    z = z_ref[0].astype(F32)
    tb, c = z.shape
    lane = _iota((tb, c), 1) % 4
    left, right = _row_neighbours(z_ref[0], GRID_W)
    up_halo = jnp.where(i > 0, up_ref[0, 0].astype(F32), 0.0)
    dn_halo = jnp.where(i < nblk - 1, dn_ref[0, 0].astype(F32), 0.0)
    up = jnp.concatenate([up_halo, z[:tb - GRID_W]], axis=0)
    down = jnp.concatenate([z[GRID_W:], dn_halo], axis=0)
    sh = jnp.where(lane == 0, left, jnp.where(lane == 1, right, jnp.where(lane == 2, up, down)))
    o_ref[0] = (z + (sh - z) * mu_ref[...]).astype(ACT)


def _shift_ctx_kernel(z_ref, mu_ref, o_ref):
    z = z_ref[0].astype(F32)
    t, c = z.shape
    lane = _iota((t, c), 1) % 2
    prev, nxt = _row_neighbours(z_ref[0], t)
    sh = jnp.where(lane == 0, prev, nxt)
    o_ref[0] = (z + (sh - z) * mu_ref[...]).astype(ACT)


def _rw_shift(za, mu, latent):
    b, t, c = za.shape
    mu_spec = pl.BlockSpec((1, c), lambda *_: (0, 0))
    if not latent:
        return pl.pallas_call(
            _shift_ctx_kernel,
            grid=(b,),
            in_specs=[pl.BlockSpec((1, t, c), lambda i: (i, 0, 0)), mu_spec],
            out_specs=pl.BlockSpec((1, t, c), lambda i: (i, 0, 0)),
            out_shape=jax.ShapeDtypeStruct((b, t, c), ACT),
            compiler_params=_cparams(("arbitrary",)),
            name="rw_shift_ctx",
        )(za, mu)
    tb = min(SHIFT_TB, t)
    nblk = t // tb
    rpb = tb // GRID_W
    nrow = t // GRID_W
    z4 = za.reshape(b, nrow, GRID_W, c)
    return pl.pallas_call(
        functools.partial(_shift_grid_kernel, nblk=nblk),
        grid=(b, nblk),
        in_specs=[
            pl.BlockSpec((1, tb, c), lambda i, j: (i, j, 0)),
            pl.BlockSpec((1, 1, GRID_W, c), lambda i, j: (i, jnp.maximum(j * rpb - 1, 0), 0, 0)),
            pl.BlockSpec((1, 1, GRID_W, c), lambda i, j: (i, jnp.minimum(j * rpb + rpb, nrow - 1), 0, 0)),
            mu_spec,
        ],
        out_specs=pl.BlockSpec((1, tb, c), lambda i, j: (i, j, 0)),
        out_shape=jax.ShapeDtypeStruct((b, t, c), ACT),
        compiler_params=_cparams(("arbitrary", "arbitrary")),
        name="rw_shift_grid",
    )(za, z4, z4, mu)


SCAN_TB = 512
PREP_SPLIT = 2
PREP_LOCKSTEP = 4
PREP_SKEW = 3
PREP_GROUP = 8


def _rw_scan_kernel(z_ref, s0_ref, wl_ref, lb_ref, kkp_ref, kap_ref, lng_ref, lnb_ref, rk_ref, g2_ref,
                    o_ref, st_ref,
                    y_ref, lw_s, a_s, b_s, kd_s, r2_s, mc_s, ds_s, gt_s, *, nchunk, nblk):
    i = pl.program_id(1)
    tb = nchunk * CHUNK

    @pl.when(i == 0)
    def _():
        st_ref[...] = s0_ref[...]
        y_ref[...] = jnp.zeros(y_ref.shape, F32)

    def y_rows(d, c):
        blk = i if d == 0 else nblk - 1 - i
        return pl.ds(pl.multiple_of((blk * nchunk + c) * CHUNK, CHUNK), CHUNK)

    def blk_rows(d):
        blk = i if d == 0 else nblk - 1 - i
        return pl.ds(pl.multiple_of(blk * tb, tb), tb)

    ones_blk = _block_mask(W_MIX, W_MIX, HEAD, HEAD)
    is_decay_lora = _iota((1, LANE), 1) < LANE // 2

    for d in range(N_DIR):
        k = z_ref[0, blk_rows(d), RW_K].astype(F32)
        la = z_ref[0, blk_rows(d), RW_LORA].astype(F32)
        la_t = jnp.where(is_decay_lora, jnp.tanh(la), la)
        w_cols = slice(W_MIX * d, W_MIX * (d + 1))
        a_cols = slice(W_MIX * (N_DIR + d), W_MIX * (N_DIR + d + 1))
        wraw = _dot(la_t, wl_ref[:, w_cols]) + lb_ref[:, w_cols]
        araw = _dot(la_t, wl_ref[:, a_cols]) + lb_ref[:, a_cols]
        lw_s[d] = jax.nn.sigmoid(wraw) * (-EXP_M_HALF)
        icl = jax.nn.sigmoid(araw)
        kkv = k * kkp_ref[...]
        ss = _dot_x2(kkv * kkv, ones_blk)
        kk = kkv * lax.rsqrt(ss + 1e-12)
        a_s[d] = -kk
        b_s[d] = kk * icl
        kd_s[d] = k * (1.0 + (icl - 1.0) * kap_ref[...])

    bm = _block_mask(N_HEADS * CHUNK, W_MIX, CHUNK, HEAD).astype(BF16)
    tri = [_tri(d, False) for d in range(N_DIR)]
    tw_strict = [_tri_wide(d, True) for d in range(N_DIR)]
    tw_incl = [_tri_wide(d, False) for d in range(N_DIR)]
    eye_w = (_iota((CHUNK, N_HEADS * CHUNK), 0) == _iota((CHUNK, N_HEADS * CHUNK), 1) % CHUNK).astype(F32)

    def rows_of(c, n):
        return pl.ds(pl.multiple_of(c * n, n), n)

    group = min(PREP_GROUP, nchunk)

    each = lambda fn, *ls: [fn(*xs) for xs in zip(*ls)]

    def prep_stages(ch):
        ds_ = [d for d, _ in ch]
        st = {}

        def front():
            load = lambda ref: [ref[d, rows_of(c, CHUNK), :] for d, c in ch]
            lw, a, st["b"], st["kd"] = load(lw_s), load(a_s), load(b_s), load(kd_s)
            r = [z_ref[0, y_rows(d, c), RW_R].astype(F32) for d, c in ch]
            st["v"] = [z_ref[0, y_rows(d, c), RW_V].astype(F32) for d, c in ch]
            cum = each(lambda d, x: _dot_left_x2(tri[d], x), ds_, lw)
            st["tot"] = each(lambda d, x: x[CHUNK - 1:CHUNK, :] if d == 0 else x[0:1, :], ds_, cum)
            st["g_inv"] = each(lambda x: jnp.exp(-x), cum)
            st["g_rem"] = each(lambda t_, x: jnp.exp(t_ - x), st["tot"], cum)
            st["at"] = each(lambda a_, x, l: a_ * jnp.exp(x - l), a, cum, lw)
            st["rt"] = each(lambda r_, x: r_ * jnp.exp(x), r, cum)

        def gram():
            x = each(lambda p, q: jnp.concatenate([p, q], axis=0), st["at"], st["rt"])
            gb = each(lambda x_, b_, g: _dot_nt(x_, _expand(b_ * g, bm)), x, st["b"], st["g_inv"])
            gk = each(lambda x_, k_, g: _dot_nt(x_, _expand(k_ * g, bm)), x, st["kd"], st["g_inv"])
            st["vbd"] = each(lambda v_: _expand(v_, bm), st["v"])
            st["apow"] = each(lambda d, g: g[0:CHUNK] * tw_strict[d], ds_, gb)
            st["av"] = each(lambda d, g, vb: _dot(g[0:CHUNK] * tw_strict[d], vb), ds_, gk, st["vbd"])
            st["tw"] = each(lambda m: eye_w + m, st["apow"])
            st["r_b"] = each(lambda d, g: g[CHUNK:] * tw_incl[d], ds_, gb)
            st["r_k"] = each(lambda d, g: g[CHUNK:] * tw_incl[d], ds_, gk)

        def double():
            st["apow"] = each(lambda m: _dot(m, _expand(m, bm)), st["apow"])
            st["tw"] = each(lambda t_, m: t_ + _dot(m, _expand(t_, bm)), st["tw"], st["apow"])

        def apply():
            st["a2"] = each(lambda t_, m: _dot(t_, _expand(m, bm)), st["tw"], st["at"])
            st["u0"] = each(lambda t_, m: _dot(t_, _expand(m, bm)), st["tw"], st["av"])

        def finish():
            r2 = each(lambda rt_, rb, m: rt_ + _dot(rb, _expand(m, bm)), st["rt"], st["r_b"], st["a2"])
            y0 = each(lambda rb, u, rk, vb: _dot(rb, _expand(u, bm)) + _dot(rk, vb),
                      st["r_b"], st["u0"], st["r_k"], st["vbd"])
            bl = each(lambda b_, g: b_ * g, st["b"], st["g_rem"])
            kl = each(lambda k_, g: k_ * g, st["kd"], st["g_rem"])
            mc = each(lambda m, bl_: _dot_tn(m, bl_) * ones_blk, st["a2"], bl)
            ds0 = each(lambda u, v_, bl_, kl_: _dot_tn(jnp.concatenate([u, v_], axis=0),
                                                       jnp.concatenate([bl_, kl_], axis=0)) * ones_blk,
                       st["u0"], st["v"], bl, kl)
            for n, (d, c) in enumerate(ch):
                rows = rows_of(c, CHUNK)
                y_ref[0, y_rows(d, c), :] += y0[n]
                r2_s[d, rows, :] = r2[n].astype(BF16)
                mc_s[d, rows_of(c, W_MIX), :] = mc[n].astype(BF16)
                ds_s[d, rows_of(c, W_MIX), :] = ds0[n]
                gt_s[d, rows_of(c, SUBLANE), :] = jnp.broadcast_to(jnp.exp(st["tot"][n]), (SUBLANE, W_MIX))

        return [front, gram] + [double] * 5 + [apply, finish]

    def prep_body(gi, carry):
        half = group // PREP_SPLIT if group >= PREP_SPLIT * PREP_LOCKSTEP else group
        halves = [prep_stages([(d, gi * group + j) for j in range(h0, min(h0 + half, group)) for d in range(N_DIR)])
                  for h0 in range(0, group, half)]
        n_stage = len(halves[0])
        for step in range(n_stage + PREP_SKEW * (len(halves) - 1)):
            for hi, stages in enumerate(halves):
                k = step - PREP_SKEW * hi
                if 0 <= k < n_stage:
                    stages[k]()
        return carry

    lax.fori_loop(0, nchunk // group, prep_body, 0)

    def scan_body(cc, carry):
        cs = [cc, nchunk - 1 - cc]
        s = [st_ref[0, d] for d in range(N_DIR)]
        sb = [x.astype(BF16) for x in s]
        upd = [_dot(sb[d], mc_s[d, rows_of(cs[d], W_MIX), :]) for d in range(N_DIR)]
        ys = [_dot_nt(r2_s[d, rows_of(cs[d], CHUNK), :], sb[d]) for d in range(N_DIR)]
        for d in range(N_DIR):
            st_ref[0, d] = (s[d] * gt_s[d, rows_of(cs[d], SUBLANE), :][0:1, :] + upd[d]
                            + ds_s[d, rows_of(cs[d], W_MIX), :])
            y_ref[0, y_rows(d, cs[d]), :] += ys[d]
        return carry

    lax.fori_loop(0, nchunk, scan_body, 0)

    @pl.when(i == nblk - 1)
    def _():
        def epilogue(j, carry):
            rows = pl.ds(pl.multiple_of(j * tb, tb), tb)
            y = y_ref[0, rows, :]
            mu = _dot_x2(y, ones_blk) * (1.0 / HEAD)
            yc = y - mu
            var = _dot_x2(yc * yc, ones_blk) * (1.0 / HEAD)
            yn = yc * lax.rsqrt(var + RW_LN_EPS) * lng_ref[...] + lnb_ref[...]
            r = z_ref[0, rows, RW_R].astype(F32)
            k = z_ref[0, rows, RW_K].astype(F32)
            v = z_ref[0, rows, RW_V].astype(F32)
            gl = z_ref[0, rows, RW_GATE].astype(F32)
            bonus = _dot_x2(r * k * rk_ref[...], ones_blk) * v
            gate = _dot(jax.nn.sigmoid(gl), g2_ref[...])
            o_ref[0, rows, :] = ((yn + bonus) * gate).astype(ACT)
            return carry

        lax.fori_loop(0, nblk, epilogue, 0)


def _rw_scan(zs, s0, p):
    b, t, c = zs.shape
    tb = min(SCAN_TB, t)
    nblk = t // tb
    nchunk = tb // CHUNK
    vec = lambda n: pl.BlockSpec((1, n), lambda i, j: (0, 0))
    seq = lambda w_: pl.BlockSpec((1, t, w_), lambda i, j: (i, 0, 0))
    st_spec = pl.BlockSpec((1, N_DIR, W_MIX, W_MIX), lambda i, j: (i, 0, 0, 0))
    return pl.pallas_call(
        functools.partial(_rw_scan_kernel, nchunk=nchunk, nblk=nblk),
        grid=(b, nblk),
        in_specs=[
            seq(c), st_spec,
            pl.BlockSpec((LANE, 2 * N_DIR * W_MIX), lambda i, j: (0, 0)),
            vec(1024), vec(W_MIX), vec(W_MIX), vec(W_MIX), vec(W_MIX), vec(W_MIX),
            pl.BlockSpec((LANE, W_MIX), lambda i, j: (0, 0)),
        ],
        out_specs=[seq(W_MIX), st_spec],
        out_shape=[
            jax.ShapeDtypeStruct((b, t, W_MIX), ACT),
            jax.ShapeDtypeStruct((b, N_DIR, W_MIX, W_MIX), F32),
        ],
        scratch_shapes=[pltpu.VMEM((1, t, W_MIX), F32)]
        + [pltpu.VMEM((N_DIR, tb, W_MIX), F32) for _ in range(4)] + [
            pltpu.VMEM((N_DIR, tb, W_MIX), BF16),
            pltpu.VMEM((N_DIR, nchunk * W_MIX, W_MIX), BF16),
            pltpu.VMEM((N_DIR, nchunk * W_MIX, W_MIX), F32),
            pltpu.VMEM((N_DIR, nchunk * 8, W_MIX), F32),
        ],
        compiler_params=_cparams(("arbitrary", "arbitrary")),
        name="rw_scan",
    )(zs, s0, p["rw_wl"], p["rw_lb"], p["rw_kk"], p["rw_ka"], p["rw_ln_g"], p["rw_ln_b"], p["rw_rk"], p["rw_g2"])


def _rwkv_mixer(za, s0_bd, p, latent):
    zs = _rw_shift(za, p["rw_mu"], latent)
    return _rw_scan(zs, s0_bd, p)


def _gla_scan_kernel(z_ref, s0_ref, wg_ref, gb_ref, lng_ref, o_ref, st_ref,
                     y_ref, la_s, qe_s, ds_s, dec_s, sp_s, *, nchunk, nblk):
    i = pl.program_id(1)
    tb = nchunk * CHUNK

    @pl.when(i == 0)
    def _():
        st_ref[...] = s0_ref[...]
        y_ref[...] = jnp.zeros(y_ref.shape, F32)

    def y_rows(d, c):
        blk = i if d == 0 else nblk - 1 - i
        return pl.ds(pl.multiple_of((blk * nchunk + c) * CHUNK, CHUNK), CHUNK)

    for d in range(N_DIR):
        blk = i if d == 0 else nblk - 1 - i
        logit = _dot(z_ref[0, pl.ds(pl.multiple_of(blk * tb, tb), tb), GLA_GATE], wg_ref[d]) + gb_ref[d]
        la_s[d] = jax.nn.log_sigmoid(logit) * (1.0 / GLA_GATE_NORM)

    bm_k = _block_mask(N_HEADS * CHUNK, GLA_HK, CHUNK, GLA_DK).astype(BF16)
    bm_v = _block_mask(N_HEADS * CHUNK, W_MIX, CHUNK, HEAD).astype(BF16)
    bm_s = _block_mask(W_MIX, GLA_HK, HEAD, GLA_DK)
    tri = [_tri(d, False) for d in range(N_DIR)]
    tw_incl = [_tri_wide(d, False) for d in range(N_DIR)]

    def rows_of(c, n):
        return pl.ds(pl.multiple_of(c * n, n), n)

    group = min(PREP_GROUP, nchunk)

    def prep_body(gi, carry):
        ch = [(d, gi * group + j) for j in range(group) for d in range(N_DIR)]
        each = lambda fn, *ls: [fn(*xs) for xs in zip(*ls)]
        ds_ = [d for d, _ in ch]
        la = [la_s[d, rows_of(c, CHUNK), :] for d, c in ch]
        q = [z_ref[0, y_rows(d, c), GLA_Q].astype(F32) * (GLA_DK ** -0.5) for d, c in ch]
        k = [z_ref[0, y_rows(d, c), GLA_K].astype(F32) for d, c in ch]
        v = [z_ref[0, y_rows(d, c), GLA_V].astype(F32) for d, c in ch]
        cum = each(lambda d, x: _dot_left_x2(tri[d], x), ds_, la)
        last = each(lambda d, x: x[CHUNK - 1:CHUNK, :] if d == 0 else x[0:1, :], ds_, cum)
        qe = each(lambda q_, x: q_ * jnp.exp(x), q, cum)
        ke = each(lambda k_, x: k_ * jnp.exp(-x), k, cum)
        kl = each(lambda k_, l, x: k_ * jnp.exp(l - x), k, last, cum)
        att = each(lambda d, q_, k_: _dot_nt(q_, _expand(k_, bm_k)) * tw_incl[d], ds_, qe, ke)
        o = each(lambda a_, v_: _dot(a_, _expand(v_, bm_v)), att, v)
        dst = each(lambda v_, k_: _dot_tn(v_, k_) * bm_s, v, kl)
        for n, (d, c) in enumerate(ch):
            y_ref[0, y_rows(d, c), :] += o[n]
            qe_s[d, rows_of(c, CHUNK), :] = qe[n].astype(BF16)
            ds_s[d, rows_of(c, W_MIX), :] = dst[n]
            dec_s[d, rows_of(c, SUBLANE), :] = jnp.broadcast_to(jnp.exp(last[n]), (SUBLANE, GLA_HK))
        return carry

    lax.fori_loop(0, nchunk // group, prep_body, 0)

    def state_body(cc, carry):
        cs = [cc, nchunk - 1 - cc]
        for d in range(N_DIR):
            s = st_ref[0, d]
            sp_s[d, rows_of(cs[d], W_MIX), :] = s.astype(BF16)
            st_ref[0, d] = s * dec_s[d, rows_of(cs[d], SUBLANE), :][0:1, :] + ds_s[d, rows_of(cs[d], W_MIX), :]
        return carry

    lax.fori_loop(0, nchunk, state_body, 0)

    def out_body(gi, carry):
        ch = [(d, gi * group + j) for j in range(group) for d in range(N_DIR)]
        ys = [_dot_nt(qe_s[d, rows_of(c, CHUNK), :], sp_s[d, rows_of(c, W_MIX), :]) for d, c in ch]
        for n, (d, c) in enumerate(ch):
            y_ref[0, y_rows(d, c), :] += ys[n]
        return carry

    lax.fori_loop(0, nchunk // group, out_body, 0)

    @pl.when(i == nblk - 1)
    def _():
        ones_blk = _block_mask(W_MIX, W_MIX, HEAD, HEAD)

        def epilogue(j, carry):
            rows = pl.ds(pl.multiple_of(j * tb, tb), tb)
            y = y_ref[0, rows, :]
            ms = _dot_x2(y * y, ones_blk) * (1.0 / HEAD)
            y = y * lax.rsqrt(ms + NORM_EPS) * lng_ref[...]
            og = z_ref[0, rows, GLA_OG].astype(F32)
            o_ref[0, rows, :] = (y * (og * jax.nn.sigmoid(og))).astype(ACT)
            return carry

        lax.fori_loop(0, nblk, epilogue, 0)


def _gla_mixer(zb, s0, p):
    b, t, c = zb.shape
    tb = min(SCAN_TB, t)
    nblk = t // tb
    seq = lambda w_: pl.BlockSpec((1, t, w_), lambda i, j: (i, 0, 0))
    st_spec = pl.BlockSpec((1, N_DIR, W_MIX, GLA_HK), lambda i, j: (i, 0, 0, 0))
    nchunk = tb // CHUNK
    return pl.pallas_call(
        functools.partial(_gla_scan_kernel, nchunk=nchunk, nblk=nblk),
        grid=(b, nblk),
        in_specs=[
            seq(c), st_spec,
            pl.BlockSpec((N_DIR, LANE, GLA_HK), lambda i, j: (0, 0, 0)),
            pl.BlockSpec((N_DIR, 1, GLA_HK), lambda i, j: (0, 0, 0)),
            pl.BlockSpec((1, W_MIX), lambda i, j: (0, 0)),
        ],
        out_specs=[seq(W_MIX), st_spec],
        out_shape=[
            jax.ShapeDtypeStruct((b, t, W_MIX), ACT),
            jax.ShapeDtypeStruct((b, N_DIR, W_MIX, GLA_HK), F32),
        ],
        scratch_shapes=[
            pltpu.VMEM((1, t, W_MIX), F32),
            pltpu.VMEM((N_DIR, tb, GLA_HK), F32),
            pltpu.VMEM((N_DIR, tb, GLA_HK), BF16),
            pltpu.VMEM((N_DIR, nchunk * W_MIX, GLA_HK), F32),
            pltpu.VMEM((N_DIR, nchunk * SUBLANE, GLA_HK), F32),
            pltpu.VMEM((N_DIR, nchunk * W_MIX, GLA_HK), BF16),
        ],
        compiler_params=_cparams(("arbitrary", "arbitrary")),
        name="gla_scan",
    )(zb, s0, p["gla_wg"], p["gla_gb"], p["gla_ln_g"])


LRU_TT = 256
LRU_HALO = 8
LRU_SUB = 8


def _lru_kernel(z_ref, h0_ref, cw_ref, cb_ref, wax_ref, bax_ref, lam_ref, y_ref, hf_ref, xpad, hfwd, *, t):
    tt = min(LRU_TT, t)
    ntile = t // tt
    xpad[0:LRU_HALO, :] = jnp.zeros((LRU_HALO, W_MIX), F32)
    xpad[LRU_HALO + t:2 * LRU_HALO + t, :] = jnp.zeros((LRU_HALO, W_MIX), F32)

    def fill(j, carry):
        base = pl.multiple_of(j * tt, tt)
        xpad[pl.ds(base + LRU_HALO, tt), :] = z_ref[0, pl.ds(base, tt), 0:W_MIX].astype(F32)
        return carry

    lax.fori_loop(0, ntile, fill, 0)
    sub = _iota((tt // LRU_SUB, LRU_SUB, W_MIX), 1)

    def tile_scan(j, h, d):
        base = pl.multiple_of(j * tt, tt)
        win = xpad[pl.ds(base, tt + 2 * LRU_HALO), :]
        xc = jnp.zeros((tt, W_MIX), F32) + cb_ref[d]
        for tap in range(LRU_CONV):
            off = LRU_HALO - (LRU_CONV - 1) + tap if d == 0 else LRU_HALO + (LRU_CONV - 1) - tap
            xc = xc + cw_ref[d, tap:tap + 1, :] * win[off:off + tt, :]
        g = _dot(xc, wax_ref[d]) + bax_ref[d]
        gr = jax.nn.sigmoid(g[:, 0:W_MIX])
        gi = jax.nn.sigmoid(g[:, W_MIX:2 * W_MIX])
        log_a = -LRU_C * gr * jax.nn.softplus(-lam_ref[d])
        a = jnp.exp(log_a)
        bv = jnp.sqrt(1.0 - jnp.exp(2.0 * log_a)) * gi * xc
        ngrp = tt // LRU_SUB
        a = a.reshape(ngrp, LRU_SUB, W_MIX)
        bv = bv.reshape(ngrp, LRU_SUB, W_MIX)
        s = 1
        while s < LRU_SUB:
            shift = s if d == 0 else LRU_SUB - s
            keep = (sub >= s) if d == 0 else (sub < LRU_SUB - s)
            a_sh = jnp.where(keep, pltpu.roll(a, shift, 1), 1.0)
            b_sh = jnp.where(keep, pltpu.roll(bv, shift, 1), 0.0)
            bv = a * b_sh + bv
            a = a * a_sh
            s *= 2
        out = [None] * ngrp
        for gg in range(ngrp):
            g = gg if d == 0 else ngrp - 1 - gg
            hg = a[g] * h + bv[g]
            out[g] = hg
            h = hg[LRU_SUB - 1:LRU_SUB, :] if d == 0 else hg[0:1, :]
        return jnp.concatenate(out, axis=0), h, base

    def fwd(j, h):
        ht, h, base = tile_scan(j, h, 0)
        hfwd[pl.ds(base, tt), :] = ht
        return h

    h_end = lax.fori_loop(0, ntile, fwd, h0_ref[0, 0:1, :])
    hf_ref[0, 0:1, :] = h_end

    def bwd(jj, h):
        j = ntile - 1 - jj
        ht, h, base = tile_scan(j, h, 1)
        gb = z_ref[0, pl.ds(base, tt), W_MIX:2 * W_MIX].astype(F32)
        y_ref[0, pl.ds(base, tt), :] = ((hfwd[pl.ds(base, tt), :] + ht) * jax.nn.gelu(gb)).astype(ACT)
        return h

    h_end = lax.fori_loop(0, ntile, bwd, h0_ref[0, 1:2, :])
    hf_ref[0, 1:2, :] = h_end


def _lru_mixer(zc, h0, p):
    b, t, c = zc.shape
    full = lambda *s: pl.BlockSpec(s, lambda i: (0,) * len(s))
    return pl.pallas_call(
        functools.partial(_lru_kernel, t=t),
        grid=(b,),
        in_specs=[
            pl.BlockSpec((1, t, c), lambda i: (i, 0, 0)),
            pl.BlockSpec((1, N_DIR, W_MIX), lambda i: (i, 0, 0)),
            full(N_DIR, LRU_CONV, W_MIX), full(N_DIR, 1, W_MIX),
            full(N_DIR, W_MIX, 2 * W_MIX), full(N_DIR, 1, 2 * W_MIX), full(N_DIR, 1, W_MIX),
        ],
        out_specs=[pl.BlockSpec((1, t, W_MIX), lambda i: (i, 0, 0)),
                   pl.BlockSpec((1, N_DIR, W_MIX), lambda i: (i, 0, 0))],
        out_shape=[jax.ShapeDtypeStruct((b, t, W_MIX), ACT), jax.ShapeDtypeStruct((b, N_DIR, W_MIX), F32)],
        scratch_shapes=[pltpu.VMEM((t + 2 * LRU_HALO, W_MIX), F32), pltpu.VMEM((t, W_MIX), F32)],
        compiler_params=_cparams(("arbitrary",)),
        name="lru",
    )(zc, h0, p["lru_cw"], p["lru_cb"], p["lru_wax"], p["lru_bax"], p["lru_lam"])


CV_TT = 256
CV_WIN = 16


def _conv_kernel(z_ref, dw_ref, dwb_ref, lng_ref, lnb_ref, pw_ref, pwb_ref, y_ref, upad, *, t, latent):
    tt = min(CV_TT, t)
    ntile = t // tt
    pad = CV_PAD * GRID_W if latent else CV_WIN
    upad[0:pad, :] = jnp.zeros((pad, W_MIX), F32)
    upad[pad + t:2 * pad + t, :] = jnp.zeros((pad, W_MIX), F32)

    def fill(j, carry):
        base = pl.multiple_of(j * tt, tt)
        z = z_ref[0, pl.ds(base, tt), :].astype(F32)
        upad[pl.ds(base + pad, tt), :] = z[:, 0:W_MIX] * jax.nn.sigmoid(z[:, W_MIX:2 * W_MIX])
        return carry

    lax.fori_loop(0, ntile, fill, 0)

    def tile(j, carry):
        base = pl.multiple_of(j * tt, tt)
        if latent:
            half = W_MIX // 2
            col = _iota((tt, half), 0) % GRID_W
            win = upad[pl.ds(pl.multiple_of(base + pad - CV_WIN, CV_WIN), tt + 2 * CV_WIN), 0:half]
            accw = jnp.zeros((tt, half), F32)
            acch = jnp.zeros((tt, half), F32)
            for tap in range(CV_KERNEL):
                dlt = tap - CV_PAD
                x = win[CV_WIN + dlt:CV_WIN + dlt + tt, :]
                if dlt < 0:
                    x = jnp.where(col >= -dlt, x, 0.0)
                elif dlt > 0:
                    x = jnp.where(col < GRID_W - dlt, x, 0.0)
                accw = accw + dw_ref[tap:tap + 1, 0:half] * x
                rows = pl.ds(pl.multiple_of(base + pad + dlt * GRID_W, GRID_W), tt)
                acch = acch + dw_ref[tap:tap + 1, half:W_MIX] * upad[rows, half:W_MIX]
            u = jnp.concatenate([accw, acch], axis=1)
        else:
            win = upad[pl.ds(base + pad - CV_WIN, tt + 2 * CV_WIN), :]
            u = jnp.zeros((tt, W_MIX), F32)
            for tap in range(CV_KERNEL):
                dlt = tap - CV_PAD
                u = u + dw_ref[tap:tap + 1, :] * win[CV_WIN + dlt:CV_WIN + dlt + tt, :]
        u = u + dwb_ref[...]
        mu = jnp.mean(u, axis=-1, keepdims=True)
        uc = u - mu
        var = jnp.mean(uc * uc, axis=-1, keepdims=True)
        un = uc * lax.rsqrt(var + 1e-5) * lng_ref[...] + lnb_ref[...]
        un = un * jax.nn.sigmoid(un)
        y_ref[0, pl.ds(base, tt), :] = (_dot(un, pw_ref[...]) + pwb_ref[...]).astype(ACT)
        return carry

    lax.fori_loop(0, ntile, tile, 0)


def _conv_mixer(zd, p, latent):
    b, t, c = zd.shape
    pad = CV_PAD * GRID_W if latent else CV_WIN
    full = lambda *s: pl.BlockSpec(s, lambda i: (0,) * len(s))
    vec = full(1, W_MIX)
    return pl.pallas_call(
        functools.partial(_conv_kernel, t=t, latent=latent),
        grid=(b,),
        in_specs=[pl.BlockSpec((1, t, c), lambda i: (i, 0, 0)),
                  full(CV_KERNEL, W_MIX), vec, vec, vec, full(W_MIX, W_MIX), vec],
        out_specs=pl.BlockSpec((1, t, W_MIX), lambda i: (i, 0, 0)),
        out_shape=jax.ShapeDtypeStruct((b, t, W_MIX), ACT),
        scratch_shapes=[pltpu.VMEM((t + 2 * pad, W_MIX), F32)],
        compiler_params=_cparams(("arbitrary",)),
        name="conv_grid" if latent else "conv_ctx",
    )(zd, p["cv_dw_w"], p["cv_dw_b"], p["cv_ln_g"], p["cv_ln_b"], p["cv_pw_w"], p["cv_pw_b"])


MOE_TM = 512
MOE_RB = 128
E_PAD = LANE


def _route(sel, scores):
    grp = []
    for g in range(N_GROUPS):
        s = sel[GROUP_SIZE * g:GROUP_SIZE * (g + 1)]
        best_pair = None
        for i in range(GROUP_SIZE):
            for j in range(i + 1, GROUP_SIZE):
                pair = s[i] + s[j]
                best_pair = pair if best_pair is None else jnp.maximum(best_pair, pair)
        grp.append(best_pair)
    best = jnp.zeros_like(grp[0], dtype=jnp.int32)
    top = grp[0]
    for g in range(1, N_GROUPS):
        better = grp[g] > top
        best = jnp.where(better, g, best)
        top = jnp.where(better, grp[g], top)
    neg = jnp.full_like(sel[0], -jnp.inf)
    msel = [jnp.where(best == e // GROUP_SIZE, sel[e], neg) for e in range(N_EXPERTS)]
    picks = []
    for _ in range(2):
        idx = jnp.zeros_like(best)
        top = msel[0]
        for e in range(1, N_EXPERTS):
            better = msel[e] > top
            idx = jnp.where(better, e, idx)
            top = jnp.where(better, msel[e], top)
        picks.append(idx)
        msel = [jnp.where(idx == e, neg, msel[e]) for e in range(N_EXPERTS)]
    chosen = [jnp.where((picks[0] == e) | (picks[1] == e), scores[e], 0.0) for e in range(N_EXPERTS)]
    total = chosen[0]
    for e in range(1, N_EXPERTS):
        total = total + chosen[e]
    return [ch / total for ch in chosen], best


def _moe_kernel(x_ref, ya_ref, yb_ref, yc_ref, yd_ref, wo_ref, g1_ref, sh2_ref, sc2_ref, g2_ref, n2_ref,
                wr_ref, br_ref, w1_ref, w3_ref, w2_ref, nf_ref, o_ref,
                x1_s, h2_s, gt_s, gate_s, acc_s, he_s, og_s, pmt_s, seg_s, *, final_norm):
    grp = pl.program_id(2)
    tm = x1_s.shape[0]

    @pl.when(grp == 0)
    def _():
        y = sum(_dot(y_ref[0], wo_ref[W_MIX * m:W_MIX * (m + 1), :])
                for m, y_ref in enumerate((ya_ref, yb_ref, yc_ref, yd_ref)))
        x1 = x_ref[0] + g1_ref[0] * y
        x1_s[...] = x1
        h2 = x1 * lax.rsqrt(jnp.mean(x1 * x1, axis=-1, keepdims=True) + NORM_EPS) * n2_ref[...]
        h2 = h2 * (1.0 + sc2_ref[0]) + sh2_ref[0]
        logits = _dot3_nt(wr_ref[...], h2)
        scores = jax.nn.sigmoid(logits)
        selm = scores + br_ref[...]
        gates, best = _route([selm[i:i + 1, :] for i in range(N_EXPERTS)],
                             [scores[i:i + 1, :] for i in range(N_EXPERTS)])
        gt_s[...] = jnp.zeros(gt_s.shape, F32)
        for i in range(N_EXPERTS):
            gt_s[i:i + 1, :] = gates[i]

        og = [jnp.where(best == g, 1.0, 0.0) for g in range(N_GROUPS)]
        og_s[...] = jnp.zeros(og_s.shape, F32)
        start = jnp.int32(0)
        starts = []
        for g in range(N_GROUPS):
            og_s[g:g + 1, :] = og[g]
            starts.append(start)
            seg_s[g] = start
            start = start + jnp.sum(og[g]).astype(jnp.int32)
            seg_s[N_GROUPS + g] = start
        before = jnp.where(_iota((tm, tm), 0) < _iota((tm, tm), 1), 1.0, 0.0).astype(BF16)
        rank = jnp.dot(og_s[...].astype(BF16), before, preferred_element_type=F32)
        pos = og[0] * (starts[0].astype(F32) + rank[0:1, :])
        for g in range(1, N_GROUPS):
            pos = pos + og[g] * (starts[g].astype(F32) + rank[g:g + 1, :])
        gt_s[N_EXPERTS:N_EXPERTS + 1, :] = pos
        gate_tok = gt_s[...].T
        slot_l = _iota((tm, tm), 1).astype(F32)
        slot_s = _iota((tm, tm), 0).astype(F32)
        pmt_s[...] = jnp.where(slot_l == gate_tok[:, N_EXPERTS:N_EXPERTS + 1], 1.0, 0.0).astype(BF16)
        pm = jnp.where(slot_s == pos, 1.0, 0.0).astype(BF16)
        h2_s[...] = jnp.dot(pm, h2.astype(BF16), preferred_element_type=F32).astype(BF16)
        gate_s[...] = _dot_left_x2(pm, gate_tok)
        acc_s[...] = jnp.zeros(acc_s.shape, F32)

    pick = (_iota((E_PAD, GROUP_SIZE * LANE), 0)
            == GROUP_SIZE * grp + _iota((E_PAD, GROUP_SIZE * LANE), 1) // LANE).astype(F32)

    def block(rb, nrows):
        rows = pl.ds(pl.multiple_of(rb * MOE_RB, MOE_RB), nrows)
        gsel = _dot_x2(gate_s[rows, :], pick)
        h2 = h2_s[rows, :]
        for j in range(GROUP_SIZE):
            he = jnp.dot(h2, w1_ref[j], preferred_element_type=F32)
            he = he * jax.nn.sigmoid(he) * jnp.dot(h2, w3_ref[j], preferred_element_type=F32)
            g = gsel[:, LANE * j:LANE * (j + 1)]
            he_s[rows, D_EXPERT * j:D_EXPERT * (j + 1)] = (he * jnp.concatenate([g, g], axis=1)).astype(BF16)
        w2g = w2_ref[...].reshape(GROUP_SIZE * D_EXPERT, D_MODEL)
        acc_s[rows, :] += jnp.dot(he_s[rows, :], w2g, preferred_element_type=F32)

    first = seg_s[grp] // MOE_RB
    last = (seg_s[N_GROUPS + grp] + (MOE_RB - 1)) // MOE_RB

    def pair(pi, carry):
        block(first + 2 * pi, 2 * MOE_RB)
        return carry

    lax.fori_loop(0, (last - first) // 2, pair, 0)

    @pl.when((last - first) % 2 == 1)
    def _():
        block(last - 1, MOE_RB)

    @pl.when(grp == N_GROUPS - 1)
    def _():
        x2 = x1_s[...] + g2_ref[0] * _dot_left_x2(pmt_s[...], acc_s[...])
        if final_norm:
            x2 = x2 * lax.rsqrt(jnp.mean(x2 * x2, axis=-1, keepdims=True) + NORM_EPS) * nf_ref[...]
        o_ref[0] = x2


def _out_moe(x, ys, m, p, norm_f, final_norm):
    b, t, _ = x.shape
    tm = min(MOE_TM, t)
    tok = lambda w_: pl.BlockSpec((1, tm, w_), lambda i, j, e: (i, j, 0))
    mod = lambda k: pl.BlockSpec((1, 1, D_MODEL), lambda i, j, e, k=k: (i, 0, k))
    full = lambda *s: pl.BlockSpec(s, lambda i, j, e: (0,) * len(s))
    return pl.pallas_call(
        functools.partial(_moe_kernel, final_norm=final_norm),
        grid=(b, t // tm, N_GROUPS),
        in_specs=[tok(D_MODEL), tok(W_MIX), tok(W_MIX), tok(W_MIX), tok(W_MIX),
                  full(D_MODEL, D_MODEL), mod(2), mod(3), mod(4), mod(5), full(1, D_MODEL),
                  full(N_EXPERTS, D_MODEL), full(N_EXPERTS, 1),
                  pl.BlockSpec((GROUP_SIZE, D_MODEL, D_EXPERT), lambda i, j, e: (e, 0, 0)),
                  pl.BlockSpec((GROUP_SIZE, D_MODEL, D_EXPERT), lambda i, j, e: (e, 0, 0)),
                  pl.BlockSpec((GROUP_SIZE, D_EXPERT, D_MODEL), lambda i, j, e: (e, 0, 0)),
                  full(1, D_MODEL)],
        out_specs=tok(D_MODEL),
        out_shape=jax.ShapeDtypeStruct((b, t, D_MODEL), F32),
        scratch_shapes=[pltpu.VMEM((tm, D_MODEL), F32), pltpu.VMEM((tm, D_MODEL), BF16),
                        pltpu.VMEM((E_PAD, tm), F32), pltpu.VMEM((tm, E_PAD), F32),
                        pltpu.VMEM((tm, D_MODEL), F32), pltpu.VMEM((tm, D_EXPERT * GROUP_SIZE), BF16),
                        pltpu.VMEM((SUBLANE, tm), F32), pltpu.VMEM((tm, tm), BF16),
                        pltpu.SMEM((2 * N_GROUPS,), jnp.int32)],
        compiler_params=_cparams(("arbitrary", "arbitrary", "arbitrary")),
        name="out_moe",
    )(x, *ys, p["w_out"], m, m, m, m, p["norm2_g"], p["w_router_t"], p["b_router"], p["e_w1"], p["e_w3"], p["e_w2"],
      norm_f)


def _gla_pack(z):
    hk = GLA_HK
    lead = z.shape[:-1]
    gl = z[..., 2 * hk + W_MIX:2 * hk + W_MIX + 32]
    return jnp.concatenate([z[..., 0:2 * hk + W_MIX], z[..., 2 * hk + W_MIX + 32:], gl,
                            jnp.zeros(lead + (96,), z.dtype)], -1)


def _gla_state_in(s):
    b = s.shape[0]
    eye = jnp.eye(N_HEADS, dtype=s.dtype)
    return jnp.einsum("bdhkv,hg->bdhvgk", s, eye).reshape(b, N_DIR, W_MIX, GLA_HK)


def _gla_state_out(st):
    b = st.shape[0]
    eye = jnp.eye(N_HEADS, dtype=st.dtype)
    return jnp.einsum("bdhvgk,hg->bdhkv", st.reshape(b, N_DIR, N_HEADS, HEAD, N_HEADS, GLA_DK), eye)


def _rw_state_in(s):
    b = s.shape[0]
    eye = jnp.eye(N_HEADS, dtype=s.dtype)
    return jnp.einsum("bdhvk,hg->bdhvgk", s, eye).reshape(b, N_DIR, W_MIX, W_MIX)


def _rw_state_out(st):
    b = st.shape[0]
    eye = jnp.eye(N_HEADS, dtype=st.dtype)
    return jnp.einsum("bdhvgk,hg->bdhvk", st.reshape(b, N_DIR, N_HEADS, HEAD, N_HEADS, HEAD), eye)


def _prep_params(d):
    L = DEPTH
    z = lambda *s: jnp.zeros(s, F32)
    out = {}
    out["rw_mu"] = jnp.concatenate([d["rw_mu"], z(L, ZA_W - d["rw_mu"].shape[-1])], -1).reshape(L, 1, ZA_W)
    w2, a2 = d["rw_w2"], d["rw_a2"]
    zz = z(L, 32, 256)
    rows = [
        jnp.concatenate([w2[:, 0], zz, zz, zz], -1),
        jnp.concatenate([zz, w2[:, 1], zz, zz], -1),
        jnp.concatenate([zz, zz, a2[:, 0], zz], -1),
        jnp.concatenate([zz, zz, zz, a2[:, 1]], -1),
    ]
    out["rw_wl"] = jnp.concatenate(rows, 1).astype(BF16)
    out["rw_lb"] = jnp.concatenate([d["rw_w0"][:, 0], d["rw_w0"][:, 1], d["rw_a0"][:, 0], d["rw_a0"][:, 1]],
                                   -1).reshape(L, 1, 1024)
    for n in ("rw_kk", "rw_ka", "rw_ln_g", "rw_ln_b"):
        out[n] = d[n].reshape(L, 1, W_MIX)
    out["rw_rk"] = d["rw_rk"].reshape(L, 1, W_MIX)
    out["rw_g2"] = jnp.concatenate([d["rw_g2"], z(L, 64, W_MIX)], 1).astype(BF16)

    gk2 = d["gla_gk2"]
    z16, z96 = z(L, 16, GLA_HK), z(L, 96, GLA_HK)
    out["gla_wg"] = jnp.stack([jnp.concatenate([gk2[:, 0], z16, z96], 1),
                               jnp.concatenate([z16, gk2[:, 1], z96], 1)], 1).astype(BF16)
    out["gla_gb"] = d["gla_gkb"].reshape(L, N_DIR, 1, GLA_HK)
    out["gla_ln_g"] = jnp.tile(d["gla_ln_g"], (1, N_HEADS)).reshape(L, 1, W_MIX)

    eye = jnp.eye(4, dtype=F32)
    bd = lambda w: jnp.einsum("ldgij,gh->ldgihj", w, eye).reshape(L, N_DIR, W_MIX, W_MIX)
    out["lru_cw"] = d["lru_conv_w"]
    out["lru_cb"] = d["lru_conv_b"].reshape(L, N_DIR, 1, W_MIX)
    out["lru_wax"] = jnp.concatenate([bd(d["lru_wa"]), bd(d["lru_wx"])], -1).astype(BF16)
    out["lru_bax"] = jnp.concatenate([d["lru_ba"], d["lru_bx"]], -1).reshape(L, N_DIR, 1, 2 * W_MIX)
    out["lru_lam"] = d["lru_lam"].reshape(L, N_DIR, 1, W_MIX)

    out["cv_dw_w"] = d["cv_dw_w"]
    for n in ("cv_dw_b", "cv_ln_g", "cv_ln_b", "cv_pw_b"):
        out[n] = d[n].reshape(L, 1, W_MIX)
    out["cv_pw_w"] = d["cv_pw_w"].astype(BF16)

    w_in = d["w_in"]
    o1 = RW_COLS
    o2 = o1 + GLA_COLS
    o3 = o2 + ZC_W
    out["w_in"] = jnp.concatenate([w_in[..., 0:o1], z(L, D_MODEL, ZA_W - o1), _gla_pack(w_in[..., o1:o2]),
                                   w_in[..., o2:o3], w_in[..., o3:]], -1).astype(BF16)
    out["w_out"] = d["w_out"].astype(BF16)
    out["norm1_g"] = d["norm1_g"].reshape(L, 1, D_MODEL)
    out["norm2_g"] = d["norm2_g"].reshape(L, 1, D_MODEL)
    for n in ("e_w1", "e_w3", "e_w2"):
        out[n] = d[n].astype(BF16)
    return out


def kernel(x_prompt, x_sample, state_rwkv, state_gla, state_lru, c, c_ctx, norm1_g, norm2_g, norm_f_g, w_ada, b_ada, w_in, w_out, rw_mu, rw_w0, rw_w2, rw_a0, rw_a2, rw_g2, rw_kk, rw_ka, rw_rk, rw_ln_g, rw_ln_b, gla_gk2, gla_gkb, gla_ln_g, lru_conv_w, lru_conv_b, lru_wa, lru_ba, lru_wx, lru_bx, lru_lam, cv_dw_w, cv_dw_b, cv_ln_g, cv_ln_b, cv_pw_w, cv_pw_b, w_router, b_router, e_w1, e_w3, e_w2):
    d = dict(norm1_g=norm1_g, norm2_g=norm2_g, w_in=w_in, w_out=w_out,
             rw_mu=rw_mu, rw_w0=rw_w0, rw_w2=rw_w2, rw_a0=rw_a0, rw_a2=rw_a2, rw_g2=rw_g2, rw_kk=rw_kk,
             rw_ka=rw_ka, rw_rk=rw_rk, rw_ln_g=rw_ln_g, rw_ln_b=rw_ln_b,
             gla_gk2=gla_gk2, gla_gkb=gla_gkb, gla_ln_g=gla_ln_g,
             lru_conv_w=lru_conv_w, lru_conv_b=lru_conv_b, lru_wa=lru_wa, lru_ba=lru_ba, lru_wx=lru_wx,
             lru_bx=lru_bx, lru_lam=lru_lam,
             cv_dw_w=cv_dw_w, cv_dw_b=cv_dw_b, cv_ln_g=cv_ln_g, cv_ln_b=cv_ln_b, cv_pw_w=cv_pw_w, cv_pw_b=cv_pw_b,
             e_w1=e_w1, e_w3=e_w3, e_w2=e_w2)
    P = _prep_params(d)
    shared = {"w_router_t": w_router.T, "b_router": b_router.reshape(N_EXPERTS, 1)}
    norm_f = norm_f_g.reshape(1, D_MODEL)

    bp, tp, _ = x_prompt.shape
    bs, ts, _ = x_sample.shape
    n_mod = 1 + bs
    mod_rows = -(-n_mod // 8) * 8
    cvec = jnp.concatenate([c_ctx[None], c, jnp.zeros((mod_rows - n_mod, D_MODEL), F32)], 0)
    mod = _ada_mod(cvec, w_ada, b_ada)

    tp_flat = min(bp * tp, 1024)
    bp_flat = bp * tp // tp_flat
    xp = x_prompt.reshape(bp_flat, tp_flat, D_MODEL)
    xs = x_sample
    zero_rw = jnp.zeros((bp, N_DIR, W_MIX, W_MIX), F32)
    zero_gla = jnp.zeros((bp, N_DIR, W_MIX, GLA_HK), F32)
    zero_lru = jnp.zeros((bp, N_DIR, W_MIX), F32)

    def layer(x, m, seq_shape, latent, s_rw, s_gla, s_lru, p, last):
        flat_shape = x.shape[:2]
        zs = _in_proj(x, m, p["norm1_g"], p["w_in"])
        za, zb, zc, zd = (z.reshape(seq_shape + (z.shape[-1],)) for z in zs)
        ya, f_rw = _rwkv_mixer(za, s_rw, p, latent)
        yb, f_gla = _gla_mixer(zb, s_gla, p)
        yc, f_lru = _lru_mixer(zc, s_lru, p)
        yd = _conv_mixer(zd, p, latent)
        ys = [y.reshape(flat_shape + (W_MIX,)) for y in (ya, yb, yc, yd)]
        return _out_moe(x, ys, m, p, norm_f, last), f_rw, f_gla, f_lru

    fin_rw, fin_gla, fin_lru = [], [], []
    for l in range(DEPTH):
        p = {n: a[l] for n, a in P.items()}
        p.update(shared)
        last = l == DEPTH - 1
        m_p = jnp.broadcast_to(mod[l, 0:1], (bp_flat, 6 * D_MODEL)).reshape(bp_flat, 1, 6 * D_MODEL)
        xp, f_rw, f_gla, f_lru = layer(xp, m_p, (bp, tp), False, zero_rw, zero_gla, zero_lru, p, last)
        fin_rw.append(_rw_state_out(f_rw))
        fin_gla.append(_gla_state_out(f_gla))
        fin_lru.append(f_lru)
        m_s = mod[l, 1:1 + bs].reshape(bs, 1, 6 * D_MODEL)
        xs, _, _, _ = layer(xs, m_s, (bs, ts), True, _rw_state_in(state_rwkv[:, l]),
                            _gla_state_in(state_gla[:, l]), state_lru[:, l], p, last)
    return (xp.reshape(bp, tp, D_MODEL), xs, jnp.stack(fin_rw, axis=1), jnp.stack(fin_gla, axis=1),
            jnp.stack(fin_lru, axis=1))
```
